```python
import jax, jax.numpy as jnp
from jax import lax
import numpy as np

D_MODEL = 1024
BATCH = 8
SEQ = 2048
DEPTH = 2

N_MIXERS = 2
N_SSD_LAYERS = (DEPTH + 1) // 2
N_DSA_LAYERS = DEPTH // 2
EPS = 1e-6

SSD_EXPAND = 2
SSD_D_INNER = SSD_EXPAND * D_MODEL
SSD_HEAD_DIM = 64
SSD_HEADS = SSD_D_INNER // SSD_HEAD_DIM
SSD_GROUPS = 8
SSD_HEADS_PER_GROUP = SSD_HEADS // SSD_GROUPS
SSD_STATE = 128
SSD_CONV = 4
SSD_CHUNK = 128
SSD_GN = SSD_GROUPS * SSD_STATE
SSD_CONV_DIM = SSD_D_INNER + 2 * SSD_GN
SSD_PROJ = SSD_D_INNER + SSD_CONV_DIM + SSD_HEADS

ATT_HEADS = 16
ATT_KV_HEADS = 4
ATT_Q_PER_KV = ATT_HEADS // ATT_KV_HEADS
ATT_HEAD_DIM = 64
IDX_HEADS = 8
IDX_HEAD_DIM = 64
TOPK_MAX = 256
Q_BLOCK = 128
DSA_Q_END = ATT_HEADS * ATT_HEAD_DIM
DSA_K_END = DSA_Q_END + ATT_KV_HEADS * ATT_HEAD_DIM
DSA_V_END = DSA_K_END + ATT_KV_HEADS * ATT_HEAD_DIM
DSA_QI_END = DSA_V_END + IDX_HEADS * IDX_HEAD_DIM
DSA_KI_END = DSA_QI_END + IDX_HEAD_DIM
DSA_PROJ = DSA_KI_END + IDX_HEADS

MOE_GROUPS = 4
MOE_EXPERTS_PER_GROUP = 4
MOE_EXPERTS = MOE_GROUPS * MOE_EXPERTS_PER_GROUP
MOE_TOPK = 2
MOE_HIDDEN = 256

kernel_name = "hybrid_ssd_dsa_hmoe_adaln"


def rms_norm(x):
    xf = x.astype(jnp.float32)
    return (xf * lax.rsqrt(jnp.mean(xf * xf, axis=-1, keepdims=True) + EPS)).astype(x.dtype)


def causal_depthwise_conv(u, w, b):
    k = w.shape[0]
    out = lax.conv_general_dilated(
        u, w[:, None, :].astype(u.dtype), window_strides=(1,), padding=[(k - 1, 0)],
        dimension_numbers=('NWC', 'WIO', 'NWC'), feature_group_count=u.shape[-1])
    return out + b


def ssd_chunked_scan(X, A, Bm, Cm):
    b, l, g, j, p = X.shape
    n = Bm.shape[-1]
    q = SSD_CHUNK
    c = l // q
    Xc = X.reshape(b, c, q, g, j, p)
    Bc = Bm.reshape(b, c, q, g, n)
    Cc = Cm.reshape(b, c, q, g, n)
    A = A.reshape(b, c, q, g, j).transpose(0, 1, 3, 4, 2)
    A_cs = jnp.cumsum(A, axis=-1)
    tril = jnp.tril(jnp.ones((q, q), dtype=bool))
    seg = A_cs[..., :, None] - A_cs[..., None, :]
    decay_in = jnp.exp(jnp.where(tril, seg, -jnp.inf))
    cb = jnp.einsum('bclgn,bcsgn->bcgls', Cc, Bc)
    y_diag = jnp.einsum('bcgjls,bcsgjp->bclgjp', cb[:, :, :, None] * decay_in, Xc)
    decay_to_end = jnp.exp(A_cs[..., -1:] - A_cs).transpose(0, 1, 4, 2, 3)
    states = jnp.einsum('bcsgn,bcsgjp->bcgjpn', Bc, Xc * decay_to_end[..., None])
    chunk_decay = jnp.exp(A_cs[..., -1])

    def step(carry, inp):
        st, dec = inp
        return carry * dec[..., None, None] + st, carry

    init = jnp.zeros((b, g, j, p, n), states.dtype)
    _, prev = lax.scan(step, init, (jnp.moveaxis(states, 1, 0), jnp.moveaxis(chunk_decay, 1, 0)))
    prev = jnp.moveaxis(prev, 0, 1)
    decay_from_start = jnp.exp(A_cs).transpose(0, 1, 4, 2, 3)
    y_off = jnp.einsum('bclgn,bcgjpn->bclgjp', Cc, prev) * decay_from_start[..., None]
    return (y_diag + y_off).reshape(b, l, g, j, p)


def ssd_mixer(h, w_in, conv_w, conv_b, dt_bias, a_log, d_skip, norm_w, w_out):
    b, l, _ = h.shape
    G, J, P = SSD_GROUPS, SSD_HEADS_PER_GROUP, SSD_HEAD_DIM
    proj = h @ w_in
    z = proj[..., :SSD_D_INNER]
    xbc = proj[..., SSD_D_INNER:SSD_D_INNER + SSD_CONV_DIM]
    dt = proj[..., SSD_D_INNER + SSD_CONV_DIM:]
    xbc = jax.nn.silu(causal_depthwise_conv(xbc, conv_w, conv_b))
    xs = xbc[..., :SSD_D_INNER].reshape(b, l, G, J, P)
    Bm = xbc[..., SSD_D_INNER:SSD_D_INNER + SSD_GN].reshape(b, l, G, SSD_STATE)
    Cm = xbc[..., SSD_D_INNER + SSD_GN:].reshape(b, l, G, SSD_STATE)
    dt = jax.nn.softplus((dt + dt_bias).astype(jnp.float32)).reshape(b, l, G, J)
    A = -jnp.exp(a_log.astype(jnp.float32)).reshape(G, J)
    y = ssd_chunked_scan(xs * dt[..., None], dt * A, Bm, Cm)
    y = y + xs * d_skip.reshape(G, J, 1)
    y = y * jax.nn.silu(z).reshape(b, l, G, J, P)
    y = rms_norm(y.reshape(b, l, G, J * P)) * norm_w.reshape(G, J * P)
    return y.reshape(b, l, SSD_D_INNER).astype(h.dtype) @ w_out


def dsa_mixer(h, w_in, q_norm, k_norm, w_out):
    b, l, _ = h.shape
    proj = h @ w_in
    q = proj[..., :DSA_Q_END].reshape(b, l, ATT_KV_HEADS, ATT_Q_PER_KV, ATT_HEAD_DIM)
    k = proj[..., DSA_Q_END:DSA_K_END].reshape(b, l, ATT_KV_HEADS, ATT_HEAD_DIM)
    v = proj[..., DSA_K_END:DSA_V_END].reshape(b, l, ATT_KV_HEADS, ATT_HEAD_DIM)
    qi = proj[..., DSA_V_END:DSA_QI_END].reshape(b, l, IDX_HEADS, IDX_HEAD_DIM)
    ki = proj[..., DSA_QI_END:DSA_KI_END]
    wi = proj[..., DSA_KI_END:]
    q = rms_norm(q) * q_norm
    k = rms_norm(k) * k_norm
    topk = min(TOPK_MAX, l // 4)
    nb = l // Q_BLOCK

    def to_blocks(t):
        return jnp.moveaxis(t.reshape((b, nb, Q_BLOCK) + t.shape[2:]), 1, 0)

    starts = jnp.arange(nb, dtype=jnp.int32) * Q_BLOCK
    key_pos = jnp.arange(l, dtype=jnp.int32)
    gather = jax.vmap(lambda table, idx: table[idx])

    def attend_block(args):
        q_b, qi_b, wi_b, start = args
        q_pos = start + jnp.arange(Q_BLOCK, dtype=jnp.int32)
        rel = jax.nn.relu(jnp.einsum('bthe,bse->bhts', qi_b, ki).astype(jnp.float32) * IDX_HEAD_DIM ** -0.5)
        score = jnp.einsum('bth,bhts->bts', wi_b.astype(jnp.float32) * IDX_HEADS ** -0.5, rel)
        causal = key_pos[None, :] <= q_pos[:, None]
        score = jnp.where(causal[None], score, -jnp.inf)
        _, idx = lax.top_k(score, topk)
        valid = idx <= q_pos[None, :, None]
        k_sel = gather(k, idx)
        v_sel = gather(v, idx)
        s = jnp.einsum('bthgd,btkhd->bthgk', q_b, k_sel).astype(jnp.float32) * ATT_HEAD_DIM ** -0.5
        s = jnp.where(valid[:, :, None, None, :], s, -jnp.inf)
        p = jax.nn.softmax(s, axis=-1)
        return jnp.einsum('bthgk,btkhd->bthgd', p.astype(v.dtype), v_sel)

    o = lax.map(attend_block, (to_blocks(q), to_blocks(qi), to_blocks(wi), starts))
    o = jnp.moveaxis(o, 0, 1).reshape(b, l, ATT_HEADS * ATT_HEAD_DIM)
    return o @ w_out


def hierarchical_moe(h, w_group, b_group, w_expert, b_expert, w1, w3, w2):
    b, l, d = h.shape
    t = h.reshape(-1, d)
    g_prob = jax.nn.softmax((t @ w_group + b_group).astype(jnp.float32), axis=-1)
    g_val, g_idx = lax.top_k(g_prob, 1)
    e_logits = (t @ w_expert + b_expert).astype(jnp.float32).reshape(-1, MOE_GROUPS, MOE_EXPERTS_PER_GROUP)
    in_group = jnp.take_along_axis(e_logits, g_idx[:, :, None], axis=1)[:, 0]
    e_val, e_idx = lax.top_k(in_group, MOE_TOPK)
    e_w = jax.nn.softmax(e_val, axis=-1) * g_val
    expert_id = g_idx * MOE_EXPERTS_PER_GROUP + e_idx
    combine = jnp.sum(jax.nn.one_hot(expert_id, MOE_EXPERTS, dtype=jnp.float32) * e_w[..., None], axis=1)
    hid = jax.nn.silu(jnp.einsum('td,edf->tef', t, w1)) * jnp.einsum('td,edf->tef', t, w3)
    out = jnp.einsum('tef,efd->td', hid * combine[..., None].astype(hid.dtype), w2)
    return out.reshape(b, l, d)


def setup_inputs(seed: int = 0) -> dict:
    key = jax.random.key(seed)
    ks = jax.random.split(key, 25)

    def nrm(k, shape, scale):
        return jax.random.normal(k, shape, jnp.float32) * scale

    D = D_MODEL
    dt0 = jnp.exp(jax.random.uniform(ks[9], (N_SSD_LAYERS, SSD_HEADS), jnp.float32,
                                     minval=float(np.log(1e-3)), maxval=float(np.log(1e-1))))
    return {
        "x": nrm(ks[0], (BATCH, SEQ, D), 1.0),
        "c": nrm(ks[1], (BATCH, D), 1.0),
        "ada_w": nrm(ks[2], (DEPTH, D, 6 * D), D ** -0.5),
        "ada_b": nrm(ks[3], (DEPTH, 6 * D), 0.01),
        "norm_mix": 1.0 + nrm(ks[4], (DEPTH, D), 0.02),
        "norm_ffn": 1.0 + nrm(ks[5], (DEPTH, D), 0.02),
        "ssd_w_in": nrm(ks[6], (N_SSD_LAYERS, D, SSD_PROJ), D ** -0.5),
        "ssd_conv_w": nrm(ks[7], (N_SSD_LAYERS, SSD_CONV, SSD_CONV_DIM), SSD_CONV ** -0.5),
        "ssd_conv_b": nrm(ks[8], (N_SSD_LAYERS, SSD_CONV_DIM), 0.01),
        "ssd_dt_bias": dt0 + jnp.log(-jnp.expm1(-dt0)),
        "ssd_a_log": jnp.log(jax.random.uniform(ks[10], (N_SSD_LAYERS, SSD_HEADS), jnp.float32, minval=1.0, maxval=16.0)),
        "ssd_d": 1.0 + nrm(ks[11], (N_SSD_LAYERS, SSD_HEADS), 0.02),
        "ssd_norm": 1.0 + nrm(ks[12], (N_SSD_LAYERS, SSD_D_INNER), 0.02),
        "ssd_w_out": nrm(ks[13], (N_SSD_LAYERS, SSD_D_INNER, D), SSD_D_INNER ** -0.5),
        "dsa_w_in": nrm(ks[14], (N_DSA_LAYERS, D, DSA_PROJ), D ** -0.5),
        "dsa_q_norm": 1.0 + nrm(ks[15], (N_DSA_LAYERS, ATT_HEAD_DIM), 0.02),
        "dsa_k_norm": 1.0 + nrm(ks[16], (N_DSA_LAYERS, ATT_HEAD_DIM), 0.02),
        "dsa_w_out": nrm(ks[17], (N_DSA_LAYERS, ATT_HEADS * ATT_HEAD_DIM, D), (ATT_HEADS * ATT_HEAD_DIM) ** -0.5),
        "moe_w_group": nrm(ks[18], (DEPTH, D, MOE_GROUPS), D ** -0.5),
        "moe_b_group": nrm(ks[19], (DEPTH, MOE_GROUPS), 0.01),
        "moe_w_expert": nrm(ks[20], (DEPTH, D, MOE_EXPERTS), D ** -0.5),
        "moe_b_expert": nrm(ks[21], (DEPTH, MOE_EXPERTS), 0.01),
        "moe_w1": nrm(ks[22], (DEPTH, MOE_EXPERTS, D, MOE_HIDDEN), D ** -0.5),
        "moe_w3": nrm(ks[23], (DEPTH, MOE_EXPERTS, D, MOE_HIDDEN), D ** -0.5),
        "moe_w2": nrm(ks[24], (DEPTH, MOE_EXPERTS, MOE_HIDDEN, D), MOE_HIDDEN ** -0.5),
    }


def reference(x, c, ada_w, ada_b, norm_mix, norm_ffn, ssd_w_in, ssd_conv_w, ssd_conv_b, ssd_dt_bias,
              ssd_a_log, ssd_d, ssd_norm, ssd_w_out, dsa_w_in, dsa_q_norm, dsa_k_norm, dsa_w_out,
              moe_w_group, moe_b_group, moe_w_expert, moe_b_expert, moe_w1, moe_w3, moe_w2):
    cond = jax.nn.silu(c)
    for i in range(DEPTH):
        mod = (cond @ ada_w[i] + ada_b[i])[:, None, :]
        shift_m, scale_m, gate_m, shift_f, scale_f, gate_f = jnp.split(mod, 6, axis=-1)
        h = rms_norm(x) * norm_mix[i] * (1.0 + scale_m) + shift_m
        j = i // N_MIXERS
        if i % N_MIXERS == 0:
            y = ssd_mixer(h, ssd_w_in[j], ssd_conv_w[j], ssd_conv_b[j], ssd_dt_bias[j], ssd_a_log[j],
                          ssd_d[j], ssd_norm[j], ssd_w_out[j])
        else:
            y = dsa_mixer(h, dsa_w_in[j], dsa_q_norm[j], dsa_k_norm[j], dsa_w_out[j])
        x = x + gate_m * y
        h = rms_norm(x) * norm_ffn[i] * (1.0 + scale_f) + shift_f
        x = x + gate_f * hierarchical_moe(h, moe_w_group[i], moe_b_group[i], moe_w_expert[i], moe_b_expert[i],
                                          moe_w1[i], moe_w3[i], moe_w2[i])
    return x
```

```python
import functools

import jax
import jax.numpy as jnp
from jax import lax
from jax.experimental import pallas as pl
from jax.experimental.pallas import tpu as pltpu

F32 = jnp.float32
BF16 = jnp.bfloat16

D_MODEL = 1024
BATCH = 8
SEQ = 2048
TOKENS = BATCH * SEQ
EPS = 1e-6

SSD_D_INNER = 2048
SSD_HEAD_DIM = 64
SSD_HEADS = 32
SSD_GROUPS = 8
SSD_HEADS_PER_GROUP = 4
SSD_STATE = 128
SSD_CONV = 4
SSD_CHUNK = 128
SSD_GN = SSD_GROUPS * SSD_STATE
SSD_CONV_DIM = SSD_D_INNER + 2 * SSD_GN
SSD_GROUP_W = SSD_HEADS_PER_GROUP * SSD_HEAD_DIM
SSD_DT_PAD = 128
SSD_PROJ_PAD = SSD_D_INNER + SSD_CONV_DIM + SSD_DT_PAD
CONV_HALO = 8

ATT_HEADS = 16
ATT_KV_HEADS = 4
ATT_Q_PER_KV = 4
ATT_HEAD_DIM = 64
IDX_HEADS = 8
IDX_HEAD_DIM = 64
TOPK = 256
Q_BLOCK = 128
DSA_Q = ATT_HEADS * ATT_HEAD_DIM
DSA_KV = ATT_KV_HEADS * ATT_HEAD_DIM
DSA_QI = IDX_HEADS * IDX_HEAD_DIM
DSA_KI_START = DSA_Q + 2 * DSA_KV + DSA_QI
DSA_WI_START = DSA_KI_START + 128
DSA_PROJ_PAD = DSA_WI_START + 128

MOE_GROUPS = 4
MOE_EPG = 4
MOE_EXPERTS = 16
MOE_HIDDEN = 256
ROUTE_PAD = 128

VMEM_LIMIT = 56 * 1024 * 1024

KEY_LO = -2139095040
KEY_HI = 2139095040


def _sigmoid(v):
    return 1.0 / (1.0 + jnp.exp(-v))


def _silu(v):
    return v * _sigmoid(v)


def _split3(a):
    hi = a.astype(BF16)
    r = a - hi.astype(F32)
    mid = r.astype(BF16)
    lo = (r - mid.astype(F32)).astype(BF16)
    return hi, mid, lo


def _dot(a, b):
    return jnp.dot(a, b, preferred_element_type=F32)


def _dot_nt(a, b):
    return lax.dot_general(a, b, (((1,), (1,)), ((), ())), preferred_element_type=F32)


def _dot3_exact_rhs(a, m):
    hi, mid, lo = _split3(a)
    return _dot(hi, m) + _dot(mid, m) + _dot(lo, m)


def _dot3_exact_lhs(m, a):
    hi, mid, lo = _split3(a)
    return _dot(m, hi) + _dot(m, mid) + _dot(m, lo)


def _norm_mod(x, nw, scale, shift):
    ms = jnp.mean(x * x, axis=-1, keepdims=True)
    return x * lax.rsqrt(ms + EPS) * nw * (1.0 + scale) + shift


MOD_TN = 1536


def _mod_kernel(c_ref, w_ref, b_ref, o_ref):
    cond = _silu(c_ref[...]).astype(BF16)
    o_ref[...] = _dot(cond, w_ref[...].astype(BF16)) + b_ref[...]


def _modulation(c, ada_w, ada_b):
    depth = ada_w.shape[0]
    n = ada_w.shape[2]
    return pl.pallas_call(
        _mod_kernel,
        grid=(depth, n // MOD_TN),
        in_specs=[
            pl.BlockSpec((BATCH, D_MODEL), lambda i, j: (0, 0)),
            pl.BlockSpec((None, D_MODEL, MOD_TN), lambda i, j: (i, 0, j)),
            pl.BlockSpec((None, 1, MOD_TN), lambda i, j: (i, 0, j)),
        ],
        out_specs=pl.BlockSpec((None, BATCH, MOD_TN), lambda i, j: (i, 0, j)),
        out_shape=jax.ShapeDtypeStruct((depth, BATCH, n), F32),
        compiler_params=pltpu.CompilerParams(
            dimension_semantics=("arbitrary", "arbitrary"), vmem_limit_bytes=VMEM_LIMIT),
        name="adaln_mod",
    )(c, ada_w, ada_b.reshape(depth, 1, n))


def _mod_spec(layer, chunk, rows_per_batch_tile):
    return pl.BlockSpec((None, None, None, 1, D_MODEL),
                        lambda i, *_: (layer, i // rows_per_batch_tile, chunk, 0, 0))


INPROJ_TM = 256


def _inproj_kernel(x_ref, nw_ref, scale_ref, shift_ref, w_ref, *o_refs, col_slices):
    h = _norm_mod(x_ref[...], nw_ref[...], scale_ref[...], shift_ref[...]).astype(BF16)
    for o_ref, (lo, hi) in zip(o_refs, col_slices):
        o_ref[...] = _dot(h, w_ref[:, lo:hi])


def _inproj(x, nw, mod5, layer, scale_chunk, shift_chunk, w, col_slices):
    tm = INPROJ_TM
    tiles_per_batch = SEQ // tm
    n_pad = w.shape[1]
    return pl.pallas_call(
        functools.partial(_inproj_kernel, col_slices=col_slices),
        grid=(TOKENS // tm,),
        in_specs=[
            pl.BlockSpec((tm, D_MODEL), lambda i: (i, 0)),
            pl.BlockSpec((1, D_MODEL), lambda i: (0, 0)),
            _mod_spec(layer, scale_chunk, tiles_per_batch),
            _mod_spec(layer, shift_chunk, tiles_per_batch),
            pl.BlockSpec((D_MODEL, n_pad), lambda i: (0, 0)),
        ],
        out_specs=[pl.BlockSpec((tm, hi - lo), lambda i: (i, 0)) for lo, hi in col_slices],
        out_shape=[jax.ShapeDtypeStruct((TOKENS, hi - lo), F32) for lo, hi in col_slices],
        compiler_params=pltpu.CompilerParams(
            dimension_semantics=("arbitrary",), vmem_limit_bytes=VMEM_LIMIT),
        name="norm_inproj",
    )(x, nw, mod5, mod5, w)


CONV_COLS = 512


def _ssd_kernel(z_ref, xbc_ref, dt_ref, xres_ref, cw_ref, cb_ref, dtb_ref, alog_ref, de_ref, nw_ref,
                e_ref, wout_ref, gate_ref, o_ref, state_ref, ext_ref, act_ref, yn_ref):
    q = SSD_CHUNK
    c = pl.program_id(1)

    @pl.when(c == 0)
    def _():
        state_ref[...] = jnp.zeros_like(state_ref)
        ext_ref[0:CONV_HALO, :] = jnp.zeros((CONV_HALO, SSD_CONV_DIM), F32)

    @pl.when(c > 0)
    def _():
        ext_ref[0:CONV_HALO, :] = ext_ref[q:q + CONV_HALO, :]

    ext_ref[CONV_HALO:CONV_HALO + q, :] = xbc_ref[...]

    for s in range(SSD_CONV_DIM // CONV_COLS):
        cs = slice(s * CONV_COLS, (s + 1) * CONV_COLS)
        acc = cb_ref[:, cs] + cw_ref[SSD_CONV - 1:SSD_CONV, cs] * ext_ref[CONV_HALO:CONV_HALO + q, cs]
        for k in range(SSD_CONV - 1):
            off = CONV_HALO - (SSD_CONV - 1) + k
            acc = acc + cw_ref[k:k + 1, cs] * ext_ref[off:off + q, cs]
        act_ref[:, cs] = _silu(acc)

    dt_raw = dt_ref[...] + dtb_ref[...]
    dt = jnp.maximum(dt_raw, 0.0) + jnp.log1p(jnp.exp(-jnp.abs(dt_raw)))
    a = dt * (-jnp.exp(alog_ref[...]))
    row = lax.broadcasted_iota(jnp.int32, (q, q), 0)
    col = lax.broadcasted_iota(jnp.int32, (q, q), 1)
    tril = row >= col
    acs = _dot3_exact_lhs(tril.astype(BF16), a)
    acs_t = acs.T
    expand = e_ref[...]
    acs_e = _dot3_exact_rhs(acs, expand)
    dt_e = _dot3_exact_rhs(dt, expand)
    tot_e = acs_e[q - 1:q, :]
    decay_from_start = jnp.exp(acs_e)
    decay_to_end = jnp.exp(tot_e - acs_e)
    chunk_decay = jnp.exp(tot_e)

    lane_head = lax.broadcasted_iota(jnp.int32, (q, SSD_GROUP_W), 1) // SSD_HEAD_DIM
    for g in range(SSD_GROUPS):
        xs = act_ref[:, g * SSD_GROUP_W:(g + 1) * SSD_GROUP_W]
        gs = slice(g * SSD_GROUP_W, (g + 1) * SSD_GROUP_W)
        bm = act_ref[:, SSD_D_INNER + g * SSD_STATE:SSD_D_INNER + (g + 1) * SSD_STATE]
        cm = act_ref[:, SSD_D_INNER + SSD_GN + g * SSD_STATE:SSD_D_INNER + SSD_GN + (g + 1) * SSD_STATE]
        bm_t = bm.T.astype(BF16)
        cm_b = cm.astype(BF16)
        cb = _dot(cm_b, bm_t)
        xd = xs * dt_e[:, gs]
        ms = []
        xds = []
        for j in range(SSD_HEADS_PER_GROUP):
            h = g * SSD_HEADS_PER_GROUP + j
            seg = acs[:, h:h + 1] - acs_t[h:h + 1, :]
            dec = jnp.exp(jnp.where(tril, seg, -jnp.inf))
            ms.append((cb * dec).astype(BF16))
            xds.append(jnp.where(lane_head == j, xd, 0.0).astype(BF16))
        y_diag = _dot(jnp.concatenate(ms, axis=1), jnp.concatenate(xds, axis=0))
        prev = state_ref[g]
        y_off = _dot(cm_b, prev.astype(BF16)) * decay_from_start[:, gs]
        state_ref[g] = prev * chunk_decay[:, gs] + _dot(bm_t, (xd * decay_to_end[:, gs]).astype(BF16))
        y = y_diag + y_off + xs * de_ref[:, gs]
        y = y * _silu(z_ref[:, gs])
        y = y * lax.rsqrt(jnp.mean(y * y, axis=-1, keepdims=True) + EPS) * nw_ref[:, gs]
        yn_ref[:, gs] = y.astype(BF16)

    out = _dot(yn_ref[...], wout_ref[...])
    o_ref[...] = xres_ref[...] + gate_ref[...] * out


def _ssd_mixer(z, xbc, dt, x, conv_w, conv_b, dt_bias, a_log, d_e, norm_w, expand, w_out, mod5, layer):
    q = SSD_CHUNK
    nc = SEQ // q
    tok = lambda w: pl.BlockSpec((q, w), lambda b, c: (b * nc + c, 0))
    full = lambda r, w: pl.BlockSpec((r, w), lambda b, c: (0, 0))
    return pl.pallas_call(
        _ssd_kernel,
        grid=(BATCH, nc),
        in_specs=[
            tok(SSD_D_INNER), tok(SSD_CONV_DIM), tok(SSD_DT_PAD), tok(D_MODEL),
            full(SSD_CONV, SSD_CONV_DIM), full(1, SSD_CONV_DIM), full(1, SSD_DT_PAD), full(1, SSD_DT_PAD),
            full(1, SSD_D_INNER), full(1, SSD_D_INNER), full(SSD_DT_PAD, SSD_D_INNER),
            full(SSD_D_INNER, D_MODEL),
            pl.BlockSpec((None, None, None, 1, D_MODEL), lambda b, c: (layer, b, 2, 0, 0)),
        ],
        out_specs=tok(D_MODEL),
        out_shape=jax.ShapeDtypeStruct((TOKENS, D_MODEL), F32),
        scratch_shapes=[
            pltpu.VMEM((SSD_GROUPS, SSD_STATE, SSD_GROUP_W), F32),
            pltpu.VMEM((q + CONV_HALO, SSD_CONV_DIM), F32),
            pltpu.VMEM((q, SSD_CONV_DIM), F32),
            pltpu.VMEM((q, SSD_D_INNER), BF16),
        ],
        compiler_params=pltpu.CompilerParams(
            dimension_semantics=("arbitrary", "arbitrary"), vmem_limit_bytes=VMEM_LIMIT),
        name="ssd_mixer",
    )(z, xbc, dt, x, conv_w, conv_b, dt_bias, a_log, d_e, norm_w, expand, w_out, mod5)


def _key_to_float(key):
    bits = jnp.where(key >= 0, key, key ^ jnp.int32(0x7FFFFFFF))
    return lax.bitcast_convert_type(bits, F32)


def _count(mask):
    return jnp.sum(jnp.where(mask, 1.0, 0.0), axis=-1, keepdims=True)


def _topk_mask(score, key_pos):
    rows = score.shape[0]
    k = float(TOPK)

    def vbody(_, carry):
        lo, hi = carry
        mid = (lo & hi) + ((lo ^ hi) >> 1)
        ok = _count(score >= _key_to_float(mid)) >= k
        return jnp.where(ok, mid, lo), jnp.where(ok, hi, mid)

    lo, _ = lax.fori_loop(0, 32, vbody, (jnp.full((rows, 1), KEY_LO, jnp.int32),
                                        jnp.full((rows, 1), KEY_HI, jnp.int32)))
    thr = _key_to_float(lo)
    gt = score > thr
    eq = score == thr
    need = k - _count(gt)

    def ibody(_, carry):
        lo, hi = carry
        mid = (lo + hi) >> 1
        ok = _count(eq & (key_pos <= mid)) >= need
        return jnp.where(ok, lo, mid), jnp.where(ok, mid, hi)

    n_keys = score.shape[1]
    _, cut = lax.fori_loop(0, max(1, (n_keys - 1).bit_length()) + 1, ibody,
                           (jnp.full((rows, 1), -1, jnp.int32), jnp.full((rows, 1), n_keys - 1, jnp.int32)))
    return gt | (eq & (key_pos <= cut))


def _dsa_kernel(q_ref, k_ref, v_ref, qi_ref, ki_ref, wi_ref, xres_ref, qn_ref, kn_ref, wout_ref, gate_ref,
                o_ref, kb_ref, vb_ref, kib_ref):
    i = pl.program_id(1)
    hd = ATT_HEAD_DIM

    @pl.when(i == 0)
    def _():
        for h in range(ATT_KV_HEADS):
            kh = k_ref[:, h * hd:(h + 1) * hd]
            kh = kh * lax.rsqrt(jnp.mean(kh * kh, axis=-1, keepdims=True) + EPS) * kn_ref[...]
            kb_ref[h] = kh.astype(BF16)
            vb_ref[h] = v_ref[:, h * hd:(h + 1) * hd].astype(BF16)
        kib_ref[...] = ki_ref[...].astype(BF16)

    kib = kib_ref[...]
    wi = wi_ref[...] * (IDX_HEADS ** -0.5)
    score = jnp.zeros((Q_BLOCK, SEQ), F32)
    for h in range(IDX_HEADS):
        qh = qi_ref[:, h * IDX_HEAD_DIM:(h + 1) * IDX_HEAD_DIM].astype(BF16)
        rel = jnp.maximum(_dot_nt(qh, kib) * (IDX_HEAD_DIM ** -0.5), 0.0)
        score = score + wi[:, h:h + 1] * rel
    key_pos = lax.broadcasted_iota(jnp.int32, (Q_BLOCK, SEQ), 1)
    q_pos = i * Q_BLOCK + lax.broadcasted_iota(jnp.int32, (Q_BLOCK, SEQ), 0)
    score = jnp.where(key_pos <= q_pos, score, -jnp.inf)
    bias = jnp.where(_topk_mask(score, key_pos), 0.0, -jnp.inf)
    bias4 = jnp.concatenate([bias] * ATT_Q_PER_KV, axis=0)

    out = jnp.zeros((Q_BLOCK, D_MODEL), F32)
    for h in range(ATT_KV_HEADS):
        qs = []
        for g in range(ATT_Q_PER_KV):
            col = (h * ATT_Q_PER_KV + g) * hd
            qh = q_ref[:, col:col + hd]
            qh = qh * lax.rsqrt(jnp.mean(qh * qh, axis=-1, keepdims=True) + EPS) * qn_ref[...]
            qs.append(qh.astype(BF16))
        s = _dot_nt(jnp.concatenate(qs, axis=0), kb_ref[h]) * (hd ** -0.5) + bias4
        m = jnp.max(s, axis=-1, keepdims=True)
        p = jnp.exp(s - m)
        p = p / jnp.sum(p, axis=-1, keepdims=True)
        o = _dot(p.astype(BF16), vb_ref[h])
        for g in range(ATT_Q_PER_KV):
            col = (h * ATT_Q_PER_KV + g) * hd
            out = out + _dot(o[g * Q_BLOCK:(g + 1) * Q_BLOCK, :].astype(BF16), wout_ref[col:col + hd, :])
    o_ref[...] = xres_ref[...] + gate_ref[...] * out


def _dsa_mixer(q, k, v, qi, ki, wi, x, q_norm, k_norm, w_out, mod5, layer):
    nb = SEQ // Q_BLOCK
    blk = lambda w: pl.BlockSpec((Q_BLOCK, w), lambda b, i: (b * nb + i, 0))
    seq = lambda w: pl.BlockSpec((SEQ, w), lambda b, i: (b, 0))
    full = lambda r, w: pl.BlockSpec((r, w), lambda b, i: (0, 0))
    return pl.pallas_call(
        _dsa_kernel,
        grid=(BATCH, nb),
        in_specs=[
            blk(DSA_Q), seq(DSA_KV), seq(DSA_KV), blk(DSA_QI), seq(IDX_HEAD_DIM), blk(IDX_HEADS), blk(D_MODEL),
            full(1, ATT_HEAD_DIM), full(1, ATT_HEAD_DIM), full(DSA_Q, D_MODEL),
            pl.BlockSpec((None, None, None, 1, D_MODEL), lambda b, i: (layer, b, 2, 0, 0)),
        ],
        out_specs=blk(D_MODEL),
        out_shape=jax.ShapeDtypeStruct((TOKENS, D_MODEL), F32),
        scratch_shapes=[
            pltpu.VMEM((ATT_KV_HEADS, SEQ, ATT_HEAD_DIM), BF16),
            pltpu.VMEM((ATT_KV_HEADS, SEQ, ATT_HEAD_DIM), BF16),
            pltpu.VMEM((SEQ, IDX_HEAD_DIM), BF16),
        ],
        compiler_params=pltpu.CompilerParams(
            dimension_semantics=("arbitrary", "arbitrary"), vmem_limit_bytes=VMEM_LIMIT),
        name="dsa_mixer",
    )(q, k, v, qi, ki, wi, x, q_norm, k_norm, w_out, mod5)


MOE_TM = 512


def _routing(logits):
    lane = lax.broadcasted_iota(jnp.int32, logits.shape, 1)
    neg = -jnp.inf
    big = jnp.int32(ROUTE_PAD)
    is_group = (lane >= MOE_EXPERTS) & (lane < MOE_EXPERTS + MOE_GROUPS)
    gl = jnp.where(is_group, logits, neg)
    g_max = jnp.max(gl, axis=-1, keepdims=True)
    g_idx = jnp.min(jnp.where(gl == g_max, lane - MOE_EXPERTS, big), axis=-1, keepdims=True)
    g_val = 1.0 / jnp.sum(jnp.exp(gl - g_max), axis=-1, keepdims=True)
    in_group = (lane < MOE_EXPERTS) & ((lane // MOE_EPG) == g_idx)
    el = jnp.where(in_group, logits, neg)
    m1 = jnp.max(el, axis=-1, keepdims=True)
    i1 = jnp.min(jnp.where(el == m1, lane, big), axis=-1, keepdims=True)
    el2 = jnp.where(lane == i1, neg, el)
    m2 = jnp.max(el2, axis=-1, keepdims=True)
    i2 = jnp.min(jnp.where(el2 == m2, lane, big), axis=-1, keepdims=True)
    r = jnp.exp(m2 - m1)
    w1 = g_val / (1.0 + r)
    w2 = g_val * r / (1.0 + r)
    return jnp.where(lane == i1, w1, jnp.where(lane == i2, w2, 0.0))


def _moe_kernel(x_ref, nw_ref, scale_ref, shift_ref, gate_ref, wr_ref, br_ref, w1_ref, w3_ref, w2_ref,
                o_ref, h_ref, comb_ref, acc_ref):
    e = pl.program_id(1)

    @pl.when(e == 0)
    def _():
        h = _norm_mod(x_ref[...], nw_ref[...], scale_ref[...], shift_ref[...]).astype(BF16)
        h_ref[...] = h
        comb_ref[...] = _routing(_dot(h, wr_ref[...]) + br_ref[...])
        acc_ref[...] = jnp.zeros_like(acc_ref)

    h = h_ref[...]
    hid = _silu(_dot(h, w1_ref[...])) * _dot(h, w3_ref[...])
    lane = lax.broadcasted_iota(jnp.int32, comb_ref.shape, 1)
    ce = jnp.sum(jnp.where(lane == e, comb_ref[...], 0.0), axis=-1, keepdims=True)
    acc_ref[...] += _dot((hid * ce).astype(BF16), w2_ref[...])

    @pl.when(e == MOE_EXPERTS - 1)
    def _():
        o_ref[...] = x_ref[...] + gate_ref[...] * acc_ref[...]


def _moe(x, nw, mod5, layer, w_route, b_route, w1, w3, w2):
    tm = MOE_TM
    tiles_per_batch = SEQ // tm
    modspec = lambda chunk: pl.BlockSpec((None, None, None, 1, D_MODEL),
                                         lambda i, e: (layer, i // tiles_per_batch, chunk, 0, 0))
    return pl.pallas_call(
        _moe_kernel,
        grid=(TOKENS // tm, MOE_EXPERTS),
        in_specs=[
            pl.BlockSpec((tm, D_MODEL), lambda i, e: (i, 0)),
            pl.BlockSpec((1, D_MODEL), lambda i, e: (0, 0)),
            modspec(4), modspec(3), modspec(5),
            pl.BlockSpec((D_MODEL, ROUTE_PAD), lambda i, e: (0, 0)),
            pl.BlockSpec((1, ROUTE_PAD), lambda i, e: (0, 0)),
            pl.BlockSpec((None, D_MODEL, MOE_HIDDEN), lambda i, e: (e, 0, 0)),
            pl.BlockSpec((None, D_MODEL, MOE_HIDDEN), lambda i, e: (e, 0, 0)),
            pl.BlockSpec((None, MOE_HIDDEN, D_MODEL), lambda i, e: (e, 0, 0)),
        ],
        out_specs=pl.BlockSpec((tm, D_MODEL), lambda i, e: (i, 0)),
        out_shape=jax.ShapeDtypeStruct((TOKENS, D_MODEL), F32),
        scratch_shapes=[
            pltpu.VMEM((tm, D_MODEL), BF16),
            pltpu.VMEM((tm, ROUTE_PAD), F32),
            pltpu.VMEM((tm, D_MODEL), F32),
        ],
        compiler_params=pltpu.CompilerParams(
            dimension_semantics=("arbitrary", "arbitrary"), vmem_limit_bytes=VMEM_LIMIT),
        name="hier_moe",
    )(x, nw, mod5, mod5, mod5, w_route, b_route, w1, w3, w2)


def _pad_cols(w, width):
    return jnp.pad(w, ((0, 0), (0, width - w.shape[1])))


def kernel(x, c, ada_w, ada_b, norm_mix, norm_ffn, ssd_w_in, ssd_conv_w, ssd_conv_b, ssd_dt_bias,
           ssd_a_log, ssd_d, ssd_norm, ssd_w_out, dsa_w_in, dsa_q_norm, dsa_k_norm, dsa_w_out,
           moe_w_group, moe_b_group, moe_w_expert, moe_b_expert, moe_w1, moe_w3, moe_w2):
    depth = ada_w.shape[0]
    xt = x.reshape(TOKENS, D_MODEL)
    mod = _modulation(c, ada_w, ada_b)
    mod5 = mod.reshape(depth, BATCH, 6, 1, D_MODEL)

    head_of_col = jnp.arange(SSD_D_INNER, dtype=jnp.int32) // SSD_HEAD_DIM
    expand = (jnp.arange(SSD_DT_PAD, dtype=jnp.int32)[:, None] == head_of_col[None, :]).astype(BF16)

    for i in range(depth):
        j = i // 2
        nw_mix = norm_mix[i].reshape(1, D_MODEL)
        if i % 2 == 0:
            w_in = _pad_cols(ssd_w_in[j], SSD_PROJ_PAD).astype(BF16)
            z, xbc, dt = _inproj(
                xt, nw_mix, mod5, i, 1, 0, w_in,
                ((0, SSD_D_INNER), (SSD_D_INNER, SSD_D_INNER + SSD_CONV_DIM),
                 (SSD_D_INNER + SSD_CONV_DIM, SSD_PROJ_PAD)))
            xt = _ssd_mixer(
                z, xbc, dt, xt, ssd_conv_w[j], ssd_conv_b[j].reshape(1, SSD_CONV_DIM),
                _pad_cols(ssd_dt_bias[j].reshape(1, SSD_HEADS), SSD_DT_PAD),
                _pad_cols(ssd_a_log[j].reshape(1, SSD_HEADS), SSD_DT_PAD),
                jnp.repeat(ssd_d[j], SSD_HEAD_DIM).reshape(1, SSD_D_INNER),
                ssd_norm[j].reshape(1, SSD_D_INNER), expand, ssd_w_out[j].astype(BF16), mod5, i)
        else:
            w = dsa_w_in[j]
            w_in = jnp.concatenate(
                [_pad_cols(w[:, :DSA_KI_START + IDX_HEAD_DIM], DSA_WI_START),
                 _pad_cols(w[:, DSA_KI_START + IDX_HEAD_DIM:], 128)], axis=1).astype(BF16)
            q, k, v, qi, ki, wi = _inproj(
                xt, nw_mix, mod5, i, 1, 0, w_in,
                ((0, DSA_Q), (DSA_Q, DSA_Q + DSA_KV), (DSA_Q + DSA_KV, DSA_Q + 2 * DSA_KV),
                 (DSA_Q + 2 * DSA_KV, DSA_KI_START), (DSA_KI_START, DSA_KI_START + IDX_HEAD_DIM),
                 (DSA_WI_START, DSA_WI_START + IDX_HEADS)))
            xt = _dsa_mixer(q, k, v, qi, ki, wi, xt, dsa_q_norm[j].reshape(1, ATT_HEAD_DIM),
                            dsa_k_norm[j].reshape(1, ATT_HEAD_DIM), dsa_w_out[j].astype(BF16), mod5, i)

        w_route = _pad_cols(jnp.concatenate([moe_w_expert[i], moe_w_group[i]], axis=1), ROUTE_PAD).astype(BF16)
        b_route = _pad_cols(jnp.concatenate([moe_b_expert[i], moe_b_group[i]]).reshape(1, -1), ROUTE_PAD)
        xt = _moe(xt, norm_ffn[i].reshape(1, D_MODEL), mod5, i, w_route, b_route,
                  moe_w1[i].astype(BF16), moe_w3[i].astype(BF16), moe_w2[i].astype(BF16))
    return xt.reshape(BATCH, SEQ, D_MODEL)
```

```python
import functools

import jax
import jax.numpy as jnp
from jax import lax
from jax.experimental import pallas as pl
from jax.experimental.pallas import tpu as pltpu

F32 = jnp.float32
BF16 = jnp.bfloat16

D_MODEL = 1024
BATCH = 8
SEQ = 2048
TOKENS = BATCH * SEQ
EPS = 1e-6

SSD_D_INNER = 2048
SSD_HEAD_DIM = 64
SSD_HEADS = 32
SSD_GROUPS = 8
SSD_HEADS_PER_GROUP = 4
SSD_STATE = 128
SSD_CONV = 4
SSD_CHUNK = 128
SSD_GN = SSD_GROUPS * SSD_STATE
SSD_CONV_DIM = SSD_D_INNER + 2 * SSD_GN
SSD_GROUP_W = SSD_HEADS_PER_GROUP * SSD_HEAD_DIM
SSD_DT_PAD = 128
SSD_PROJ_PAD = SSD_D_INNER + SSD_CONV_DIM + SSD_DT_PAD
CONV_HALO = 8

ATT_HEADS = 16
ATT_KV_HEADS = 4
ATT_Q_PER_KV = 4
ATT_HEAD_DIM = 64
IDX_HEADS = 8
IDX_HEAD_DIM = 64
TOPK = 256
Q_BLOCK = 128
DSA_Q = ATT_HEADS * ATT_HEAD_DIM
DSA_KV = ATT_KV_HEADS * ATT_HEAD_DIM
DSA_QI = IDX_HEADS * IDX_HEAD_DIM
DSA_KI_START = DSA_Q + 2 * DSA_KV + DSA_QI
DSA_WI_START = DSA_KI_START + 128
DSA_PROJ_PAD = DSA_WI_START + 128

MOE_GROUPS = 4
MOE_EPG = 4
MOE_EXPERTS = 16
MOE_HIDDEN = 256
ROUTE_PAD = 128

VMEM_LIMIT = 56 * 1024 * 1024


def _sigmoid(v):
    return 1.0 / (1.0 + jnp.exp(-v))


def _silu(v):
    return v * _sigmoid(v)


def _split3(a):
    hi = a.astype(BF16)
    r = a - hi.astype(F32)
    mid = r.astype(BF16)
    lo = (r - mid.astype(F32)).astype(BF16)
    return hi, mid, lo


def _dot(a, b):
    return jnp.dot(a, b, preferred_element_type=F32)


def _dot_nt(a, b):
    return lax.dot_general(a, b, (((1,), (1,)), ((), ())), preferred_element_type=F32)


def _dot3_exact_rhs(a, m):
    hi, mid, lo = _split3(a)
    return _dot(hi, m) + _dot(mid, m) + _dot(lo, m)


def _dot3_exact_lhs(m, a):
    hi, mid, lo = _split3(a)
    return _dot(m, hi) + _dot(m, mid) + _dot(m, lo)


def _norm_mod(x, nw, scale, shift):
    ms = jnp.mean(x * x, axis=-1, keepdims=True)
    return x * lax.rsqrt(ms + EPS) * nw * (1.0 + scale) + shift


MOD_TN = 1536


def _mod_kernel(c_ref, w_ref, b_ref, o_ref):
    cond = _silu(c_ref[...]).astype(BF16)
    o_ref[...] = _dot(cond, w_ref[...].astype(BF16)) + b_ref[...]


def _modulation(c, ada_w, ada_b):
    depth = ada_w.shape[0]
    n = ada_w.shape[2]
    return pl.pallas_call(
        _mod_kernel,
        grid=(depth, n // MOD_TN),
        in_specs=[
            pl.BlockSpec((BATCH, D_MODEL), lambda i, j: (0, 0)),
            pl.BlockSpec((None, D_MODEL, MOD_TN), lambda i, j: (i, 0, j)),
            pl.BlockSpec((None, 1, MOD_TN), lambda i, j: (i, 0, j)),
        ],
        out_specs=pl.BlockSpec((None, BATCH, MOD_TN), lambda i, j: (i, 0, j)),
        out_shape=jax.ShapeDtypeStruct((depth, BATCH, n), F32),
        compiler_params=pltpu.CompilerParams(
            dimension_semantics=("arbitrary", "arbitrary"), vmem_limit_bytes=VMEM_LIMIT),
        name="adaln_mod",
    )(c, ada_w, ada_b.reshape(depth, 1, n))


def _mod_spec(layer, chunk, rows_per_batch_tile):
    return pl.BlockSpec((None, None, None, 1, D_MODEL),
                        lambda i, *_: (layer, i // rows_per_batch_tile, chunk, 0, 0))


INPROJ_TM = 256


def _inproj_kernel(x_ref, nw_ref, scale_ref, shift_ref, w_ref, *o_refs, col_slices):
    h = _norm_mod(x_ref[...], nw_ref[...], scale_ref[...], shift_ref[...]).astype(BF16)
    for o_ref, (lo, hi) in zip(o_refs, col_slices):
        o_ref[...] = _dot(h, w_ref[:, lo:hi])


def _inproj(x, nw, mod5, layer, scale_chunk, shift_chunk, w, col_slices):
    tm = INPROJ_TM
    tiles_per_batch = SEQ // tm
    n_pad = w.shape[1]
    return pl.pallas_call(
        functools.partial(_inproj_kernel, col_slices=col_slices),
        grid=(TOKENS // tm,),
        in_specs=[
            pl.BlockSpec((tm, D_MODEL), lambda i: (i, 0)),
            pl.BlockSpec((1, D_MODEL), lambda i: (0, 0)),
            _mod_spec(layer, scale_chunk, tiles_per_batch),
            _mod_spec(layer, shift_chunk, tiles_per_batch),
            pl.BlockSpec((D_MODEL, n_pad), lambda i: (0, 0)),
        ],
        out_specs=[pl.BlockSpec((tm, hi - lo), lambda i: (i, 0)) for lo, hi in col_slices],
        out_shape=[jax.ShapeDtypeStruct((TOKENS, hi - lo), F32) for lo, hi in col_slices],
        compiler_params=pltpu.CompilerParams(
            dimension_semantics=("arbitrary",), vmem_limit_bytes=VMEM_LIMIT),
        name="norm_inproj",
    )(x, nw, mod5, mod5, w)


CONV_COLS = 512


def _ssd_kernel(z_ref, xbc_ref, dt_ref, xres_ref, cw_ref, cb_ref, dtb_ref, alog_ref, de_ref, nw_ref,
                e_ref, wout_ref, gate_ref, o_ref, state_ref, ext_ref, act_ref, yn_ref):
    q = SSD_CHUNK
    c = pl.program_id(1)

    @pl.when(c == 0)
    def _():
        state_ref[...] = jnp.zeros_like(state_ref)
        ext_ref[0:CONV_HALO, :] = jnp.zeros((CONV_HALO, SSD_CONV_DIM), F32)

    @pl.when(c > 0)
    def _():
        ext_ref[0:CONV_HALO, :] = ext_ref[q:q + CONV_HALO, :]

    ext_ref[CONV_HALO:CONV_HALO + q, :] = xbc_ref[...]

    for s in range(SSD_CONV_DIM // CONV_COLS):
        cs = slice(s * CONV_COLS, (s + 1) * CONV_COLS)
        acc = cb_ref[:, cs] + cw_ref[SSD_CONV - 1:SSD_CONV, cs] * ext_ref[CONV_HALO:CONV_HALO + q, cs]
        for k in range(SSD_CONV - 1):
            off = CONV_HALO - (SSD_CONV - 1) + k
            acc = acc + cw_ref[k:k + 1, cs] * ext_ref[off:off + q, cs]
        act_ref[:, cs] = _silu(acc)

    dt_raw = dt_ref[...] + dtb_ref[...]
    dt = jnp.maximum(dt_raw, 0.0) + jnp.log1p(jnp.exp(-jnp.abs(dt_raw)))
    a = dt * (-jnp.exp(alog_ref[...]))
    row = lax.broadcasted_iota(jnp.int32, (q, q), 0)
    col = lax.broadcasted_iota(jnp.int32, (q, q), 1)
    tril = row >= col
    acs = _dot3_exact_lhs(tril.astype(BF16), a)
    acs_t = acs.T
    expand = e_ref[...]
    acs_e = _dot3_exact_rhs(acs, expand)
    dt_e = _dot3_exact_rhs(dt, expand)
    tot_e = acs_e[q - 1:q, :]
    decay_from_start = jnp.exp(acs_e)
    decay_to_end = jnp.exp(tot_e - acs_e)
    chunk_decay = jnp.exp(tot_e)

    lane_head = lax.broadcasted_iota(jnp.int32, (q, SSD_GROUP_W), 1) // SSD_HEAD_DIM
    for g in range(SSD_GROUPS):
        xs = act_ref[:, g * SSD_GROUP_W:(g + 1) * SSD_GROUP_W]
        gs = slice(g * SSD_GROUP_W, (g + 1) * SSD_GROUP_W)
        bm = act_ref[:, SSD_D_INNER + g * SSD_STATE:SSD_D_INNER + (g + 1) * SSD_STATE]
        cm = act_ref[:, SSD_D_INNER + SSD_GN + g * SSD_STATE:SSD_D_INNER + SSD_GN + (g + 1) * SSD_STATE]
        bm_t = bm.T.astype(BF16)
        cm_b = cm.astype(BF16)
        cb = _dot(cm_b, bm_t)
        xd = xs * dt_e[:, gs]
        ms = []
        xds = []
        for j in range(SSD_HEADS_PER_GROUP):
            h = g * SSD_HEADS_PER_GROUP + j
            seg = acs[:, h:h + 1] - acs_t[h:h + 1, :]
            dec = jnp.exp(jnp.where(tril, seg, -jnp.inf))
            ms.append((cb * dec).astype(BF16))
            xds.append(jnp.where(lane_head == j, xd, 0.0).astype(BF16))
        y_diag = _dot(jnp.concatenate(ms, axis=1), jnp.concatenate(xds, axis=0))
        prev = state_ref[g]
        y_off = _dot(cm_b, prev.astype(BF16)) * decay_from_start[:, gs]
        state_ref[g] = prev * chunk_decay[:, gs] + _dot(bm_t, (xd * decay_to_end[:, gs]).astype(BF16))
        y = y_diag + y_off + xs * de_ref[:, gs]
        y = y * _silu(z_ref[:, gs])
        y = y * lax.rsqrt(jnp.mean(y * y, axis=-1, keepdims=True) + EPS) * nw_ref[:, gs]
        yn_ref[:, gs] = y.astype(BF16)

    out = _dot(yn_ref[...], wout_ref[...])
    o_ref[...] = xres_ref[...] + gate_ref[...] * out


def _ssd_mixer(z, xbc, dt, x, conv_w, conv_b, dt_bias, a_log, d_e, norm_w, expand, w_out, mod5, layer):
    q = SSD_CHUNK
    nc = SEQ // q
    tok = lambda w: pl.BlockSpec((q, w), lambda b, c: (b * nc + c, 0))
    full = lambda r, w: pl.BlockSpec((r, w), lambda b, c: (0, 0))
    return pl.pallas_call(
        _ssd_kernel,
        grid=(BATCH, nc),
        in_specs=[
            tok(SSD_D_INNER), tok(SSD_CONV_DIM), tok(SSD_DT_PAD), tok(D_MODEL),
            full(SSD_CONV, SSD_CONV_DIM), full(1, SSD_CONV_DIM), full(1, SSD_DT_PAD), full(1, SSD_DT_PAD),
            full(1, SSD_D_INNER), full(1, SSD_D_INNER), full(SSD_DT_PAD, SSD_D_INNER),
            full(SSD_D_INNER, D_MODEL),
            pl.BlockSpec((None, None, None, 1, D_MODEL), lambda b, c: (layer, b, 2, 0, 0)),
        ],
        out_specs=tok(D_MODEL),
        out_shape=jax.ShapeDtypeStruct((TOKENS, D_MODEL), F32),
        scratch_shapes=[
            pltpu.VMEM((SSD_GROUPS, SSD_STATE, SSD_GROUP_W), F32),
            pltpu.VMEM((q + CONV_HALO, SSD_CONV_DIM), F32),
            pltpu.VMEM((q, SSD_CONV_DIM), F32),
            pltpu.VMEM((q, SSD_D_INNER), BF16),
        ],
        compiler_params=pltpu.CompilerParams(
            dimension_semantics=("arbitrary", "arbitrary"), vmem_limit_bytes=VMEM_LIMIT),
        name="ssd_mixer",
    )(z, xbc, dt, x, conv_w, conv_b, dt_bias, a_log, d_e, norm_w, expand, w_out, mod5)


DSA_KEY_TILE = 256
DSA_CLASSES = 4
DSA_BLOCKS_PER_CLASS = (SEQ // Q_BLOCK) // DSA_CLASSES
N_BISECT = 12
F32_MIN = float(jnp.finfo(jnp.float32).min)
LOG2E = 1.4426950408889634


def _count(mask):
    return jnp.sum(jnp.where(mask, 1.0, 0.0), axis=-1, keepdims=True)


def _select_topk(score_ref, q_pos, n_keys):
    kf = float(TOPK)
    small = (q_pos + 1) <= TOPK
    sc = score_ref[...]
    hi0 = jnp.max(sc, axis=-1, keepdims=True)
    lo0 = jnp.min(jnp.where(sc == -jnp.inf, jnp.inf, sc), axis=-1, keepdims=True)

    def bisect(_, carry):
        lo, hi = carry
        mid = lo + 0.5 * (hi - lo)
        ok = _count(score_ref[...] >= mid) >= kf
        return jnp.where(ok, mid, lo), jnp.where(ok, hi, mid)

    _, hi = lax.fori_loop(0, N_BISECT, bisect, (lo0, hi0))

    v0 = jnp.max(jnp.where(sc <= hi, sc, -jnp.inf), axis=-1, keepdims=True)
    c0 = _count(sc >= v0)
    pend0 = jnp.where((c0 >= kf) | small, 0.0, 1.0)

    def walk_cond(carry):
        return (carry[2] > 0.0) & (carry[3] < n_keys)

    def walk(carry):
        v, pend, _, it = carry
        s = score_ref[...]
        v2 = jnp.max(jnp.where(s < v, s, -jnp.inf), axis=-1, keepdims=True)
        c2 = _count(s >= v2)
        v = jnp.where(pend > 0.0, v2, v)
        pend = jnp.where(c2 >= kf, 0.0, pend)
        return v, pend, jnp.max(pend), it + 1

    v, _, _, _ = lax.while_loop(walk_cond, walk, (v0, pend0, jnp.max(pend0), jnp.int32(0)))
    thr = jnp.where(small, F32_MIN, v)

    key_pos = lax.broadcasted_iota(jnp.int32, (Q_BLOCK, n_keys), 1)
    gt = sc > thr
    eq = sc == thr
    need = kf - _count(gt)
    tie = jnp.where(_count(eq) > need, 1.0, 0.0)

    def search_cut():
        def body(_, carry):
            lo, hi = carry
            mid = (lo + hi) >> 1
            s = score_ref[...]
            ok = _count((s == thr) & (key_pos <= mid)) >= need
            return jnp.where(ok, lo, mid), jnp.where(ok, mid, hi)

        init = (jnp.full((Q_BLOCK, 1), -1, jnp.int32), jnp.full((Q_BLOCK, 1), n_keys - 1, jnp.int32))
        return lax.fori_loop(0, (n_keys - 1).bit_length() + 1, body, init)[1]

    cut = lax.cond(jnp.max(tie) > 0.0, search_cut, lambda: jnp.full((Q_BLOCK, 1), n_keys - 1, jnp.int32))
    score_ref[...] = jnp.where(gt | (eq & (key_pos <= cut)), 0.0, -jnp.inf)


def _dsa_inproj_kernel(x_ref, nw_ref, scale_ref, shift_ref, w_ref, qn_ref, kn_ref,
                       q_ref, k_ref, v_ref, qi_ref, ki_ref, wi_ref):
    hd = ATT_HEAD_DIM
    h = _norm_mod(x_ref[...], nw_ref[...], scale_ref[...], shift_ref[...]).astype(BF16)

    def head_norm(t, w):
        return t * lax.rsqrt(jnp.mean(t * t, axis=-1, keepdims=True) + EPS) * w

    q = _dot(h, w_ref[:, 0:DSA_Q])
    qw = qn_ref[...] * (hd ** -0.5 * LOG2E)
    for n in range(ATT_HEADS):
        q_ref[n] = head_norm(q[:, n * hd:(n + 1) * hd], qw).astype(BF16)
    kv = _dot(h, w_ref[:, DSA_Q:DSA_Q + 2 * DSA_KV])
    for n in range(ATT_KV_HEADS):
        k_ref[n] = head_norm(kv[:, n * hd:(n + 1) * hd], kn_ref[...]).astype(BF16)
        v_ref[n] = kv[:, DSA_KV + n * hd:DSA_KV + (n + 1) * hd].astype(BF16)
    qi = _dot(h, w_ref[:, DSA_Q + 2 * DSA_KV:DSA_KI_START])
    for n in range(IDX_HEADS):
        qi_ref[n] = qi[:, n * IDX_HEAD_DIM:(n + 1) * IDX_HEAD_DIM].astype(BF16)
    ki_ref[...] = _dot(h, w_ref[:, DSA_KI_START:DSA_KI_START + IDX_HEAD_DIM]).astype(BF16)
    wi_ref[...] = _dot(h, w_ref[:, DSA_WI_START:DSA_WI_START + IDX_HEADS]) * ((IDX_HEADS * IDX_HEAD_DIM) ** -0.5)


def _dsa_inproj(x, nw, mod5, layer, w, q_norm, k_norm):
    tm = INPROJ_TM
    tiles_per_batch = SEQ // tm
    heads = lambda n: pl.BlockSpec((n, tm, ATT_HEAD_DIM), lambda i: (0, i, 0))
    return pl.pallas_call(
        _dsa_inproj_kernel,
        grid=(TOKENS // tm,),
        in_specs=[
            pl.BlockSpec((tm, D_MODEL), lambda i: (i, 0)),
            pl.BlockSpec((1, D_MODEL), lambda i: (0, 0)),
            _mod_spec(layer, 1, tiles_per_batch),
            _mod_spec(layer, 0, tiles_per_batch),
            pl.BlockSpec((D_MODEL, DSA_PROJ_PAD), lambda i: (0, 0)),
            pl.BlockSpec((1, ATT_HEAD_DIM), lambda i: (0, 0)),
            pl.BlockSpec((1, ATT_HEAD_DIM), lambda i: (0, 0)),
        ],
        out_specs=[heads(ATT_HEADS), heads(ATT_KV_HEADS), heads(ATT_KV_HEADS), heads(IDX_HEADS),
                   pl.BlockSpec((tm, IDX_HEAD_DIM), lambda i: (i, 0)),
                   pl.BlockSpec((tm, IDX_HEADS), lambda i: (i, 0))],
        out_shape=[jax.ShapeDtypeStruct((ATT_HEADS, TOKENS, ATT_HEAD_DIM), BF16),
                   jax.ShapeDtypeStruct((ATT_KV_HEADS, TOKENS, ATT_HEAD_DIM), BF16),
                   jax.ShapeDtypeStruct((ATT_KV_HEADS, TOKENS, ATT_HEAD_DIM), BF16),
                   jax.ShapeDtypeStruct((IDX_HEADS, TOKENS, IDX_HEAD_DIM), BF16),
                   jax.ShapeDtypeStruct((TOKENS, IDX_HEAD_DIM), BF16),
                   jax.ShapeDtypeStruct((TOKENS, IDX_HEADS), F32)],
        compiler_params=pltpu.CompilerParams(
            dimension_semantics=("arbitrary",), vmem_limit_bytes=VMEM_LIMIT),
        name="dsa_inproj",
    )(x, nw, mod5, mod5, w, q_norm, k_norm)


def _dsa_kernel(q_ref, k_ref, v_ref, qi_ref, ki_ref, wi_ref, xres_ref, wout_ref, gate_ref, o_ref,
                score_ref, ocat_ref, *, n_keys, first_block):
    hd = ATT_HEAD_DIM
    q_pos = (first_block + pl.program_id(1)) * Q_BLOCK + lax.broadcasted_iota(jnp.int32, (Q_BLOCK, 1), 0)

    wi = wi_ref[...]
    qi = qi_ref[...].reshape(IDX_HEADS * Q_BLOCK, IDX_HEAD_DIM)
    for kt in range(n_keys // DSA_KEY_TILE):
        ks = slice(kt * DSA_KEY_TILE, (kt + 1) * DSA_KEY_TILE)
        raw = _dot_nt(qi, ki_ref[ks, :])
        acc = jnp.zeros((Q_BLOCK, DSA_KEY_TILE), F32)
        for n in range(IDX_HEADS):
            acc = acc + wi[:, n:n + 1] * jnp.maximum(raw[n * Q_BLOCK:(n + 1) * Q_BLOCK, :], 0.0)
        key_pos = kt * DSA_KEY_TILE + lax.broadcasted_iota(jnp.int32, (Q_BLOCK, DSA_KEY_TILE), 1)
        score_ref[:, ks] = jnp.where(key_pos <= q_pos, acc, -jnp.inf)

    _select_topk(score_ref, q_pos, n_keys)
    bias = score_ref[...][None, :, :]

    for n in range(ATT_KV_HEADS):
        q4 = q_ref[n * ATT_Q_PER_KV:(n + 1) * ATT_Q_PER_KV].reshape(ATT_Q_PER_KV * Q_BLOCK, hd)
        s = _dot_nt(q4, k_ref[n]).reshape(ATT_Q_PER_KV, Q_BLOCK, n_keys) + bias
        p = jnp.exp2(s - jnp.max(s, axis=-1, keepdims=True))
        denom = jnp.sum(p, axis=-1, keepdims=True).reshape(ATT_Q_PER_KV * Q_BLOCK, 1)
        o = _dot(p.reshape(ATT_Q_PER_KV * Q_BLOCK, n_keys).astype(BF16), v_ref[n]) * (1.0 / denom)
        for g in range(ATT_Q_PER_KV):
            col = (n * ATT_Q_PER_KV + g) * hd
            ocat_ref[:, col:col + hd] = o[g * Q_BLOCK:(g + 1) * Q_BLOCK, :]
    out = _dot(ocat_ref[...].astype(BF16), wout_ref[...])
    o_ref[...] = xres_ref[...] + gate_ref[...] * out


def _dsa_mixer(q, k, v, qi, ki, wi, x, w_out, mod5, layer):
    nb = SEQ // Q_BLOCK
    k4 = k.reshape(ATT_KV_HEADS, BATCH, SEQ, ATT_HEAD_DIM)
    v4 = v.reshape(ATT_KV_HEADS, BATCH, SEQ, ATT_HEAD_DIM)
    ki3 = ki.reshape(BATCH, SEQ, IDX_HEAD_DIM)
    for cls in range(DSA_CLASSES):
        n_keys = (cls + 1) * (SEQ // DSA_CLASSES)
        first_block = cls * DSA_BLOCKS_PER_CLASS
        row = lambda b, i, fb=first_block: b * nb + fb + i
        heads = lambda n: pl.BlockSpec((n, Q_BLOCK, ATT_HEAD_DIM), lambda b, i: (0, row(b, i), 0))
        keys = pl.BlockSpec((ATT_KV_HEADS, None, n_keys, ATT_HEAD_DIM), lambda b, i: (0, b, 0, 0))
        x = pl.pallas_call(
            functools.partial(_dsa_kernel, n_keys=n_keys, first_block=first_block),
            grid=(BATCH, DSA_BLOCKS_PER_CLASS),
            in_specs=[
                heads(ATT_HEADS), keys, keys, heads(IDX_HEADS),
                pl.BlockSpec((None, n_keys, IDX_HEAD_DIM), lambda b, i: (b, 0, 0)),
                pl.BlockSpec((Q_BLOCK, IDX_HEADS), lambda b, i: (row(b, i), 0)),
                pl.BlockSpec((Q_BLOCK, D_MODEL), lambda b, i: (row(b, i), 0)),
                pl.BlockSpec((DSA_Q, D_MODEL), lambda b, i: (0, 0)),
                pl.BlockSpec((None, None, None, 1, D_MODEL), lambda b, i: (layer, b, 2, 0, 0)),
            ],
            out_specs=pl.BlockSpec((Q_BLOCK, D_MODEL), lambda b, i: (row(b, i), 0)),
            out_shape=jax.ShapeDtypeStruct((TOKENS, D_MODEL), F32),
            scratch_shapes=[
                pltpu.VMEM((Q_BLOCK, n_keys), F32),
                pltpu.VMEM((Q_BLOCK, D_MODEL), F32),
            ],
            input_output_aliases={6: 0},
            compiler_params=pltpu.CompilerParams(
                dimension_semantics=("arbitrary", "arbitrary"), vmem_limit_bytes=VMEM_LIMIT),
            name=f"dsa_mixer_c{cls}",
        )(q, k4, v4, qi, ki3, wi, x, w_out, mod5)
    return x


MOE_TM = 512


def _routing(logits):
    lane = lax.broadcasted_iota(jnp.int32, logits.shape, 1)
    neg = -jnp.inf
    big = jnp.int32(ROUTE_PAD)
    is_group = (lane >= MOE_EXPERTS) & (lane < MOE_EXPERTS + MOE_GROUPS)
    gl = jnp.where(is_group, logits, neg)
    g_max = jnp.max(gl, axis=-1, keepdims=True)
    g_idx = jnp.min(jnp.where(gl == g_max, lane - MOE_EXPERTS, big), axis=-1, keepdims=True)
    g_val = 1.0 / jnp.sum(jnp.exp(gl - g_max), axis=-1, keepdims=True)
    in_group = (lane < MOE_EXPERTS) & ((lane // MOE_EPG) == g_idx)
    el = jnp.where(in_group, logits, neg)
    m1 = jnp.max(el, axis=-1, keepdims=True)
    i1 = jnp.min(jnp.where(el == m1, lane, big), axis=-1, keepdims=True)
    el2 = jnp.where(lane == i1, neg, el)
    m2 = jnp.max(el2, axis=-1, keepdims=True)
    i2 = jnp.min(jnp.where(el2 == m2, lane, big), axis=-1, keepdims=True)
    r = jnp.exp(m2 - m1)
    w1 = g_val / (1.0 + r)
    w2 = g_val * r / (1.0 + r)
    return jnp.where(lane == i1, w1, jnp.where(lane == i2, w2, 0.0))


def _moe_kernel(x_ref, nw_ref, scale_ref, shift_ref, gate_ref, wr_ref, br_ref, w1_ref, w3_ref, w2_ref,
                o_ref, h_ref, comb_ref, acc_ref):
    e = pl.program_id(1)

    @pl.when(e == 0)
    def _():
        h = _norm_mod(x_ref[...], nw_ref[...], scale_ref[...], shift_ref[...]).astype(BF16)
        h_ref[...] = h
        comb_ref[...] = _routing(_dot(h, wr_ref[...]) + br_ref[...])
        acc_ref[...] = jnp.zeros_like(acc_ref)

    h = h_ref[...]
    hid = _silu(_dot(h, w1_ref[...])) * _dot(h, w3_ref[...])
    lane = lax.broadcasted_iota(jnp.int32, comb_ref.shape, 1)
    ce = jnp.sum(jnp.where(lane == e, comb_ref[...], 0.0), axis=-1, keepdims=True)
    acc_ref[...] += _dot((hid * ce).astype(BF16), w2_ref[...])

    @pl.when(e == MOE_EXPERTS - 1)
    def _():
        o_ref[...] = x_ref[...] + gate_ref[...] * acc_ref[...]


def _moe(x, nw, mod5, layer, w_route, b_route, w1, w3, w2):
    tm = MOE_TM
    tiles_per_batch = SEQ // tm
    modspec = lambda chunk: pl.BlockSpec((None, None, None, 1, D_MODEL),
                                         lambda i, e: (layer, i // tiles_per_batch, chunk, 0, 0))
    return pl.pallas_call(
        _moe_kernel,
        grid=(TOKENS // tm, MOE_EXPERTS),
        in_specs=[
            pl.BlockSpec((tm, D_MODEL), lambda i, e: (i, 0)),
            pl.BlockSpec((1, D_MODEL), lambda i, e: (0, 0)),
            modspec(4), modspec(3), modspec(5),
            pl.BlockSpec((D_MODEL, ROUTE_PAD), lambda i, e: (0, 0)),
            pl.BlockSpec((1, ROUTE_PAD), lambda i, e: (0, 0)),
            pl.BlockSpec((None, D_MODEL, MOE_HIDDEN), lambda i, e: (e, 0, 0)),
            pl.BlockSpec((None, D_MODEL, MOE_HIDDEN), lambda i, e: (e, 0, 0)),
            pl.BlockSpec((None, MOE_HIDDEN, D_MODEL), lambda i, e: (e, 0, 0)),
        ],
        out_specs=pl.BlockSpec((tm, D_MODEL), lambda i, e: (i, 0)),
        out_shape=jax.ShapeDtypeStruct((TOKENS, D_MODEL), F32),
        scratch_shapes=[
            pltpu.VMEM((tm, D_MODEL), BF16),
            pltpu.VMEM((tm, ROUTE_PAD), F32),
            pltpu.VMEM((tm, D_MODEL), F32),
        ],
        compiler_params=pltpu.CompilerParams(
            dimension_semantics=("arbitrary", "arbitrary"), vmem_limit_bytes=VMEM_LIMIT),
        name="hier_moe",
    )(x, nw, mod5, mod5, mod5, w_route, b_route, w1, w3, w2)


def _pad_cols(w, width):
    return jnp.pad(w, ((0, 0), (0, width - w.shape[1])))


def kernel(x, c, ada_w, ada_b, norm_mix, norm_ffn, ssd_w_in, ssd_conv_w, ssd_conv_b, ssd_dt_bias,
           ssd_a_log, ssd_d, ssd_norm, ssd_w_out, dsa_w_in, dsa_q_norm, dsa_k_norm, dsa_w_out,
           moe_w_group, moe_b_group, moe_w_expert, moe_b_expert, moe_w1, moe_w3, moe_w2):
    depth = ada_w.shape[0]
    xt = x.reshape(TOKENS, D_MODEL)
    mod = _modulation(c, ada_w, ada_b)
    mod5 = mod.reshape(depth, BATCH, 6, 1, D_MODEL)

    head_of_col = jnp.arange(SSD_D_INNER, dtype=jnp.int32) // SSD_HEAD_DIM
    expand = (jnp.arange(SSD_DT_PAD, dtype=jnp.int32)[:, None] == head_of_col[None, :]).astype(BF16)

    for i in range(depth):
        j = i // 2
        nw_mix = norm_mix[i].reshape(1, D_MODEL)
        if i % 2 == 0:
            w_in = _pad_cols(ssd_w_in[j], SSD_PROJ_PAD).astype(BF16)
            z, xbc, dt = _inproj(
                xt, nw_mix, mod5, i, 1, 0, w_in,
                ((0, SSD_D_INNER), (SSD_D_INNER, SSD_D_INNER + SSD_CONV_DIM),
                 (SSD_D_INNER + SSD_CONV_DIM, SSD_PROJ_PAD)))
            xt = _ssd_mixer(
                z, xbc, dt, xt, ssd_conv_w[j], ssd_conv_b[j].reshape(1, SSD_CONV_DIM),
                _pad_cols(ssd_dt_bias[j].reshape(1, SSD_HEADS), SSD_DT_PAD),
                _pad_cols(ssd_a_log[j].reshape(1, SSD_HEADS), SSD_DT_PAD),
                jnp.repeat(ssd_d[j], SSD_HEAD_DIM).reshape(1, SSD_D_INNER),
                ssd_norm[j].reshape(1, SSD_D_INNER), expand, ssd_w_out[j].astype(BF16), mod5, i)
        else:
            w = dsa_w_in[j]
            w_in = jnp.concatenate(
                [_pad_cols(w[:, :DSA_KI_START + IDX_HEAD_DIM], DSA_WI_START),
                 _pad_cols(w[:, DSA_KI_START + IDX_HEAD_DIM:], 128)], axis=1).astype(BF16)
            q, k, v, qi, ki, wi = _dsa_inproj(
                xt, nw_mix, mod5, i, w_in, dsa_q_norm[j].reshape(1, ATT_HEAD_DIM),
                dsa_k_norm[j].reshape(1, ATT_HEAD_DIM))
            xt = _dsa_mixer(q, k, v, qi, ki, wi, xt, dsa_w_out[j].astype(BF16), mod5, i)

        w_route = _pad_cols(jnp.concatenate([moe_w_expert[i], moe_w_group[i]], axis=1), ROUTE_PAD).astype(BF16)
        b_route = _pad_cols(jnp.concatenate([moe_b_expert[i], moe_b_group[i]]).reshape(1, -1), ROUTE_PAD)
        xt = _moe(xt, norm_ffn[i].reshape(1, D_MODEL), mod5, i, w_route, b_route,
                  moe_w1[i].astype(BF16), moe_w3[i].astype(BF16), moe_w2[i].astype(BF16))
    return xt.reshape(BATCH, SEQ, D_MODEL)
```

```python
import functools

import jax
import jax.numpy as jnp
from jax import lax
from jax.experimental import pallas as pl
from jax.experimental.pallas import tpu as pltpu

F32 = jnp.float32
BF16 = jnp.bfloat16

D_MODEL = 1024
BATCH = 8
SEQ = 2048
TOKENS = BATCH * SEQ
EPS = 1e-6

SSD_D_INNER = 2048
SSD_HEAD_DIM = 64
SSD_HEADS = 32
SSD_GROUPS = 8
SSD_HEADS_PER_GROUP = 4
SSD_STATE = 128
SSD_CONV = 4
SSD_CHUNK = 128
SSD_GN = SSD_GROUPS * SSD_STATE
SSD_CONV_DIM = SSD_D_INNER + 2 * SSD_GN
SSD_GROUP_W = SSD_HEADS_PER_GROUP * SSD_HEAD_DIM
SSD_DT_PAD = 128
SSD_PROJ_PAD = SSD_D_INNER + SSD_CONV_DIM + SSD_DT_PAD
CONV_HALO = 8

ATT_HEADS = 16
ATT_KV_HEADS = 4
ATT_Q_PER_KV = 4
ATT_HEAD_DIM = 64
IDX_HEADS = 8
IDX_HEAD_DIM = 64
TOPK = 256
Q_BLOCK = 128
DSA_Q = ATT_HEADS * ATT_HEAD_DIM
DSA_KV = ATT_KV_HEADS * ATT_HEAD_DIM
DSA_QI = IDX_HEADS * IDX_HEAD_DIM
DSA_KI_START = DSA_Q + 2 * DSA_KV + DSA_QI
DSA_WI_START = DSA_KI_START + 128
DSA_PROJ_PAD = DSA_WI_START + 128

MOE_GROUPS = 4
MOE_EPG = 4
MOE_EXPERTS = 16
MOE_HIDDEN = 256
ROUTE_PAD = 128

VMEM_LIMIT = 56 * 1024 * 1024


def _sigmoid(v):
    return 1.0 / (1.0 + jnp.exp(-v))


def _silu(v):
    return v * _sigmoid(v)


def _split3(a):
    hi = a.astype(BF16)
    r = a - hi.astype(F32)
    mid = r.astype(BF16)
    lo = (r - mid.astype(F32)).astype(BF16)
    return hi, mid, lo


def _dot(a, b):
    return jnp.dot(a, b, preferred_element_type=F32)


def _dot_nt(a, b):
    return lax.dot_general(a, b, (((1,), (1,)), ((), ())), preferred_element_type=F32)


def _dot3_exact_rhs(a, m):
    hi, mid, lo = _split3(a)
    return _dot(hi, m) + _dot(mid, m) + _dot(lo, m)


def _dot3_exact_lhs(m, a):
    hi, mid, lo = _split3(a)
    return _dot(m, hi) + _dot(m, mid) + _dot(m, lo)


def _norm_mod(x, nw, scale, shift):
    ms = jnp.mean(x * x, axis=-1, keepdims=True)
    return x * lax.rsqrt(ms + EPS) * nw * (1.0 + scale) + shift


MOD_TN = 1536


def _mod_kernel(c_ref, w_ref, b_ref, o_ref):
    cond = _silu(c_ref[...]).astype(BF16)
    o_ref[...] = _dot(cond, w_ref[...].astype(BF16)) + b_ref[...]


def _modulation(c, ada_w, ada_b):
    depth = ada_w.shape[0]
    n = ada_w.shape[2]
    return pl.pallas_call(
        _mod_kernel,
        grid=(depth, n // MOD_TN),
        in_specs=[
            pl.BlockSpec((BATCH, D_MODEL), lambda i, j: (0, 0)),
            pl.BlockSpec((None, D_MODEL, MOD_TN), lambda i, j: (i, 0, j)),
            pl.BlockSpec((None, 1, MOD_TN), lambda i, j: (i, 0, j)),
        ],
        out_specs=pl.BlockSpec((None, BATCH, MOD_TN), lambda i, j: (i, 0, j)),
        out_shape=jax.ShapeDtypeStruct((depth, BATCH, n), F32),
        compiler_params=pltpu.CompilerParams(
            dimension_semantics=("arbitrary", "arbitrary"), vmem_limit_bytes=VMEM_LIMIT),
        name="adaln_mod",
    )(c, ada_w, ada_b.reshape(depth, 1, n))


def _mod_spec(layer, chunk, rows_per_batch_tile):
    return pl.BlockSpec((None, None, None, 1, D_MODEL),
                        lambda i, *_: (layer, i // rows_per_batch_tile, chunk, 0, 0))


INPROJ_TM = 256


def _inproj_kernel(x_ref, nw_ref, scale_ref, shift_ref, w_ref, *o_refs, col_slices):
    h = _norm_mod(x_ref[...], nw_ref[...], scale_ref[...], shift_ref[...]).astype(BF16)
    for o_ref, (lo, hi) in zip(o_refs, col_slices):
        o_ref[...] = _dot(h, w_ref[:, lo:hi])


def _inproj(x, nw, mod5, layer, scale_chunk, shift_chunk, w, col_slices):
    tm = INPROJ_TM
    tiles_per_batch = SEQ // tm
    n_pad = w.shape[1]
    return pl.pallas_call(
        functools.partial(_inproj_kernel, col_slices=col_slices),
        grid=(TOKENS // tm,),
        in_specs=[
            pl.BlockSpec((tm, D_MODEL), lambda i: (i, 0)),
            pl.BlockSpec((1, D_MODEL), lambda i: (0, 0)),
            _mod_spec(layer, scale_chunk, tiles_per_batch),
            _mod_spec(layer, shift_chunk, tiles_per_batch),
            pl.BlockSpec((D_MODEL, n_pad), lambda i: (0, 0)),
        ],
        out_specs=[pl.BlockSpec((tm, hi - lo), lambda i: (i, 0)) for lo, hi in col_slices],
        out_shape=[jax.ShapeDtypeStruct((TOKENS, hi - lo), F32) for lo, hi in col_slices],
        compiler_params=pltpu.CompilerParams(
            dimension_semantics=("arbitrary",), vmem_limit_bytes=VMEM_LIMIT),
        name="norm_inproj",
    )(x, nw, mod5, mod5, w)


CONV_COLS = 512


def _ssd_kernel(z_ref, xbc_ref, dt_ref, xres_ref, cw_ref, cb_ref, dtb_ref, alog_ref, de_ref, nw_ref,
                e_ref, wout_ref, gate_ref, o_ref, state_ref, ext_ref, act_ref, yn_ref):
    q = SSD_CHUNK
    c = pl.program_id(1)

    @pl.when(c == 0)
    def _():
        state_ref[...] = jnp.zeros_like(state_ref)
        ext_ref[0:CONV_HALO, :] = jnp.zeros((CONV_HALO, SSD_CONV_DIM), F32)

    @pl.when(c > 0)
    def _():
        ext_ref[0:CONV_HALO, :] = ext_ref[q:q + CONV_HALO, :]

    ext_ref[CONV_HALO:CONV_HALO + q, :] = xbc_ref[...]

    for s in range(SSD_CONV_DIM // CONV_COLS):
        cs = slice(s * CONV_COLS, (s + 1) * CONV_COLS)
        acc = cb_ref[:, cs] + cw_ref[SSD_CONV - 1:SSD_CONV, cs] * ext_ref[CONV_HALO:CONV_HALO + q, cs]
        for k in range(SSD_CONV - 1):
            off = CONV_HALO - (SSD_CONV - 1) + k
            acc = acc + cw_ref[k:k + 1, cs] * ext_ref[off:off + q, cs]
        act_ref[:, cs] = _silu(acc)

    dt_raw = dt_ref[...] + dtb_ref[...]
    dt = jnp.maximum(dt_raw, 0.0) + jnp.log1p(jnp.exp(-jnp.abs(dt_raw)))
    a = dt * (-jnp.exp(alog_ref[...]))
    row = lax.broadcasted_iota(jnp.int32, (q, q), 0)
    col = lax.broadcasted_iota(jnp.int32, (q, q), 1)
    tril = row >= col
    acs = _dot3_exact_lhs(tril.astype(BF16), a)
    acs_t = acs.T
    expand = e_ref[...]
    acs_e = _dot3_exact_rhs(acs, expand)
    dt_e = _dot3_exact_rhs(dt, expand)
    tot_e = acs_e[q - 1:q, :]
    decay_from_start = jnp.exp(acs_e)
    decay_to_end = jnp.exp(tot_e - acs_e)
    chunk_decay = jnp.exp(tot_e)

    lane_head = lax.broadcasted_iota(jnp.int32, (q, SSD_GROUP_W), 1) // SSD_HEAD_DIM
    for g in range(SSD_GROUPS):
        xs = act_ref[:, g * SSD_GROUP_W:(g + 1) * SSD_GROUP_W]
        gs = slice(g * SSD_GROUP_W, (g + 1) * SSD_GROUP_W)
        bm = act_ref[:, SSD_D_INNER + g * SSD_STATE:SSD_D_INNER + (g + 1) * SSD_STATE]
        cm = act_ref[:, SSD_D_INNER + SSD_GN + g * SSD_STATE:SSD_D_INNER + SSD_GN + (g + 1) * SSD_STATE]
        bm_t = bm.T.astype(BF16)
        cm_b = cm.astype(BF16)
        cb = _dot(cm_b, bm_t)
        xd = xs * dt_e[:, gs]
        ms = []
        xds = []
        for j in range(SSD_HEADS_PER_GROUP):
            h = g * SSD_HEADS_PER_GROUP + j
            seg = acs[:, h:h + 1] - acs_t[h:h + 1, :]
            dec = jnp.exp(jnp.where(tril, seg, -jnp.inf))
            ms.append((cb * dec).astype(BF16))
            xds.append(jnp.where(lane_head == j, xd, 0.0).astype(BF16))
        y_diag = _dot(jnp.concatenate(ms, axis=1), jnp.concatenate(xds, axis=0))
        prev = state_ref[g]
        y_off = _dot(cm_b, prev.astype(BF16)) * decay_from_start[:, gs]
        state_ref[g] = prev * chunk_decay[:, gs] + _dot(bm_t, (xd * decay_to_end[:, gs]).astype(BF16))
        y = y_diag + y_off + xs * de_ref[:, gs]
        y = y * _silu(z_ref[:, gs])
        y = y * lax.rsqrt(jnp.mean(y * y, axis=-1, keepdims=True) + EPS) * nw_ref[:, gs]
        yn_ref[:, gs] = y.astype(BF16)

    out = _dot(yn_ref[...], wout_ref[...])
    o_ref[...] = xres_ref[...] + gate_ref[...] * out


def _ssd_mixer(z, xbc, dt, x, conv_w, conv_b, dt_bias, a_log, d_e, norm_w, expand, w_out, mod5, layer):
    q = SSD_CHUNK
    nc = SEQ // q
    tok = lambda w: pl.BlockSpec((q, w), lambda b, c: (b * nc + c, 0))
    full = lambda r, w: pl.BlockSpec((r, w), lambda b, c: (0, 0))
    return pl.pallas_call(
        _ssd_kernel,
        grid=(BATCH, nc),
        in_specs=[
            tok(SSD_D_INNER), tok(SSD_CONV_DIM), tok(SSD_DT_PAD), tok(D_MODEL),
            full(SSD_CONV, SSD_CONV_DIM), full(1, SSD_CONV_DIM), full(1, SSD_DT_PAD), full(1, SSD_DT_PAD),
            full(1, SSD_D_INNER), full(1, SSD_D_INNER), full(SSD_DT_PAD, SSD_D_INNER),
            full(SSD_D_INNER, D_MODEL),
            pl.BlockSpec((None, None, None, 1, D_MODEL), lambda b, c: (layer, b, 2, 0, 0)),
        ],
        out_specs=tok(D_MODEL),
        out_shape=jax.ShapeDtypeStruct((TOKENS, D_MODEL), F32),
        scratch_shapes=[
            pltpu.VMEM((SSD_GROUPS, SSD_STATE, SSD_GROUP_W), F32),
            pltpu.VMEM((q + CONV_HALO, SSD_CONV_DIM), F32),
            pltpu.VMEM((q, SSD_CONV_DIM), F32),
            pltpu.VMEM((q, SSD_D_INNER), BF16),
        ],
        compiler_params=pltpu.CompilerParams(
            dimension_semantics=("arbitrary", "arbitrary"), vmem_limit_bytes=VMEM_LIMIT),
        name="ssd_mixer",
    )(z, xbc, dt, x, conv_w, conv_b, dt_bias, a_log, d_e, norm_w, expand, w_out, mod5)


DSA_KEY_TILE = 256
DSA_CLASSES = 4
DSA_BLOCKS_PER_CLASS = (SEQ // Q_BLOCK) // DSA_CLASSES
N_BISECT = 12
F32_MIN = float(jnp.finfo(jnp.float32).min)
LOG2E = 1.4426950408889634


def _count(mask):
    return jnp.sum(jnp.where(mask, 1.0, 0.0), axis=-1, keepdims=True)


def _select_topk(score_ref, q_pos, n_keys):
    kf = float(TOPK)
    small = (q_pos + 1) <= TOPK
    sc = score_ref[...]
    hi0 = jnp.max(sc, axis=-1, keepdims=True)
    lo0 = jnp.min(jnp.where(sc == -jnp.inf, jnp.inf, sc), axis=-1, keepdims=True)

    def bisect(_, carry):
        lo, hi = carry
        mid = lo + 0.5 * (hi - lo)
        ok = _count(score_ref[...] >= mid) >= kf
        return jnp.where(ok, mid, lo), jnp.where(ok, hi, mid)

    _, hi = lax.fori_loop(0, N_BISECT, bisect, (lo0, hi0))

    v0 = jnp.max(jnp.where(sc <= hi, sc, -jnp.inf), axis=-1, keepdims=True)
    c0 = _count(sc >= v0)
    pend0 = jnp.where((c0 >= kf) | small, 0.0, 1.0)

    def walk_cond(carry):
        return (carry[2] > 0.0) & (carry[3] < n_keys)

    def walk(carry):
        v, pend, _, it = carry
        s = score_ref[...]
        v2 = jnp.max(jnp.where(s < v, s, -jnp.inf), axis=-1, keepdims=True)
        c2 = _count(s >= v2)
        v = jnp.where(pend > 0.0, v2, v)
        pend = jnp.where(c2 >= kf, 0.0, pend)
        return v, pend, jnp.max(pend), it + 1

    v, _, _, _ = lax.while_loop(walk_cond, walk, (v0, pend0, jnp.max(pend0), jnp.int32(0)))
    thr = jnp.where(small, F32_MIN, v)

    key_pos = lax.broadcasted_iota(jnp.int32, (Q_BLOCK, n_keys), 1)
    gt = sc > thr
    eq = sc == thr
    need = kf - _count(gt)
    tie = jnp.where(_count(eq) > need, 1.0, 0.0)

    def search_cut():
        def body(_, carry):
            lo, hi = carry
            mid = (lo + hi) >> 1
            s = score_ref[...]
            ok = _count((s == thr) & (key_pos <= mid)) >= need
            return jnp.where(ok, lo, mid), jnp.where(ok, mid, hi)

        init = (jnp.full((Q_BLOCK, 1), -1, jnp.int32), jnp.full((Q_BLOCK, 1), n_keys - 1, jnp.int32))
        return lax.fori_loop(0, (n_keys - 1).bit_length() + 1, body, init)[1]

    cut = lax.cond(jnp.max(tie) > 0.0, search_cut, lambda: jnp.full((Q_BLOCK, 1), n_keys - 1, jnp.int32))
    score_ref[...] = jnp.where(gt | (eq & (key_pos <= cut)), 0.0, -jnp.inf)


def _dsa_inproj_kernel(x_ref, nw_ref, scale_ref, shift_ref, w_ref, qn_ref, kn_ref, seg_ref, segt_ref,
                       q_ref, k_ref, v_ref, qi_ref, ki_ref, wi_ref):
    hd = ATT_HEAD_DIM
    h = _norm_mod(x_ref[...], nw_ref[...], scale_ref[...], shift_ref[...]).astype(BF16)

    def head_norm(t, w):
        width = t.shape[1]
        ss = _dot((t * t).astype(BF16), seg_ref[0:width, :])
        r = lax.rsqrt(ss * (1.0 / hd) + EPS)
        r_hi = r.astype(BF16)
        r_lo = (r - r_hi.astype(F32)).astype(BF16)
        return t * (_dot(r_hi, segt_ref[:, 0:width]) + _dot(r_lo, segt_ref[:, 0:width])) * w

    q = head_norm(_dot(h, w_ref[:, 0:DSA_Q]), qn_ref[...] * (hd ** -0.5 * LOG2E))
    for n in range(ATT_HEADS):
        q_ref[n] = q[:, n * hd:(n + 1) * hd].astype(BF16)
    kv = _dot(h, w_ref[:, DSA_Q:DSA_Q + 2 * DSA_KV])
    k = head_norm(kv[:, 0:DSA_KV], kn_ref[...])
    for n in range(ATT_KV_HEADS):
        k_ref[n] = k[:, n * hd:(n + 1) * hd].astype(BF16)
        v_ref[n] = kv[:, DSA_KV + n * hd:DSA_KV + (n + 1) * hd].astype(BF16)
    qi = _dot(h, w_ref[:, DSA_Q + 2 * DSA_KV:DSA_KI_START])
    for n in range(IDX_HEADS):
        qi_ref[n] = qi[:, n * IDX_HEAD_DIM:(n + 1) * IDX_HEAD_DIM].astype(BF16)
    ki_ref[...] = _dot(h, w_ref[:, DSA_KI_START:DSA_KI_START + IDX_HEAD_DIM]).astype(BF16)
    wi_ref[...] = _dot(h, w_ref[:, DSA_WI_START:DSA_WI_START + IDX_HEADS]) * ((IDX_HEADS * IDX_HEAD_DIM) ** -0.5)


def _dsa_inproj(x, nw, mod5, layer, w, q_norm, k_norm):
    tm = INPROJ_TM
    tiles_per_batch = SEQ // tm
    heads = lambda n: pl.BlockSpec((n, tm, ATT_HEAD_DIM), lambda i: (0, i, 0))
    head_of = jnp.arange(DSA_Q, dtype=jnp.int32) // ATT_HEAD_DIM
    seg = (head_of[:, None] == jnp.arange(128, dtype=jnp.int32)[None, :]).astype(BF16)
    q_norm = jnp.tile(q_norm, (1, ATT_HEADS))
    k_norm = jnp.tile(k_norm, (1, ATT_KV_HEADS))
    return pl.pallas_call(
        _dsa_inproj_kernel,
        grid=(TOKENS // tm,),
        in_specs=[
            pl.BlockSpec((tm, D_MODEL), lambda i: (i, 0)),
            pl.BlockSpec((1, D_MODEL), lambda i: (0, 0)),
            _mod_spec(layer, 1, tiles_per_batch),
            _mod_spec(layer, 0, tiles_per_batch),
            pl.BlockSpec((D_MODEL, DSA_PROJ_PAD), lambda i: (0, 0)),
            pl.BlockSpec((1, DSA_Q), lambda i: (0, 0)),
            pl.BlockSpec((1, DSA_KV), lambda i: (0, 0)),
            pl.BlockSpec((DSA_Q, 128), lambda i: (0, 0)),
            pl.BlockSpec((128, DSA_Q), lambda i: (0, 0)),
        ],
        out_specs=[heads(ATT_HEADS), heads(ATT_KV_HEADS), heads(ATT_KV_HEADS), heads(IDX_HEADS),
                   pl.BlockSpec((tm, IDX_HEAD_DIM), lambda i: (i, 0)),
                   pl.BlockSpec((tm, IDX_HEADS), lambda i: (i, 0))],
        out_shape=[jax.ShapeDtypeStruct((ATT_HEADS, TOKENS, ATT_HEAD_DIM), BF16),
                   jax.ShapeDtypeStruct((ATT_KV_HEADS, TOKENS, ATT_HEAD_DIM), BF16),
                   jax.ShapeDtypeStruct((ATT_KV_HEADS, TOKENS, ATT_HEAD_DIM), BF16),
                   jax.ShapeDtypeStruct((IDX_HEADS, TOKENS, IDX_HEAD_DIM), BF16),
                   jax.ShapeDtypeStruct((TOKENS, IDX_HEAD_DIM), BF16),
                   jax.ShapeDtypeStruct((TOKENS, IDX_HEADS), F32)],
        compiler_params=pltpu.CompilerParams(
            dimension_semantics=("arbitrary",), vmem_limit_bytes=VMEM_LIMIT),
        name="dsa_inproj",
    )(x, nw, mod5, mod5, w, q_norm, k_norm, seg, seg.T)


def _dsa_kernel(q_ref, k_ref, v_ref, qi_ref, ki_ref, wi_ref, xres_ref, wout_ref, gate_ref, o_ref,
                score_ref, ocat_ref, *, n_keys, first_block):
    hd = ATT_HEAD_DIM
    q_pos = (first_block + pl.program_id(1)) * Q_BLOCK + lax.broadcasted_iota(jnp.int32, (Q_BLOCK, 1), 0)

    wi = wi_ref[...]
    qi = qi_ref[...].reshape(IDX_HEADS * Q_BLOCK, IDX_HEAD_DIM)
    for kt in range(n_keys // DSA_KEY_TILE):
        ks = slice(kt * DSA_KEY_TILE, (kt + 1) * DSA_KEY_TILE)
        raw = _dot_nt(qi, ki_ref[ks, :])
        acc = jnp.zeros((Q_BLOCK, DSA_KEY_TILE), F32)
        for n in range(IDX_HEADS):
            acc = acc + wi[:, n:n + 1] * jnp.maximum(raw[n * Q_BLOCK:(n + 1) * Q_BLOCK, :], 0.0)
        key_pos = kt * DSA_KEY_TILE + lax.broadcasted_iota(jnp.int32, (Q_BLOCK, DSA_KEY_TILE), 1)
        score_ref[:, ks] = jnp.where(key_pos <= q_pos, acc, -jnp.inf)

    _select_topk(score_ref, q_pos, n_keys)
    bias = score_ref[...][None, :, :]

    for n in range(ATT_KV_HEADS):
        q4 = q_ref[n * ATT_Q_PER_KV:(n + 1) * ATT_Q_PER_KV].reshape(ATT_Q_PER_KV * Q_BLOCK, hd)
        s = _dot_nt(q4, k_ref[n]).reshape(ATT_Q_PER_KV, Q_BLOCK, n_keys) + bias
        p = jnp.exp2(s - jnp.max(s, axis=-1, keepdims=True))
        denom = jnp.sum(p, axis=-1, keepdims=True).reshape(ATT_Q_PER_KV * Q_BLOCK, 1)
        o = _dot(p.reshape(ATT_Q_PER_KV * Q_BLOCK, n_keys).astype(BF16), v_ref[n]) * (1.0 / denom)
        for g in range(ATT_Q_PER_KV):
            col = (n * ATT_Q_PER_KV + g) * hd
            ocat_ref[:, col:col + hd] = o[g * Q_BLOCK:(g + 1) * Q_BLOCK, :]
    out = _dot(ocat_ref[...].astype(BF16), wout_ref[...])
    o_ref[...] = xres_ref[...] + gate_ref[...] * out


def _dsa_mixer(q, k, v, qi, ki, wi, x, w_out, mod5, layer):
    nb = SEQ // Q_BLOCK
    k4 = k.reshape(ATT_KV_HEADS, BATCH, SEQ, ATT_HEAD_DIM)
    v4 = v.reshape(ATT_KV_HEADS, BATCH, SEQ, ATT_HEAD_DIM)
    ki3 = ki.reshape(BATCH, SEQ, IDX_HEAD_DIM)
    for cls in range(DSA_CLASSES):
        n_keys = (cls + 1) * (SEQ // DSA_CLASSES)
        first_block = cls * DSA_BLOCKS_PER_CLASS
        row = lambda b, i, fb=first_block: b * nb + fb + i
        heads = lambda n: pl.BlockSpec((n, Q_BLOCK, ATT_HEAD_DIM), lambda b, i: (0, row(b, i), 0))
        keys = pl.BlockSpec((ATT_KV_HEADS, None, n_keys, ATT_HEAD_DIM), lambda b, i: (0, b, 0, 0))
        x = pl.pallas_call(
            functools.partial(_dsa_kernel, n_keys=n_keys, first_block=first_block),
            grid=(BATCH, DSA_BLOCKS_PER_CLASS),
            in_specs=[
                heads(ATT_HEADS), keys, keys, heads(IDX_HEADS),
                pl.BlockSpec((None, n_keys, IDX_HEAD_DIM), lambda b, i: (b, 0, 0)),
                pl.BlockSpec((Q_BLOCK, IDX_HEADS), lambda b, i: (row(b, i), 0)),
                pl.BlockSpec((Q_BLOCK, D_MODEL), lambda b, i: (row(b, i), 0)),
                pl.BlockSpec((DSA_Q, D_MODEL), lambda b, i: (0, 0)),
                pl.BlockSpec((None, None, None, 1, D_MODEL), lambda b, i: (layer, b, 2, 0, 0)),
            ],
            out_specs=pl.BlockSpec((Q_BLOCK, D_MODEL), lambda b, i: (row(b, i), 0)),
            out_shape=jax.ShapeDtypeStruct((TOKENS, D_MODEL), F32),
            scratch_shapes=[
                pltpu.VMEM((Q_BLOCK, n_keys), F32),
                pltpu.VMEM((Q_BLOCK, D_MODEL), F32),
            ],
            input_output_aliases={6: 0},
            compiler_params=pltpu.CompilerParams(
                dimension_semantics=("arbitrary", "arbitrary"), vmem_limit_bytes=VMEM_LIMIT),
            name=f"dsa_mixer_c{cls}",
        )(q, k4, v4, qi, ki3, wi, x, w_out, mod5)
    return x


MOE_TM = 512


def _routing(logits):
    lane = lax.broadcasted_iota(jnp.int32, logits.shape, 1)
    neg = -jnp.inf
    big = jnp.int32(ROUTE_PAD)
    is_group = (lane >= MOE_EXPERTS) & (lane < MOE_EXPERTS + MOE_GROUPS)
    gl = jnp.where(is_group, logits, neg)
    g_max = jnp.max(gl, axis=-1, keepdims=True)
    g_idx = jnp.min(jnp.where(gl == g_max, lane - MOE_EXPERTS, big), axis=-1, keepdims=True)
    g_val = 1.0 / jnp.sum(jnp.exp(gl - g_max), axis=-1, keepdims=True)
    in_group = (lane < MOE_EXPERTS) & ((lane // MOE_EPG) == g_idx)
    el = jnp.where(in_group, logits, neg)
    m1 = jnp.max(el, axis=-1, keepdims=True)
    i1 = jnp.min(jnp.where(el == m1, lane, big), axis=-1, keepdims=True)
    el2 = jnp.where(lane == i1, neg, el)
    m2 = jnp.max(el2, axis=-1, keepdims=True)
    i2 = jnp.min(jnp.where(el2 == m2, lane, big), axis=-1, keepdims=True)
    r = jnp.exp(m2 - m1)
    w1 = g_val / (1.0 + r)
    w2 = g_val * r / (1.0 + r)
    return jnp.where(lane == i1, w1, jnp.where(lane == i2, w2, 0.0))


MOE_GROUP_HIDDEN = MOE_EPG * MOE_HIDDEN


def _moe_kernel(x_ref, nw_ref, scale_ref, shift_ref, gate_ref, wr_ref, br_ref, ex_ref, w1_ref, w3_ref, w2_ref,
                o_ref, h_ref, comb_ref, acc_ref):
    g = pl.program_id(1)

    @pl.when(g == 0)
    def _():
        h = _norm_mod(x_ref[...], nw_ref[...], scale_ref[...], shift_ref[...]).astype(BF16)
        h_ref[...] = h
        comb_ref[...] = _routing(_dot(h, wr_ref[...]) + br_ref[...])

    h = h_ref[...]
    hid = _silu(_dot(h, w1_ref[...])) * _dot(h, w3_ref[...])
    comb = comb_ref[...]
    comb_hi = comb.astype(BF16)
    comb_lo = (comb - comb_hi.astype(F32)).astype(BF16)
    ce = _dot(comb_hi, ex_ref[...]) + _dot(comb_lo, ex_ref[...])
    part = _dot((hid * ce).astype(BF16), w2_ref[...])

    @pl.when(g == 0)
    def _():
        acc_ref[...] = part

    @pl.when(g > 0)
    def _():
        acc_ref[...] += part

    @pl.when(g == MOE_GROUPS - 1)
    def _():
        o_ref[...] = x_ref[...] + gate_ref[...] * acc_ref[...]


def _moe(x, nw, mod5, layer, w_route, b_route, w1, w3, w2):
    tm = MOE_TM
    tiles_per_batch = SEQ // tm
    modspec = lambda chunk: pl.BlockSpec((None, None, None, 1, D_MODEL),
                                         lambda i, g: (layer, i // tiles_per_batch, chunk, 0, 0))
    lane = jnp.arange(ROUTE_PAD, dtype=jnp.int32)[None, :, None]
    unit = jnp.arange(MOE_GROUP_HIDDEN, dtype=jnp.int32)[None, None, :]
    grp = jnp.arange(MOE_GROUPS, dtype=jnp.int32)[:, None, None]
    spread = (lane == grp * MOE_EPG + unit // MOE_HIDDEN).astype(BF16)
    return pl.pallas_call(
        _moe_kernel,
        grid=(TOKENS // tm, MOE_GROUPS),
        in_specs=[
            pl.BlockSpec((tm, D_MODEL), lambda i, g: (i, 0)),
            pl.BlockSpec((1, D_MODEL), lambda i, g: (0, 0)),
            modspec(4), modspec(3), modspec(5),
            pl.BlockSpec((D_MODEL, ROUTE_PAD), lambda i, g: (0, 0)),
            pl.BlockSpec((1, ROUTE_PAD), lambda i, g: (0, 0)),
            pl.BlockSpec((None, ROUTE_PAD, MOE_GROUP_HIDDEN), lambda i, g: (g, 0, 0)),
            pl.BlockSpec((None, D_MODEL, MOE_GROUP_HIDDEN), lambda i, g: (g, 0, 0)),
            pl.BlockSpec((None, D_MODEL, MOE_GROUP_HIDDEN), lambda i, g: (g, 0, 0)),
            pl.BlockSpec((None, MOE_GROUP_HIDDEN, D_MODEL), lambda i, g: (g, 0, 0)),
        ],
        out_specs=pl.BlockSpec((tm, D_MODEL), lambda i, g: (i, 0)),
        out_shape=jax.ShapeDtypeStruct((TOKENS, D_MODEL), F32),
        scratch_shapes=[
            pltpu.VMEM((tm, D_MODEL), BF16),
            pltpu.VMEM((tm, ROUTE_PAD), F32),
            pltpu.VMEM((tm, D_MODEL), F32),
        ],
        compiler_params=pltpu.CompilerParams(
            dimension_semantics=("arbitrary", "arbitrary"), vmem_limit_bytes=VMEM_LIMIT),
        name="hier_moe",
    )(x, nw, mod5, mod5, mod5, w_route, b_route, spread, w1, w3, w2)


def _pad_cols(w, width):
    return jnp.pad(w, ((0, 0), (0, width - w.shape[1])))


def kernel(x, c, ada_w, ada_b, norm_mix, norm_ffn, ssd_w_in, ssd_conv_w, ssd_conv_b, ssd_dt_bias,
           ssd_a_log, ssd_d, ssd_norm, ssd_w_out, dsa_w_in, dsa_q_norm, dsa_k_norm, dsa_w_out,
           moe_w_group, moe_b_group, moe_w_expert, moe_b_expert, moe_w1, moe_w3, moe_w2):
    depth = ada_w.shape[0]
    xt = x.reshape(TOKENS, D_MODEL)
    mod = _modulation(c, ada_w, ada_b)
    mod5 = mod.reshape(depth, BATCH, 6, 1, D_MODEL)

    head_of_col = jnp.arange(SSD_D_INNER, dtype=jnp.int32) // SSD_HEAD_DIM
    expand = (jnp.arange(SSD_DT_PAD, dtype=jnp.int32)[:, None] == head_of_col[None, :]).astype(BF16)

    for i in range(depth):
        j = i // 2
        nw_mix = norm_mix[i].reshape(1, D_MODEL)
        if i % 2 == 0:
            w_in = _pad_cols(ssd_w_in[j], SSD_PROJ_PAD).astype(BF16)
            z, xbc, dt = _inproj(
                xt, nw_mix, mod5, i, 1, 0, w_in,
                ((0, SSD_D_INNER), (SSD_D_INNER, SSD_D_INNER + SSD_CONV_DIM),
                 (SSD_D_INNER + SSD_CONV_DIM, SSD_PROJ_PAD)))
            xt = _ssd_mixer(
                z, xbc, dt, xt, ssd_conv_w[j], ssd_conv_b[j].reshape(1, SSD_CONV_DIM),
                _pad_cols(ssd_dt_bias[j].reshape(1, SSD_HEADS), SSD_DT_PAD),
                _pad_cols(ssd_a_log[j].reshape(1, SSD_HEADS), SSD_DT_PAD),
                jnp.repeat(ssd_d[j], SSD_HEAD_DIM).reshape(1, SSD_D_INNER),
                ssd_norm[j].reshape(1, SSD_D_INNER), expand, ssd_w_out[j].astype(BF16), mod5, i)
        else:
            w = dsa_w_in[j]
            w_in = jnp.concatenate(
                [_pad_cols(w[:, :DSA_KI_START + IDX_HEAD_DIM], DSA_WI_START),
                 _pad_cols(w[:, DSA_KI_START + IDX_HEAD_DIM:], 128)], axis=1).astype(BF16)
            q, k, v, qi, ki, wi = _dsa_inproj(
                xt, nw_mix, mod5, i, w_in, dsa_q_norm[j].reshape(1, ATT_HEAD_DIM),
                dsa_k_norm[j].reshape(1, ATT_HEAD_DIM))
            xt = _dsa_mixer(q, k, v, qi, ki, wi, xt, dsa_w_out[j].astype(BF16), mod5, i)

        w_route = _pad_cols(jnp.concatenate([moe_w_expert[i], moe_w_group[i]], axis=1), ROUTE_PAD).astype(BF16)
        b_route = _pad_cols(jnp.concatenate([moe_b_expert[i], moe_b_group[i]]).reshape(1, -1), ROUTE_PAD)
        side_by_side = lambda w: w.astype(BF16).reshape(MOE_GROUPS, MOE_EPG, D_MODEL, MOE_HIDDEN).transpose(
            0, 2, 1, 3).reshape(MOE_GROUPS, D_MODEL, MOE_GROUP_HIDDEN)
        xt = _moe(xt, norm_ffn[i].reshape(1, D_MODEL), mod5, i, w_route, b_route,
                  side_by_side(moe_w1[i]), side_by_side(moe_w3[i]),
                  moe_w2[i].astype(BF16).reshape(MOE_GROUPS, MOE_GROUP_HIDDEN, D_MODEL))
    return xt.reshape(BATCH, SEQ, D_MODEL)
```

```python
import functools

import jax
import jax.numpy as jnp
from jax import lax
from jax.experimental import pallas as pl
from jax.experimental.pallas import tpu as pltpu

F32 = jnp.float32
BF16 = jnp.bfloat16

D_MODEL = 1024
BATCH = 8
SEQ = 2048
TOKENS = BATCH * SEQ
EPS = 1e-6

SSD_D_INNER = 2048
SSD_HEAD_DIM = 64
SSD_HEADS = 32
SSD_GROUPS = 8
SSD_HEADS_PER_GROUP = 4
SSD_STATE = 128
SSD_CONV = 4
SSD_CHUNK = 128
SSD_GN = SSD_GROUPS * SSD_STATE
SSD_CONV_DIM = SSD_D_INNER + 2 * SSD_GN
SSD_GROUP_W = SSD_HEADS_PER_GROUP * SSD_HEAD_DIM
SSD_DT_PAD = 128
SSD_PROJ_PAD = SSD_D_INNER + SSD_CONV_DIM + SSD_DT_PAD
CONV_HALO = 8

ATT_HEADS = 16
ATT_KV_HEADS = 4
ATT_Q_PER_KV = 4
ATT_HEAD_DIM = 64
IDX_HEADS = 8
IDX_HEAD_DIM = 64
TOPK = 256
Q_BLOCK = 128
DSA_Q = ATT_HEADS * ATT_HEAD_DIM
DSA_KV = ATT_KV_HEADS * ATT_HEAD_DIM
DSA_QI = IDX_HEADS * IDX_HEAD_DIM
DSA_KI_START = DSA_Q + 2 * DSA_KV + DSA_QI
DSA_WI_START = DSA_KI_START + 128
DSA_PROJ_PAD = DSA_WI_START + 128

MOE_GROUPS = 4
MOE_EPG = 4
MOE_EXPERTS = 16
MOE_HIDDEN = 256
ROUTE_PAD = 128

VMEM_LIMIT = 56 * 1024 * 1024


def _sigmoid(v):
    return 1.0 / (1.0 + jnp.exp(-v))


def _silu(v):
    return v * _sigmoid(v)


def _split3(a):
    hi = a.astype(BF16)
    r = a - hi.astype(F32)
    mid = r.astype(BF16)
    lo = (r - mid.astype(F32)).astype(BF16)
    return hi, mid, lo


def _dot(a, b):
    return jnp.dot(a, b, preferred_element_type=F32)


def _dot_nt(a, b):
    return lax.dot_general(a, b, (((1,), (1,)), ((), ())), preferred_element_type=F32)


def _dot3_exact_rhs(a, m):
    hi, mid, lo = _split3(a)
    return _dot(hi, m) + _dot(mid, m) + _dot(lo, m)


def _dot3_exact_lhs(m, a):
    hi, mid, lo = _split3(a)
    return _dot(m, hi) + _dot(m, mid) + _dot(m, lo)


def _norm_mod(x, nw, scale, shift):
    ms = jnp.mean(x * x, axis=-1, keepdims=True)
    return x * lax.rsqrt(ms + EPS) * nw * (1.0 + scale) + shift


MOD_TN = 1536


def _mod_kernel(c_ref, w_ref, b_ref, o_ref):
    cond = _silu(c_ref[...]).astype(BF16)
    o_ref[...] = _dot(cond, w_ref[...].astype(BF16)) + b_ref[...]


def _modulation(c, ada_w, ada_b):
    depth = ada_w.shape[0]
    n = ada_w.shape[2]
    return pl.pallas_call(
        _mod_kernel,
        grid=(depth, n // MOD_TN),
        in_specs=[
            pl.BlockSpec((BATCH, D_MODEL), lambda i, j: (0, 0)),
            pl.BlockSpec((None, D_MODEL, MOD_TN), lambda i, j: (i, 0, j)),
            pl.BlockSpec((None, 1, MOD_TN), lambda i, j: (i, 0, j)),
        ],
        out_specs=pl.BlockSpec((None, BATCH, MOD_TN), lambda i, j: (i, 0, j)),
        out_shape=jax.ShapeDtypeStruct((depth, BATCH, n), F32),
        compiler_params=pltpu.CompilerParams(
            dimension_semantics=("arbitrary", "arbitrary"), vmem_limit_bytes=VMEM_LIMIT),
        name="adaln_mod",
    )(c, ada_w, ada_b.reshape(depth, 1, n))


def _mod_spec(layer, chunk, rows_per_batch_tile):
    return pl.BlockSpec((None, None, None, 1, D_MODEL),
                        lambda i, *_: (layer, i // rows_per_batch_tile, chunk, 0, 0))


INPROJ_TM = 256


def _inproj_kernel(x_ref, nw_ref, scale_ref, shift_ref, w_ref, *o_refs, col_slices):
    h = _norm_mod(x_ref[...], nw_ref[...], scale_ref[...], shift_ref[...]).astype(BF16)
    for o_ref, (lo, hi) in zip(o_refs, col_slices):
        o_ref[...] = _dot(h, w_ref[:, lo:hi])


def _inproj(x, nw, mod5, layer, scale_chunk, shift_chunk, w, col_slices):
    tm = INPROJ_TM
    tiles_per_batch = SEQ // tm
    n_pad = w.shape[1]
    return pl.pallas_call(
        functools.partial(_inproj_kernel, col_slices=col_slices),
        grid=(TOKENS // tm,),
        in_specs=[
            pl.BlockSpec((tm, D_MODEL), lambda i: (i, 0)),
            pl.BlockSpec((1, D_MODEL), lambda i: (0, 0)),
            _mod_spec(layer, scale_chunk, tiles_per_batch),
            _mod_spec(layer, shift_chunk, tiles_per_batch),
            pl.BlockSpec((D_MODEL, n_pad), lambda i: (0, 0)),
        ],
        out_specs=[pl.BlockSpec((tm, hi - lo), lambda i: (i, 0)) for lo, hi in col_slices],
        out_shape=[jax.ShapeDtypeStruct((TOKENS, hi - lo), F32) for lo, hi in col_slices],
        compiler_params=pltpu.CompilerParams(
            dimension_semantics=("arbitrary",), vmem_limit_bytes=VMEM_LIMIT),
        name="norm_inproj",
    )(x, nw, mod5, mod5, w)


CONV_COLS = 512


def _ssd_kernel(z_ref, xbc_ref, dt_ref, xres_ref, cw_ref, cb_ref, dtb_ref, alog_ref, de_ref, nw_ref,
                e_ref, wout_ref, gate_ref, o_ref, state_ref, ext_ref, act_ref, yn_ref):
    q = SSD_CHUNK
    c = pl.program_id(1)

    @pl.when(c == 0)
    def _():
        state_ref[...] = jnp.zeros_like(state_ref)
        ext_ref[0:CONV_HALO, :] = jnp.zeros((CONV_HALO, SSD_CONV_DIM), F32)

    @pl.when(c > 0)
    def _():
        ext_ref[0:CONV_HALO, :] = ext_ref[q:q + CONV_HALO, :]

    ext_ref[CONV_HALO:CONV_HALO + q, :] = xbc_ref[...]

    for s in range(SSD_CONV_DIM // CONV_COLS):
        cs = slice(s * CONV_COLS, (s + 1) * CONV_COLS)
        acc = cb_ref[:, cs] + cw_ref[SSD_CONV - 1:SSD_CONV, cs] * ext_ref[CONV_HALO:CONV_HALO + q, cs]
        for k in range(SSD_CONV - 1):
            off = CONV_HALO - (SSD_CONV - 1) + k
            acc = acc + cw_ref[k:k + 1, cs] * ext_ref[off:off + q, cs]
        act_ref[:, cs] = _silu(acc)

    dt_raw = dt_ref[...] + dtb_ref[...]
    dt = jnp.maximum(dt_raw, 0.0) + jnp.log1p(jnp.exp(-jnp.abs(dt_raw)))
    a = dt * (-jnp.exp(alog_ref[...]))
    row = lax.broadcasted_iota(jnp.int32, (q, q), 0)
    col = lax.broadcasted_iota(jnp.int32, (q, q), 1)
    tril = row >= col
    acs = _dot3_exact_lhs(tril.astype(BF16), a)
    acs_t = acs.T
    expand = e_ref[...]
    acs_e = _dot3_exact_rhs(acs, expand)
    dt_e = _dot3_exact_rhs(dt, expand)
    tot_e = acs_e[q - 1:q, :]
    decay_from_start = jnp.exp(acs_e)
    decay_to_end = jnp.exp(tot_e - acs_e)
    chunk_decay = jnp.exp(tot_e)

    lane_head = lax.broadcasted_iota(jnp.int32, (q, SSD_GROUP_W), 1) // SSD_HEAD_DIM
    for g in range(SSD_GROUPS):
        xs = act_ref[:, g * SSD_GROUP_W:(g + 1) * SSD_GROUP_W]
        gs = slice(g * SSD_GROUP_W, (g + 1) * SSD_GROUP_W)
        bm = act_ref[:, SSD_D_INNER + g * SSD_STATE:SSD_D_INNER + (g + 1) * SSD_STATE]
        cm = act_ref[:, SSD_D_INNER + SSD_GN + g * SSD_STATE:SSD_D_INNER + SSD_GN + (g + 1) * SSD_STATE]
        bm_t = bm.T.astype(BF16)
        cm_b = cm.astype(BF16)
        cb = _dot(cm_b, bm_t)
        xd = xs * dt_e[:, gs]
        ms = []
        xds = []
        for j in range(SSD_HEADS_PER_GROUP):
            h = g * SSD_HEADS_PER_GROUP + j
            seg = acs[:, h:h + 1] - acs_t[h:h + 1, :]
            dec = jnp.exp(jnp.where(tril, seg, -jnp.inf))
            ms.append((cb * dec).astype(BF16))
            xds.append(jnp.where(lane_head == j, xd, 0.0).astype(BF16))
        y_diag = _dot(jnp.concatenate(ms, axis=1), jnp.concatenate(xds, axis=0))
        prev = state_ref[g]
        y_off = _dot(cm_b, prev.astype(BF16)) * decay_from_start[:, gs]
        state_ref[g] = prev * chunk_decay[:, gs] + _dot(bm_t, (xd * decay_to_end[:, gs]).astype(BF16))
        y = y_diag + y_off + xs * de_ref[:, gs]
        y = y * _silu(z_ref[:, gs])
        y = y * lax.rsqrt(jnp.mean(y * y, axis=-1, keepdims=True) + EPS) * nw_ref[:, gs]
        yn_ref[:, gs] = y.astype(BF16)

    out = _dot(yn_ref[...], wout_ref[...])
    o_ref[...] = xres_ref[...] + gate_ref[...] * out


def _ssd_mixer(z, xbc, dt, x, conv_w, conv_b, dt_bias, a_log, d_e, norm_w, expand, w_out, mod5, layer):
    q = SSD_CHUNK
    nc = SEQ // q
    tok = lambda w: pl.BlockSpec((q, w), lambda b, c: (b * nc + c, 0))
    full = lambda r, w: pl.BlockSpec((r, w), lambda b, c: (0, 0))
    return pl.pallas_call(
        _ssd_kernel,
        grid=(BATCH, nc),
        in_specs=[
            tok(SSD_D_INNER), tok(SSD_CONV_DIM), tok(SSD_DT_PAD), tok(D_MODEL),
            full(SSD_CONV, SSD_CONV_DIM), full(1, SSD_CONV_DIM), full(1, SSD_DT_PAD), full(1, SSD_DT_PAD),
            full(1, SSD_D_INNER), full(1, SSD_D_INNER), full(SSD_DT_PAD, SSD_D_INNER),
            full(SSD_D_INNER, D_MODEL),
            pl.BlockSpec((None, None, None, 1, D_MODEL), lambda b, c: (layer, b, 2, 0, 0)),
        ],
        out_specs=tok(D_MODEL),
        out_shape=jax.ShapeDtypeStruct((TOKENS, D_MODEL), F32),
        scratch_shapes=[
            pltpu.VMEM((SSD_GROUPS, SSD_STATE, SSD_GROUP_W), F32),
            pltpu.VMEM((q + CONV_HALO, SSD_CONV_DIM), F32),
            pltpu.VMEM((q, SSD_CONV_DIM), F32),
            pltpu.VMEM((q, SSD_D_INNER), BF16),
        ],
        compiler_params=pltpu.CompilerParams(
            dimension_semantics=("arbitrary", "arbitrary"), vmem_limit_bytes=VMEM_LIMIT),
        name="ssd_mixer",
    )(z, xbc, dt, x, conv_w, conv_b, dt_bias, a_log, d_e, norm_w, expand, w_out, mod5)


DSA_KEY_TILE = 256
DSA_CLASSES = 4
DSA_BLOCKS_PER_CLASS = (SEQ // Q_BLOCK) // DSA_CLASSES
N_BISECT = 12
F32_MIN = float(jnp.finfo(jnp.float32).min)
LOG2E = 1.4426950408889634


def _count(mask):
    return jnp.sum(jnp.where(mask, 1.0, 0.0), axis=-1, keepdims=True)


def _select_topk(score_ref, q_pos, n_keys):
    kf = float(TOPK)
    small = (q_pos + 1) <= TOPK
    sc = score_ref[...]
    hi0 = jnp.max(sc, axis=-1, keepdims=True)
    lo0 = jnp.min(jnp.where(sc == -jnp.inf, jnp.inf, sc), axis=-1, keepdims=True)

    def bisect(_, carry):
        lo, hi = carry
        mid = lo + 0.5 * (hi - lo)
        ok = _count(score_ref[...] >= mid) >= kf
        return jnp.where(ok, mid, lo), jnp.where(ok, hi, mid)

    _, hi = lax.fori_loop(0, N_BISECT, bisect, (lo0, hi0))

    v0 = jnp.max(jnp.where(sc <= hi, sc, -jnp.inf), axis=-1, keepdims=True)
    c0 = _count(sc >= v0)
    pend0 = jnp.where((c0 >= kf) | small, 0.0, 1.0)

    def walk_cond(carry):
        return (carry[2] > 0.0) & (carry[3] < n_keys)

    def walk(carry):
        v, pend, _, it = carry
        s = score_ref[...]
        v2 = jnp.max(jnp.where(s < v, s, -jnp.inf), axis=-1, keepdims=True)
        c2 = _count(s >= v2)
        v = jnp.where(pend > 0.0, v2, v)
        pend = jnp.where(c2 >= kf, 0.0, pend)
        return v, pend, jnp.max(pend), it + 1

    v, _, _, _ = lax.while_loop(walk_cond, walk, (v0, pend0, jnp.max(pend0), jnp.int32(0)))
    thr = jnp.where(small, F32_MIN, v)

    key_pos = lax.broadcasted_iota(jnp.int32, (Q_BLOCK, n_keys), 1)
    gt = sc > thr
    eq = sc == thr
    need = kf - _count(gt)
    tie = jnp.where(_count(eq) > need, 1.0, 0.0)

    def search_cut():
        def body(_, carry):
            lo, hi = carry
            mid = (lo + hi) >> 1
            s = score_ref[...]
            ok = _count((s == thr) & (key_pos <= mid)) >= need
            return jnp.where(ok, lo, mid), jnp.where(ok, mid, hi)

        init = (jnp.full((Q_BLOCK, 1), -1, jnp.int32), jnp.full((Q_BLOCK, 1), n_keys - 1, jnp.int32))
        return lax.fori_loop(0, (n_keys - 1).bit_length() + 1, body, init)[1]

    cut = lax.cond(jnp.max(tie) > 0.0, search_cut, lambda: jnp.full((Q_BLOCK, 1), n_keys - 1, jnp.int32))
    score_ref[...] = jnp.where(gt | (eq & (key_pos <= cut)), 0.0, -jnp.inf)


def _dsa_inproj_kernel(x_ref, nw_ref, scale_ref, shift_ref, w_ref, qn_ref, kn_ref, seg_ref, segt_ref,
                       q_ref, k_ref, v_ref, qi_ref, ki_ref, wi_ref):
    hd = ATT_HEAD_DIM
    h = _norm_mod(x_ref[...], nw_ref[...], scale_ref[...], shift_ref[...]).astype(BF16)

    def head_norm(t, w):
        width = t.shape[1]
        ss = _dot((t * t).astype(BF16), seg_ref[0:width, :])
        r = lax.rsqrt(ss * (1.0 / hd) + EPS)
        r_hi = r.astype(BF16)
        r_lo = (r - r_hi.astype(F32)).astype(BF16)
        return t * (_dot(r_hi, segt_ref[:, 0:width]) + _dot(r_lo, segt_ref[:, 0:width])) * w

    q = head_norm(_dot(h, w_ref[:, 0:DSA_Q]), qn_ref[...] * (hd ** -0.5 * LOG2E))
    for n in range(ATT_HEADS):
        q_ref[n] = q[:, n * hd:(n + 1) * hd].astype(BF16)
    kv = _dot(h, w_ref[:, DSA_Q:DSA_Q + 2 * DSA_KV])
    k = head_norm(kv[:, 0:DSA_KV], kn_ref[...])
    for n in range(ATT_KV_HEADS):
        k_ref[n] = k[:, n * hd:(n + 1) * hd].astype(BF16)
        v_ref[n] = kv[:, DSA_KV + n * hd:DSA_KV + (n + 1) * hd].astype(BF16)
    qi = _dot(h, w_ref[:, DSA_Q + 2 * DSA_KV:DSA_KI_START])
    for n in range(IDX_HEADS):
        qi_ref[n] = qi[:, n * IDX_HEAD_DIM:(n + 1) * IDX_HEAD_DIM].astype(BF16)
    ki_ref[...] = _dot(h, w_ref[:, DSA_KI_START:DSA_KI_START + IDX_HEAD_DIM]).astype(BF16)
    wi_ref[...] = _dot(h, w_ref[:, DSA_WI_START:DSA_WI_START + IDX_HEADS]) * ((IDX_HEADS * IDX_HEAD_DIM) ** -0.5)


def _dsa_inproj(x, nw, mod5, layer, w, q_norm, k_norm):
    tm = INPROJ_TM
    tiles_per_batch = SEQ // tm
    heads = lambda n: pl.BlockSpec((n, tm, ATT_HEAD_DIM), lambda i: (0, i, 0))
    head_of = jnp.arange(DSA_Q, dtype=jnp.int32) // ATT_HEAD_DIM
    seg = (head_of[:, None] == jnp.arange(128, dtype=jnp.int32)[None, :]).astype(BF16)
    q_norm = jnp.tile(q_norm, (1, ATT_HEADS))
    k_norm = jnp.tile(k_norm, (1, ATT_KV_HEADS))
    return pl.pallas_call(
        _dsa_inproj_kernel,
        grid=(TOKENS // tm,),
        in_specs=[
            pl.BlockSpec((tm, D_MODEL), lambda i: (i, 0)),
            pl.BlockSpec((1, D_MODEL), lambda i: (0, 0)),
            _mod_spec(layer, 1, tiles_per_batch),
            _mod_spec(layer, 0, tiles_per_batch),
            pl.BlockSpec((D_MODEL, DSA_PROJ_PAD), lambda i: (0, 0)),
            pl.BlockSpec((1, DSA_Q), lambda i: (0, 0)),
            pl.BlockSpec((1, DSA_KV), lambda i: (0, 0)),
            pl.BlockSpec((DSA_Q, 128), lambda i: (0, 0)),
            pl.BlockSpec((128, DSA_Q), lambda i: (0, 0)),
        ],
        out_specs=[heads(ATT_HEADS), heads(ATT_KV_HEADS), heads(ATT_KV_HEADS), heads(IDX_HEADS),
                   pl.BlockSpec((tm, IDX_HEAD_DIM), lambda i: (i, 0)),
                   pl.BlockSpec((tm, IDX_HEADS), lambda i: (i, 0))],
        out_shape=[jax.ShapeDtypeStruct((ATT_HEADS, TOKENS, ATT_HEAD_DIM), BF16),
                   jax.ShapeDtypeStruct((ATT_KV_HEADS, TOKENS, ATT_HEAD_DIM), BF16),
                   jax.ShapeDtypeStruct((ATT_KV_HEADS, TOKENS, ATT_HEAD_DIM), BF16),
                   jax.ShapeDtypeStruct((IDX_HEADS, TOKENS, IDX_HEAD_DIM), BF16),
                   jax.ShapeDtypeStruct((TOKENS, IDX_HEAD_DIM), BF16),
                   jax.ShapeDtypeStruct((TOKENS, IDX_HEADS), F32)],
        compiler_params=pltpu.CompilerParams(
            dimension_semantics=("arbitrary",), vmem_limit_bytes=VMEM_LIMIT),
        name="dsa_inproj",
    )(x, nw, mod5, mod5, w, q_norm, k_norm, seg, seg.T)


def _dsa_kernel(q_ref, k_ref, v_ref, qi_ref, ki_ref, wi_ref, xres_ref, wout_ref, gate_ref, o_ref,
                score_ref, ocat_ref, *, n_keys, first_block):
    hd = ATT_HEAD_DIM
    q_pos = (first_block + pl.program_id(1)) * Q_BLOCK + lax.broadcasted_iota(jnp.int32, (Q_BLOCK, 1), 0)

    wi = wi_ref[...]
    qi = qi_ref[...].reshape(IDX_HEADS * Q_BLOCK, IDX_HEAD_DIM)
    for kt in range(n_keys // DSA_KEY_TILE):
        ks = slice(kt * DSA_KEY_TILE, (kt + 1) * DSA_KEY_TILE)
        raw = _dot_nt(qi, ki_ref[ks, :])
        acc = jnp.zeros((Q_BLOCK, DSA_KEY_TILE), F32)
        for n in range(IDX_HEADS):
            acc = acc + wi[:, n:n + 1] * jnp.maximum(raw[n * Q_BLOCK:(n + 1) * Q_BLOCK, :], 0.0)
        key_pos = kt * DSA_KEY_TILE + lax.broadcasted_iota(jnp.int32, (Q_BLOCK, DSA_KEY_TILE), 1)
        score_ref[:, ks] = jnp.where(key_pos <= q_pos, acc, -jnp.inf)

    _select_topk(score_ref, q_pos, n_keys)
    bias = score_ref[...][None, :, :]

    for n in range(ATT_KV_HEADS):
        q4 = q_ref[n * ATT_Q_PER_KV:(n + 1) * ATT_Q_PER_KV].reshape(ATT_Q_PER_KV * Q_BLOCK, hd)
        s = _dot_nt(q4, k_ref[n]).reshape(ATT_Q_PER_KV, Q_BLOCK, n_keys) + bias
        p = jnp.exp2(s - jnp.max(s, axis=-1, keepdims=True))
        denom = jnp.sum(p, axis=-1, keepdims=True).reshape(ATT_Q_PER_KV * Q_BLOCK, 1)
        o = _dot(p.reshape(ATT_Q_PER_KV * Q_BLOCK, n_keys).astype(BF16), v_ref[n]) * (1.0 / denom)
        for g in range(ATT_Q_PER_KV):
            col = (n * ATT_Q_PER_KV + g) * hd
            ocat_ref[:, col:col + hd] = o[g * Q_BLOCK:(g + 1) * Q_BLOCK, :]
    out = _dot(ocat_ref[...].astype(BF16), wout_ref[...])
    o_ref[...] = xres_ref[...] + gate_ref[...] * out


def _dsa_mixer(q, k, v, qi, ki, wi, x, w_out, mod5, layer):
    nb = SEQ // Q_BLOCK
    k4 = k.reshape(ATT_KV_HEADS, BATCH, SEQ, ATT_HEAD_DIM)
    v4 = v.reshape(ATT_KV_HEADS, BATCH, SEQ, ATT_HEAD_DIM)
    ki3 = ki.reshape(BATCH, SEQ, IDX_HEAD_DIM)
    for cls in range(DSA_CLASSES):
        n_keys = (cls + 1) * (SEQ // DSA_CLASSES)
        first_block = cls * DSA_BLOCKS_PER_CLASS
        row = lambda b, i, fb=first_block: b * nb + fb + i
        heads = lambda n: pl.BlockSpec((n, Q_BLOCK, ATT_HEAD_DIM), lambda b, i: (0, row(b, i), 0))
        keys = pl.BlockSpec((ATT_KV_HEADS, None, n_keys, ATT_HEAD_DIM), lambda b, i: (0, b, 0, 0))
        x = pl.pallas_call(
            functools.partial(_dsa_kernel, n_keys=n_keys, first_block=first_block),
            grid=(BATCH, DSA_BLOCKS_PER_CLASS),
            in_specs=[
                heads(ATT_HEADS), keys, keys, heads(IDX_HEADS),
                pl.BlockSpec((None, n_keys, IDX_HEAD_DIM), lambda b, i: (b, 0, 0)),
                pl.BlockSpec((Q_BLOCK, IDX_HEADS), lambda b, i: (row(b, i), 0)),
                pl.BlockSpec((Q_BLOCK, D_MODEL), lambda b, i: (row(b, i), 0)),
                pl.BlockSpec((DSA_Q, D_MODEL), lambda b, i: (0, 0)),
                pl.BlockSpec((None, None, None, 1, D_MODEL), lambda b, i: (layer, b, 2, 0, 0)),
            ],
            out_specs=pl.BlockSpec((Q_BLOCK, D_MODEL), lambda b, i: (row(b, i), 0)),
            out_shape=jax.ShapeDtypeStruct((TOKENS, D_MODEL), F32),
            scratch_shapes=[
                pltpu.VMEM((Q_BLOCK, n_keys), F32),
                pltpu.VMEM((Q_BLOCK, D_MODEL), F32),
            ],
            input_output_aliases={6: 0},
            compiler_params=pltpu.CompilerParams(
                dimension_semantics=("arbitrary", "arbitrary"), vmem_limit_bytes=VMEM_LIMIT),
            name=f"dsa_mixer_c{cls}",
        )(q, k4, v4, qi, ki3, wi, x, w_out, mod5)
    return x


MOE_PAIRS = MOE_EPG * (MOE_EPG - 1) // 2
MOE_CLASSES = MOE_GROUPS * MOE_PAIRS
PAIR_LO = (0, 0, 0, 1, 1, 2)
PAIR_HI = (1, 2, 3, 2, 3, 3)
ROUTE_TM = 512
META_W = 128
META_CLASS, META_RANK, META_WLO, META_WHI = 0, 1, 2, 3
TILE_ROWS, TILE_LANES = 8, 128
H_WORDS = D_MODEL // 2
H_SUBLANES = H_WORDS // TILE_LANES
SORT_BLOCK = 256
N_SORT_BLOCKS = TOKENS // SORT_BLOCK
MAX_ITEMS = N_SORT_BLOCKS + MOE_CLASSES
DMA_CHUNK = 32
DMA_CHUNKS_IN_FLIGHT = 4
COMBINE_TM = 256


def _route_kernel(x_ref, nw_ref, scale_ref, shift_ref, wr_ref, br_ref, pay_ref, meta_ref, cnt_ref, carry_ref):
    tm = ROUTE_TM

    @pl.when(pl.program_id(0) == 0)
    def _():
        carry_ref[...] = jnp.zeros_like(carry_ref)

    h = _norm_mod(x_ref[...], nw_ref[...], scale_ref[...], shift_ref[...])
    logits = _dot(h.astype(BF16), wr_ref[...]) + br_ref[...]
    lane = lax.broadcasted_iota(jnp.int32, logits.shape, 1)
    neg = -jnp.inf
    big = jnp.int32(ROUTE_PAD)
    is_group = (lane >= MOE_EXPERTS) & (lane < MOE_EXPERTS + MOE_GROUPS)
    gl = jnp.where(is_group, logits, neg)
    g_max = jnp.max(gl, axis=-1, keepdims=True)
    g_idx = jnp.min(jnp.where(gl == g_max, lane - MOE_EXPERTS, big), axis=-1, keepdims=True)
    g_val = 1.0 / jnp.sum(jnp.exp(gl - g_max), axis=-1, keepdims=True)
    in_group = (lane < MOE_EXPERTS) & ((lane // MOE_EPG) == g_idx)
    el = jnp.where(in_group, logits, neg)
    m1 = jnp.max(el, axis=-1, keepdims=True)
    i1 = jnp.min(jnp.where(el == m1, lane, big), axis=-1, keepdims=True)
    el2 = jnp.where(lane == i1, neg, el)
    m2 = jnp.max(el2, axis=-1, keepdims=True)
    i2 = jnp.min(jnp.where(el2 == m2, lane, big), axis=-1, keepdims=True)
    r = jnp.exp(m2 - m1)
    w_top1 = g_val / (1.0 + r)
    w_top2 = g_val * r / (1.0 + r)

    lo = jnp.minimum(i1, i2) - g_idx * MOE_EPG
    hi = jnp.maximum(i1, i2) - g_idx * MOE_EPG
    pair = (lo * (2 * MOE_EPG - 1 - lo)) // 2 + (hi - lo - 1)
    cls = g_idx * MOE_PAIRS + pair
    w_lo = jnp.where(i1 < i2, w_top1, w_top2)
    w_hi = jnp.where(i1 < i2, w_top2, w_top1)

    onehot = lane == cls
    row = lax.broadcasted_iota(jnp.int32, (tm, tm), 0)
    col = lax.broadcasted_iota(jnp.int32, (tm, tm), 1)
    before = _dot((row > col).astype(BF16), onehot.astype(BF16)) + carry_ref[...]
    rank = jnp.sum(jnp.where(onehot, before, 0.0), axis=-1, keepdims=True)
    carry_ref[...] += jnp.sum(jnp.where(onehot, 1.0, 0.0), axis=0, keepdims=True)
    cnt_ref[...] = carry_ref[...]

    meta = jnp.where(lane == META_CLASS, cls.astype(F32),
                     jnp.where(lane == META_RANK, rank,
                               jnp.where(lane == META_WLO, w_lo, jnp.where(lane == META_WHI, w_hi, 0.0))))
    meta_ref[...] = meta
    words = pltpu.pack_elementwise([h[:, 0:H_WORDS], h[:, H_WORDS:D_MODEL]], packed_dtype=BF16)
    for s in range(H_SUBLANES):
        pay_ref[pl.ds(s, tm, stride=TILE_ROWS), :] = words[:, s * TILE_LANES:(s + 1) * TILE_LANES]
    pay_ref[pl.ds(H_SUBLANES, tm, stride=TILE_ROWS), :] = lax.bitcast_convert_type(meta, jnp.int32)
    for s in range(H_SUBLANES + 1, TILE_ROWS):
        pay_ref[pl.ds(s, tm, stride=TILE_ROWS), :] = jnp.zeros((tm, TILE_LANES), jnp.int32)


def _moe_route(x, nw, mod5, layer, w_route, b_route):
    tm = ROUTE_TM
    tiles_per_batch = SEQ // tm
    return pl.pallas_call(
        _route_kernel,
        grid=(TOKENS // tm,),
        in_specs=[
            pl.BlockSpec((tm, D_MODEL), lambda i: (i, 0)),
            pl.BlockSpec((1, D_MODEL), lambda i: (0, 0)),
            _mod_spec(layer, 4, tiles_per_batch),
            _mod_spec(layer, 3, tiles_per_batch),
            pl.BlockSpec((D_MODEL, ROUTE_PAD), lambda i: (0, 0)),
            pl.BlockSpec((1, ROUTE_PAD), lambda i: (0, 0)),
        ],
        out_specs=[pl.BlockSpec((tm * TILE_ROWS, TILE_LANES), lambda i: (i, 0)),
                   pl.BlockSpec((tm, META_W), lambda i: (i, 0)),
                   pl.BlockSpec((1, ROUTE_PAD), lambda i: (0, 0))],
        out_shape=[jax.ShapeDtypeStruct((TOKENS * TILE_ROWS, TILE_LANES), jnp.int32),
                   jax.ShapeDtypeStruct((TOKENS, META_W), F32),
                   jax.ShapeDtypeStruct((1, ROUTE_PAD), F32)],
        scratch_shapes=[pltpu.VMEM((1, ROUTE_PAD), F32)],
        compiler_params=pltpu.CompilerParams(
            dimension_semantics=("arbitrary",), vmem_limit_bytes=VMEM_LIMIT),
        name="moe_route",
    )(x, nw, mod5, mod5, w_route, b_route)


def _permute_kernel(pos_ref, src_hbm, dst_hbm, sem):
    def chunk_wait():
        pltpu.make_async_copy(src_hbm.at[pl.ds(0, DMA_CHUNK)], dst_hbm.at[pl.ds(0, DMA_CHUNK)], sem).wait()

    def chunk(c, _):
        @pl.when(c >= DMA_CHUNKS_IN_FLIGHT)
        def _():
            chunk_wait()

        for r in range(DMA_CHUNK):
            t = c * DMA_CHUNK + r
            pltpu.make_async_copy(src_hbm.at[t], dst_hbm.at[pos_ref[t]], sem).start()
        return 0

    lax.fori_loop(0, TOKENS // DMA_CHUNK, chunk, 0)
    for _ in range(DMA_CHUNKS_IN_FLIGHT):
        chunk_wait()


def _moe_permute(pos, payload):
    return pl.pallas_call(
        _permute_kernel,
        grid_spec=pltpu.PrefetchScalarGridSpec(
            num_scalar_prefetch=1, grid=(1,),
            in_specs=[pl.BlockSpec(memory_space=pl.ANY)],
            out_specs=pl.BlockSpec(memory_space=pl.ANY),
            scratch_shapes=[pltpu.SemaphoreType.DMA(())]),
        out_shape=jax.ShapeDtypeStruct((TOKENS, TILE_ROWS, TILE_LANES), jnp.int32),
        compiler_params=pltpu.CompilerParams(dimension_semantics=("arbitrary",)),
        name="moe_permute",
    )(pos, payload)


def _experts_kernel(blk_ref, cls_ref, elo_ref, ehi_ref, first_ref, last_ref, valid_ref,
                    pay_ref, w1lo_ref, w3lo_ref, w2lo_ref, w1hi_ref, w3hi_ref, w2hi_ref, o_ref, acc_ref):
    k = pl.program_id(0)
    rows = SORT_BLOCK

    @pl.when(valid_ref[k] == 1)
    def _():
        def sublane(s):
            return pay_ref[pl.ds(s, rows, stride=TILE_ROWS), :]

        halves = [[pltpu.unpack_elementwise(sublane(s), index=i, packed_dtype=BF16, unpacked_dtype=F32)
                   for s in range(H_SUBLANES)] for i in range(2)]
        hb = jnp.concatenate(halves[0] + halves[1], axis=1).astype(BF16)
        meta = lax.bitcast_convert_type(sublane(H_SUBLANES), F32)
        mine = meta[:, META_CLASS:META_CLASS + 1] == cls_ref[k].astype(F32)
        w_lo = jnp.where(mine, meta[:, META_WLO:META_WLO + 1], 0.0)
        w_hi = jnp.where(mine, meta[:, META_WHI:META_WHI + 1], 0.0)
        hid_lo = _silu(_dot(hb, w1lo_ref[...])) * _dot(hb, w3lo_ref[...]) * w_lo
        hid_hi = _silu(_dot(hb, w1hi_ref[...])) * _dot(hb, w3hi_ref[...]) * w_hi
        y = _dot(hid_lo.astype(BF16), w2lo_ref[...]) + _dot(hid_hi.astype(BF16), w2hi_ref[...])

        @pl.when(first_ref[k] == 1)
        def _():
            acc_ref[...] = y

        @pl.when(first_ref[k] == 0)
        def _():
            acc_ref[...] += y

        @pl.when(last_ref[k] == 1)
        def _():
            for s in range(TILE_ROWS):
                o_ref[pl.ds(s, rows, stride=TILE_ROWS), :] = acc_ref[:, s * TILE_LANES:(s + 1) * TILE_LANES]


def _moe_experts(items, payload_sorted, w1, w3, w2):
    blk, cls, elo, ehi, first, last, valid = items
    tiles = pl.BlockSpec((SORT_BLOCK * TILE_ROWS, TILE_LANES), lambda k, blk, *_: (blk[k], 0))
    w_in = lambda which: pl.BlockSpec(
        (None, D_MODEL, MOE_HIDDEN), lambda k, blk, cls, elo, ehi, *_: ((elo, ehi)[which][k], 0, 0))
    w_out = lambda which: pl.BlockSpec(
        (None, MOE_HIDDEN, D_MODEL), lambda k, blk, cls, elo, ehi, *_: ((elo, ehi)[which][k], 0, 0))
    return pl.pallas_call(
        _experts_kernel,
        grid_spec=pltpu.PrefetchScalarGridSpec(
            num_scalar_prefetch=7, grid=(MAX_ITEMS,),
            in_specs=[tiles, w_in(0), w_in(0), w_out(0), w_in(1), w_in(1), w_out(1)],
            out_specs=tiles,
            scratch_shapes=[pltpu.VMEM((SORT_BLOCK, D_MODEL), F32)]),
        out_shape=jax.ShapeDtypeStruct((TOKENS * TILE_ROWS, TILE_LANES), F32),
        compiler_params=pltpu.CompilerParams(
            dimension_semantics=("arbitrary",), vmem_limit_bytes=VMEM_LIMIT),
        name="moe_experts",
    )(blk, cls, elo, ehi, first, last, valid, payload_sorted, w1, w3, w2, w1, w3, w2)


def _combine_kernel(pos_ref, x_ref, gate_ref, y_hbm, o_ref, buf_ref, sem):
    tm = COMBINE_TM
    i = pl.program_id(0)
    n = pl.num_programs(0)

    def gather(tile, slot):
        def issue(r, _):
            pltpu.make_async_copy(y_hbm.at[pos_ref[tile * tm + r]],
                                  buf_ref.at[pl.ds((slot * tm + r) * TILE_ROWS, TILE_ROWS), :], sem.at[slot]).start()
            return 0
        lax.fori_loop(0, tm, issue, 0, unroll=16)

    @pl.when(i == 0)
    def _():
        gather(0, 0)

    @pl.when(i + 1 < n)
    def _():
        gather(i + 1, (i + 1) % 2)

    slot = i % 2
    base = slot * tm * TILE_ROWS
    whole_slot = buf_ref.at[pl.ds(base, tm * TILE_ROWS), :]
    pltpu.make_async_copy(whole_slot, whole_slot, sem.at[slot]).wait()
    y = jnp.concatenate([buf_ref[pl.ds(base + s, tm, stride=TILE_ROWS), :] for s in range(TILE_ROWS)], axis=1)
    o_ref[...] = x_ref[...] + gate_ref[...] * y


def _moe_combine(pos, x, mod5, layer, y_sorted):
    tm = COMBINE_TM
    tiles_per_batch = SEQ // tm
    return pl.pallas_call(
        _combine_kernel,
        grid_spec=pltpu.PrefetchScalarGridSpec(
            num_scalar_prefetch=1, grid=(TOKENS // tm,),
            in_specs=[
                pl.BlockSpec((tm, D_MODEL), lambda i, pos: (i, 0)),
                pl.BlockSpec((None, None, None, 1, D_MODEL),
                             lambda i, pos: (layer, i // tiles_per_batch, 5, 0, 0)),
                pl.BlockSpec(memory_space=pl.ANY),
            ],
            out_specs=pl.BlockSpec((tm, D_MODEL), lambda i, pos: (i, 0)),
            scratch_shapes=[pltpu.VMEM((2 * tm * TILE_ROWS, TILE_LANES), F32), pltpu.SemaphoreType.DMA((2,))]),
        out_shape=jax.ShapeDtypeStruct((TOKENS, D_MODEL), F32),
        compiler_params=pltpu.CompilerParams(
            dimension_semantics=("arbitrary",), vmem_limit_bytes=VMEM_LIMIT),
        name="moe_combine",
    )(pos, x, mod5, y_sorted)


def _moe_plan(meta, counts):
    cls = meta[:, META_CLASS].astype(jnp.int32)
    rank = meta[:, META_RANK].astype(jnp.int32)
    count = counts[0, :MOE_CLASSES].astype(jnp.int32)
    ends = jnp.cumsum(count)
    starts = ends - count
    pos = starts[cls] + rank

    first_blk = starts // SORT_BLOCK
    n_items = jnp.where(count > 0, (ends - 1) // SORT_BLOCK - first_blk + 1, 0)
    item_end = jnp.cumsum(n_items)
    item_start = item_end - n_items
    k = jnp.arange(MAX_ITEMS, dtype=jnp.int32)
    valid = k < item_end[-1]
    kc = jnp.minimum(k, item_end[-1] - 1)
    icls = jnp.searchsorted(item_end, kc, side="right").astype(jnp.int32)
    blk = first_blk[icls] + (kc - item_start[icls])
    first = jnp.concatenate([jnp.ones((1,), jnp.int32), (blk[1:] != blk[:-1]).astype(jnp.int32)])
    last = jnp.concatenate([(blk[1:] != blk[:-1]) | ~valid[1:], jnp.ones((1,), bool)]).astype(jnp.int32)
    group = icls // MOE_PAIRS
    pair = icls % MOE_PAIRS
    elo = group * MOE_EPG + jnp.asarray(PAIR_LO, jnp.int32)[pair]
    ehi = group * MOE_EPG + jnp.asarray(PAIR_HI, jnp.int32)[pair]
    return pos, (blk, icls, elo, ehi, first * valid, last * valid, valid.astype(jnp.int32))


def _moe(x, nw, mod5, layer, w_route, b_route, w1, w3, w2):
    payload, meta, counts = _moe_route(x, nw, mod5, layer, w_route, b_route)
    pos, items = _moe_plan(meta, counts)
    tiles = (TOKENS, TILE_ROWS, TILE_LANES)
    sorted_payload = _moe_permute(pos, payload.reshape(tiles)).reshape(TOKENS * TILE_ROWS, TILE_LANES)
    y_sorted = _moe_experts(items, sorted_payload, w1, w3, w2)
    return _moe_combine(pos, x, mod5, layer, y_sorted.reshape(tiles))


def _pad_cols(w, width):
    return jnp.pad(w, ((0, 0), (0, width - w.shape[1])))


def kernel(x, c, ada_w, ada_b, norm_mix, norm_ffn, ssd_w_in, ssd_conv_w, ssd_conv_b, ssd_dt_bias,
           ssd_a_log, ssd_d, ssd_norm, ssd_w_out, dsa_w_in, dsa_q_norm, dsa_k_norm, dsa_w_out,
           moe_w_group, moe_b_group, moe_w_expert, moe_b_expert, moe_w1, moe_w3, moe_w2):
    depth = ada_w.shape[0]
    xt = x.reshape(TOKENS, D_MODEL)
    mod = _modulation(c, ada_w, ada_b)
    mod5 = mod.reshape(depth, BATCH, 6, 1, D_MODEL)

    head_of_col = jnp.arange(SSD_D_INNER, dtype=jnp.int32) // SSD_HEAD_DIM
    expand = (jnp.arange(SSD_DT_PAD, dtype=jnp.int32)[:, None] == head_of_col[None, :]).astype(BF16)

    for i in range(depth):
        j = i // 2
        nw_mix = norm_mix[i].reshape(1, D_MODEL)
        if i % 2 == 0:
            w_in = _pad_cols(ssd_w_in[j], SSD_PROJ_PAD).astype(BF16)
            z, xbc, dt = _inproj(
                xt, nw_mix, mod5, i, 1, 0, w_in,
                ((0, SSD_D_INNER), (SSD_D_INNER, SSD_D_INNER + SSD_CONV_DIM),
                 (SSD_D_INNER + SSD_CONV_DIM, SSD_PROJ_PAD)))
            xt = _ssd_mixer(
                z, xbc, dt, xt, ssd_conv_w[j], ssd_conv_b[j].reshape(1, SSD_CONV_DIM),
                _pad_cols(ssd_dt_bias[j].reshape(1, SSD_HEADS), SSD_DT_PAD),
                _pad_cols(ssd_a_log[j].reshape(1, SSD_HEADS), SSD_DT_PAD),
                jnp.repeat(ssd_d[j], SSD_HEAD_DIM).reshape(1, SSD_D_INNER),
                ssd_norm[j].reshape(1, SSD_D_INNER), expand, ssd_w_out[j].astype(BF16), mod5, i)
        else:
            w = dsa_w_in[j]
            w_in = jnp.concatenate(
                [_pad_cols(w[:, :DSA_KI_START + IDX_HEAD_DIM], DSA_WI_START),
                 _pad_cols(w[:, DSA_KI_START + IDX_HEAD_DIM:], 128)], axis=1).astype(BF16)
            q, k, v, qi, ki, wi = _dsa_inproj(
                xt, nw_mix, mod5, i, w_in, dsa_q_norm[j].reshape(1, ATT_HEAD_DIM),
                dsa_k_norm[j].reshape(1, ATT_HEAD_DIM))
            xt = _dsa_mixer(q, k, v, qi, ki, wi, xt, dsa_w_out[j].astype(BF16), mod5, i)

        w_route = _pad_cols(jnp.concatenate([moe_w_expert[i], moe_w_group[i]], axis=1), ROUTE_PAD).astype(BF16)
        b_route = _pad_cols(jnp.concatenate([moe_b_expert[i], moe_b_group[i]]).reshape(1, -1), ROUTE_PAD)
        xt = _moe(xt, norm_ffn[i].reshape(1, D_MODEL), mod5, i, w_route, b_route,
                  moe_w1[i].astype(BF16), moe_w3[i].astype(BF16), moe_w2[i].astype(BF16))
    return xt.reshape(BATCH, SEQ, D_MODEL)
```

```python
import functools

import jax
import jax.numpy as jnp
from jax import lax
from jax.experimental import pallas as pl
from jax.experimental.pallas import tpu as pltpu

F32 = jnp.float32
BF16 = jnp.bfloat16

D_MODEL = 1024
BATCH = 8
SEQ = 2048
TOKENS = BATCH * SEQ
EPS = 1e-6

SSD_D_INNER = 2048
SSD_HEAD_DIM = 64
SSD_HEADS = 32
SSD_GROUPS = 8
SSD_HEADS_PER_GROUP = 4
SSD_STATE = 128
SSD_CONV = 4
SSD_CHUNK = 128
SSD_GN = SSD_GROUPS * SSD_STATE
SSD_CONV_DIM = SSD_D_INNER + 2 * SSD_GN
SSD_GROUP_W = SSD_HEADS_PER_GROUP * SSD_HEAD_DIM
SSD_DT_PAD = 128
SSD_PROJ_PAD = SSD_D_INNER + SSD_CONV_DIM + SSD_DT_PAD
CONV_HALO = 8

ATT_HEADS = 16
ATT_KV_HEADS = 4
ATT_Q_PER_KV = 4
ATT_HEAD_DIM = 64
IDX_HEADS = 8
IDX_HEAD_DIM = 64
TOPK = 256
Q_BLOCK = 128
DSA_Q = ATT_HEADS * ATT_HEAD_DIM
DSA_KV = ATT_KV_HEADS * ATT_HEAD_DIM
DSA_QI = IDX_HEADS * IDX_HEAD_DIM
DSA_KI_START = DSA_Q + 2 * DSA_KV + DSA_QI
DSA_WI_START = DSA_KI_START + 128
DSA_PROJ_PAD = DSA_WI_START + 128

MOE_GROUPS = 4
MOE_EPG = 4
MOE_EXPERTS = 16
MOE_HIDDEN = 256
ROUTE_PAD = 128

VMEM_LIMIT = 56 * 1024 * 1024


def _sigmoid(v):
    return 1.0 / (1.0 + jnp.exp(-v))


def _silu(v):
    return v * _sigmoid(v)


def _split3(a):
    hi = a.astype(BF16)
    r = a - hi.astype(F32)
    mid = r.astype(BF16)
    lo = (r - mid.astype(F32)).astype(BF16)
    return hi, mid, lo


def _dot(a, b):
    return jnp.dot(a, b, preferred_element_type=F32)


def _dot_nt(a, b):
    return lax.dot_general(a, b, (((1,), (1,)), ((), ())), preferred_element_type=F32)


def _dot3_exact_rhs(a, m):
    hi, mid, lo = _split3(a)
    return _dot(hi, m) + _dot(mid, m) + _dot(lo, m)


def _dot3_exact_lhs(m, a):
    hi, mid, lo = _split3(a)
    return _dot(m, hi) + _dot(m, mid) + _dot(m, lo)


def _norm_mod(x, nw, scale, shift):
    ms = jnp.mean(x * x, axis=-1, keepdims=True)
    return x * lax.rsqrt(ms + EPS) * nw * (1.0 + scale) + shift


MOD_TN = 1536


def _mod_kernel(c_ref, w_ref, b_ref, o_ref):
    cond = _silu(c_ref[...]).astype(BF16)
    o_ref[...] = _dot(cond, w_ref[...].astype(BF16)) + b_ref[...]


def _modulation(c, ada_w, ada_b):
    depth = ada_w.shape[0]
    n = ada_w.shape[2]
    return pl.pallas_call(
        _mod_kernel,
        grid=(depth, n // MOD_TN),
        in_specs=[
            pl.BlockSpec((BATCH, D_MODEL), lambda i, j: (0, 0)),
            pl.BlockSpec((None, D_MODEL, MOD_TN), lambda i, j: (i, 0, j)),
            pl.BlockSpec((None, 1, MOD_TN), lambda i, j: (i, 0, j)),
        ],
        out_specs=pl.BlockSpec((None, BATCH, MOD_TN), lambda i, j: (i, 0, j)),
        out_shape=jax.ShapeDtypeStruct((depth, BATCH, n), F32),
        compiler_params=pltpu.CompilerParams(
            dimension_semantics=("arbitrary", "arbitrary"), vmem_limit_bytes=VMEM_LIMIT),
        name="adaln_mod",
    )(c, ada_w, ada_b.reshape(depth, 1, n))


def _mod_spec(layer, chunk, rows_per_batch_tile):
    return pl.BlockSpec((None, None, None, 1, D_MODEL),
                        lambda i, *_: (layer, i // rows_per_batch_tile, chunk, 0, 0))


INPROJ_TM = 256


def _inproj_kernel(x_ref, nw_ref, scale_ref, shift_ref, w_ref, *o_refs, col_slices):
    h = _norm_mod(x_ref[...], nw_ref[...], scale_ref[...], shift_ref[...]).astype(BF16)
    for o_ref, (lo, hi) in zip(o_refs, col_slices):
        o_ref[...] = _dot(h, w_ref[:, lo:hi])


def _inproj(x, nw, mod5, layer, scale_chunk, shift_chunk, w, col_slices):
    tm = INPROJ_TM
    tiles_per_batch = SEQ // tm
    n_pad = w.shape[1]
    return pl.pallas_call(
        functools.partial(_inproj_kernel, col_slices=col_slices),
        grid=(TOKENS // tm,),
        in_specs=[
            pl.BlockSpec((tm, D_MODEL), lambda i: (i, 0)),
            pl.BlockSpec((1, D_MODEL), lambda i: (0, 0)),
            _mod_spec(layer, scale_chunk, tiles_per_batch),
            _mod_spec(layer, shift_chunk, tiles_per_batch),
            pl.BlockSpec((D_MODEL, n_pad), lambda i: (0, 0)),
        ],
        out_specs=[pl.BlockSpec((tm, hi - lo), lambda i: (i, 0)) for lo, hi in col_slices],
        out_shape=[jax.ShapeDtypeStruct((TOKENS, hi - lo), F32) for lo, hi in col_slices],
        compiler_params=pltpu.CompilerParams(
            dimension_semantics=("arbitrary",), vmem_limit_bytes=VMEM_LIMIT),
        name="norm_inproj",
    )(x, nw, mod5, mod5, w)


CONV_COLS = 512


def _ssd_kernel(z_ref, xbc_ref, dt_ref, xres_ref, cw_ref, cb_ref, dtb_ref, alog_ref, de_ref, nw_ref,
                e_ref, wout_ref, gate_ref, o_ref, state_ref, ext_ref, act_ref, yn_ref):
    q = SSD_CHUNK
    c = pl.program_id(1)

    @pl.when(c == 0)
    def _():
        state_ref[...] = jnp.zeros_like(state_ref)
        ext_ref[0:CONV_HALO, :] = jnp.zeros((CONV_HALO, SSD_CONV_DIM), F32)

    @pl.when(c > 0)
    def _():
        ext_ref[0:CONV_HALO, :] = ext_ref[q:q + CONV_HALO, :]

    ext_ref[CONV_HALO:CONV_HALO + q, :] = xbc_ref[...]

    for s in range(SSD_CONV_DIM // CONV_COLS):
        cs = slice(s * CONV_COLS, (s + 1) * CONV_COLS)
        acc = cb_ref[:, cs] + cw_ref[SSD_CONV - 1:SSD_CONV, cs] * ext_ref[CONV_HALO:CONV_HALO + q, cs]
        for k in range(SSD_CONV - 1):
            off = CONV_HALO - (SSD_CONV - 1) + k
            acc = acc + cw_ref[k:k + 1, cs] * ext_ref[off:off + q, cs]
        act_ref[:, cs] = _silu(acc)

    dt_raw = dt_ref[...] + dtb_ref[...]
    dt = jnp.maximum(dt_raw, 0.0) + jnp.log1p(jnp.exp(-jnp.abs(dt_raw)))
    a = dt * (-jnp.exp(alog_ref[...]))
    row = lax.broadcasted_iota(jnp.int32, (q, q), 0)
    col = lax.broadcasted_iota(jnp.int32, (q, q), 1)
    tril = row >= col
    acs = _dot3_exact_lhs(tril.astype(BF16), a)
    acs_t = acs.T
    expand = e_ref[...]
    acs_e = _dot3_exact_rhs(acs, expand)
    dt_e = _dot3_exact_rhs(dt, expand)
    tot_e = acs_e[q - 1:q, :]
    decay_from_start = jnp.exp(acs_e)
    decay_to_end = jnp.exp(tot_e - acs_e)
    chunk_decay = jnp.exp(tot_e)

    lane_head = lax.broadcasted_iota(jnp.int32, (q, SSD_GROUP_W), 1) // SSD_HEAD_DIM
    for g in range(SSD_GROUPS):
        xs = act_ref[:, g * SSD_GROUP_W:(g + 1) * SSD_GROUP_W]
        gs = slice(g * SSD_GROUP_W, (g + 1) * SSD_GROUP_W)
        bm = act_ref[:, SSD_D_INNER + g * SSD_STATE:SSD_D_INNER + (g + 1) * SSD_STATE]
        cm = act_ref[:, SSD_D_INNER + SSD_GN + g * SSD_STATE:SSD_D_INNER + SSD_GN + (g + 1) * SSD_STATE]
        bm_t = bm.T.astype(BF16)
        cm_b = cm.astype(BF16)
        cb = _dot(cm_b, bm_t)
        xd = xs * dt_e[:, gs]
        ms = []
        xds = []
        for j in range(SSD_HEADS_PER_GROUP):
            h = g * SSD_HEADS_PER_GROUP + j
            seg = acs[:, h:h + 1] - acs_t[h:h + 1, :]
            dec = jnp.exp(jnp.where(tril, seg, -jnp.inf))
            ms.append((cb * dec).astype(BF16))
            xds.append(jnp.where(lane_head == j, xd, 0.0).astype(BF16))
        y_diag = _dot(jnp.concatenate(ms, axis=1), jnp.concatenate(xds, axis=0))
        prev = state_ref[g]
        y_off = _dot(cm_b, prev.astype(BF16)) * decay_from_start[:, gs]
        state_ref[g] = prev * chunk_decay[:, gs] + _dot(bm_t, (xd * decay_to_end[:, gs]).astype(BF16))
        y = y_diag + y_off + xs * de_ref[:, gs]
        y = y * _silu(z_ref[:, gs])
        y = y * lax.rsqrt(jnp.mean(y * y, axis=-1, keepdims=True) + EPS) * nw_ref[:, gs]
        yn_ref[:, gs] = y.astype(BF16)

    out = _dot(yn_ref[...], wout_ref[...])
    o_ref[...] = xres_ref[...] + gate_ref[...] * out


def _ssd_mixer(z, xbc, dt, x, conv_w, conv_b, dt_bias, a_log, d_e, norm_w, expand, w_out, mod5, layer):
    q = SSD_CHUNK
    nc = SEQ // q
    tok = lambda w: pl.BlockSpec((q, w), lambda b, c: (b * nc + c, 0))
    full = lambda r, w: pl.BlockSpec((r, w), lambda b, c: (0, 0))
    return pl.pallas_call(
        _ssd_kernel,
        grid=(BATCH, nc),
        in_specs=[
            tok(SSD_D_INNER), tok(SSD_CONV_DIM), tok(SSD_DT_PAD), tok(D_MODEL),
            full(SSD_CONV, SSD_CONV_DIM), full(1, SSD_CONV_DIM), full(1, SSD_DT_PAD), full(1, SSD_DT_PAD),
            full(1, SSD_D_INNER), full(1, SSD_D_INNER), full(SSD_DT_PAD, SSD_D_INNER),
            full(SSD_D_INNER, D_MODEL),
            pl.BlockSpec((None, None, None, 1, D_MODEL), lambda b, c: (layer, b, 2, 0, 0)),
        ],
        out_specs=tok(D_MODEL),
        out_shape=jax.ShapeDtypeStruct((TOKENS, D_MODEL), F32),
        scratch_shapes=[
            pltpu.VMEM((SSD_GROUPS, SSD_STATE, SSD_GROUP_W), F32),
            pltpu.VMEM((q + CONV_HALO, SSD_CONV_DIM), F32),
            pltpu.VMEM((q, SSD_CONV_DIM), F32),
            pltpu.VMEM((q, SSD_D_INNER), BF16),
        ],
        compiler_params=pltpu.CompilerParams(
            dimension_semantics=("arbitrary", "arbitrary"), vmem_limit_bytes=VMEM_LIMIT),
        name="ssd_mixer",
    )(z, xbc, dt, x, conv_w, conv_b, dt_bias, a_log, d_e, norm_w, expand, w_out, mod5)


DSA_KEY_TILE = 256
DSA_CLASSES = 4
DSA_BLOCKS_PER_CLASS = (SEQ // Q_BLOCK) // DSA_CLASSES
N_BISECT = 12
F32_MIN = float(jnp.finfo(jnp.float32).min)
LOG2E = 1.4426950408889634


def _count(mask):
    return jnp.sum(jnp.where(mask, 1.0, 0.0), axis=-1, keepdims=True)


def _select_topk(score_ref, q_pos, n_keys):
    kf = float(TOPK)
    small = (q_pos + 1) <= TOPK
    sc = score_ref[...]
    hi0 = jnp.max(sc, axis=-1, keepdims=True)
    lo0 = jnp.min(jnp.where(sc == -jnp.inf, jnp.inf, sc), axis=-1, keepdims=True)

    def bisect(_, carry):
        lo, hi = carry
        mid = lo + 0.5 * (hi - lo)
        ok = _count(score_ref[...] >= mid) >= kf
        return jnp.where(ok, mid, lo), jnp.where(ok, hi, mid)

    _, hi = lax.fori_loop(0, N_BISECT, bisect, (lo0, hi0))

    v0 = jnp.max(jnp.where(sc <= hi, sc, -jnp.inf), axis=-1, keepdims=True)
    c0 = _count(sc >= v0)
    pend0 = jnp.where((c0 >= kf) | small, 0.0, 1.0)

    def walk_cond(carry):
        return (carry[2] > 0.0) & (carry[3] < n_keys)

    def walk(carry):
        v, pend, _, it = carry
        s = score_ref[...]
        v2 = jnp.max(jnp.where(s < v, s, -jnp.inf), axis=-1, keepdims=True)
        c2 = _count(s >= v2)
        v = jnp.where(pend > 0.0, v2, v)
        pend = jnp.where(c2 >= kf, 0.0, pend)
        return v, pend, jnp.max(pend), it + 1

    v, _, _, _ = lax.while_loop(walk_cond, walk, (v0, pend0, jnp.max(pend0), jnp.int32(0)))
    thr = jnp.where(small, F32_MIN, v)

    key_pos = lax.broadcasted_iota(jnp.int32, (Q_BLOCK, n_keys), 1)
    gt = sc > thr
    eq = sc == thr
    need = kf - _count(gt)
    tie = jnp.where(_count(eq) > need, 1.0, 0.0)

    def search_cut():
        def body(_, carry):
            lo, hi = carry
            mid = (lo + hi) >> 1
            s = score_ref[...]
            ok = _count((s == thr) & (key_pos <= mid)) >= need
            return jnp.where(ok, lo, mid), jnp.where(ok, mid, hi)

        init = (jnp.full((Q_BLOCK, 1), -1, jnp.int32), jnp.full((Q_BLOCK, 1), n_keys - 1, jnp.int32))
        return lax.fori_loop(0, (n_keys - 1).bit_length() + 1, body, init)[1]

    cut = lax.cond(jnp.max(tie) > 0.0, search_cut, lambda: jnp.full((Q_BLOCK, 1), n_keys - 1, jnp.int32))
    score_ref[...] = jnp.where(gt | (eq & (key_pos <= cut)), 0.0, -jnp.inf)


def _dsa_inproj_kernel(x_ref, nw_ref, scale_ref, shift_ref, w_ref, qn_ref, kn_ref, seg_ref, segt_ref,
                       q_ref, k_ref, v_ref, qi_ref, ki_ref, wi_ref):
    hd = ATT_HEAD_DIM
    h = _norm_mod(x_ref[...], nw_ref[...], scale_ref[...], shift_ref[...]).astype(BF16)

    def head_norm(t, w):
        width = t.shape[1]
        ss = _dot((t * t).astype(BF16), seg_ref[0:width, :])
        r = lax.rsqrt(ss * (1.0 / hd) + EPS)
        r_hi = r.astype(BF16)
        r_lo = (r - r_hi.astype(F32)).astype(BF16)
        return t * (_dot(r_hi, segt_ref[:, 0:width]) + _dot(r_lo, segt_ref[:, 0:width])) * w

    q = head_norm(_dot(h, w_ref[:, 0:DSA_Q]), qn_ref[...] * (hd ** -0.5 * LOG2E))
    for n in range(ATT_HEADS):
        q_ref[n] = q[:, n * hd:(n + 1) * hd].astype(BF16)
    kv = _dot(h, w_ref[:, DSA_Q:DSA_Q + 2 * DSA_KV])
    k = head_norm(kv[:, 0:DSA_KV], kn_ref[...])
    for n in range(ATT_KV_HEADS):
        k_ref[n] = k[:, n * hd:(n + 1) * hd].astype(BF16)
        v_ref[n] = kv[:, DSA_KV + n * hd:DSA_KV + (n + 1) * hd].astype(BF16)
    qi = _dot(h, w_ref[:, DSA_Q + 2 * DSA_KV:DSA_KI_START])
    for n in range(IDX_HEADS):
        qi_ref[n] = qi[:, n * IDX_HEAD_DIM:(n + 1) * IDX_HEAD_DIM].astype(BF16)
    ki_ref[...] = _dot(h, w_ref[:, DSA_KI_START:DSA_KI_START + IDX_HEAD_DIM]).astype(BF16)
    wi_ref[...] = _dot(h, w_ref[:, DSA_WI_START:DSA_WI_START + IDX_HEADS]) * ((IDX_HEADS * IDX_HEAD_DIM) ** -0.5)


def _dsa_inproj(x, nw, mod5, layer, w, q_norm, k_norm):
    tm = INPROJ_TM
    tiles_per_batch = SEQ // tm
    heads = lambda n: pl.BlockSpec((n, tm, ATT_HEAD_DIM), lambda i: (0, i, 0))
    head_of = jnp.arange(DSA_Q, dtype=jnp.int32) // ATT_HEAD_DIM
    seg = (head_of[:, None] == jnp.arange(128, dtype=jnp.int32)[None, :]).astype(BF16)
    q_norm = jnp.tile(q_norm, (1, ATT_HEADS))
    k_norm = jnp.tile(k_norm, (1, ATT_KV_HEADS))
    return pl.pallas_call(
        _dsa_inproj_kernel,
        grid=(TOKENS // tm,),
        in_specs=[
            pl.BlockSpec((tm, D_MODEL), lambda i: (i, 0)),
            pl.BlockSpec((1, D_MODEL), lambda i: (0, 0)),
            _mod_spec(layer, 1, tiles_per_batch),
            _mod_spec(layer, 0, tiles_per_batch),
            pl.BlockSpec((D_MODEL, DSA_PROJ_PAD), lambda i: (0, 0)),
            pl.BlockSpec((1, DSA_Q), lambda i: (0, 0)),
            pl.BlockSpec((1, DSA_KV), lambda i: (0, 0)),
            pl.BlockSpec((DSA_Q, 128), lambda i: (0, 0)),
            pl.BlockSpec((128, DSA_Q), lambda i: (0, 0)),
        ],
        out_specs=[heads(ATT_HEADS), heads(ATT_KV_HEADS), heads(ATT_KV_HEADS), heads(IDX_HEADS),
                   pl.BlockSpec((tm, IDX_HEAD_DIM), lambda i: (i, 0)),
                   pl.BlockSpec((tm, IDX_HEADS), lambda i: (i, 0))],
        out_shape=[jax.ShapeDtypeStruct((ATT_HEADS, TOKENS, ATT_HEAD_DIM), BF16),
                   jax.ShapeDtypeStruct((ATT_KV_HEADS, TOKENS, ATT_HEAD_DIM), BF16),
                   jax.ShapeDtypeStruct((ATT_KV_HEADS, TOKENS, ATT_HEAD_DIM), BF16),
                   jax.ShapeDtypeStruct((IDX_HEADS, TOKENS, IDX_HEAD_DIM), BF16),
                   jax.ShapeDtypeStruct((TOKENS, IDX_HEAD_DIM), BF16),
                   jax.ShapeDtypeStruct((TOKENS, IDX_HEADS), F32)],
        compiler_params=pltpu.CompilerParams(
            dimension_semantics=("arbitrary",), vmem_limit_bytes=VMEM_LIMIT),
        name="dsa_inproj",
    )(x, nw, mod5, mod5, w, q_norm, k_norm, seg, seg.T)


def _dsa_kernel(q_ref, k_ref, v_ref, qi_ref, ki_ref, wi_ref, xres_ref, wout_ref, gate_ref, o_ref,
                score_ref, ocat_ref, *, n_keys, first_block):
    hd = ATT_HEAD_DIM
    q_pos = (first_block + pl.program_id(1)) * Q_BLOCK + lax.broadcasted_iota(jnp.int32, (Q_BLOCK, 1), 0)

    wi = wi_ref[...]
    qi = qi_ref[...].reshape(IDX_HEADS * Q_BLOCK, IDX_HEAD_DIM)
    for kt in range(n_keys // DSA_KEY_TILE):
        ks = slice(kt * DSA_KEY_TILE, (kt + 1) * DSA_KEY_TILE)
        raw = _dot_nt(qi, ki_ref[ks, :])
        acc = jnp.zeros((Q_BLOCK, DSA_KEY_TILE), F32)
        for n in range(IDX_HEADS):
            acc = acc + wi[:, n:n + 1] * jnp.maximum(raw[n * Q_BLOCK:(n + 1) * Q_BLOCK, :], 0.0)
        key_pos = kt * DSA_KEY_TILE + lax.broadcasted_iota(jnp.int32, (Q_BLOCK, DSA_KEY_TILE), 1)
        score_ref[:, ks] = jnp.where(key_pos <= q_pos, acc, -jnp.inf)

    _select_topk(score_ref, q_pos, n_keys)
    bias = score_ref[...][None, :, :]

    for n in range(ATT_KV_HEADS):
        q4 = q_ref[n * ATT_Q_PER_KV:(n + 1) * ATT_Q_PER_KV].reshape(ATT_Q_PER_KV * Q_BLOCK, hd)
        s = _dot_nt(q4, k_ref[n]).reshape(ATT_Q_PER_KV, Q_BLOCK, n_keys) + bias
        p = jnp.exp2(s - jnp.max(s, axis=-1, keepdims=True))
        denom = jnp.sum(p, axis=-1, keepdims=True).reshape(ATT_Q_PER_KV * Q_BLOCK, 1)
        o = _dot(p.reshape(ATT_Q_PER_KV * Q_BLOCK, n_keys).astype(BF16), v_ref[n]) * (1.0 / denom)
        for g in range(ATT_Q_PER_KV):
            col = (n * ATT_Q_PER_KV + g) * hd
            ocat_ref[:, col:col + hd] = o[g * Q_BLOCK:(g + 1) * Q_BLOCK, :]
    out = _dot(ocat_ref[...].astype(BF16), wout_ref[...])
    o_ref[...] = xres_ref[...] + gate_ref[...] * out


def _dsa_mixer(q, k, v, qi, ki, wi, x, w_out, mod5, layer):
    nb = SEQ // Q_BLOCK
    k4 = k.reshape(ATT_KV_HEADS, BATCH, SEQ, ATT_HEAD_DIM)
    v4 = v.reshape(ATT_KV_HEADS, BATCH, SEQ, ATT_HEAD_DIM)
    ki3 = ki.reshape(BATCH, SEQ, IDX_HEAD_DIM)
    for cls in range(DSA_CLASSES):
        n_keys = (cls + 1) * (SEQ // DSA_CLASSES)
        first_block = cls * DSA_BLOCKS_PER_CLASS
        row = lambda b, i, fb=first_block: b * nb + fb + i
        heads = lambda n: pl.BlockSpec((n, Q_BLOCK, ATT_HEAD_DIM), lambda b, i: (0, row(b, i), 0))
        keys = pl.BlockSpec((ATT_KV_HEADS, None, n_keys, ATT_HEAD_DIM), lambda b, i: (0, b, 0, 0))
        x = pl.pallas_call(
            functools.partial(_dsa_kernel, n_keys=n_keys, first_block=first_block),
            grid=(BATCH, DSA_BLOCKS_PER_CLASS),
            in_specs=[
                heads(ATT_HEADS), keys, keys, heads(IDX_HEADS),
                pl.BlockSpec((None, n_keys, IDX_HEAD_DIM), lambda b, i: (b, 0, 0)),
                pl.BlockSpec((Q_BLOCK, IDX_HEADS), lambda b, i: (row(b, i), 0)),
                pl.BlockSpec((Q_BLOCK, D_MODEL), lambda b, i: (row(b, i), 0)),
                pl.BlockSpec((DSA_Q, D_MODEL), lambda b, i: (0, 0)),
                pl.BlockSpec((None, None, None, 1, D_MODEL), lambda b, i: (layer, b, 2, 0, 0)),
            ],
            out_specs=pl.BlockSpec((Q_BLOCK, D_MODEL), lambda b, i: (row(b, i), 0)),
            out_shape=jax.ShapeDtypeStruct((TOKENS, D_MODEL), F32),
            scratch_shapes=[
                pltpu.VMEM((Q_BLOCK, n_keys), F32),
                pltpu.VMEM((Q_BLOCK, D_MODEL), F32),
            ],
            input_output_aliases={6: 0},
            compiler_params=pltpu.CompilerParams(
                dimension_semantics=("arbitrary", "arbitrary"), vmem_limit_bytes=VMEM_LIMIT),
            name=f"dsa_mixer_c{cls}",
        )(q, k4, v4, qi, ki3, wi, x, w_out, mod5)
    return x


MOE_PAIRS = MOE_EPG * (MOE_EPG - 1) // 2
MOE_CLASSES = MOE_GROUPS * MOE_PAIRS
PAIR_LO = (0, 0, 0, 1, 1, 2)
PAIR_HI = (1, 2, 3, 2, 3, 3)
ROUTE_TM = 1024
META_W = 128
META_CLASS, META_RANK, META_WLO, META_WHI = 0, 1, 2, 3
TILE_ROWS, TILE_LANES = 8, 128
H_WORDS = D_MODEL // 2
H_SUBLANES = H_WORDS // TILE_LANES
SORT_BLOCK = 256
N_SORT_BLOCKS = TOKENS // SORT_BLOCK
MAX_ITEMS = N_SORT_BLOCKS + MOE_CLASSES
PERMUTE_TM = 256
COMBINE_TM = 256


def _route_kernel(x_ref, nw_ref, scale_ref, shift_ref, wr_ref, br_ref, pay_ref, cls_ref, rank_ref, cnt_ref,
                  carry_ref):
    tm = ROUTE_TM

    @pl.when(pl.program_id(0) == 0)
    def _():
        carry_ref[...] = jnp.zeros_like(carry_ref)

    h = _norm_mod(x_ref[...], nw_ref[...], scale_ref[...], shift_ref[...])
    logits = _dot(h.astype(BF16), wr_ref[...]) + br_ref[...]
    lane = lax.broadcasted_iota(jnp.int32, logits.shape, 1)
    neg = -jnp.inf
    big = jnp.int32(ROUTE_PAD)
    is_group = (lane >= MOE_EXPERTS) & (lane < MOE_EXPERTS + MOE_GROUPS)
    gl = jnp.where(is_group, logits, neg)
    g_max = jnp.max(gl, axis=-1, keepdims=True)
    g_idx = jnp.min(jnp.where(gl == g_max, lane - MOE_EXPERTS, big), axis=-1, keepdims=True)
    g_val = 1.0 / jnp.sum(jnp.exp(gl - g_max), axis=-1, keepdims=True)
    in_group = (lane < MOE_EXPERTS) & ((lane // MOE_EPG) == g_idx)
    el = jnp.where(in_group, logits, neg)
    m1 = jnp.max(el, axis=-1, keepdims=True)
    i1 = jnp.min(jnp.where(el == m1, lane, big), axis=-1, keepdims=True)
    el2 = jnp.where(lane == i1, neg, el)
    m2 = jnp.max(el2, axis=-1, keepdims=True)
    i2 = jnp.min(jnp.where(el2 == m2, lane, big), axis=-1, keepdims=True)
    r = jnp.exp(m2 - m1)
    w_top1 = g_val / (1.0 + r)
    w_top2 = g_val * r / (1.0 + r)

    lo = jnp.minimum(i1, i2) - g_idx * MOE_EPG
    hi = jnp.maximum(i1, i2) - g_idx * MOE_EPG
    pair = (lo * (2 * MOE_EPG - 1 - lo)) // 2 + (hi - lo - 1)
    cls = g_idx * MOE_PAIRS + pair
    w_lo = jnp.where(i1 < i2, w_top1, w_top2)
    w_hi = jnp.where(i1 < i2, w_top2, w_top1)

    onehot = lane == cls
    row = lax.broadcasted_iota(jnp.int32, (tm, tm), 0)
    col = lax.broadcasted_iota(jnp.int32, (tm, tm), 1)
    before = _dot((row > col).astype(BF16), onehot.astype(BF16)) + carry_ref[...]
    rank = jnp.sum(jnp.where(onehot, before, 0.0), axis=-1, keepdims=True)
    carry_ref[...] += jnp.sum(jnp.where(onehot, 1.0, 0.0), axis=0, keepdims=True)
    cnt_ref[...] = carry_ref[...]

    meta = jnp.where(lane == META_CLASS, cls.astype(F32),
                     jnp.where(lane == META_RANK, rank,
                               jnp.where(lane == META_WLO, w_lo, jnp.where(lane == META_WHI, w_hi, 0.0))))

    eye = lax.broadcasted_iota(jnp.int32, (128, 128), 0) == lax.broadcasted_iota(jnp.int32, (128, 128), 1)

    def column_to_rows(v):
        return jnp.concatenate([jnp.sum(jnp.where(eye, v[b * 128:(b + 1) * 128, :], 0.0), axis=0, keepdims=True)
                                for b in range(tm // 128)], axis=0)

    cls_ref[...] = column_to_rows(cls.astype(F32)).astype(jnp.int32)
    rank_ref[...] = column_to_rows(rank).astype(jnp.int32)
    words = pltpu.pack_elementwise([h[:, 0:H_WORDS], h[:, H_WORDS:D_MODEL]], packed_dtype=BF16)
    for s in range(H_SUBLANES):
        pay_ref[pl.ds(s, tm, stride=TILE_ROWS), :] = words[:, s * TILE_LANES:(s + 1) * TILE_LANES]
    pay_ref[pl.ds(H_SUBLANES, tm, stride=TILE_ROWS), :] = lax.bitcast_convert_type(meta, jnp.int32)
    for s in range(H_SUBLANES + 1, TILE_ROWS):
        pay_ref[pl.ds(s, tm, stride=TILE_ROWS), :] = jnp.zeros((tm, TILE_LANES), jnp.int32)


def _moe_route(x, nw, mod5, layer, w_route, b_route):
    tm = ROUTE_TM
    tiles_per_batch = SEQ // tm
    return pl.pallas_call(
        _route_kernel,
        grid=(TOKENS // tm,),
        in_specs=[
            pl.BlockSpec((tm, D_MODEL), lambda i: (i, 0)),
            pl.BlockSpec((1, D_MODEL), lambda i: (0, 0)),
            _mod_spec(layer, 4, tiles_per_batch),
            _mod_spec(layer, 3, tiles_per_batch),
            pl.BlockSpec((D_MODEL, ROUTE_PAD), lambda i: (0, 0)),
            pl.BlockSpec((1, ROUTE_PAD), lambda i: (0, 0)),
        ],
        out_specs=[pl.BlockSpec((tm * TILE_ROWS, TILE_LANES), lambda i: (i, 0)),
                   pl.BlockSpec((tm // 128, 128), lambda i: (i, 0)),
                   pl.BlockSpec((tm // 128, 128), lambda i: (i, 0)),
                   pl.BlockSpec((1, ROUTE_PAD), lambda i: (0, 0))],
        out_shape=[jax.ShapeDtypeStruct((TOKENS * TILE_ROWS, TILE_LANES), jnp.int32),
                   jax.ShapeDtypeStruct((TOKENS // 128, 128), jnp.int32),
                   jax.ShapeDtypeStruct((TOKENS // 128, 128), jnp.int32),
                   jax.ShapeDtypeStruct((1, ROUTE_PAD), F32)],
        scratch_shapes=[pltpu.VMEM((1, ROUTE_PAD), F32)],
        compiler_params=pltpu.CompilerParams(
            dimension_semantics=("arbitrary",), vmem_limit_bytes=VMEM_LIMIT),
        name="moe_route",
    )(x, nw, mod5, mod5, w_route, b_route)


def _permute_kernel(pos_ref, src_ref, dst_hbm, stage_ref, sem):
    i = pl.program_id(0)
    slot = i % 2
    rows = PERMUTE_TM * TILE_ROWS

    def slot_wait(s):
        whole = stage_ref.at[pl.ds(s * rows, rows), :]
        pltpu.make_async_copy(whole, whole, sem.at[s]).wait()

    @pl.when(i >= 2)
    def _():
        slot_wait(slot)

    base = pl.multiple_of(slot * rows, rows)
    stage_ref[pl.ds(base, rows), :] = src_ref[...]

    def issue(r, _):
        pltpu.make_async_copy(stage_ref.at[pl.ds(base + r * TILE_ROWS, TILE_ROWS), :],
                              dst_hbm.at[pos_ref[i * PERMUTE_TM + r]], sem.at[slot]).start()
        return 0

    lax.fori_loop(0, PERMUTE_TM, issue, 0, unroll=16)

    @pl.when(i == pl.num_programs(0) - 1)
    def _():
        slot_wait(1 - slot)
        slot_wait(slot)


def _moe_permute(pos, payload):
    rows = PERMUTE_TM * TILE_ROWS
    return pl.pallas_call(
        _permute_kernel,
        grid_spec=pltpu.PrefetchScalarGridSpec(
            num_scalar_prefetch=1, grid=(TOKENS // PERMUTE_TM,),
            in_specs=[pl.BlockSpec((rows, TILE_LANES), lambda i, pos: (i, 0))],
            out_specs=pl.BlockSpec(memory_space=pl.ANY),
            scratch_shapes=[pltpu.VMEM((2 * rows, TILE_LANES), jnp.int32), pltpu.SemaphoreType.DMA((2,))]),
        out_shape=jax.ShapeDtypeStruct((TOKENS, TILE_ROWS, TILE_LANES), jnp.int32),
        compiler_params=pltpu.CompilerParams(
            dimension_semantics=("arbitrary",), vmem_limit_bytes=VMEM_LIMIT),
        name="moe_permute",
    )(pos, payload)


def _experts_kernel(blk_ref, cls_ref, elo_ref, ehi_ref, first_ref, last_ref, valid_ref,
                    pay_ref, w1lo_ref, w3lo_ref, w2lo_ref, w1hi_ref, w3hi_ref, w2hi_ref, o_ref, acc_ref):
    k = pl.program_id(0)
    rows = SORT_BLOCK

    @pl.when(valid_ref[k] == 1)
    def _():
        def sublane(s):
            return pay_ref[pl.ds(s, rows, stride=TILE_ROWS), :]

        halves = [[pltpu.unpack_elementwise(sublane(s), index=i, packed_dtype=BF16, unpacked_dtype=F32)
                   for s in range(H_SUBLANES)] for i in range(2)]
        hb = jnp.concatenate(halves[0] + halves[1], axis=1).astype(BF16)
        meta = lax.bitcast_convert_type(sublane(H_SUBLANES), F32)
        mine = meta[:, META_CLASS:META_CLASS + 1] == cls_ref[k].astype(F32)
        w_lo = jnp.where(mine, meta[:, META_WLO:META_WLO + 1], 0.0)
        w_hi = jnp.where(mine, meta[:, META_WHI:META_WHI + 1], 0.0)
        hid_lo = _silu(_dot(hb, w1lo_ref[...])) * _dot(hb, w3lo_ref[...]) * w_lo
        hid_hi = _silu(_dot(hb, w1hi_ref[...])) * _dot(hb, w3hi_ref[...]) * w_hi
        y = _dot(hid_lo.astype(BF16), w2lo_ref[...]) + _dot(hid_hi.astype(BF16), w2hi_ref[...])

        @pl.when(first_ref[k] == 1)
        def _():
            acc_ref[...] = y

        @pl.when(first_ref[k] == 0)
        def _():
            acc_ref[...] += y

        @pl.when(last_ref[k] == 1)
        def _():
            for s in range(TILE_ROWS):
                o_ref[pl.ds(s, rows, stride=TILE_ROWS), :] = acc_ref[:, s * TILE_LANES:(s + 1) * TILE_LANES]


def _moe_experts(items, payload_sorted, w1, w3, w2):
    blk, cls, elo, ehi, first, last, valid = items
    tiles = pl.BlockSpec((SORT_BLOCK * TILE_ROWS, TILE_LANES), lambda k, blk, *_: (blk[k], 0))
    w_in = lambda which: pl.BlockSpec(
        (None, D_MODEL, MOE_HIDDEN), lambda k, blk, cls, elo, ehi, *_: ((elo, ehi)[which][k], 0, 0))
    w_out = lambda which: pl.BlockSpec(
        (None, MOE_HIDDEN, D_MODEL), lambda k, blk, cls, elo, ehi, *_: ((elo, ehi)[which][k], 0, 0))
    return pl.pallas_call(
        _experts_kernel,
        grid_spec=pltpu.PrefetchScalarGridSpec(
            num_scalar_prefetch=7, grid=(MAX_ITEMS,),
            in_specs=[tiles, w_in(0), w_in(0), w_out(0), w_in(1), w_in(1), w_out(1)],
            out_specs=tiles,
            scratch_shapes=[pltpu.VMEM((SORT_BLOCK, D_MODEL), F32)]),
        out_shape=jax.ShapeDtypeStruct((TOKENS * TILE_ROWS, TILE_LANES), F32),
        compiler_params=pltpu.CompilerParams(
            dimension_semantics=("arbitrary",), vmem_limit_bytes=VMEM_LIMIT),
        name="moe_experts",
    )(blk, cls, elo, ehi, first, last, valid, payload_sorted, w1, w3, w2, w1, w3, w2)


def _combine_kernel(pos_ref, x_ref, gate_ref, y_hbm, o_ref, buf_ref, sem):
    tm = COMBINE_TM
    i = pl.program_id(0)
    n = pl.num_programs(0)

    def gather(tile, slot):
        def issue(r, _):
            pltpu.make_async_copy(y_hbm.at[pos_ref[tile * tm + r]],
                                  buf_ref.at[pl.ds((slot * tm + r) * TILE_ROWS, TILE_ROWS), :], sem.at[slot]).start()
            return 0
        lax.fori_loop(0, tm, issue, 0, unroll=16)

    @pl.when(i == 0)
    def _():
        gather(0, 0)

    @pl.when(i + 1 < n)
    def _():
        gather(i + 1, (i + 1) % 2)

    slot = i % 2
    base = slot * tm * TILE_ROWS
    whole_slot = buf_ref.at[pl.ds(base, tm * TILE_ROWS), :]
    pltpu.make_async_copy(whole_slot, whole_slot, sem.at[slot]).wait()
    y = jnp.concatenate([buf_ref[pl.ds(base + s, tm, stride=TILE_ROWS), :] for s in range(TILE_ROWS)], axis=1)
    o_ref[...] = x_ref[...] + gate_ref[...] * y


def _moe_combine(pos, x, mod5, layer, y_sorted):
    tm = COMBINE_TM
    tiles_per_batch = SEQ // tm
    return pl.pallas_call(
        _combine_kernel,
        grid_spec=pltpu.PrefetchScalarGridSpec(
            num_scalar_prefetch=1, grid=(TOKENS // tm,),
            in_specs=[
                pl.BlockSpec((tm, D_MODEL), lambda i, pos: (i, 0)),
                pl.BlockSpec((None, None, None, 1, D_MODEL),
                             lambda i, pos: (layer, i // tiles_per_batch, 5, 0, 0)),
                pl.BlockSpec(memory_space=pl.ANY),
            ],
            out_specs=pl.BlockSpec((tm, D_MODEL), lambda i, pos: (i, 0)),
            scratch_shapes=[pltpu.VMEM((2 * tm * TILE_ROWS, TILE_LANES), F32), pltpu.SemaphoreType.DMA((2,))]),
        out_shape=jax.ShapeDtypeStruct((TOKENS, D_MODEL), F32),
        compiler_params=pltpu.CompilerParams(
            dimension_semantics=("arbitrary",), vmem_limit_bytes=VMEM_LIMIT),
        name="moe_combine",
    )(pos, x, mod5, y_sorted)


def _moe_plan(cls, rank, counts):
    count = counts[0, :MOE_CLASSES].astype(jnp.int32)
    ends = jnp.cumsum(count)
    starts = ends - count
    pos = (starts[cls] + rank).reshape(TOKENS)

    first_blk = starts // SORT_BLOCK
    n_items = jnp.where(count > 0, (ends - 1) // SORT_BLOCK - first_blk + 1, 0)
    item_end = jnp.cumsum(n_items)
    item_start = item_end - n_items
    k = jnp.arange(MAX_ITEMS, dtype=jnp.int32)
    valid = k < item_end[-1]
    kc = jnp.minimum(k, item_end[-1] - 1)
    icls = jnp.searchsorted(item_end, kc, side="right").astype(jnp.int32)
    blk = first_blk[icls] + (kc - item_start[icls])
    first = jnp.concatenate([jnp.ones((1,), jnp.int32), (blk[1:] != blk[:-1]).astype(jnp.int32)])
    last = jnp.concatenate([(blk[1:] != blk[:-1]) | ~valid[1:], jnp.ones((1,), bool)]).astype(jnp.int32)
    group = icls // MOE_PAIRS
    pair = icls % MOE_PAIRS
    elo = group * MOE_EPG + jnp.asarray(PAIR_LO, jnp.int32)[pair]
    ehi = group * MOE_EPG + jnp.asarray(PAIR_HI, jnp.int32)[pair]
    return pos, (blk, icls, elo, ehi, first * valid, last * valid, valid.astype(jnp.int32))


def _moe(x, nw, mod5, layer, w_route, b_route, w1, w3, w2):
    payload, cls, rank, counts = _moe_route(x, nw, mod5, layer, w_route, b_route)
    pos, items = _moe_plan(cls, rank, counts)
    tiles = (TOKENS, TILE_ROWS, TILE_LANES)
    sorted_payload = _moe_permute(pos, payload).reshape(TOKENS * TILE_ROWS, TILE_LANES)
    y_sorted = _moe_experts(items, sorted_payload, w1, w3, w2)
    return _moe_combine(pos, x, mod5, layer, y_sorted.reshape(tiles))


def _pad_cols(w, width):
    return jnp.pad(w, ((0, 0), (0, width - w.shape[1])))


def kernel(x, c, ada_w, ada_b, norm_mix, norm_ffn, ssd_w_in, ssd_conv_w, ssd_conv_b, ssd_dt_bias,
           ssd_a_log, ssd_d, ssd_norm, ssd_w_out, dsa_w_in, dsa_q_norm, dsa_k_norm, dsa_w_out,
           moe_w_group, moe_b_group, moe_w_expert, moe_b_expert, moe_w1, moe_w3, moe_w2):
    depth = ada_w.shape[0]
    xt = x.reshape(TOKENS, D_MODEL)
    mod = _modulation(c, ada_w, ada_b)
    mod5 = mod.reshape(depth, BATCH, 6, 1, D_MODEL)

    head_of_col = jnp.arange(SSD_D_INNER, dtype=jnp.int32) // SSD_HEAD_DIM
    expand = (jnp.arange(SSD_DT_PAD, dtype=jnp.int32)[:, None] == head_of_col[None, :]).astype(BF16)

    for i in range(depth):
        j = i // 2
        nw_mix = norm_mix[i].reshape(1, D_MODEL)
        if i % 2 == 0:
            w_in = _pad_cols(ssd_w_in[j], SSD_PROJ_PAD).astype(BF16)
            z, xbc, dt = _inproj(
                xt, nw_mix, mod5, i, 1, 0, w_in,
                ((0, SSD_D_INNER), (SSD_D_INNER, SSD_D_INNER + SSD_CONV_DIM),
                 (SSD_D_INNER + SSD_CONV_DIM, SSD_PROJ_PAD)))
            xt = _ssd_mixer(
                z, xbc, dt, xt, ssd_conv_w[j], ssd_conv_b[j].reshape(1, SSD_CONV_DIM),
                _pad_cols(ssd_dt_bias[j].reshape(1, SSD_HEADS), SSD_DT_PAD),
                _pad_cols(ssd_a_log[j].reshape(1, SSD_HEADS), SSD_DT_PAD),
                jnp.repeat(ssd_d[j], SSD_HEAD_DIM).reshape(1, SSD_D_INNER),
                ssd_norm[j].reshape(1, SSD_D_INNER), expand, ssd_w_out[j].astype(BF16), mod5, i)
        else:
            w = dsa_w_in[j]
            w_in = jnp.concatenate(
                [_pad_cols(w[:, :DSA_KI_START + IDX_HEAD_DIM], DSA_WI_START),
                 _pad_cols(w[:, DSA_KI_START + IDX_HEAD_DIM:], 128)], axis=1).astype(BF16)
            q, k, v, qi, ki, wi = _dsa_inproj(
                xt, nw_mix, mod5, i, w_in, dsa_q_norm[j].reshape(1, ATT_HEAD_DIM),
                dsa_k_norm[j].reshape(1, ATT_HEAD_DIM))
            xt = _dsa_mixer(q, k, v, qi, ki, wi, xt, dsa_w_out[j].astype(BF16), mod5, i)

        w_route = _pad_cols(jnp.concatenate([moe_w_expert[i], moe_w_group[i]], axis=1), ROUTE_PAD).astype(BF16)
        b_route = _pad_cols(jnp.concatenate([moe_b_expert[i], moe_b_group[i]]).reshape(1, -1), ROUTE_PAD)
        xt = _moe(xt, norm_ffn[i].reshape(1, D_MODEL), mod5, i, w_route, b_route,
                  moe_w1[i].astype(BF16), moe_w3[i].astype(BF16), moe_w2[i].astype(BF16))
    return xt.reshape(BATCH, SEQ, D_MODEL)
```

```python
import functools

import jax
import jax.numpy as jnp
from jax import lax
from jax.experimental import pallas as pl
from jax.experimental.pallas import tpu as pltpu

F32 = jnp.float32
BF16 = jnp.bfloat16

D_MODEL = 1024
BATCH = 8
SEQ = 2048
TOKENS = BATCH * SEQ
EPS = 1e-6

SSD_D_INNER = 2048
SSD_HEAD_DIM = 64
SSD_HEADS = 32
SSD_GROUPS = 8
SSD_HEADS_PER_GROUP = 4
SSD_STATE = 128
SSD_CONV = 4
SSD_CHUNK = 128
SSD_GN = SSD_GROUPS * SSD_STATE
SSD_CONV_DIM = SSD_D_INNER + 2 * SSD_GN
SSD_GROUP_W = SSD_HEADS_PER_GROUP * SSD_HEAD_DIM
SSD_DT_PAD = 128
SSD_PROJ_PAD = SSD_D_INNER + SSD_CONV_DIM + SSD_DT_PAD
CONV_HALO = 8

ATT_HEADS = 16
ATT_KV_HEADS = 4
ATT_Q_PER_KV = 4
ATT_HEAD_DIM = 64
IDX_HEADS = 8
IDX_HEAD_DIM = 64
TOPK = 256
Q_BLOCK = 128
DSA_Q = ATT_HEADS * ATT_HEAD_DIM
DSA_KV = ATT_KV_HEADS * ATT_HEAD_DIM
DSA_QI = IDX_HEADS * IDX_HEAD_DIM
DSA_KI_START = DSA_Q + 2 * DSA_KV + DSA_QI
DSA_WI_START = DSA_KI_START + 128
DSA_PROJ_PAD = DSA_WI_START + 128

MOE_GROUPS = 4
MOE_EPG = 4
MOE_EXPERTS = 16
MOE_HIDDEN = 256
ROUTE_PAD = 128

VMEM_LIMIT = 56 * 1024 * 1024


def _sigmoid(v):
    return 1.0 / (1.0 + jnp.exp(-v))


def _silu(v):
    return v * _sigmoid(v)


def _split3(a):
    hi = a.astype(BF16)
    r = a - hi.astype(F32)
    mid = r.astype(BF16)
    lo = (r - mid.astype(F32)).astype(BF16)
    return hi, mid, lo


def _dot(a, b):
    return jnp.dot(a, b, preferred_element_type=F32)


def _dot_nt(a, b):
    return lax.dot_general(a, b, (((1,), (1,)), ((), ())), preferred_element_type=F32)


def _dot3_exact_rhs(a, m):
    hi, mid, lo = _split3(a)
    return _dot(hi, m) + _dot(mid, m) + _dot(lo, m)


def _dot3_exact_lhs(m, a):
    hi, mid, lo = _split3(a)
    return _dot(m, hi) + _dot(m, mid) + _dot(m, lo)


def _norm_mod(x, nw, scale, shift):
    ms = jnp.mean(x * x, axis=-1, keepdims=True)
    return x * lax.rsqrt(ms + EPS) * nw * (1.0 + scale) + shift


MOD_TN = 1536


def _mod_kernel(c_ref, w_ref, b_ref, o_ref):
    cond = _silu(c_ref[...]).astype(BF16)
    o_ref[...] = _dot(cond, w_ref[...].astype(BF16)) + b_ref[...]


def _modulation(c, ada_w, ada_b):
    depth = ada_w.shape[0]
    n = ada_w.shape[2]
    return pl.pallas_call(
        _mod_kernel,
        grid=(depth, n // MOD_TN),
        in_specs=[
            pl.BlockSpec((BATCH, D_MODEL), lambda i, j: (0, 0)),
            pl.BlockSpec((None, D_MODEL, MOD_TN), lambda i, j: (i, 0, j)),
            pl.BlockSpec((None, 1, MOD_TN), lambda i, j: (i, 0, j)),
        ],
        out_specs=pl.BlockSpec((None, BATCH, MOD_TN), lambda i, j: (i, 0, j)),
        out_shape=jax.ShapeDtypeStruct((depth, BATCH, n), F32),
        compiler_params=pltpu.CompilerParams(
            dimension_semantics=("arbitrary", "arbitrary"), vmem_limit_bytes=VMEM_LIMIT),
        name="adaln_mod",
    )(c, ada_w, ada_b.reshape(depth, 1, n))


def _mod_spec(layer, chunk, rows_per_batch_tile):
    return pl.BlockSpec((None, None, None, 1, D_MODEL),
                        lambda i, *_: (layer, i // rows_per_batch_tile, chunk, 0, 0))


INPROJ_TM = 256


def _inproj_kernel(x_ref, nw_ref, scale_ref, shift_ref, w_ref, *o_refs, col_slices):
    h = _norm_mod(x_ref[...], nw_ref[...], scale_ref[...], shift_ref[...]).astype(BF16)
    for o_ref, (lo, hi) in zip(o_refs, col_slices):
        o_ref[...] = _dot(h, w_ref[:, lo:hi])


def _inproj(x, nw, mod5, layer, scale_chunk, shift_chunk, w, col_slices):
    tm = INPROJ_TM
    tiles_per_batch = SEQ // tm
    n_pad = w.shape[1]
    return pl.pallas_call(
        functools.partial(_inproj_kernel, col_slices=col_slices),
        grid=(TOKENS // tm,),
        in_specs=[
            pl.BlockSpec((tm, D_MODEL), lambda i: (i, 0)),
            pl.BlockSpec((1, D_MODEL), lambda i: (0, 0)),
            _mod_spec(layer, scale_chunk, tiles_per_batch),
            _mod_spec(layer, shift_chunk, tiles_per_batch),
            pl.BlockSpec((D_MODEL, n_pad), lambda i: (0, 0)),
        ],
        out_specs=[pl.BlockSpec((tm, hi - lo), lambda i: (i, 0)) for lo, hi in col_slices],
        out_shape=[jax.ShapeDtypeStruct((TOKENS, hi - lo), F32) for lo, hi in col_slices],
        compiler_params=pltpu.CompilerParams(
            dimension_semantics=("arbitrary",), vmem_limit_bytes=VMEM_LIMIT),
        name="norm_inproj",
    )(x, nw, mod5, mod5, w)


CONV_COLS = 512


def _ssd_kernel(z_ref, xbc_ref, dt_ref, xres_ref, cw_ref, cb_ref, dtb_ref, alog_ref, de_ref, nw_ref,
                e_ref, wout_ref, gate_ref, o_ref, state_ref, ext_ref, act_ref, yn_ref):
    q = SSD_CHUNK
    c = pl.program_id(1)

    @pl.when(c == 0)
    def _():
        state_ref[...] = jnp.zeros_like(state_ref)
        ext_ref[0:CONV_HALO, :] = jnp.zeros((CONV_HALO, SSD_CONV_DIM), F32)

    @pl.when(c > 0)
    def _():
        ext_ref[0:CONV_HALO, :] = ext_ref[q:q + CONV_HALO, :]

    ext_ref[CONV_HALO:CONV_HALO + q, :] = xbc_ref[...]

    for s in range(SSD_CONV_DIM // CONV_COLS):
        cs = slice(s * CONV_COLS, (s + 1) * CONV_COLS)
        acc = cb_ref[:, cs] + cw_ref[SSD_CONV - 1:SSD_CONV, cs] * ext_ref[CONV_HALO:CONV_HALO + q, cs]
        for k in range(SSD_CONV - 1):
            off = CONV_HALO - (SSD_CONV - 1) + k
            acc = acc + cw_ref[k:k + 1, cs] * ext_ref[off:off + q, cs]
        act_ref[:, cs] = _silu(acc)

    dt_raw = dt_ref[...] + dtb_ref[...]
    dt = jnp.maximum(dt_raw, 0.0) + jnp.log1p(jnp.exp(-jnp.abs(dt_raw)))
    a = dt * (-jnp.exp(alog_ref[...]))
    row = lax.broadcasted_iota(jnp.int32, (q, q), 0)
    col = lax.broadcasted_iota(jnp.int32, (q, q), 1)
    tril = row >= col
    acs = _dot3_exact_lhs(tril.astype(BF16), a)
    acs_t = acs.T
    expand = e_ref[...]
    acs_e = _dot3_exact_rhs(acs, expand)
    dt_e = _dot3_exact_rhs(dt, expand)
    tot_e = acs_e[q - 1:q, :]
    decay_from_start = jnp.exp(acs_e)
    decay_to_end = jnp.exp(tot_e - acs_e)
    chunk_decay = jnp.exp(tot_e)

    lane_head = lax.broadcasted_iota(jnp.int32, (q, SSD_GROUP_W), 1) // SSD_HEAD_DIM
    for g in range(SSD_GROUPS):
        xs = act_ref[:, g * SSD_GROUP_W:(g + 1) * SSD_GROUP_W]
        gs = slice(g * SSD_GROUP_W, (g + 1) * SSD_GROUP_W)
        bm = act_ref[:, SSD_D_INNER + g * SSD_STATE:SSD_D_INNER + (g + 1) * SSD_STATE]
        cm = act_ref[:, SSD_D_INNER + SSD_GN + g * SSD_STATE:SSD_D_INNER + SSD_GN + (g + 1) * SSD_STATE]
        bm_t = bm.T.astype(BF16)
        cm_b = cm.astype(BF16)
        cb = _dot(cm_b, bm_t)
        xd = xs * dt_e[:, gs]
        ms = []
        xds = []
        for j in range(SSD_HEADS_PER_GROUP):
            h = g * SSD_HEADS_PER_GROUP + j
            seg = acs[:, h:h + 1] - acs_t[h:h + 1, :]
            dec = jnp.exp(jnp.where(tril, seg, -jnp.inf))
            ms.append((cb * dec).astype(BF16))
            xds.append(jnp.where(lane_head == j, xd, 0.0).astype(BF16))
        y_diag = _dot(jnp.concatenate(ms, axis=1), jnp.concatenate(xds, axis=0))
        prev = state_ref[g]
        y_off = _dot(cm_b, prev.astype(BF16)) * decay_from_start[:, gs]
        state_ref[g] = prev * chunk_decay[:, gs] + _dot(bm_t, (xd * decay_to_end[:, gs]).astype(BF16))
        y = y_diag + y_off + xs * de_ref[:, gs]
        y = y * _silu(z_ref[:, gs])
        y = y * lax.rsqrt(jnp.mean(y * y, axis=-1, keepdims=True) + EPS) * nw_ref[:, gs]
        yn_ref[:, gs] = y.astype(BF16)

    out = _dot(yn_ref[...], wout_ref[...])
    o_ref[...] = xres_ref[...] + gate_ref[...] * out


def _ssd_mixer(z, xbc, dt, x, conv_w, conv_b, dt_bias, a_log, d_e, norm_w, expand, w_out, mod5, layer):
    q = SSD_CHUNK
    nc = SEQ // q
    tok = lambda w: pl.BlockSpec((q, w), lambda b, c: (b * nc + c, 0))
    full = lambda r, w: pl.BlockSpec((r, w), lambda b, c: (0, 0))
    return pl.pallas_call(
        _ssd_kernel,
        grid=(BATCH, nc),
        in_specs=[
            tok(SSD_D_INNER), tok(SSD_CONV_DIM), tok(SSD_DT_PAD), tok(D_MODEL),
            full(SSD_CONV, SSD_CONV_DIM), full(1, SSD_CONV_DIM), full(1, SSD_DT_PAD), full(1, SSD_DT_PAD),
            full(1, SSD_D_INNER), full(1, SSD_D_INNER), full(SSD_DT_PAD, SSD_D_INNER),
            full(SSD_D_INNER, D_MODEL),
            pl.BlockSpec((None, None, None, 1, D_MODEL), lambda b, c: (layer, b, 2, 0, 0)),
        ],
        out_specs=tok(D_MODEL),
        out_shape=jax.ShapeDtypeStruct((TOKENS, D_MODEL), F32),
        scratch_shapes=[
            pltpu.VMEM((SSD_GROUPS, SSD_STATE, SSD_GROUP_W), F32),
            pltpu.VMEM((q + CONV_HALO, SSD_CONV_DIM), F32),
            pltpu.VMEM((q, SSD_CONV_DIM), F32),
            pltpu.VMEM((q, SSD_D_INNER), BF16),
        ],
        compiler_params=pltpu.CompilerParams(
            dimension_semantics=("arbitrary", "arbitrary"), vmem_limit_bytes=VMEM_LIMIT),
        name="ssd_mixer",
    )(z, xbc, dt, x, conv_w, conv_b, dt_bias, a_log, d_e, norm_w, expand, w_out, mod5)


DSA_KEY_TILE = 256
DSA_CLASSES = 8
V_EXT = 2 * ATT_HEAD_DIM
DSA_BLOCKS_PER_CLASS = (SEQ // Q_BLOCK) // DSA_CLASSES
N_BISECT = 12
F32_MIN = float(jnp.finfo(jnp.float32).min)
LOG2E = 1.4426950408889634


def _count(mask):
    return jnp.sum(jnp.where(mask, 1.0, 0.0), axis=-1, keepdims=True)


def _select_topk(score_ref, q_pos, n_keys):
    kf = float(TOPK)
    small = (q_pos + 1) <= TOPK
    sc = score_ref[...]
    hi0 = jnp.max(sc, axis=-1, keepdims=True)
    lo0 = jnp.min(jnp.where(sc == -jnp.inf, jnp.inf, sc), axis=-1, keepdims=True)

    def bisect(_, carry):
        lo, hi = carry
        mid = lo + 0.5 * (hi - lo)
        ok = _count(score_ref[...] >= mid) >= kf
        return jnp.where(ok, mid, lo), jnp.where(ok, hi, mid)

    _, hi = lax.fori_loop(0, N_BISECT, bisect, (lo0, hi0))

    v0 = jnp.max(jnp.where(sc <= hi, sc, -jnp.inf), axis=-1, keepdims=True)
    c0 = _count(sc >= v0)
    pend0 = jnp.where((c0 >= kf) | small, 0.0, 1.0)

    def walk_cond(carry):
        return (carry[2] > 0.0) & (carry[3] < n_keys)

    def walk(carry):
        v, pend, _, it = carry
        s = score_ref[...]
        v2 = jnp.max(jnp.where(s < v, s, -jnp.inf), axis=-1, keepdims=True)
        c2 = _count(s >= v2)
        v = jnp.where(pend > 0.0, v2, v)
        pend = jnp.where(c2 >= kf, 0.0, pend)
        return v, pend, jnp.max(pend), it + 1

    v, _, _, _ = lax.while_loop(walk_cond, walk, (v0, pend0, jnp.max(pend0), jnp.int32(0)))
    thr = jnp.where(small, F32_MIN, v)

    key_pos = lax.broadcasted_iota(jnp.int32, (Q_BLOCK, n_keys), 1)
    gt = sc > thr
    eq = sc == thr
    need = kf - _count(gt)
    tie = jnp.where(_count(eq) > need, 1.0, 0.0)

    def search_cut():
        def body(_, carry):
            lo, hi = carry
            mid = (lo + hi) >> 1
            s = score_ref[...]
            ok = _count((s == thr) & (key_pos <= mid)) >= need
            return jnp.where(ok, lo, mid), jnp.where(ok, mid, hi)

        init = (jnp.full((Q_BLOCK, 1), -1, jnp.int32), jnp.full((Q_BLOCK, 1), n_keys - 1, jnp.int32))
        return lax.fori_loop(0, (n_keys - 1).bit_length() + 1, body, init)[1]

    cut = lax.cond(jnp.max(tie) > 0.0, search_cut, lambda: jnp.full((Q_BLOCK, 1), n_keys - 1, jnp.int32))
    score_ref[...] = jnp.where(gt | (eq & (key_pos <= cut)), 0.0, -jnp.inf)


def _dsa_inproj_kernel(x_ref, nw_ref, scale_ref, shift_ref, w_ref, qn_ref, kn_ref, seg_ref, segt_ref,
                       q_ref, k_ref, v_ref, qi_ref, ki_ref, wi_ref):
    hd = ATT_HEAD_DIM
    h = _norm_mod(x_ref[...], nw_ref[...], scale_ref[...], shift_ref[...]).astype(BF16)

    def head_norm(t, w):
        width = t.shape[1]
        ss = _dot((t * t).astype(BF16), seg_ref[0:width, :])
        r = lax.rsqrt(ss * (1.0 / hd) + EPS)
        r_hi = r.astype(BF16)
        r_lo = (r - r_hi.astype(F32)).astype(BF16)
        return t * (_dot(r_hi, segt_ref[:, 0:width]) + _dot(r_lo, segt_ref[:, 0:width])) * w

    q = head_norm(_dot(h, w_ref[:, 0:DSA_Q]), qn_ref[...] * (hd ** -0.5 * LOG2E))
    for n in range(ATT_HEADS):
        q_ref[n] = q[:, n * hd:(n + 1) * hd].astype(BF16)
    kv = _dot(h, w_ref[:, DSA_Q:DSA_Q + 2 * DSA_KV])
    k = head_norm(kv[:, 0:DSA_KV], kn_ref[...])
    for n in range(ATT_KV_HEADS):
        k_ref[n] = k[:, n * hd:(n + 1) * hd].astype(BF16)
        v_ref[n] = jnp.concatenate([kv[:, DSA_KV + n * hd:DSA_KV + (n + 1) * hd],
                                    jnp.ones((kv.shape[0], V_EXT - hd), F32)], axis=1).astype(BF16)
    qi = _dot(h, w_ref[:, DSA_Q + 2 * DSA_KV:DSA_KI_START])
    for n in range(IDX_HEADS):
        qi_ref[n] = qi[:, n * IDX_HEAD_DIM:(n + 1) * IDX_HEAD_DIM].astype(BF16)
    ki_ref[...] = _dot(h, w_ref[:, DSA_KI_START:DSA_KI_START + IDX_HEAD_DIM]).astype(BF16)
    wi_ref[...] = _dot(h, w_ref[:, DSA_WI_START:DSA_WI_START + IDX_HEADS]) * ((IDX_HEADS * IDX_HEAD_DIM) ** -0.5)


def _dsa_inproj(x, nw, mod5, layer, w, q_norm, k_norm):
    tm = INPROJ_TM
    tiles_per_batch = SEQ // tm
    heads = lambda n: pl.BlockSpec((n, tm, ATT_HEAD_DIM), lambda i: (0, i, 0))
    head_of = jnp.arange(DSA_Q, dtype=jnp.int32) // ATT_HEAD_DIM
    seg = (head_of[:, None] == jnp.arange(128, dtype=jnp.int32)[None, :]).astype(BF16)
    q_norm = jnp.tile(q_norm, (1, ATT_HEADS))
    k_norm = jnp.tile(k_norm, (1, ATT_KV_HEADS))
    return pl.pallas_call(
        _dsa_inproj_kernel,
        grid=(TOKENS // tm,),
        in_specs=[
            pl.BlockSpec((tm, D_MODEL), lambda i: (i, 0)),
            pl.BlockSpec((1, D_MODEL), lambda i: (0, 0)),
            _mod_spec(layer, 1, tiles_per_batch),
            _mod_spec(layer, 0, tiles_per_batch),
            pl.BlockSpec((D_MODEL, DSA_PROJ_PAD), lambda i: (0, 0)),
            pl.BlockSpec((1, DSA_Q), lambda i: (0, 0)),
            pl.BlockSpec((1, DSA_KV), lambda i: (0, 0)),
            pl.BlockSpec((DSA_Q, 128), lambda i: (0, 0)),
            pl.BlockSpec((128, DSA_Q), lambda i: (0, 0)),
        ],
        out_specs=[heads(ATT_HEADS), heads(ATT_KV_HEADS),
                   pl.BlockSpec((ATT_KV_HEADS, tm, V_EXT), lambda i: (0, i, 0)), heads(IDX_HEADS),
                   pl.BlockSpec((tm, IDX_HEAD_DIM), lambda i: (i, 0)),
                   pl.BlockSpec((tm, IDX_HEADS), lambda i: (i, 0))],
        out_shape=[jax.ShapeDtypeStruct((ATT_HEADS, TOKENS, ATT_HEAD_DIM), BF16),
                   jax.ShapeDtypeStruct((ATT_KV_HEADS, TOKENS, ATT_HEAD_DIM), BF16),
                   jax.ShapeDtypeStruct((ATT_KV_HEADS, TOKENS, V_EXT), BF16),
                   jax.ShapeDtypeStruct((IDX_HEADS, TOKENS, IDX_HEAD_DIM), BF16),
                   jax.ShapeDtypeStruct((TOKENS, IDX_HEAD_DIM), BF16),
                   jax.ShapeDtypeStruct((TOKENS, IDX_HEADS), F32)],
        compiler_params=pltpu.CompilerParams(
            dimension_semantics=("arbitrary",), vmem_limit_bytes=VMEM_LIMIT),
        name="dsa_inproj",
    )(x, nw, mod5, mod5, w, q_norm, k_norm, seg, seg.T)


def _dsa_kernel(q_ref, k_ref, v_ref, qi_ref, ki_ref, wi_ref, xres_ref, wout_ref, gate_ref, o_ref,
                score_ref, ocat_ref, *, n_keys, first_block):
    hd = ATT_HEAD_DIM
    q_pos = (first_block + pl.program_id(1)) * Q_BLOCK + lax.broadcasted_iota(jnp.int32, (Q_BLOCK, 1), 0)

    wi = wi_ref[...]
    qi = qi_ref[...].reshape(IDX_HEADS * Q_BLOCK, IDX_HEAD_DIM)
    for kt in range(n_keys // DSA_KEY_TILE):
        ks = slice(kt * DSA_KEY_TILE, (kt + 1) * DSA_KEY_TILE)
        raw = _dot_nt(qi, ki_ref[ks, :])
        acc = jnp.zeros((Q_BLOCK, DSA_KEY_TILE), F32)
        for n in range(IDX_HEADS):
            acc = acc + wi[:, n:n + 1] * jnp.maximum(raw[n * Q_BLOCK:(n + 1) * Q_BLOCK, :], 0.0)
        key_pos = kt * DSA_KEY_TILE + lax.broadcasted_iota(jnp.int32, (Q_BLOCK, DSA_KEY_TILE), 1)
        score_ref[:, ks] = jnp.where(key_pos <= q_pos, acc, -jnp.inf)

    _select_topk(score_ref, q_pos, n_keys)
    bias = score_ref[...][None, :, :]

    for n in range(ATT_KV_HEADS):
        q4 = q_ref[n * ATT_Q_PER_KV:(n + 1) * ATT_Q_PER_KV].reshape(ATT_Q_PER_KV * Q_BLOCK, hd)
        s = _dot_nt(q4, k_ref[n]).reshape(ATT_Q_PER_KV, Q_BLOCK, n_keys) + bias
        p = jnp.exp2(s - jnp.max(s, axis=-1, keepdims=True))
        o = _dot(p.reshape(ATT_Q_PER_KV * Q_BLOCK, n_keys).astype(BF16), v_ref[n])
        o = o[:, 0:hd] * (1.0 / o[:, hd:hd + 1])
        for g in range(ATT_Q_PER_KV):
            col = (n * ATT_Q_PER_KV + g) * hd
            ocat_ref[:, col:col + hd] = o[g * Q_BLOCK:(g + 1) * Q_BLOCK, :]
    out = _dot(ocat_ref[...].astype(BF16), wout_ref[...])
    o_ref[...] = xres_ref[...] + gate_ref[...] * out


def _dsa_mixer(q, k, v, qi, ki, wi, x, w_out, mod5, layer):
    nb = SEQ // Q_BLOCK
    k4 = k.reshape(ATT_KV_HEADS, BATCH, SEQ, ATT_HEAD_DIM)
    v4 = v.reshape(ATT_KV_HEADS, BATCH, SEQ, V_EXT)
    ki3 = ki.reshape(BATCH, SEQ, IDX_HEAD_DIM)
    for cls in range(DSA_CLASSES):
        n_keys = (cls + 1) * (SEQ // DSA_CLASSES)
        first_block = cls * DSA_BLOCKS_PER_CLASS
        row = lambda b, i, fb=first_block: b * nb + fb + i
        heads = lambda n: pl.BlockSpec((n, Q_BLOCK, ATT_HEAD_DIM), lambda b, i: (0, row(b, i), 0))
        keys = lambda width: pl.BlockSpec((ATT_KV_HEADS, None, n_keys, width), lambda b, i: (0, b, 0, 0))
        x = pl.pallas_call(
            functools.partial(_dsa_kernel, n_keys=n_keys, first_block=first_block),
            grid=(BATCH, DSA_BLOCKS_PER_CLASS),
            in_specs=[
                heads(ATT_HEADS), keys(ATT_HEAD_DIM), keys(V_EXT), heads(IDX_HEADS),
                pl.BlockSpec((None, n_keys, IDX_HEAD_DIM), lambda b, i: (b, 0, 0)),
                pl.BlockSpec((Q_BLOCK, IDX_HEADS), lambda b, i: (row(b, i), 0)),
                pl.BlockSpec((Q_BLOCK, D_MODEL), lambda b, i: (row(b, i), 0)),
                pl.BlockSpec((DSA_Q, D_MODEL), lambda b, i: (0, 0)),
                pl.BlockSpec((None, None, None, 1, D_MODEL), lambda b, i: (layer, b, 2, 0, 0)),
            ],
            out_specs=pl.BlockSpec((Q_BLOCK, D_MODEL), lambda b, i: (row(b, i), 0)),
            out_shape=jax.ShapeDtypeStruct((TOKENS, D_MODEL), F32),
            scratch_shapes=[
                pltpu.VMEM((Q_BLOCK, n_keys), F32),
                pltpu.VMEM((Q_BLOCK, D_MODEL), F32),
            ],
            input_output_aliases={6: 0},
            compiler_params=pltpu.CompilerParams(
                dimension_semantics=("arbitrary", "arbitrary"), vmem_limit_bytes=VMEM_LIMIT),
            name=f"dsa_mixer_c{cls}",
        )(q, k4, v4, qi, ki3, wi, x, w_out, mod5)
    return x


MOE_PAIRS = MOE_EPG * (MOE_EPG - 1) // 2
MOE_CLASSES = MOE_GROUPS * MOE_PAIRS
PAIR_LO = (0, 0, 0, 1, 1, 2)
PAIR_HI = (1, 2, 3, 2, 3, 3)
ROUTE_TM = 1024
META_W = 128
META_CLASS, META_RANK, META_WLO, META_WHI = 0, 1, 2, 3
TILE_ROWS, TILE_LANES = 8, 128
H_WORDS = D_MODEL // 2
H_SUBLANES = H_WORDS // TILE_LANES
SORT_BLOCK = 256
N_SORT_BLOCKS = TOKENS // SORT_BLOCK
MAX_ITEMS = N_SORT_BLOCKS + MOE_CLASSES
PERMUTE_TM = 256
COMBINE_TM = 256


def _route_kernel(x_ref, nw_ref, scale_ref, shift_ref, wr_ref, br_ref, pay_ref, cls_ref, rank_ref, cnt_ref,
                  carry_ref):
    tm = ROUTE_TM

    @pl.when(pl.program_id(0) == 0)
    def _():
        carry_ref[...] = jnp.zeros_like(carry_ref)

    h = _norm_mod(x_ref[...], nw_ref[...], scale_ref[...], shift_ref[...])
    logits = _dot(h.astype(BF16), wr_ref[...]) + br_ref[...]
    lane = lax.broadcasted_iota(jnp.int32, logits.shape, 1)
    neg = -jnp.inf
    big = jnp.int32(ROUTE_PAD)
    is_group = (lane >= MOE_EXPERTS) & (lane < MOE_EXPERTS + MOE_GROUPS)
    gl = jnp.where(is_group, logits, neg)
    g_max = jnp.max(gl, axis=-1, keepdims=True)
    g_idx = jnp.min(jnp.where(gl == g_max, lane - MOE_EXPERTS, big), axis=-1, keepdims=True)
    g_val = 1.0 / jnp.sum(jnp.exp(gl - g_max), axis=-1, keepdims=True)
    in_group = (lane < MOE_EXPERTS) & ((lane // MOE_EPG) == g_idx)
    el = jnp.where(in_group, logits, neg)
    m1 = jnp.max(el, axis=-1, keepdims=True)
    i1 = jnp.min(jnp.where(el == m1, lane, big), axis=-1, keepdims=True)
    el2 = jnp.where(lane == i1, neg, el)
    m2 = jnp.max(el2, axis=-1, keepdims=True)
    i2 = jnp.min(jnp.where(el2 == m2, lane, big), axis=-1, keepdims=True)
    r = jnp.exp(m2 - m1)
    w_top1 = g_val / (1.0 + r)
    w_top2 = g_val * r / (1.0 + r)

    lo = jnp.minimum(i1, i2) - g_idx * MOE_EPG
    hi = jnp.maximum(i1, i2) - g_idx * MOE_EPG
    pair = (lo * (2 * MOE_EPG - 1 - lo)) // 2 + (hi - lo - 1)
    cls = g_idx * MOE_PAIRS + pair
    w_lo = jnp.where(i1 < i2, w_top1, w_top2)
    w_hi = jnp.where(i1 < i2, w_top2, w_top1)

    onehot = lane == cls
    row = lax.broadcasted_iota(jnp.int32, (tm, tm), 0)
    col = lax.broadcasted_iota(jnp.int32, (tm, tm), 1)
    before = _dot((row > col).astype(BF16), onehot.astype(BF16)) + carry_ref[...]
    rank = jnp.sum(jnp.where(onehot, before, 0.0), axis=-1, keepdims=True)
    carry_ref[...] += jnp.sum(jnp.where(onehot, 1.0, 0.0), axis=0, keepdims=True)
    cnt_ref[...] = carry_ref[...]

    meta = jnp.where(lane == META_CLASS, cls.astype(F32),
                     jnp.where(lane == META_RANK, rank,
                               jnp.where(lane == META_WLO, w_lo, jnp.where(lane == META_WHI, w_hi, 0.0))))

    eye = lax.broadcasted_iota(jnp.int32, (128, 128), 0) == lax.broadcasted_iota(jnp.int32, (128, 128), 1)

    def column_to_rows(v):
        return jnp.concatenate([jnp.sum(jnp.where(eye, v[b * 128:(b + 1) * 128, :], 0.0), axis=0, keepdims=True)
                                for b in range(tm // 128)], axis=0)

    cls_ref[...] = column_to_rows(cls.astype(F32)).astype(jnp.int32)
    rank_ref[...] = column_to_rows(rank).astype(jnp.int32)
    words = pltpu.pack_elementwise([h[:, 0:H_WORDS], h[:, H_WORDS:D_MODEL]], packed_dtype=BF16)
    for s in range(H_SUBLANES):
        pay_ref[pl.ds(s, tm, stride=TILE_ROWS), :] = words[:, s * TILE_LANES:(s + 1) * TILE_LANES]
    pay_ref[pl.ds(H_SUBLANES, tm, stride=TILE_ROWS), :] = lax.bitcast_convert_type(meta, jnp.int32)
    for s in range(H_SUBLANES + 1, TILE_ROWS):
        pay_ref[pl.ds(s, tm, stride=TILE_ROWS), :] = jnp.zeros((tm, TILE_LANES), jnp.int32)


def _moe_route(x, nw, mod5, layer, w_route, b_route):
    tm = ROUTE_TM
    tiles_per_batch = SEQ // tm
    return pl.pallas_call(
        _route_kernel,
        grid=(TOKENS // tm,),
        in_specs=[
            pl.BlockSpec((tm, D_MODEL), lambda i: (i, 0)),
            pl.BlockSpec((1, D_MODEL), lambda i: (0, 0)),
            _mod_spec(layer, 4, tiles_per_batch),
            _mod_spec(layer, 3, tiles_per_batch),
            pl.BlockSpec((D_MODEL, ROUTE_PAD), lambda i: (0, 0)),
            pl.BlockSpec((1, ROUTE_PAD), lambda i: (0, 0)),
        ],
        out_specs=[pl.BlockSpec((tm * TILE_ROWS, TILE_LANES), lambda i: (i, 0)),
                   pl.BlockSpec((tm // 128, 128), lambda i: (i, 0)),
                   pl.BlockSpec((tm // 128, 128), lambda i: (i, 0)),
                   pl.BlockSpec((1, ROUTE_PAD), lambda i: (0, 0))],
        out_shape=[jax.ShapeDtypeStruct((TOKENS * TILE_ROWS, TILE_LANES), jnp.int32),
                   jax.ShapeDtypeStruct((TOKENS // 128, 128), jnp.int32),
                   jax.ShapeDtypeStruct((TOKENS // 128, 128), jnp.int32),
                   jax.ShapeDtypeStruct((1, ROUTE_PAD), F32)],
        scratch_shapes=[pltpu.VMEM((1, ROUTE_PAD), F32)],
        compiler_params=pltpu.CompilerParams(
            dimension_semantics=("arbitrary",), vmem_limit_bytes=VMEM_LIMIT),
        name="moe_route",
    )(x, nw, mod5, mod5, w_route, b_route)


def _permute_kernel(pos_ref, src_ref, dst_hbm, stage_ref, sem):
    i = pl.program_id(0)
    slot = i % 2
    rows = PERMUTE_TM * TILE_ROWS

    def slot_wait(s):
        whole = stage_ref.at[pl.ds(s * rows, rows), :]
        pltpu.make_async_copy(whole, whole, sem.at[s]).wait()

    @pl.when(i >= 2)
    def _():
        slot_wait(slot)

    base = pl.multiple_of(slot * rows, rows)
    stage_ref[pl.ds(base, rows), :] = src_ref[...]

    def issue(r, _):
        pltpu.make_async_copy(stage_ref.at[pl.ds(base + r * TILE_ROWS, TILE_ROWS), :],
                              dst_hbm.at[pos_ref[i * PERMUTE_TM + r]], sem.at[slot]).start()
        return 0

    lax.fori_loop(0, PERMUTE_TM, issue, 0, unroll=16)

    @pl.when(i == pl.num_programs(0) - 1)
    def _():
        slot_wait(1 - slot)
        slot_wait(slot)


def _moe_permute(pos, payload):
    rows = PERMUTE_TM * TILE_ROWS
    return pl.pallas_call(
        _permute_kernel,
        grid_spec=pltpu.PrefetchScalarGridSpec(
            num_scalar_prefetch=1, grid=(TOKENS // PERMUTE_TM,),
            in_specs=[pl.BlockSpec((rows, TILE_LANES), lambda i, pos: (i, 0))],
            out_specs=pl.BlockSpec(memory_space=pl.ANY),
            scratch_shapes=[pltpu.VMEM((2 * rows, TILE_LANES), jnp.int32), pltpu.SemaphoreType.DMA((2,))]),
        out_shape=jax.ShapeDtypeStruct((TOKENS, TILE_ROWS, TILE_LANES), jnp.int32),
        compiler_params=pltpu.CompilerParams(
            dimension_semantics=("arbitrary",), vmem_limit_bytes=VMEM_LIMIT),
        name="moe_permute",
    )(pos, payload)


def _experts_kernel(blk_ref, cls_ref, elo_ref, ehi_ref, first_ref, last_ref, valid_ref,
                    pay_ref, w1lo_ref, w3lo_ref, w2lo_ref, w1hi_ref, w3hi_ref, w2hi_ref, o_ref, acc_ref):
    k = pl.program_id(0)
    rows = SORT_BLOCK

    @pl.when(valid_ref[k] == 1)
    def _():
        def sublane(s):
            return pay_ref[pl.ds(s, rows, stride=TILE_ROWS), :]

        halves = [[pltpu.unpack_elementwise(sublane(s), index=i, packed_dtype=BF16, unpacked_dtype=F32)
                   for s in range(H_SUBLANES)] for i in range(2)]
        hb = jnp.concatenate(halves[0] + halves[1], axis=1).astype(BF16)
        meta = lax.bitcast_convert_type(sublane(H_SUBLANES), F32)
        mine = meta[:, META_CLASS:META_CLASS + 1] == cls_ref[k].astype(F32)
        w_lo = jnp.where(mine, meta[:, META_WLO:META_WLO + 1], 0.0)
        w_hi = jnp.where(mine, meta[:, META_WHI:META_WHI + 1], 0.0)
        bf = lambda w_ref: w_ref[...].astype(BF16)
        hid_lo = _silu(_dot(hb, bf(w1lo_ref))) * _dot(hb, bf(w3lo_ref)) * w_lo
        hid_hi = _silu(_dot(hb, bf(w1hi_ref))) * _dot(hb, bf(w3hi_ref)) * w_hi
        y = _dot(hid_lo.astype(BF16), bf(w2lo_ref)) + _dot(hid_hi.astype(BF16), bf(w2hi_ref))

        @pl.when(first_ref[k] == 1)
        def _():
            acc_ref[...] = y

        @pl.when(first_ref[k] == 0)
        def _():
            acc_ref[...] += y

        @pl.when(last_ref[k] == 1)
        def _():
            for s in range(TILE_ROWS):
                o_ref[pl.ds(s, rows, stride=TILE_ROWS), :] = acc_ref[:, s * TILE_LANES:(s + 1) * TILE_LANES]


def _moe_experts(items, payload_sorted, w1, w3, w2):
    blk, cls, elo, ehi, first, last, valid = items
    tiles = pl.BlockSpec((SORT_BLOCK * TILE_ROWS, TILE_LANES), lambda k, blk, *_: (blk[k], 0))
    w_in = lambda which: pl.BlockSpec(
        (None, D_MODEL, MOE_HIDDEN), lambda k, blk, cls, elo, ehi, *_: ((elo, ehi)[which][k], 0, 0))
    w_out = lambda which: pl.BlockSpec(
        (None, MOE_HIDDEN, D_MODEL), lambda k, blk, cls, elo, ehi, *_: ((elo, ehi)[which][k], 0, 0))
    return pl.pallas_call(
        _experts_kernel,
        grid_spec=pltpu.PrefetchScalarGridSpec(
            num_scalar_prefetch=7, grid=(MAX_ITEMS,),
            in_specs=[tiles, w_in(0), w_in(0), w_out(0), w_in(1), w_in(1), w_out(1)],
            out_specs=tiles,
            scratch_shapes=[pltpu.VMEM((SORT_BLOCK, D_MODEL), F32)]),
        out_shape=jax.ShapeDtypeStruct((TOKENS * TILE_ROWS, TILE_LANES), F32),
        compiler_params=pltpu.CompilerParams(
            dimension_semantics=("arbitrary",), vmem_limit_bytes=VMEM_LIMIT),
        name="moe_experts",
    )(blk, cls, elo, ehi, first, last, valid, payload_sorted, w1, w3, w2, w1, w3, w2)


def _combine_kernel(pos_ref, x_ref, gate_ref, y_hbm, o_ref, buf_ref, sem):
    tm = COMBINE_TM
    i = pl.program_id(0)
    n = pl.num_programs(0)

    def gather(tile, slot):
        def issue(r, _):
            pltpu.make_async_copy(y_hbm.at[pos_ref[tile * tm + r]],
                                  buf_ref.at[pl.ds((slot * tm + r) * TILE_ROWS, TILE_ROWS), :], sem.at[slot]).start()
            return 0
        lax.fori_loop(0, tm, issue, 0, unroll=16)

    @pl.when(i == 0)
    def _():
        gather(0, 0)

    @pl.when(i + 1 < n)
    def _():
        gather(i + 1, (i + 1) % 2)

    slot = i % 2
    base = slot * tm * TILE_ROWS
    whole_slot = buf_ref.at[pl.ds(base, tm * TILE_ROWS), :]
    pltpu.make_async_copy(whole_slot, whole_slot, sem.at[slot]).wait()
    y = jnp.concatenate([buf_ref[pl.ds(base + s, tm, stride=TILE_ROWS), :] for s in range(TILE_ROWS)], axis=1)
    o_ref[...] = x_ref[...] + gate_ref[...] * y


def _moe_combine(pos, x, mod5, layer, y_sorted):
    tm = COMBINE_TM
    tiles_per_batch = SEQ // tm
    return pl.pallas_call(
        _combine_kernel,
        grid_spec=pltpu.PrefetchScalarGridSpec(
            num_scalar_prefetch=1, grid=(TOKENS // tm,),
            in_specs=[
                pl.BlockSpec((tm, D_MODEL), lambda i, pos: (i, 0)),
                pl.BlockSpec((None, None, None, 1, D_MODEL),
                             lambda i, pos: (layer, i // tiles_per_batch, 5, 0, 0)),
                pl.BlockSpec(memory_space=pl.ANY),
            ],
            out_specs=pl.BlockSpec((tm, D_MODEL), lambda i, pos: (i, 0)),
            scratch_shapes=[pltpu.VMEM((2 * tm * TILE_ROWS, TILE_LANES), F32), pltpu.SemaphoreType.DMA((2,))]),
        out_shape=jax.ShapeDtypeStruct((TOKENS, D_MODEL), F32),
        compiler_params=pltpu.CompilerParams(
            dimension_semantics=("arbitrary",), vmem_limit_bytes=VMEM_LIMIT),
        name="moe_combine",
    )(pos, x, mod5, y_sorted)


def _moe_plan(cls, rank, counts):
    count = counts[0, :MOE_CLASSES].astype(jnp.int32)
    ends = jnp.cumsum(count)
    starts = ends - count
    pos = (starts[cls] + rank).reshape(TOKENS)

    first_blk = starts // SORT_BLOCK
    n_items = jnp.where(count > 0, (ends - 1) // SORT_BLOCK - first_blk + 1, 0)
    item_end = jnp.cumsum(n_items)
    item_start = item_end - n_items
    k = jnp.arange(MAX_ITEMS, dtype=jnp.int32)
    valid = k < item_end[-1]
    kc = jnp.minimum(k, item_end[-1] - 1)
    icls = jnp.searchsorted(item_end, kc, side="right").astype(jnp.int32)
    blk = first_blk[icls] + (kc - item_start[icls])
    first = jnp.concatenate([jnp.ones((1,), jnp.int32), (blk[1:] != blk[:-1]).astype(jnp.int32)])
    last = jnp.concatenate([(blk[1:] != blk[:-1]) | ~valid[1:], jnp.ones((1,), bool)]).astype(jnp.int32)
    group = icls // MOE_PAIRS
    pair = icls % MOE_PAIRS
    elo = group * MOE_EPG + jnp.asarray(PAIR_LO, jnp.int32)[pair]
    ehi = group * MOE_EPG + jnp.asarray(PAIR_HI, jnp.int32)[pair]
    return pos, (blk, icls, elo, ehi, first * valid, last * valid, valid.astype(jnp.int32))


def _moe(x, nw, mod5, layer, w_route, b_route, w1, w3, w2):
    payload, cls, rank, counts = _moe_route(x, nw, mod5, layer, w_route, b_route)
    pos, items = _moe_plan(cls, rank, counts)
    tiles = (TOKENS, TILE_ROWS, TILE_LANES)
    sorted_payload = _moe_permute(pos, payload).reshape(TOKENS * TILE_ROWS, TILE_LANES)
    y_sorted = _moe_experts(items, sorted_payload, w1, w3, w2)
    return _moe_combine(pos, x, mod5, layer, y_sorted.reshape(tiles))


def _pad_cols(w, width):
    return jnp.pad(w, ((0, 0), (0, width - w.shape[1])))


def kernel(x, c, ada_w, ada_b, norm_mix, norm_ffn, ssd_w_in, ssd_conv_w, ssd_conv_b, ssd_dt_bias,
           ssd_a_log, ssd_d, ssd_norm, ssd_w_out, dsa_w_in, dsa_q_norm, dsa_k_norm, dsa_w_out,
           moe_w_group, moe_b_group, moe_w_expert, moe_b_expert, moe_w1, moe_w3, moe_w2):
    depth = ada_w.shape[0]
    xt = x.reshape(TOKENS, D_MODEL)
    mod = _modulation(c, ada_w, ada_b)
    mod5 = mod.reshape(depth, BATCH, 6, 1, D_MODEL)

    head_of_col = jnp.arange(SSD_D_INNER, dtype=jnp.int32) // SSD_HEAD_DIM
    expand = (jnp.arange(SSD_DT_PAD, dtype=jnp.int32)[:, None] == head_of_col[None, :]).astype(BF16)

    for i in range(depth):
        j = i // 2
        nw_mix = norm_mix[i].reshape(1, D_MODEL)
        if i % 2 == 0:
            w_in = _pad_cols(ssd_w_in[j], SSD_PROJ_PAD).astype(BF16)
            z, xbc, dt = _inproj(
                xt, nw_mix, mod5, i, 1, 0, w_in,
                ((0, SSD_D_INNER), (SSD_D_INNER, SSD_D_INNER + SSD_CONV_DIM),
                 (SSD_D_INNER + SSD_CONV_DIM, SSD_PROJ_PAD)))
            xt = _ssd_mixer(
                z, xbc, dt, xt, ssd_conv_w[j], ssd_conv_b[j].reshape(1, SSD_CONV_DIM),
                _pad_cols(ssd_dt_bias[j].reshape(1, SSD_HEADS), SSD_DT_PAD),
                _pad_cols(ssd_a_log[j].reshape(1, SSD_HEADS), SSD_DT_PAD),
                jnp.repeat(ssd_d[j], SSD_HEAD_DIM).reshape(1, SSD_D_INNER),
                ssd_norm[j].reshape(1, SSD_D_INNER), expand, ssd_w_out[j].astype(BF16), mod5, i)
        else:
            w = dsa_w_in[j]
            w_in = jnp.concatenate(
                [_pad_cols(w[:, :DSA_KI_START + IDX_HEAD_DIM], DSA_WI_START),
                 _pad_cols(w[:, DSA_KI_START + IDX_HEAD_DIM:], 128)], axis=1).astype(BF16)
            q, k, v, qi, ki, wi = _dsa_inproj(
                xt, nw_mix, mod5, i, w_in, dsa_q_norm[j].reshape(1, ATT_HEAD_DIM),
                dsa_k_norm[j].reshape(1, ATT_HEAD_DIM))
            xt = _dsa_mixer(q, k, v, qi, ki, wi, xt, dsa_w_out[j].astype(BF16), mod5, i)

        w_route = _pad_cols(jnp.concatenate([moe_w_expert[i], moe_w_group[i]], axis=1), ROUTE_PAD).astype(BF16)
        b_route = _pad_cols(jnp.concatenate([moe_b_expert[i], moe_b_group[i]]).reshape(1, -1), ROUTE_PAD)
        xt = _moe(xt, norm_ffn[i].reshape(1, D_MODEL), mod5, i, w_route, b_route,
                  moe_w1[i], moe_w3[i], moe_w2[i])
    return xt.reshape(BATCH, SEQ, D_MODEL)
```

```python
import functools

import jax
import jax.numpy as jnp
from jax import lax
from jax.experimental import pallas as pl
from jax.experimental.pallas import tpu as pltpu

F32 = jnp.float32
BF16 = jnp.bfloat16

D_MODEL = 1024
BATCH = 8
SEQ = 2048
TOKENS = BATCH * SEQ
EPS = 1e-6

SSD_D_INNER = 2048
SSD_HEAD_DIM = 64
SSD_HEADS = 32
SSD_GROUPS = 8
SSD_HEADS_PER_GROUP = 4
SSD_STATE = 128
SSD_CONV = 4
SSD_CHUNK = 128
SSD_GN = SSD_GROUPS * SSD_STATE
SSD_CONV_DIM = SSD_D_INNER + 2 * SSD_GN
SSD_GROUP_W = SSD_HEADS_PER_GROUP * SSD_HEAD_DIM
SSD_DT_PAD = 128
SSD_PROJ_PAD = SSD_D_INNER + SSD_CONV_DIM + SSD_DT_PAD
CONV_HALO = 8

ATT_HEADS = 16
ATT_KV_HEADS = 4
ATT_Q_PER_KV = 4
ATT_HEAD_DIM = 64
IDX_HEADS = 8
IDX_HEAD_DIM = 64
TOPK = 256
Q_BLOCK = 128
DSA_Q = ATT_HEADS * ATT_HEAD_DIM
DSA_KV = ATT_KV_HEADS * ATT_HEAD_DIM
DSA_QI = IDX_HEADS * IDX_HEAD_DIM
DSA_KI_START = DSA_Q + 2 * DSA_KV + DSA_QI
DSA_WI_START = DSA_KI_START + 128
DSA_PROJ_PAD = DSA_WI_START + 128

MOE_GROUPS = 4
MOE_EPG = 4
MOE_EXPERTS = 16
MOE_HIDDEN = 256
ROUTE_PAD = 128

VMEM_LIMIT = 56 * 1024 * 1024


def _sigmoid(v):
    return 1.0 / (1.0 + jnp.exp(-v))


def _silu(v):
    return v * _sigmoid(v)


def _split3(a):
    hi = a.astype(BF16)
    r = a - hi.astype(F32)
    mid = r.astype(BF16)
    lo = (r - mid.astype(F32)).astype(BF16)
    return hi, mid, lo


def _dot(a, b):
    return jnp.dot(a, b, preferred_element_type=F32)


def _dot_nt(a, b):
    return lax.dot_general(a, b, (((1,), (1,)), ((), ())), preferred_element_type=F32)


def _dot3_exact_rhs(a, m):
    hi, mid, lo = _split3(a)
    return _dot(hi, m) + _dot(mid, m) + _dot(lo, m)


def _dot3_exact_lhs(m, a):
    hi, mid, lo = _split3(a)
    return _dot(m, hi) + _dot(m, mid) + _dot(m, lo)


def _norm_mod(x, nw, scale, shift):
    ms = jnp.mean(x * x, axis=-1, keepdims=True)
    return x * lax.rsqrt(ms + EPS) * nw * (1.0 + scale) + shift


MOD_TN = 1536


def _mod_kernel(c_ref, w_ref, b_ref, o_ref):
    cond = _silu(c_ref[...]).astype(BF16)
    o_ref[...] = _dot(cond, w_ref[...].astype(BF16)) + b_ref[...]


def _modulation(c, ada_w, ada_b):
    depth = ada_w.shape[0]
    n = ada_w.shape[2]
    return pl.pallas_call(
        _mod_kernel,
        grid=(depth, n // MOD_TN),
        in_specs=[
            pl.BlockSpec((BATCH, D_MODEL), lambda i, j: (0, 0)),
            pl.BlockSpec((None, D_MODEL, MOD_TN), lambda i, j: (i, 0, j)),
            pl.BlockSpec((None, 1, MOD_TN), lambda i, j: (i, 0, j)),
        ],
        out_specs=pl.BlockSpec((None, BATCH, MOD_TN), lambda i, j: (i, 0, j)),
        out_shape=jax.ShapeDtypeStruct((depth, BATCH, n), F32),
        compiler_params=pltpu.CompilerParams(
            dimension_semantics=("arbitrary", "arbitrary"), vmem_limit_bytes=VMEM_LIMIT),
        name="adaln_mod",
    )(c, ada_w, ada_b.reshape(depth, 1, n))


def _mod_spec(layer, chunk, rows_per_batch_tile):
    return pl.BlockSpec((None, None, None, 1, D_MODEL),
                        lambda i, *_: (layer, i // rows_per_batch_tile, chunk, 0, 0))


INPROJ_TM = 256


def _inproj_kernel(x_ref, nw_ref, scale_ref, shift_ref, w_ref, *o_refs, col_slices):
    h = _norm_mod(x_ref[...], nw_ref[...], scale_ref[...], shift_ref[...]).astype(BF16)
    for o_ref, (lo, hi) in zip(o_refs, col_slices):
        o_ref[...] = _dot(h, w_ref[:, lo:hi])


def _inproj(x, nw, mod5, layer, scale_chunk, shift_chunk, w, col_slices):
    tm = INPROJ_TM
    tiles_per_batch = SEQ // tm
    n_pad = w.shape[1]
    return pl.pallas_call(
        functools.partial(_inproj_kernel, col_slices=col_slices),
        grid=(TOKENS // tm,),
        in_specs=[
            pl.BlockSpec((tm, D_MODEL), lambda i: (i, 0)),
            pl.BlockSpec((1, D_MODEL), lambda i: (0, 0)),
            _mod_spec(layer, scale_chunk, tiles_per_batch),
            _mod_spec(layer, shift_chunk, tiles_per_batch),
            pl.BlockSpec((D_MODEL, n_pad), lambda i: (0, 0)),
        ],
        out_specs=[pl.BlockSpec((tm, hi - lo), lambda i: (i, 0)) for lo, hi in col_slices],
        out_shape=[jax.ShapeDtypeStruct((TOKENS, hi - lo), F32) for lo, hi in col_slices],
        compiler_params=pltpu.CompilerParams(
            dimension_semantics=("arbitrary",), vmem_limit_bytes=VMEM_LIMIT),
        name="norm_inproj",
    )(x, nw, mod5, mod5, w)


CONV_COLS = 512


def _ssd_kernel(z_ref, xbc_ref, dt_ref, xres_ref, cw_ref, cb_ref, dtb_ref, alog_ref, de_ref, nw_ref,
                e_ref, wout_ref, gate_ref, o_ref, state_ref, ext_ref, act_ref, yn_ref):
    q = SSD_CHUNK
    c = pl.program_id(1)

    @pl.when(c == 0)
    def _():
        state_ref[...] = jnp.zeros_like(state_ref)
        ext_ref[0:CONV_HALO, :] = jnp.zeros((CONV_HALO, SSD_CONV_DIM), F32)

    @pl.when(c > 0)
    def _():
        ext_ref[0:CONV_HALO, :] = ext_ref[q:q + CONV_HALO, :]

    ext_ref[CONV_HALO:CONV_HALO + q, :] = xbc_ref[...]

    for s in range(SSD_CONV_DIM // CONV_COLS):
        cs = slice(s * CONV_COLS, (s + 1) * CONV_COLS)
        acc = cb_ref[:, cs] + cw_ref[SSD_CONV - 1:SSD_CONV, cs] * ext_ref[CONV_HALO:CONV_HALO + q, cs]
        for k in range(SSD_CONV - 1):
            off = CONV_HALO - (SSD_CONV - 1) + k
            acc = acc + cw_ref[k:k + 1, cs] * ext_ref[off:off + q, cs]
        act_ref[:, cs] = _silu(acc)

    dt_raw = dt_ref[...] + dtb_ref[...]
    dt = jnp.maximum(dt_raw, 0.0) + jnp.log1p(jnp.exp(-jnp.abs(dt_raw)))
    a = dt * (-jnp.exp(alog_ref[...]))
    row = lax.broadcasted_iota(jnp.int32, (q, q), 0)
    col = lax.broadcasted_iota(jnp.int32, (q, q), 1)
    tril = row >= col
    acs = _dot3_exact_lhs(tril.astype(BF16), a)
    acs_t = acs.T
    expand = e_ref[...]
    acs_e = _dot3_exact_rhs(acs, expand)
    dt_e = _dot3_exact_rhs(dt, expand)
    tot_e = acs_e[q - 1:q, :]
    decay_from_start = jnp.exp(acs_e)
    decay_to_end = jnp.exp(tot_e - acs_e)
    chunk_decay = jnp.exp(tot_e)

    lane_head = lax.broadcasted_iota(jnp.int32, (q, SSD_GROUP_W), 1) // SSD_HEAD_DIM
    for g in range(SSD_GROUPS):
        xs = act_ref[:, g * SSD_GROUP_W:(g + 1) * SSD_GROUP_W]
        gs = slice(g * SSD_GROUP_W, (g + 1) * SSD_GROUP_W)
        bm = act_ref[:, SSD_D_INNER + g * SSD_STATE:SSD_D_INNER + (g + 1) * SSD_STATE]
        cm = act_ref[:, SSD_D_INNER + SSD_GN + g * SSD_STATE:SSD_D_INNER + SSD_GN + (g + 1) * SSD_STATE]
        bm_t = bm.T.astype(BF16)
        cm_b = cm.astype(BF16)
        cb = _dot(cm_b, bm_t)
        xd = xs * dt_e[:, gs]
        ms = []
        xds = []
        for j in range(SSD_HEADS_PER_GROUP):
            h = g * SSD_HEADS_PER_GROUP + j
            seg = acs[:, h:h + 1] - acs_t[h:h + 1, :]
            dec = jnp.exp(jnp.where(tril, seg, -jnp.inf))
            ms.append((cb * dec).astype(BF16))
            xds.append(jnp.where(lane_head == j, xd, 0.0).astype(BF16))
        y_diag = _dot(jnp.concatenate(ms, axis=1), jnp.concatenate(xds, axis=0))
        prev = state_ref[g]
        y_off = _dot(cm_b, prev.astype(BF16)) * decay_from_start[:, gs]
        state_ref[g] = prev * chunk_decay[:, gs] + _dot(bm_t, (xd * decay_to_end[:, gs]).astype(BF16))
        y = y_diag + y_off + xs * de_ref[:, gs]
        y = y * _silu(z_ref[:, gs])
        y = y * lax.rsqrt(jnp.mean(y * y, axis=-1, keepdims=True) + EPS) * nw_ref[:, gs]
        yn_ref[:, gs] = y.astype(BF16)

    out = _dot(yn_ref[...], wout_ref[...])
    o_ref[...] = xres_ref[...] + gate_ref[...] * out


def _ssd_mixer(z, xbc, dt, x, conv_w, conv_b, dt_bias, a_log, d_e, norm_w, expand, w_out, mod5, layer):
    q = SSD_CHUNK
    nc = SEQ // q
    tok = lambda w: pl.BlockSpec((q, w), lambda b, c: (b * nc + c, 0))
    full = lambda r, w: pl.BlockSpec((r, w), lambda b, c: (0, 0))
    return pl.pallas_call(
        _ssd_kernel,
        grid=(BATCH, nc),
        in_specs=[
            tok(SSD_D_INNER), tok(SSD_CONV_DIM), tok(SSD_DT_PAD), tok(D_MODEL),
            full(SSD_CONV, SSD_CONV_DIM), full(1, SSD_CONV_DIM), full(1, SSD_DT_PAD), full(1, SSD_DT_PAD),
            full(1, SSD_D_INNER), full(1, SSD_D_INNER), full(SSD_DT_PAD, SSD_D_INNER),
            full(SSD_D_INNER, D_MODEL),
            pl.BlockSpec((None, None, None, 1, D_MODEL), lambda b, c: (layer, b, 2, 0, 0)),
        ],
        out_specs=tok(D_MODEL),
        out_shape=jax.ShapeDtypeStruct((TOKENS, D_MODEL), F32),
        scratch_shapes=[
            pltpu.VMEM((SSD_GROUPS, SSD_STATE, SSD_GROUP_W), F32),
            pltpu.VMEM((q + CONV_HALO, SSD_CONV_DIM), F32),
            pltpu.VMEM((q, SSD_CONV_DIM), F32),
            pltpu.VMEM((q, SSD_D_INNER), BF16),
        ],
        compiler_params=pltpu.CompilerParams(
            dimension_semantics=("arbitrary", "arbitrary"), vmem_limit_bytes=VMEM_LIMIT),
        name="ssd_mixer",
    )(z, xbc, dt, x, conv_w, conv_b, dt_bias, a_log, d_e, norm_w, expand, w_out, mod5)


DSA_KEY_TILE = 256
DSA_CLASSES = 8
V_EXT = 2 * ATT_HEAD_DIM
DSA_BLOCKS_PER_CLASS = (SEQ // Q_BLOCK) // DSA_CLASSES
N_BISECT = 12
F32_MIN = float(jnp.finfo(jnp.float32).min)
LOG2E = 1.4426950408889634


def _count(mask):
    return jnp.sum(jnp.where(mask, 1.0, 0.0), axis=-1, keepdims=True)


def _select_topk(score_ref, q_pos, n_keys):
    kf = float(TOPK)
    small = (q_pos + 1) <= TOPK
    sc = score_ref[...]
    hi0 = jnp.max(sc, axis=-1, keepdims=True)
    lo0 = jnp.min(jnp.where(sc == -jnp.inf, jnp.inf, sc), axis=-1, keepdims=True)

    def bisect(_, carry):
        lo, hi = carry
        mid = lo + 0.5 * (hi - lo)
        ok = _count(score_ref[...] >= mid) >= kf
        return jnp.where(ok, mid, lo), jnp.where(ok, hi, mid)

    _, hi = lax.fori_loop(0, N_BISECT, bisect, (lo0, hi0))

    v0 = jnp.max(jnp.where(sc <= hi, sc, -jnp.inf), axis=-1, keepdims=True)
    c0 = _count(sc >= v0)
    pend0 = jnp.where((c0 >= kf) | small, 0.0, 1.0)

    def walk_cond(carry):
        return (carry[2] > 0.0) & (carry[3] < n_keys)

    def walk(carry):
        v, pend, _, it = carry
        s = score_ref[...]
        v2 = jnp.max(jnp.where(s < v, s, -jnp.inf), axis=-1, keepdims=True)
        c2 = _count(s >= v2)
        v = jnp.where(pend > 0.0, v2, v)
        pend = jnp.where(c2 >= kf, 0.0, pend)
        return v, pend, jnp.max(pend), it + 1

    v, _, _, _ = lax.while_loop(walk_cond, walk, (v0, pend0, jnp.max(pend0), jnp.int32(0)))
    thr = jnp.where(small, F32_MIN, v)

    key_pos = lax.broadcasted_iota(jnp.int32, (Q_BLOCK, n_keys), 1)
    gt = sc > thr
    eq = sc == thr
    need = kf - _count(gt)
    tie = jnp.where(_count(eq) > need, 1.0, 0.0)

    def search_cut():
        def body(_, carry):
            lo, hi = carry
            mid = (lo + hi) >> 1
            s = score_ref[...]
            ok = _count((s == thr) & (key_pos <= mid)) >= need
            return jnp.where(ok, lo, mid), jnp.where(ok, mid, hi)

        init = (jnp.full((Q_BLOCK, 1), -1, jnp.int32), jnp.full((Q_BLOCK, 1), n_keys - 1, jnp.int32))
        return lax.fori_loop(0, (n_keys - 1).bit_length() + 1, body, init)[1]

    cut = lax.cond(jnp.max(tie) > 0.0, search_cut, lambda: jnp.full((Q_BLOCK, 1), n_keys - 1, jnp.int32))
    score_ref[...] = jnp.where(gt | (eq & (key_pos <= cut)), 0.0, -jnp.inf)


def _dsa_inproj_kernel(x_ref, nw_ref, scale_ref, shift_ref, w_ref, qn_ref, kn_ref, seg_ref, segt_ref,
                       q_ref, k_ref, v_ref, qi_ref, ki_ref, wi_ref):
    hd = ATT_HEAD_DIM
    h = _norm_mod(x_ref[...], nw_ref[...], scale_ref[...], shift_ref[...]).astype(BF16)

    def head_norm(t, w):
        width = t.shape[1]
        ss = _dot((t * t).astype(BF16), seg_ref[0:width, :])
        r = lax.rsqrt(ss * (1.0 / hd) + EPS)
        r_hi = r.astype(BF16)
        r_lo = (r - r_hi.astype(F32)).astype(BF16)
        return t * (_dot(r_hi, segt_ref[:, 0:width]) + _dot(r_lo, segt_ref[:, 0:width])) * w

    q = head_norm(_dot(h, w_ref[:, 0:DSA_Q]), qn_ref[...] * (hd ** -0.5 * LOG2E))
    for n in range(ATT_HEADS):
        q_ref[n] = q[:, n * hd:(n + 1) * hd].astype(BF16)
    kv = _dot(h, w_ref[:, DSA_Q:DSA_Q + 2 * DSA_KV])
    k = head_norm(kv[:, 0:DSA_KV], kn_ref[...])
    for n in range(ATT_KV_HEADS):
        k_ref[n] = k[:, n * hd:(n + 1) * hd].astype(BF16)
        v_ref[n] = jnp.concatenate([kv[:, DSA_KV + n * hd:DSA_KV + (n + 1) * hd],
                                    jnp.ones((kv.shape[0], V_EXT - hd), F32)], axis=1).astype(BF16)
    qi = _dot(h, w_ref[:, DSA_Q + 2 * DSA_KV:DSA_KI_START])
    for n in range(IDX_HEADS):
        qi_ref[n] = qi[:, n * IDX_HEAD_DIM:(n + 1) * IDX_HEAD_DIM].astype(BF16)
    ki_ref[...] = _dot(h, w_ref[:, DSA_KI_START:DSA_KI_START + IDX_HEAD_DIM]).astype(BF16)
    wi_ref[...] = _dot(h, w_ref[:, DSA_WI_START:DSA_WI_START + IDX_HEADS]) * ((IDX_HEADS * IDX_HEAD_DIM) ** -0.5)


def _dsa_inproj(x, nw, mod5, layer, w, q_norm, k_norm):
    tm = INPROJ_TM
    tiles_per_batch = SEQ // tm
    heads = lambda n: pl.BlockSpec((n, tm, ATT_HEAD_DIM), lambda i: (0, i, 0))
    head_of = jnp.arange(DSA_Q, dtype=jnp.int32) // ATT_HEAD_DIM
    seg = (head_of[:, None] == jnp.arange(128, dtype=jnp.int32)[None, :]).astype(BF16)
    q_norm = jnp.tile(q_norm, (1, ATT_HEADS))
    k_norm = jnp.tile(k_norm, (1, ATT_KV_HEADS))
    return pl.pallas_call(
        _dsa_inproj_kernel,
        grid=(TOKENS // tm,),
        in_specs=[
            pl.BlockSpec((tm, D_MODEL), lambda i: (i, 0)),
            pl.BlockSpec((1, D_MODEL), lambda i: (0, 0)),
            _mod_spec(layer, 1, tiles_per_batch),
            _mod_spec(layer, 0, tiles_per_batch),
            pl.BlockSpec((D_MODEL, DSA_PROJ_PAD), lambda i: (0, 0)),
            pl.BlockSpec((1, DSA_Q), lambda i: (0, 0)),
            pl.BlockSpec((1, DSA_KV), lambda i: (0, 0)),
            pl.BlockSpec((DSA_Q, 128), lambda i: (0, 0)),
            pl.BlockSpec((128, DSA_Q), lambda i: (0, 0)),
        ],
        out_specs=[heads(ATT_HEADS), heads(ATT_KV_HEADS),
                   pl.BlockSpec((ATT_KV_HEADS, tm, V_EXT), lambda i: (0, i, 0)), heads(IDX_HEADS),
                   pl.BlockSpec((tm, IDX_HEAD_DIM), lambda i: (i, 0)),
                   pl.BlockSpec((tm, IDX_HEADS), lambda i: (i, 0))],
        out_shape=[jax.ShapeDtypeStruct((ATT_HEADS, TOKENS, ATT_HEAD_DIM), BF16),
                   jax.ShapeDtypeStruct((ATT_KV_HEADS, TOKENS, ATT_HEAD_DIM), BF16),
                   jax.ShapeDtypeStruct((ATT_KV_HEADS, TOKENS, V_EXT), BF16),
                   jax.ShapeDtypeStruct((IDX_HEADS, TOKENS, IDX_HEAD_DIM), BF16),
                   jax.ShapeDtypeStruct((TOKENS, IDX_HEAD_DIM), BF16),
                   jax.ShapeDtypeStruct((TOKENS, IDX_HEADS), F32)],
        compiler_params=pltpu.CompilerParams(
            dimension_semantics=("arbitrary",), vmem_limit_bytes=VMEM_LIMIT),
        name="dsa_inproj",
    )(x, nw, mod5, mod5, w, q_norm, k_norm, seg, seg.T)


def _dsa_kernel(q_ref, k_ref, v_ref, qi_ref, ki_ref, wi_ref, xres_ref, wout_ref, gate_ref, o_ref,
                score_ref, ocat_ref, *, n_keys, first_block):
    hd = ATT_HEAD_DIM
    q_pos = (first_block + pl.program_id(1)) * Q_BLOCK + lax.broadcasted_iota(jnp.int32, (Q_BLOCK, 1), 0)

    wi = wi_ref[...]
    qi = qi_ref[...].reshape(IDX_HEADS * Q_BLOCK, IDX_HEAD_DIM)
    for kt in range(n_keys // DSA_KEY_TILE):
        ks = slice(kt * DSA_KEY_TILE, (kt + 1) * DSA_KEY_TILE)
        raw = _dot_nt(qi, ki_ref[ks, :])
        acc = jnp.zeros((Q_BLOCK, DSA_KEY_TILE), F32)
        for n in range(IDX_HEADS):
            acc = acc + wi[:, n:n + 1] * jnp.maximum(raw[n * Q_BLOCK:(n + 1) * Q_BLOCK, :], 0.0)
        key_pos = kt * DSA_KEY_TILE + lax.broadcasted_iota(jnp.int32, (Q_BLOCK, DSA_KEY_TILE), 1)
        score_ref[:, ks] = jnp.where(key_pos <= q_pos, acc, -jnp.inf)

    if n_keys > TOPK:
        _select_topk(score_ref, q_pos, n_keys)
        bias = score_ref[...][None, :, :]
    else:
        bias = jnp.where(score_ref[...] == -jnp.inf, -jnp.inf, 0.0)[None, :, :]

    for n in range(ATT_KV_HEADS):
        q4 = q_ref[n * ATT_Q_PER_KV:(n + 1) * ATT_Q_PER_KV].reshape(ATT_Q_PER_KV * Q_BLOCK, hd)
        s = _dot_nt(q4, k_ref[n]).reshape(ATT_Q_PER_KV, Q_BLOCK, n_keys) + bias
        p = jnp.exp2(s - jnp.max(s, axis=-1, keepdims=True))
        o = _dot(p.reshape(ATT_Q_PER_KV * Q_BLOCK, n_keys).astype(BF16), v_ref[n])
        o = o[:, 0:hd] * (1.0 / o[:, hd:hd + 1])
        for g in range(ATT_Q_PER_KV):
            col = (n * ATT_Q_PER_KV + g) * hd
            ocat_ref[:, col:col + hd] = o[g * Q_BLOCK:(g + 1) * Q_BLOCK, :]
    out = _dot(ocat_ref[...].astype(BF16), wout_ref[...])
    o_ref[...] = xres_ref[...] + gate_ref[...] * out


def _dsa_mixer(q, k, v, qi, ki, wi, x, w_out, mod5, layer):
    nb = SEQ // Q_BLOCK
    k4 = k.reshape(ATT_KV_HEADS, BATCH, SEQ, ATT_HEAD_DIM)
    v4 = v.reshape(ATT_KV_HEADS, BATCH, SEQ, V_EXT)
    ki3 = ki.reshape(BATCH, SEQ, IDX_HEAD_DIM)
    for cls in range(DSA_CLASSES):
        n_keys = (cls + 1) * (SEQ // DSA_CLASSES)
        first_block = cls * DSA_BLOCKS_PER_CLASS
        row = lambda b, i, fb=first_block: b * nb + fb + i
        heads = lambda n: pl.BlockSpec((n, Q_BLOCK, ATT_HEAD_DIM), lambda b, i: (0, row(b, i), 0))
        keys = lambda width: pl.BlockSpec((ATT_KV_HEADS, None, n_keys, width), lambda b, i: (0, b, 0, 0))
        x = pl.pallas_call(
            functools.partial(_dsa_kernel, n_keys=n_keys, first_block=first_block),
            grid=(BATCH, DSA_BLOCKS_PER_CLASS),
            in_specs=[
                heads(ATT_HEADS), keys(ATT_HEAD_DIM), keys(V_EXT), heads(IDX_HEADS),
                pl.BlockSpec((None, n_keys, IDX_HEAD_DIM), lambda b, i: (b, 0, 0)),
                pl.BlockSpec((Q_BLOCK, IDX_HEADS), lambda b, i: (row(b, i), 0)),
                pl.BlockSpec((Q_BLOCK, D_MODEL), lambda b, i: (row(b, i), 0)),
                pl.BlockSpec((DSA_Q, D_MODEL), lambda b, i: (0, 0)),
                pl.BlockSpec((None, None, None, 1, D_MODEL), lambda b, i: (layer, b, 2, 0, 0)),
            ],
            out_specs=pl.BlockSpec((Q_BLOCK, D_MODEL), lambda b, i: (row(b, i), 0)),
            out_shape=jax.ShapeDtypeStruct((TOKENS, D_MODEL), F32),
            scratch_shapes=[
                pltpu.VMEM((Q_BLOCK, n_keys), F32),
                pltpu.VMEM((Q_BLOCK, D_MODEL), F32),
            ],
            input_output_aliases={6: 0},
            compiler_params=pltpu.CompilerParams(
                dimension_semantics=("arbitrary", "arbitrary"), vmem_limit_bytes=VMEM_LIMIT),
            name=f"dsa_mixer_c{cls}",
        )(q, k4, v4, qi, ki3, wi, x, w_out, mod5)
    return x


MOE_PAIRS = MOE_EPG * (MOE_EPG - 1) // 2
MOE_CLASSES = MOE_GROUPS * MOE_PAIRS
PAIR_LO = (0, 0, 0, 1, 1, 2)
PAIR_HI = (1, 2, 3, 2, 3, 3)
ROUTE_TM = 1024
META_W = 128
META_CLASS, META_RANK, META_WLO, META_WHI = 0, 1, 2, 3
TILE_ROWS, TILE_LANES = 8, 128
H_WORDS = D_MODEL // 2
H_SUBLANES = H_WORDS // TILE_LANES
SORT_BLOCK = 256
N_SORT_BLOCKS = TOKENS // SORT_BLOCK
MAX_ITEMS = N_SORT_BLOCKS + MOE_CLASSES
PERMUTE_TM = 256
COMBINE_TM = 256


def _route_kernel(x_ref, nw_ref, scale_ref, shift_ref, wr_ref, br_ref, pay_ref, cls_ref, rank_ref, cnt_ref,
                  carry_ref):
    tm = ROUTE_TM

    @pl.when(pl.program_id(0) == 0)
    def _():
        carry_ref[...] = jnp.zeros_like(carry_ref)

    h = _norm_mod(x_ref[...], nw_ref[...], scale_ref[...], shift_ref[...])
    logits = _dot(h.astype(BF16), wr_ref[...]) + br_ref[...]
    lane = lax.broadcasted_iota(jnp.int32, logits.shape, 1)
    neg = -jnp.inf
    big = jnp.int32(ROUTE_PAD)
    is_group = (lane >= MOE_EXPERTS) & (lane < MOE_EXPERTS + MOE_GROUPS)
    gl = jnp.where(is_group, logits, neg)
    g_max = jnp.max(gl, axis=-1, keepdims=True)
    g_idx = jnp.min(jnp.where(gl == g_max, lane - MOE_EXPERTS, big), axis=-1, keepdims=True)
    g_val = 1.0 / jnp.sum(jnp.exp(gl - g_max), axis=-1, keepdims=True)
    in_group = (lane < MOE_EXPERTS) & ((lane // MOE_EPG) == g_idx)
    el = jnp.where(in_group, logits, neg)
    m1 = jnp.max(el, axis=-1, keepdims=True)
    i1 = jnp.min(jnp.where(el == m1, lane, big), axis=-1, keepdims=True)
    el2 = jnp.where(lane == i1, neg, el)
    m2 = jnp.max(el2, axis=-1, keepdims=True)
    i2 = jnp.min(jnp.where(el2 == m2, lane, big), axis=-1, keepdims=True)
    r = jnp.exp(m2 - m1)
    w_top1 = g_val / (1.0 + r)
    w_top2 = g_val * r / (1.0 + r)

    lo = jnp.minimum(i1, i2) - g_idx * MOE_EPG
    hi = jnp.maximum(i1, i2) - g_idx * MOE_EPG
    pair = (lo * (2 * MOE_EPG - 1 - lo)) // 2 + (hi - lo - 1)
    cls = g_idx * MOE_PAIRS + pair
    w_lo = jnp.where(i1 < i2, w_top1, w_top2)
    w_hi = jnp.where(i1 < i2, w_top2, w_top1)

    onehot = lane == cls
    row = lax.broadcasted_iota(jnp.int32, (tm, tm), 0)
    col = lax.broadcasted_iota(jnp.int32, (tm, tm), 1)
    before = _dot((row > col).astype(BF16), onehot.astype(BF16)) + carry_ref[...]
    rank = jnp.sum(jnp.where(onehot, before, 0.0), axis=-1, keepdims=True)
    carry_ref[...] += jnp.sum(jnp.where(onehot, 1.0, 0.0), axis=0, keepdims=True)
    cnt_ref[...] = carry_ref[...]

    meta = jnp.where(lane == META_CLASS, cls.astype(F32),
                     jnp.where(lane == META_RANK, rank,
                               jnp.where(lane == META_WLO, w_lo, jnp.where(lane == META_WHI, w_hi, 0.0))))

    eye = lax.broadcasted_iota(jnp.int32, (128, 128), 0) == lax.broadcasted_iota(jnp.int32, (128, 128), 1)

    def column_to_rows(v):
        return jnp.concatenate([jnp.sum(jnp.where(eye, v[b * 128:(b + 1) * 128, :], 0.0), axis=0, keepdims=True)
                                for b in range(tm // 128)], axis=0)

    cls_ref[...] = column_to_rows(cls.astype(F32)).astype(jnp.int32)
    rank_ref[...] = column_to_rows(rank).astype(jnp.int32)
    words = pltpu.pack_elementwise([h[:, 0:H_WORDS], h[:, H_WORDS:D_MODEL]], packed_dtype=BF16)
    for s in range(H_SUBLANES):
        pay_ref[pl.ds(s, tm, stride=TILE_ROWS), :] = words[:, s * TILE_LANES:(s + 1) * TILE_LANES]
    pay_ref[pl.ds(H_SUBLANES, tm, stride=TILE_ROWS), :] = lax.bitcast_convert_type(meta, jnp.int32)
    for s in range(H_SUBLANES + 1, TILE_ROWS):
        pay_ref[pl.ds(s, tm, stride=TILE_ROWS), :] = jnp.zeros((tm, TILE_LANES), jnp.int32)


def _moe_route(x, nw, mod5, layer, w_route, b_route):
    tm = ROUTE_TM
    tiles_per_batch = SEQ // tm
    return pl.pallas_call(
        _route_kernel,
        grid=(TOKENS // tm,),
        in_specs=[
            pl.BlockSpec((tm, D_MODEL), lambda i: (i, 0)),
            pl.BlockSpec((1, D_MODEL), lambda i: (0, 0)),
            _mod_spec(layer, 4, tiles_per_batch),
            _mod_spec(layer, 3, tiles_per_batch),
            pl.BlockSpec((D_MODEL, ROUTE_PAD), lambda i: (0, 0)),
            pl.BlockSpec((1, ROUTE_PAD), lambda i: (0, 0)),
        ],
        out_specs=[pl.BlockSpec((tm * TILE_ROWS, TILE_LANES), lambda i: (i, 0)),
                   pl.BlockSpec((tm // 128, 128), lambda i: (i, 0)),
                   pl.BlockSpec((tm // 128, 128), lambda i: (i, 0)),
                   pl.BlockSpec((1, ROUTE_PAD), lambda i: (0, 0))],
        out_shape=[jax.ShapeDtypeStruct((TOKENS * TILE_ROWS, TILE_LANES), jnp.int32),
                   jax.ShapeDtypeStruct((TOKENS // 128, 128), jnp.int32),
                   jax.ShapeDtypeStruct((TOKENS // 128, 128), jnp.int32),
                   jax.ShapeDtypeStruct((1, ROUTE_PAD), F32)],
        scratch_shapes=[pltpu.VMEM((1, ROUTE_PAD), F32)],
        compiler_params=pltpu.CompilerParams(
            dimension_semantics=("arbitrary",), vmem_limit_bytes=VMEM_LIMIT),
        name="moe_route",
    )(x, nw, mod5, mod5, w_route, b_route)


def _permute_kernel(pos_ref, src_ref, dst_hbm, stage_ref, sem):
    i = pl.program_id(0)
    slot = i % 2
    rows = PERMUTE_TM * TILE_ROWS

    def slot_wait(s):
        whole = stage_ref.at[pl.ds(s * rows, rows), :]
        pltpu.make_async_copy(whole, whole, sem.at[s]).wait()

    @pl.when(i >= 2)
    def _():
        slot_wait(slot)

    base = pl.multiple_of(slot * rows, rows)
    stage_ref[pl.ds(base, rows), :] = src_ref[...]

    def issue(r, _):
        pltpu.make_async_copy(stage_ref.at[pl.ds(base + r * TILE_ROWS, TILE_ROWS), :],
                              dst_hbm.at[pos_ref[i * PERMUTE_TM + r]], sem.at[slot]).start()
        return 0

    lax.fori_loop(0, PERMUTE_TM, issue, 0, unroll=16)

    @pl.when(i == pl.num_programs(0) - 1)
    def _():
        slot_wait(1 - slot)
        slot_wait(slot)


def _moe_permute(pos, payload):
    rows = PERMUTE_TM * TILE_ROWS
    return pl.pallas_call(
        _permute_kernel,
        grid_spec=pltpu.PrefetchScalarGridSpec(
            num_scalar_prefetch=1, grid=(TOKENS // PERMUTE_TM,),
            in_specs=[pl.BlockSpec((rows, TILE_LANES), lambda i, pos: (i, 0))],
            out_specs=pl.BlockSpec(memory_space=pl.ANY),
            scratch_shapes=[pltpu.VMEM((2 * rows, TILE_LANES), jnp.int32), pltpu.SemaphoreType.DMA((2,))]),
        out_shape=jax.ShapeDtypeStruct((TOKENS, TILE_ROWS, TILE_LANES), jnp.int32),
        compiler_params=pltpu.CompilerParams(
            dimension_semantics=("arbitrary",), vmem_limit_bytes=VMEM_LIMIT),
        name="moe_permute",
    )(pos, payload)


def _experts_kernel(blk_ref, cls_ref, elo_ref, ehi_ref, first_ref, last_ref, valid_ref,
                    pay_ref, w1lo_ref, w3lo_ref, w2lo_ref, w1hi_ref, w3hi_ref, w2hi_ref, o_ref, acc_ref):
    k = pl.program_id(0)
    rows = SORT_BLOCK

    @pl.when(valid_ref[k] == 1)
    def _():
        def sublane(s):
            return pay_ref[pl.ds(s, rows, stride=TILE_ROWS), :]

        halves = [[pltpu.unpack_elementwise(sublane(s), index=i, packed_dtype=BF16, unpacked_dtype=F32)
                   for s in range(H_SUBLANES)] for i in range(2)]
        hb = jnp.concatenate(halves[0] + halves[1], axis=1).astype(BF16)
        meta = lax.bitcast_convert_type(sublane(H_SUBLANES), F32)
        mine = meta[:, META_CLASS:META_CLASS + 1] == cls_ref[k].astype(F32)
        w_lo = jnp.where(mine, meta[:, META_WLO:META_WLO + 1], 0.0)
        w_hi = jnp.where(mine, meta[:, META_WHI:META_WHI + 1], 0.0)
        bf = lambda w_ref: w_ref[...].astype(BF16)
        hid_lo = _silu(_dot(hb, bf(w1lo_ref))) * _dot(hb, bf(w3lo_ref)) * w_lo
        hid_hi = _silu(_dot(hb, bf(w1hi_ref))) * _dot(hb, bf(w3hi_ref)) * w_hi
        y = _dot(hid_lo.astype(BF16), bf(w2lo_ref)) + _dot(hid_hi.astype(BF16), bf(w2hi_ref))

        @pl.when(first_ref[k] == 1)
        def _():
            acc_ref[...] = y

        @pl.when(first_ref[k] == 0)
        def _():
            acc_ref[...] += y

        @pl.when(last_ref[k] == 1)
        def _():
            for s in range(TILE_ROWS):
                o_ref[pl.ds(s, rows, stride=TILE_ROWS), :] = acc_ref[:, s * TILE_LANES:(s + 1) * TILE_LANES]


def _moe_experts(items, payload_sorted, layer, w1, w3, w2):
    blk, cls, elo, ehi, first, last, valid = items
    tiles = pl.BlockSpec((SORT_BLOCK * TILE_ROWS, TILE_LANES), lambda k, blk, *_: (blk[k], 0))
    w_in = lambda which: pl.BlockSpec(
        (None, None, D_MODEL, MOE_HIDDEN), lambda k, blk, cls, elo, ehi, *_: (layer, (elo, ehi)[which][k], 0, 0))
    w_out = lambda which: pl.BlockSpec(
        (None, None, MOE_HIDDEN, D_MODEL), lambda k, blk, cls, elo, ehi, *_: (layer, (elo, ehi)[which][k], 0, 0))
    return pl.pallas_call(
        _experts_kernel,
        grid_spec=pltpu.PrefetchScalarGridSpec(
            num_scalar_prefetch=7, grid=(MAX_ITEMS,),
            in_specs=[tiles, w_in(0), w_in(0), w_out(0), w_in(1), w_in(1), w_out(1)],
            out_specs=tiles,
            scratch_shapes=[pltpu.VMEM((SORT_BLOCK, D_MODEL), F32)]),
        out_shape=jax.ShapeDtypeStruct((TOKENS * TILE_ROWS, TILE_LANES), F32),
        compiler_params=pltpu.CompilerParams(
            dimension_semantics=("arbitrary",), vmem_limit_bytes=VMEM_LIMIT),
        name="moe_experts",
    )(blk, cls, elo, ehi, first, last, valid, payload_sorted, w1, w3, w2, w1, w3, w2)


def _combine_kernel(pos_ref, x_ref, gate_ref, y_hbm, o_ref, buf_ref, sem):
    tm = COMBINE_TM
    i = pl.program_id(0)
    n = pl.num_programs(0)

    def gather(tile, slot):
        def issue(r, _):
            pltpu.make_async_copy(y_hbm.at[pos_ref[tile * tm + r]],
                                  buf_ref.at[pl.ds((slot * tm + r) * TILE_ROWS, TILE_ROWS), :], sem.at[slot]).start()
            return 0
        lax.fori_loop(0, tm, issue, 0, unroll=16)

    @pl.when(i == 0)
    def _():
        gather(0, 0)

    @pl.when(i + 1 < n)
    def _():
        gather(i + 1, (i + 1) % 2)

    slot = i % 2
    base = slot * tm * TILE_ROWS
    whole_slot = buf_ref.at[pl.ds(base, tm * TILE_ROWS), :]
    pltpu.make_async_copy(whole_slot, whole_slot, sem.at[slot]).wait()
    y = jnp.concatenate([buf_ref[pl.ds(base + s, tm, stride=TILE_ROWS), :] for s in range(TILE_ROWS)], axis=1)
    o_ref[...] = x_ref[...] + gate_ref[...] * y


def _moe_combine(pos, x, mod5, layer, y_sorted):
    tm = COMBINE_TM
    tiles_per_batch = SEQ // tm
    return pl.pallas_call(
        _combine_kernel,
        grid_spec=pltpu.PrefetchScalarGridSpec(
            num_scalar_prefetch=1, grid=(TOKENS // tm,),
            in_specs=[
                pl.BlockSpec((tm, D_MODEL), lambda i, pos: (i, 0)),
                pl.BlockSpec((None, None, None, 1, D_MODEL),
                             lambda i, pos: (layer, i // tiles_per_batch, 5, 0, 0)),
                pl.BlockSpec(memory_space=pl.ANY),
            ],
            out_specs=pl.BlockSpec((tm, D_MODEL), lambda i, pos: (i, 0)),
            scratch_shapes=[pltpu.VMEM((2 * tm * TILE_ROWS, TILE_LANES), F32), pltpu.SemaphoreType.DMA((2,))]),
        out_shape=jax.ShapeDtypeStruct((TOKENS, D_MODEL), F32),
        compiler_params=pltpu.CompilerParams(
            dimension_semantics=("arbitrary",), vmem_limit_bytes=VMEM_LIMIT),
        name="moe_combine",
    )(pos, x, mod5, y_sorted)


def _moe_plan(cls, rank, counts):
    count = counts[0, :MOE_CLASSES].astype(jnp.int32)
    ends = jnp.cumsum(count)
    starts = ends - count
    pos = (starts[cls] + rank).reshape(TOKENS)

    first_blk = starts // SORT_BLOCK
    n_items = jnp.where(count > 0, (ends - 1) // SORT_BLOCK - first_blk + 1, 0)
    item_end = jnp.cumsum(n_items)
    item_start = item_end - n_items
    k = jnp.arange(MAX_ITEMS, dtype=jnp.int32)
    valid = k < item_end[-1]
    kc = jnp.minimum(k, item_end[-1] - 1)
    icls = jnp.searchsorted(item_end, kc, side="right").astype(jnp.int32)
    blk = first_blk[icls] + (kc - item_start[icls])
    first = jnp.concatenate([jnp.ones((1,), jnp.int32), (blk[1:] != blk[:-1]).astype(jnp.int32)])
    last = jnp.concatenate([(blk[1:] != blk[:-1]) | ~valid[1:], jnp.ones((1,), bool)]).astype(jnp.int32)
    group = icls // MOE_PAIRS
    pair = icls % MOE_PAIRS
    elo = group * MOE_EPG + jnp.asarray(PAIR_LO, jnp.int32)[pair]
    ehi = group * MOE_EPG + jnp.asarray(PAIR_HI, jnp.int32)[pair]
    return pos, (blk, icls, elo, ehi, first * valid, last * valid, valid.astype(jnp.int32))


def _moe(x, nw, mod5, layer, w_route, b_route, w1, w3, w2):
    payload, cls, rank, counts = _moe_route(x, nw, mod5, layer, w_route, b_route)
    pos, items = _moe_plan(cls, rank, counts)
    tiles = (TOKENS, TILE_ROWS, TILE_LANES)
    sorted_payload = _moe_permute(pos, payload).reshape(TOKENS * TILE_ROWS, TILE_LANES)
    y_sorted = _moe_experts(items, sorted_payload, layer, w1, w3, w2)
    return _moe_combine(pos, x, mod5, layer, y_sorted.reshape(tiles))


def _pad_cols(w, width):
    return jnp.pad(w, ((0, 0), (0, width - w.shape[1])))


def kernel(x, c, ada_w, ada_b, norm_mix, norm_ffn, ssd_w_in, ssd_conv_w, ssd_conv_b, ssd_dt_bias,
           ssd_a_log, ssd_d, ssd_norm, ssd_w_out, dsa_w_in, dsa_q_norm, dsa_k_norm, dsa_w_out,
           moe_w_group, moe_b_group, moe_w_expert, moe_b_expert, moe_w1, moe_w3, moe_w2):
    depth = ada_w.shape[0]
    xt = x.reshape(TOKENS, D_MODEL)
    mod = _modulation(c, ada_w, ada_b)
    mod5 = mod.reshape(depth, BATCH, 6, 1, D_MODEL)

    head_of_col = jnp.arange(SSD_D_INNER, dtype=jnp.int32) // SSD_HEAD_DIM
    expand = (jnp.arange(SSD_DT_PAD, dtype=jnp.int32)[:, None] == head_of_col[None, :]).astype(BF16)

    for i in range(depth):
        j = i // 2
        nw_mix = norm_mix[i].reshape(1, D_MODEL)
        if i % 2 == 0:
            w_in = _pad_cols(ssd_w_in[j], SSD_PROJ_PAD).astype(BF16)
            z, xbc, dt = _inproj(
                xt, nw_mix, mod5, i, 1, 0, w_in,
                ((0, SSD_D_INNER), (SSD_D_INNER, SSD_D_INNER + SSD_CONV_DIM),
                 (SSD_D_INNER + SSD_CONV_DIM, SSD_PROJ_PAD)))
            xt = _ssd_mixer(
                z, xbc, dt, xt, ssd_conv_w[j], ssd_conv_b[j].reshape(1, SSD_CONV_DIM),
                _pad_cols(ssd_dt_bias[j].reshape(1, SSD_HEADS), SSD_DT_PAD),
                _pad_cols(ssd_a_log[j].reshape(1, SSD_HEADS), SSD_DT_PAD),
                jnp.repeat(ssd_d[j], SSD_HEAD_DIM).reshape(1, SSD_D_INNER),
                ssd_norm[j].reshape(1, SSD_D_INNER), expand, ssd_w_out[j].astype(BF16), mod5, i)
        else:
            w = dsa_w_in[j]
            w_in = jnp.concatenate(
                [_pad_cols(w[:, :DSA_KI_START + IDX_HEAD_DIM], DSA_WI_START),
                 _pad_cols(w[:, DSA_KI_START + IDX_HEAD_DIM:], 128)], axis=1).astype(BF16)
            q, k, v, qi, ki, wi = _dsa_inproj(
                xt, nw_mix, mod5, i, w_in, dsa_q_norm[j].reshape(1, ATT_HEAD_DIM),
                dsa_k_norm[j].reshape(1, ATT_HEAD_DIM))
            xt = _dsa_mixer(q, k, v, qi, ki, wi, xt, dsa_w_out[j].astype(BF16), mod5, i)

        w_route = _pad_cols(jnp.concatenate([moe_w_expert[i], moe_w_group[i]], axis=1), ROUTE_PAD).astype(BF16)
        b_route = _pad_cols(jnp.concatenate([moe_b_expert[i], moe_b_group[i]]).reshape(1, -1), ROUTE_PAD)
        xt = _moe(xt, norm_ffn[i].reshape(1, D_MODEL), mod5, i, w_route, b_route,
                  moe_w1, moe_w3, moe_w2)
    return xt.reshape(BATCH, SEQ, D_MODEL)
```

```python
import functools

import jax
import jax.numpy as jnp
from jax import lax
from jax.experimental import pallas as pl
from jax.experimental.pallas import tpu as pltpu

F32 = jnp.float32
BF16 = jnp.bfloat16

D_MODEL = 1024
BATCH = 8
SEQ = 2048
TOKENS = BATCH * SEQ
EPS = 1e-6

SSD_D_INNER = 2048
SSD_HEAD_DIM = 64
SSD_HEADS = 32
SSD_GROUPS = 8
SSD_HEADS_PER_GROUP = 4
SSD_STATE = 128
SSD_CONV = 4
SSD_CHUNK = 128
SSD_GN = SSD_GROUPS * SSD_STATE
SSD_CONV_DIM = SSD_D_INNER + 2 * SSD_GN
SSD_GROUP_W = SSD_HEADS_PER_GROUP * SSD_HEAD_DIM
SSD_DT_PAD = 128
SSD_PROJ_PAD = SSD_D_INNER + SSD_CONV_DIM + SSD_DT_PAD
CONV_HALO = 8

ATT_HEADS = 16
ATT_KV_HEADS = 4
ATT_Q_PER_KV = 4
ATT_HEAD_DIM = 64
IDX_HEADS = 8
IDX_HEAD_DIM = 64
TOPK = 256
Q_BLOCK = 128
DSA_Q = ATT_HEADS * ATT_HEAD_DIM
DSA_KV = ATT_KV_HEADS * ATT_HEAD_DIM
DSA_QI = IDX_HEADS * IDX_HEAD_DIM
DSA_KI_START = DSA_Q + 2 * DSA_KV + DSA_QI
DSA_WI_START = DSA_KI_START + 128
DSA_PROJ_PAD = DSA_WI_START + 128

MOE_GROUPS = 4
MOE_EPG = 4
MOE_EXPERTS = 16
MOE_HIDDEN = 256
ROUTE_PAD = 128

VMEM_LIMIT = 56 * 1024 * 1024


def _sigmoid(v):
    return 1.0 / (1.0 + jnp.exp(-v))


def _silu(v):
    return v * _sigmoid(v)


def _split3(a):
    hi = a.astype(BF16)
    r = a - hi.astype(F32)
    mid = r.astype(BF16)
    lo = (r - mid.astype(F32)).astype(BF16)
    return hi, mid, lo


def _dot(a, b):
    return jnp.dot(a, b, preferred_element_type=F32)


def _dot_nt(a, b):
    return lax.dot_general(a, b, (((1,), (1,)), ((), ())), preferred_element_type=F32)


def _dot3_exact_rhs(a, m):
    hi, mid, lo = _split3(a)
    return _dot(hi, m) + _dot(mid, m) + _dot(lo, m)


def _dot3_exact_lhs(m, a):
    hi, mid, lo = _split3(a)
    return _dot(m, hi) + _dot(m, mid) + _dot(m, lo)


def _norm_mod(x, nw, scale, shift):
    ms = jnp.mean(x * x, axis=-1, keepdims=True)
    return x * lax.rsqrt(ms + EPS) * nw * (1.0 + scale) + shift


MOD_TN = 1536


def _mod_kernel(c_ref, w_ref, b_ref, o_ref):
    cond = _silu(c_ref[...]).astype(BF16)
    o_ref[...] = _dot(cond, w_ref[...].astype(BF16)) + b_ref[...]


def _modulation(c, ada_w, ada_b):
    depth = ada_w.shape[0]
    n = ada_w.shape[2]
    return pl.pallas_call(
        _mod_kernel,
        grid=(depth, n // MOD_TN),
        in_specs=[
            pl.BlockSpec((BATCH, D_MODEL), lambda i, j: (0, 0)),
            pl.BlockSpec((None, D_MODEL, MOD_TN), lambda i, j: (i, 0, j)),
            pl.BlockSpec((None, 1, MOD_TN), lambda i, j: (i, 0, j)),
        ],
        out_specs=pl.BlockSpec((None, BATCH, MOD_TN), lambda i, j: (i, 0, j)),
        out_shape=jax.ShapeDtypeStruct((depth, BATCH, n), F32),
        compiler_params=pltpu.CompilerParams(
            dimension_semantics=("arbitrary", "arbitrary"), vmem_limit_bytes=VMEM_LIMIT),
        name="adaln_mod",
    )(c, ada_w, ada_b.reshape(depth, 1, n))


def _mod_spec(layer, chunk, rows_per_batch_tile):
    return pl.BlockSpec((None, None, None, 1, D_MODEL),
                        lambda i, *_: (layer, i // rows_per_batch_tile, chunk, 0, 0))


INPROJ_TM = 256


def _inproj_kernel(x_ref, nw_ref, scale_ref, shift_ref, w_ref, *o_refs, col_slices):
    h = _norm_mod(x_ref[...], nw_ref[...], scale_ref[...], shift_ref[...]).astype(BF16)
    for n, (o_ref, (lo, hi)) in enumerate(zip(o_refs, col_slices)):
        r = _dot(h, w_ref[:, lo:hi])
        o_ref[...] = _silu(r) if n == 0 else r


def _inproj(x, nw, mod5, layer, scale_chunk, shift_chunk, w, col_slices):
    tm = INPROJ_TM
    tiles_per_batch = SEQ // tm
    n_pad = w.shape[1]
    return pl.pallas_call(
        functools.partial(_inproj_kernel, col_slices=col_slices),
        grid=(TOKENS // tm,),
        in_specs=[
            pl.BlockSpec((tm, D_MODEL), lambda i: (i, 0)),
            pl.BlockSpec((1, D_MODEL), lambda i: (0, 0)),
            _mod_spec(layer, scale_chunk, tiles_per_batch),
            _mod_spec(layer, shift_chunk, tiles_per_batch),
            pl.BlockSpec((D_MODEL, n_pad), lambda i: (0, 0)),
        ],
        out_specs=[pl.BlockSpec((tm, hi - lo), lambda i: (i, 0)) for lo, hi in col_slices],
        out_shape=[jax.ShapeDtypeStruct((TOKENS, hi - lo), F32) for lo, hi in col_slices],
        compiler_params=pltpu.CompilerParams(
            dimension_semantics=("arbitrary",), vmem_limit_bytes=VMEM_LIMIT),
        name="norm_inproj",
    )(x, nw, mod5, mod5, w)


CONV_COLS = 512


def _ssd_kernel(z_ref, xbc_ref, dt_ref, xres_ref, cw_ref, cb_ref, dtb_ref, alog_ref, de_ref, nw_ref,
                e_ref, wout_ref, gate_ref, o_ref, state_ref, ext_ref, act_ref, yn_ref):
    q = SSD_CHUNK
    c = pl.program_id(1)

    @pl.when(c == 0)
    def _():
        state_ref[...] = jnp.zeros_like(state_ref)
        ext_ref[0:CONV_HALO, :] = jnp.zeros((CONV_HALO, SSD_CONV_DIM), F32)

    @pl.when(c > 0)
    def _():
        ext_ref[0:CONV_HALO, :] = ext_ref[q:q + CONV_HALO, :]

    ext_ref[CONV_HALO:CONV_HALO + q, :] = xbc_ref[...]

    for s in range(SSD_CONV_DIM // CONV_COLS):
        cs = slice(s * CONV_COLS, (s + 1) * CONV_COLS)
        u = ext_ref[:, cs]
        acc = cw_ref[0:1, cs] * u
        for k in range(1, SSD_CONV):
            acc = pltpu.roll(acc, 1, axis=0) + cw_ref[k:k + 1, cs] * u
        act_ref[:, cs] = _silu(acc[CONV_HALO:CONV_HALO + q, :] + cb_ref[:, cs])

    dt_raw = dt_ref[...] + dtb_ref[...]
    dt = jnp.maximum(dt_raw, 0.0) + jnp.log1p(jnp.exp(-jnp.abs(dt_raw)))
    a = dt * (-jnp.exp(alog_ref[...]) * LOG2E)
    row = lax.broadcasted_iota(jnp.int32, (q, q), 0)
    col = lax.broadcasted_iota(jnp.int32, (q, q), 1)
    tril = row >= col
    acs = _dot3_exact_lhs(tril.astype(BF16), a)
    acs_t = acs.T
    expand = e_ref[...]
    acs_e = _dot3_exact_rhs(acs, expand)
    dt_e = _dot3_exact_rhs(dt, expand)
    tot_e = acs_e[q - 1:q, :]
    decay_from_start = jnp.exp2(acs_e)
    decay_to_end = jnp.exp2(tot_e - acs_e)
    chunk_decay = jnp.exp2(tot_e)

    lane_head = lax.broadcasted_iota(jnp.int32, (q, SSD_GROUP_W), 1) // SSD_HEAD_DIM
    for g in range(SSD_GROUPS):
        xs = act_ref[:, g * SSD_GROUP_W:(g + 1) * SSD_GROUP_W]
        gs = slice(g * SSD_GROUP_W, (g + 1) * SSD_GROUP_W)
        bm = act_ref[:, SSD_D_INNER + g * SSD_STATE:SSD_D_INNER + (g + 1) * SSD_STATE]
        cm = act_ref[:, SSD_D_INNER + SSD_GN + g * SSD_STATE:SSD_D_INNER + SSD_GN + (g + 1) * SSD_STATE]
        bm_t = bm.T.astype(BF16)
        cm_b = cm.astype(BF16)
        cb = _dot(cm_b, bm_t)
        xd = xs * dt_e[:, gs]
        ms = []
        xds = []
        for j in range(SSD_HEADS_PER_GROUP):
            h = g * SSD_HEADS_PER_GROUP + j
            seg = acs[:, h:h + 1] - acs_t[h:h + 1, :]
            dec = jnp.exp2(jnp.where(tril, seg, -jnp.inf))
            ms.append((cb * dec).astype(BF16))
            xds.append(jnp.where(lane_head == j, xd, 0.0).astype(BF16))
        y_diag = _dot(jnp.concatenate(ms, axis=1), jnp.concatenate(xds, axis=0))
        prev = state_ref[g]
        y_off = _dot(cm_b, prev.astype(BF16)) * decay_from_start[:, gs]
        state_ref[g] = prev * chunk_decay[:, gs] + _dot(bm_t, (xd * decay_to_end[:, gs]).astype(BF16))
        y = y_diag + y_off + xs * de_ref[:, gs]
        y = y * z_ref[:, gs]
        y = y * lax.rsqrt(jnp.mean(y * y, axis=-1, keepdims=True) + EPS) * nw_ref[:, gs]
        yn_ref[:, gs] = y.astype(BF16)

    out = _dot(yn_ref[...], wout_ref[...])
    o_ref[...] = xres_ref[...] + gate_ref[...] * out


def _ssd_mixer(z, xbc, dt, x, conv_w, conv_b, dt_bias, a_log, d_e, norm_w, expand, w_out, mod5, layer):
    q = SSD_CHUNK
    nc = SEQ // q
    tok = lambda w: pl.BlockSpec((q, w), lambda b, c: (b * nc + c, 0))
    full = lambda r, w: pl.BlockSpec((r, w), lambda b, c: (0, 0))
    return pl.pallas_call(
        _ssd_kernel,
        grid=(BATCH, nc),
        in_specs=[
            tok(SSD_D_INNER), tok(SSD_CONV_DIM), tok(SSD_DT_PAD), tok(D_MODEL),
            full(SSD_CONV, SSD_CONV_DIM), full(1, SSD_CONV_DIM), full(1, SSD_DT_PAD), full(1, SSD_DT_PAD),
            full(1, SSD_D_INNER), full(1, SSD_D_INNER), full(SSD_DT_PAD, SSD_D_INNER),
            full(SSD_D_INNER, D_MODEL),
            pl.BlockSpec((None, None, None, 1, D_MODEL), lambda b, c: (layer, b, 2, 0, 0)),
        ],
        out_specs=tok(D_MODEL),
        out_shape=jax.ShapeDtypeStruct((TOKENS, D_MODEL), F32),
        scratch_shapes=[
            pltpu.VMEM((SSD_GROUPS, SSD_STATE, SSD_GROUP_W), F32),
            pltpu.VMEM((q + CONV_HALO, SSD_CONV_DIM), F32),
            pltpu.VMEM((q, SSD_CONV_DIM), F32),
            pltpu.VMEM((q, SSD_D_INNER), BF16),
        ],
        compiler_params=pltpu.CompilerParams(
            dimension_semantics=("arbitrary", "arbitrary"), vmem_limit_bytes=VMEM_LIMIT),
        name="ssd_mixer",
    )(z, xbc, dt, x, conv_w, conv_b, dt_bias, a_log, d_e, norm_w, expand, w_out, mod5)


DSA_KEY_TILE = 256
DSA_CLASSES = 8
V_EXT = 2 * ATT_HEAD_DIM
DSA_BLOCKS_PER_CLASS = (SEQ // Q_BLOCK) // DSA_CLASSES
N_BISECT = 12
F32_MIN = float(jnp.finfo(jnp.float32).min)
LOG2E = 1.4426950408889634


def _count(mask):
    return jnp.sum(jnp.where(mask, 1.0, 0.0), axis=-1, keepdims=True)


def _select_topk(score_ref, q_pos, n_keys):
    kf = float(TOPK)
    small = (q_pos + 1) <= TOPK
    sc = score_ref[...]
    hi0 = jnp.max(sc, axis=-1, keepdims=True)
    lo0 = jnp.min(jnp.where(sc == -jnp.inf, jnp.inf, sc), axis=-1, keepdims=True)

    def bisect(_, carry):
        lo, hi = carry
        mid = lo + 0.5 * (hi - lo)
        ok = _count(score_ref[...] >= mid) >= kf
        return jnp.where(ok, mid, lo), jnp.where(ok, hi, mid)

    _, hi = lax.fori_loop(0, N_BISECT, bisect, (lo0, hi0))

    v0 = jnp.max(jnp.where(sc <= hi, sc, -jnp.inf), axis=-1, keepdims=True)
    c0 = _count(sc >= v0)
    pend0 = jnp.where((c0 >= kf) | small, 0.0, 1.0)

    def walk_cond(carry):
        return (carry[2] > 0.0) & (carry[3] < n_keys)

    def walk(carry):
        v, pend, _, it = carry
        s = score_ref[...]
        v2 = jnp.max(jnp.where(s < v, s, -jnp.inf), axis=-1, keepdims=True)
        c2 = _count(s >= v2)
        v = jnp.where(pend > 0.0, v2, v)
        pend = jnp.where(c2 >= kf, 0.0, pend)
        return v, pend, jnp.max(pend), it + 1

    v, _, _, _ = lax.while_loop(walk_cond, walk, (v0, pend0, jnp.max(pend0), jnp.int32(0)))
    thr = jnp.where(small, F32_MIN, v)

    key_pos = lax.broadcasted_iota(jnp.int32, (Q_BLOCK, n_keys), 1)
    gt = sc > thr
    eq = sc == thr
    need = kf - _count(gt)
    tie = jnp.where(_count(eq) > need, 1.0, 0.0)

    def search_cut():
        def body(_, carry):
            lo, hi = carry
            mid = (lo + hi) >> 1
            s = score_ref[...]
            ok = _count((s == thr) & (key_pos <= mid)) >= need
            return jnp.where(ok, lo, mid), jnp.where(ok, mid, hi)

        init = (jnp.full((Q_BLOCK, 1), -1, jnp.int32), jnp.full((Q_BLOCK, 1), n_keys - 1, jnp.int32))
        return lax.fori_loop(0, (n_keys - 1).bit_length() + 1, body, init)[1]

    cut = lax.cond(jnp.max(tie) > 0.0, search_cut, lambda: jnp.full((Q_BLOCK, 1), n_keys - 1, jnp.int32))
    score_ref[...] = jnp.where(gt | (eq & (key_pos <= cut)), 0.0, -jnp.inf)


def _dsa_inproj_kernel(x_ref, nw_ref, scale_ref, shift_ref, w_ref, qn_ref, kn_ref, seg_ref, segt_ref,
                       q_ref, k_ref, v_ref, qi_ref, ki_ref, wi_ref):
    hd = ATT_HEAD_DIM
    h = _norm_mod(x_ref[...], nw_ref[...], scale_ref[...], shift_ref[...]).astype(BF16)

    def head_norm(t, w):
        width = t.shape[1]
        ss = _dot((t * t).astype(BF16), seg_ref[0:width, :])
        r = lax.rsqrt(ss * (1.0 / hd) + EPS)
        r_hi = r.astype(BF16)
        r_lo = (r - r_hi.astype(F32)).astype(BF16)
        return t * (_dot(r_hi, segt_ref[:, 0:width]) + _dot(r_lo, segt_ref[:, 0:width])) * w

    q = head_norm(_dot(h, w_ref[:, 0:DSA_Q]), qn_ref[...] * (hd ** -0.5 * LOG2E))
    for n in range(ATT_HEADS):
        q_ref[n] = q[:, n * hd:(n + 1) * hd].astype(BF16)
    kv = _dot(h, w_ref[:, DSA_Q:DSA_Q + 2 * DSA_KV])
    k = head_norm(kv[:, 0:DSA_KV], kn_ref[...])
    for n in range(ATT_KV_HEADS):
        k_ref[n] = k[:, n * hd:(n + 1) * hd].astype(BF16)
        v_ref[n] = jnp.concatenate([kv[:, DSA_KV + n * hd:DSA_KV + (n + 1) * hd],
                                    jnp.ones((kv.shape[0], V_EXT - hd), F32)], axis=1).astype(BF16)
    qi = _dot(h, w_ref[:, DSA_Q + 2 * DSA_KV:DSA_KI_START])
    for n in range(IDX_HEADS):
        qi_ref[n] = qi[:, n * IDX_HEAD_DIM:(n + 1) * IDX_HEAD_DIM].astype(BF16)
    ki_ref[...] = _dot(h, w_ref[:, DSA_KI_START:DSA_KI_START + IDX_HEAD_DIM]).astype(BF16)
    wi_ref[...] = _dot(h, w_ref[:, DSA_WI_START:DSA_WI_START + IDX_HEADS]) * ((IDX_HEADS * IDX_HEAD_DIM) ** -0.5)


def _dsa_inproj(x, nw, mod5, layer, w, q_norm, k_norm):
    tm = INPROJ_TM
    tiles_per_batch = SEQ // tm
    heads = lambda n: pl.BlockSpec((n, tm, ATT_HEAD_DIM), lambda i: (0, i, 0))
    head_of = jnp.arange(DSA_Q, dtype=jnp.int32) // ATT_HEAD_DIM
    seg = (head_of[:, None] == jnp.arange(128, dtype=jnp.int32)[None, :]).astype(BF16)
    q_norm = jnp.tile(q_norm, (1, ATT_HEADS))
    k_norm = jnp.tile(k_norm, (1, ATT_KV_HEADS))
    return pl.pallas_call(
        _dsa_inproj_kernel,
        grid=(TOKENS // tm,),
        in_specs=[
            pl.BlockSpec((tm, D_MODEL), lambda i: (i, 0)),
            pl.BlockSpec((1, D_MODEL), lambda i: (0, 0)),
            _mod_spec(layer, 1, tiles_per_batch),
            _mod_spec(layer, 0, tiles_per_batch),
            pl.BlockSpec((D_MODEL, DSA_PROJ_PAD), lambda i: (0, 0)),
            pl.BlockSpec((1, DSA_Q), lambda i: (0, 0)),
            pl.BlockSpec((1, DSA_KV), lambda i: (0, 0)),
            pl.BlockSpec((DSA_Q, 128), lambda i: (0, 0)),
            pl.BlockSpec((128, DSA_Q), lambda i: (0, 0)),
        ],
        out_specs=[heads(ATT_HEADS), heads(ATT_KV_HEADS),
                   pl.BlockSpec((ATT_KV_HEADS, tm, V_EXT), lambda i: (0, i, 0)), heads(IDX_HEADS),
                   pl.BlockSpec((tm, IDX_HEAD_DIM), lambda i: (i, 0)),
                   pl.BlockSpec((tm, IDX_HEADS), lambda i: (i, 0))],
        out_shape=[jax.ShapeDtypeStruct((ATT_HEADS, TOKENS, ATT_HEAD_DIM), BF16),
                   jax.ShapeDtypeStruct((ATT_KV_HEADS, TOKENS, ATT_HEAD_DIM), BF16),
                   jax.ShapeDtypeStruct((ATT_KV_HEADS, TOKENS, V_EXT), BF16),
                   jax.ShapeDtypeStruct((IDX_HEADS, TOKENS, IDX_HEAD_DIM), BF16),
                   jax.ShapeDtypeStruct((TOKENS, IDX_HEAD_DIM), BF16),
                   jax.ShapeDtypeStruct((TOKENS, IDX_HEADS), F32)],
        compiler_params=pltpu.CompilerParams(
            dimension_semantics=("arbitrary",), vmem_limit_bytes=VMEM_LIMIT),
        name="dsa_inproj",
    )(x, nw, mod5, mod5, w, q_norm, k_norm, seg, seg.T)


def _dsa_kernel(q_ref, k_ref, v_ref, qi_ref, ki_ref, wi_ref, xres_ref, wout_ref, gate_ref, o_ref,
                score_ref, ocat_ref, *, n_keys, first_block):
    hd = ATT_HEAD_DIM
    q_pos = (first_block + pl.program_id(1)) * Q_BLOCK + lax.broadcasted_iota(jnp.int32, (Q_BLOCK, 1), 0)

    wi = wi_ref[...]
    qi = qi_ref[...].reshape(IDX_HEADS * Q_BLOCK, IDX_HEAD_DIM)
    for kt in range(n_keys // DSA_KEY_TILE):
        ks = slice(kt * DSA_KEY_TILE, (kt + 1) * DSA_KEY_TILE)
        raw = _dot_nt(qi, ki_ref[ks, :])
        acc = jnp.zeros((Q_BLOCK, DSA_KEY_TILE), F32)
        for n in range(IDX_HEADS):
            acc = acc + wi[:, n:n + 1] * jnp.maximum(raw[n * Q_BLOCK:(n + 1) * Q_BLOCK, :], 0.0)
        key_pos = kt * DSA_KEY_TILE + lax.broadcasted_iota(jnp.int32, (Q_BLOCK, DSA_KEY_TILE), 1)
        score_ref[:, ks] = jnp.where(key_pos <= q_pos, acc, -jnp.inf)

    if n_keys > TOPK:
        _select_topk(score_ref, q_pos, n_keys)
        bias = score_ref[...][None, :, :]
    else:
        bias = jnp.where(score_ref[...] == -jnp.inf, -jnp.inf, 0.0)[None, :, :]

    for n in range(ATT_KV_HEADS):
        q4 = q_ref[n * ATT_Q_PER_KV:(n + 1) * ATT_Q_PER_KV].reshape(ATT_Q_PER_KV * Q_BLOCK, hd)
        s = _dot_nt(q4, k_ref[n]).reshape(ATT_Q_PER_KV, Q_BLOCK, n_keys) + bias
        p = jnp.exp2(s - jnp.max(s, axis=-1, keepdims=True))
        o = _dot(p.reshape(ATT_Q_PER_KV * Q_BLOCK, n_keys).astype(BF16), v_ref[n])
        o = o[:, 0:hd] * (1.0 / o[:, hd:hd + 1])
        for g in range(ATT_Q_PER_KV):
            col = (n * ATT_Q_PER_KV + g) * hd
            ocat_ref[:, col:col + hd] = o[g * Q_BLOCK:(g + 1) * Q_BLOCK, :]
    out = _dot(ocat_ref[...].astype(BF16), wout_ref[...])
    o_ref[...] = xres_ref[...] + gate_ref[...] * out


def _dsa_mixer(q, k, v, qi, ki, wi, x, w_out, mod5, layer):
    nb = SEQ // Q_BLOCK
    k4 = k.reshape(ATT_KV_HEADS, BATCH, SEQ, ATT_HEAD_DIM)
    v4 = v.reshape(ATT_KV_HEADS, BATCH, SEQ, V_EXT)
    ki3 = ki.reshape(BATCH, SEQ, IDX_HEAD_DIM)
    for cls in range(DSA_CLASSES):
        n_keys = (cls + 1) * (SEQ // DSA_CLASSES)
        first_block = cls * DSA_BLOCKS_PER_CLASS
        row = lambda b, i, fb=first_block: b * nb + fb + i
        heads = lambda n: pl.BlockSpec((n, Q_BLOCK, ATT_HEAD_DIM), lambda b, i: (0, row(b, i), 0))
        keys = lambda width: pl.BlockSpec((ATT_KV_HEADS, None, n_keys, width), lambda b, i: (0, b, 0, 0))
        x = pl.pallas_call(
            functools.partial(_dsa_kernel, n_keys=n_keys, first_block=first_block),
            grid=(BATCH, DSA_BLOCKS_PER_CLASS),
            in_specs=[
                heads(ATT_HEADS), keys(ATT_HEAD_DIM), keys(V_EXT), heads(IDX_HEADS),
                pl.BlockSpec((None, n_keys, IDX_HEAD_DIM), lambda b, i: (b, 0, 0)),
                pl.BlockSpec((Q_BLOCK, IDX_HEADS), lambda b, i: (row(b, i), 0)),
                pl.BlockSpec((Q_BLOCK, D_MODEL), lambda b, i: (row(b, i), 0)),
                pl.BlockSpec((DSA_Q, D_MODEL), lambda b, i: (0, 0)),
                pl.BlockSpec((None, None, None, 1, D_MODEL), lambda b, i: (layer, b, 2, 0, 0)),
            ],
            out_specs=pl.BlockSpec((Q_BLOCK, D_MODEL), lambda b, i: (row(b, i), 0)),
            out_shape=jax.ShapeDtypeStruct((TOKENS, D_MODEL), F32),
            scratch_shapes=[
                pltpu.VMEM((Q_BLOCK, n_keys), F32),
                pltpu.VMEM((Q_BLOCK, D_MODEL), F32),
            ],
            input_output_aliases={6: 0},
            compiler_params=pltpu.CompilerParams(
                dimension_semantics=("arbitrary", "arbitrary"), vmem_limit_bytes=VMEM_LIMIT),
            name=f"dsa_mixer_c{cls}",
        )(q, k4, v4, qi, ki3, wi, x, w_out, mod5)
    return x


MOE_PAIRS = MOE_EPG * (MOE_EPG - 1) // 2
MOE_CLASSES = MOE_GROUPS * MOE_PAIRS
PAIR_LO = (0, 0, 0, 1, 1, 2)
PAIR_HI = (1, 2, 3, 2, 3, 3)
ROUTE_TM = 1024
META_W = 128
META_CLASS, META_RANK, META_WLO, META_WHI = 0, 1, 2, 3
TILE_ROWS, TILE_LANES = 8, 128
H_WORDS = D_MODEL // 2
H_SUBLANES = H_WORDS // TILE_LANES
SORT_BLOCK = 256
N_SORT_BLOCKS = TOKENS // SORT_BLOCK
MAX_ITEMS = N_SORT_BLOCKS + MOE_CLASSES
PERMUTE_TM = 512
COMBINE_TM = 512


def _route_kernel(x_ref, nw_ref, scale_ref, shift_ref, wrt_ref, brt_ref, tri_ref, pay_ref, cls_ref, rank_ref,
                  cnt_ref, carry_ref):
    tm = ROUTE_TM

    @pl.when(pl.program_id(0) == 0)
    def _():
        carry_ref[...] = jnp.zeros_like(carry_ref)

    h = _norm_mod(x_ref[...], nw_ref[...], scale_ref[...], shift_ref[...])
    logits = _dot_nt(wrt_ref[...], h.astype(BF16)) + brt_ref[...]
    sub = lax.broadcasted_iota(jnp.int32, logits.shape, 0)
    neg = -jnp.inf
    big = jnp.int32(ROUTE_PAD)
    is_group = (sub >= MOE_EXPERTS) & (sub < MOE_EXPERTS + MOE_GROUPS)
    gl = jnp.where(is_group, logits, neg)
    g_max = jnp.max(gl, axis=0, keepdims=True)
    g_idx = jnp.min(jnp.where(gl == g_max, sub - MOE_EXPERTS, big), axis=0, keepdims=True)
    g_val = 1.0 / jnp.sum(jnp.exp(gl - g_max), axis=0, keepdims=True)
    in_group = (sub < MOE_EXPERTS) & ((sub // MOE_EPG) == g_idx)
    el = jnp.where(in_group, logits, neg)
    m1 = jnp.max(el, axis=0, keepdims=True)
    i1 = jnp.min(jnp.where(el == m1, sub, big), axis=0, keepdims=True)
    el2 = jnp.where(sub == i1, neg, el)
    m2 = jnp.max(el2, axis=0, keepdims=True)
    i2 = jnp.min(jnp.where(el2 == m2, sub, big), axis=0, keepdims=True)
    r = jnp.exp(m2 - m1)
    w_top1 = g_val / (1.0 + r)
    w_top2 = g_val * r / (1.0 + r)

    lo = jnp.minimum(i1, i2) - g_idx * MOE_EPG
    hi = jnp.maximum(i1, i2) - g_idx * MOE_EPG
    pair = (lo * (2 * MOE_EPG - 1 - lo)) // 2 + (hi - lo - 1)
    cls = g_idx * MOE_PAIRS + pair
    w_lo = jnp.where(i1 < i2, w_top1, w_top2)
    w_hi = jnp.where(i1 < i2, w_top2, w_top1)

    onehot = sub == cls
    before = _dot(onehot.astype(BF16), tri_ref[...]) + carry_ref[...]
    rank = jnp.sum(jnp.where(onehot, before, 0.0), axis=0, keepdims=True)
    carry_ref[...] += jnp.sum(jnp.where(onehot, 1.0, 0.0), axis=1, keepdims=True)
    cnt_ref[...] = carry_ref[...]
    cls_ref[...] = cls
    rank_ref[...] = rank.astype(jnp.int32)

    words = pltpu.pack_elementwise([h[:, 0:H_WORDS], h[:, H_WORDS:D_MODEL]], packed_dtype=BF16)
    for s in range(H_SUBLANES):
        pay_ref[pl.ds(s, tm, stride=TILE_ROWS), :] = words[:, s * TILE_LANES:(s + 1) * TILE_LANES]
    sub8 = lax.broadcasted_iota(jnp.int32, (TILE_ROWS, tm), 0)
    rec = jnp.where(sub8 == META_CLASS, cls.astype(F32),
                    jnp.where(sub8 == META_RANK, rank,
                              jnp.where(sub8 == META_WLO, w_lo, jnp.where(sub8 == META_WHI, w_hi, 0.0))))
    rec = jnp.concatenate([rec, jnp.zeros((META_W - TILE_ROWS, tm), F32)], axis=0)
    for b in range(tm // META_W):
        meta = rec[:, b * META_W:(b + 1) * META_W].T
        pay_ref[pl.ds(b * META_W * TILE_ROWS + H_SUBLANES, META_W, stride=TILE_ROWS), :] = (
            lax.bitcast_convert_type(meta, jnp.int32))
    for s in range(H_SUBLANES + 1, TILE_ROWS):
        pay_ref[pl.ds(s, tm, stride=TILE_ROWS), :] = jnp.zeros((tm, TILE_LANES), jnp.int32)


def _moe_route(x, nw, mod5, layer, w_route, b_route):
    tm = ROUTE_TM
    tiles_per_batch = SEQ // tm
    tri = (jnp.arange(tm, dtype=jnp.int32)[:, None] < jnp.arange(tm, dtype=jnp.int32)[None, :]).astype(BF16)
    return pl.pallas_call(
        _route_kernel,
        grid=(TOKENS // tm,),
        in_specs=[
            pl.BlockSpec((tm, D_MODEL), lambda i: (i, 0)),
            pl.BlockSpec((1, D_MODEL), lambda i: (0, 0)),
            _mod_spec(layer, 4, tiles_per_batch),
            _mod_spec(layer, 3, tiles_per_batch),
            pl.BlockSpec((ROUTE_PAD, D_MODEL), lambda i: (0, 0)),
            pl.BlockSpec((ROUTE_PAD, 1), lambda i: (0, 0)),
            pl.BlockSpec((tm, tm), lambda i: (0, 0)),
        ],
        out_specs=[pl.BlockSpec((tm * TILE_ROWS, TILE_LANES), lambda i: (i, 0)),
                   pl.BlockSpec((1, tm), lambda i: (0, i)),
                   pl.BlockSpec((1, tm), lambda i: (0, i)),
                   pl.BlockSpec((ROUTE_PAD, 1), lambda i: (0, 0))],
        out_shape=[jax.ShapeDtypeStruct((TOKENS * TILE_ROWS, TILE_LANES), jnp.int32),
                   jax.ShapeDtypeStruct((1, TOKENS), jnp.int32),
                   jax.ShapeDtypeStruct((1, TOKENS), jnp.int32),
                   jax.ShapeDtypeStruct((ROUTE_PAD, 1), F32)],
        scratch_shapes=[pltpu.VMEM((ROUTE_PAD, 1), F32)],
        compiler_params=pltpu.CompilerParams(
            dimension_semantics=("arbitrary",), vmem_limit_bytes=VMEM_LIMIT),
        name="moe_route",
    )(x, nw, mod5, mod5, w_route.T, b_route.reshape(ROUTE_PAD, 1), tri)


def _permute_kernel(pos_ref, src_ref, dst_hbm, stage_ref, sem):
    i = pl.program_id(0)
    slot = i % 2
    rows = PERMUTE_TM * TILE_ROWS

    def slot_wait(s):
        whole = stage_ref.at[pl.ds(s * rows, rows), :]
        pltpu.make_async_copy(whole, whole, sem.at[s]).wait()

    @pl.when(i >= 2)
    def _():
        slot_wait(slot)

    base = pl.multiple_of(slot * rows, rows)
    stage_ref[pl.ds(base, rows), :] = src_ref[...]

    def issue(r, _):
        pltpu.make_async_copy(stage_ref.at[pl.ds(base + r * TILE_ROWS, TILE_ROWS), :],
                              dst_hbm.at[pos_ref[i * PERMUTE_TM + r]], sem.at[slot]).start()
        return 0

    lax.fori_loop(0, PERMUTE_TM, issue, 0, unroll=16)

    @pl.when(i == pl.num_programs(0) - 1)
    def _():
        slot_wait(1 - slot)
        slot_wait(slot)


def _moe_permute(pos, payload):
    rows = PERMUTE_TM * TILE_ROWS
    return pl.pallas_call(
        _permute_kernel,
        grid_spec=pltpu.PrefetchScalarGridSpec(
            num_scalar_prefetch=1, grid=(TOKENS // PERMUTE_TM,),
            in_specs=[pl.BlockSpec((rows, TILE_LANES), lambda i, pos: (i, 0))],
            out_specs=pl.BlockSpec(memory_space=pl.ANY),
            scratch_shapes=[pltpu.VMEM((2 * rows, TILE_LANES), jnp.int32), pltpu.SemaphoreType.DMA((2,))]),
        out_shape=jax.ShapeDtypeStruct((TOKENS, TILE_ROWS, TILE_LANES), jnp.int32),
        compiler_params=pltpu.CompilerParams(
            dimension_semantics=("arbitrary",), vmem_limit_bytes=VMEM_LIMIT),
        name="moe_permute",
    )(pos, payload)


def _experts_kernel(blk_ref, cls_ref, elo_ref, ehi_ref, first_ref, last_ref, valid_ref,
                    pay_ref, w1lo_ref, w3lo_ref, w2lo_ref, w1hi_ref, w3hi_ref, w2hi_ref, o_ref, acc_ref):
    k = pl.program_id(0)
    rows = SORT_BLOCK

    @pl.when(valid_ref[k] == 1)
    def _():
        def sublane(s):
            return pay_ref[pl.ds(s, rows, stride=TILE_ROWS), :]

        halves = [[pltpu.unpack_elementwise(sublane(s), index=i, packed_dtype=BF16, unpacked_dtype=F32)
                   for s in range(H_SUBLANES)] for i in range(2)]
        hb = jnp.concatenate(halves[0] + halves[1], axis=1).astype(BF16)
        meta = lax.bitcast_convert_type(sublane(H_SUBLANES), F32)
        mine = meta[:, META_CLASS:META_CLASS + 1] == cls_ref[k].astype(F32)
        w_lo = jnp.where(mine, meta[:, META_WLO:META_WLO + 1], 0.0)
        w_hi = jnp.where(mine, meta[:, META_WHI:META_WHI + 1], 0.0)
        bf = lambda w_ref: w_ref[...].astype(BF16)
        hid_lo = _silu(_dot(hb, bf(w1lo_ref))) * _dot(hb, bf(w3lo_ref)) * w_lo
        hid_hi = _silu(_dot(hb, bf(w1hi_ref))) * _dot(hb, bf(w3hi_ref)) * w_hi
        y = _dot(hid_lo.astype(BF16), bf(w2lo_ref)) + _dot(hid_hi.astype(BF16), bf(w2hi_ref))

        @pl.when(first_ref[k] == 1)
        def _():
            acc_ref[...] = y

        @pl.when(first_ref[k] == 0)
        def _():
            acc_ref[...] += y

        @pl.when(last_ref[k] == 1)
        def _():
            for s in range(TILE_ROWS):
                o_ref[pl.ds(s, rows, stride=TILE_ROWS), :] = acc_ref[:, s * TILE_LANES:(s + 1) * TILE_LANES]


def _moe_experts(items, payload_sorted, layer, w1, w3, w2):
    blk, cls, elo, ehi, first, last, valid = items
    tiles = pl.BlockSpec((SORT_BLOCK * TILE_ROWS, TILE_LANES), lambda k, blk, *_: (blk[k], 0))
    w_in = lambda which: pl.BlockSpec(
        (None, None, D_MODEL, MOE_HIDDEN), lambda k, blk, cls, elo, ehi, *_: (layer, (elo, ehi)[which][k], 0, 0))
    w_out = lambda which: pl.BlockSpec(
        (None, None, MOE_HIDDEN, D_MODEL), lambda k, blk, cls, elo, ehi, *_: (layer, (elo, ehi)[which][k], 0, 0))
    return pl.pallas_call(
        _experts_kernel,
        grid_spec=pltpu.PrefetchScalarGridSpec(
            num_scalar_prefetch=7, grid=(MAX_ITEMS,),
            in_specs=[tiles, w_in(0), w_in(0), w_out(0), w_in(1), w_in(1), w_out(1)],
            out_specs=tiles,
            scratch_shapes=[pltpu.VMEM((SORT_BLOCK, D_MODEL), F32)]),
        out_shape=jax.ShapeDtypeStruct((TOKENS * TILE_ROWS, TILE_LANES), F32),
        compiler_params=pltpu.CompilerParams(
            dimension_semantics=("arbitrary",), vmem_limit_bytes=VMEM_LIMIT),
        name="moe_experts",
    )(blk, cls, elo, ehi, first, last, valid, payload_sorted, w1, w3, w2, w1, w3, w2)


def _combine_kernel(pos_ref, x_ref, gate_ref, y_hbm, o_ref, buf_ref, sem):
    tm = COMBINE_TM
    i = pl.program_id(0)
    n = pl.num_programs(0)

    def gather(tile, slot):
        def issue(r, _):
            pltpu.make_async_copy(y_hbm.at[pos_ref[tile * tm + r]],
                                  buf_ref.at[pl.ds((slot * tm + r) * TILE_ROWS, TILE_ROWS), :], sem.at[slot]).start()
            return 0
        lax.fori_loop(0, tm, issue, 0, unroll=16)

    @pl.when(i == 0)
    def _():
        gather(0, 0)

    @pl.when(i + 1 < n)
    def _():
        gather(i + 1, (i + 1) % 2)

    slot = i % 2
    base = slot * tm * TILE_ROWS
    whole_slot = buf_ref.at[pl.ds(base, tm * TILE_ROWS), :]
    pltpu.make_async_copy(whole_slot, whole_slot, sem.at[slot]).wait()
    y = jnp.concatenate([buf_ref[pl.ds(base + s, tm, stride=TILE_ROWS), :] for s in range(TILE_ROWS)], axis=1)
    o_ref[...] = x_ref[...] + gate_ref[...] * y


def _moe_combine(pos, x, mod5, layer, y_sorted):
    tm = COMBINE_TM
    tiles_per_batch = SEQ // tm
    return pl.pallas_call(
        _combine_kernel,
        grid_spec=pltpu.PrefetchScalarGridSpec(
            num_scalar_prefetch=1, grid=(TOKENS // tm,),
            in_specs=[
                pl.BlockSpec((tm, D_MODEL), lambda i, pos: (i, 0)),
                pl.BlockSpec((None, None, None, 1, D_MODEL),
                             lambda i, pos: (layer, i // tiles_per_batch, 5, 0, 0)),
                pl.BlockSpec(memory_space=pl.ANY),
            ],
            out_specs=pl.BlockSpec((tm, D_MODEL), lambda i, pos: (i, 0)),
            scratch_shapes=[pltpu.VMEM((2 * tm * TILE_ROWS, TILE_LANES), F32), pltpu.SemaphoreType.DMA((2,))]),
        out_shape=jax.ShapeDtypeStruct((TOKENS, D_MODEL), F32),
        compiler_params=pltpu.CompilerParams(
            dimension_semantics=("arbitrary",), vmem_limit_bytes=VMEM_LIMIT),
        name="moe_combine",
    )(pos, x, mod5, y_sorted)


def _moe_plan(cls, rank, counts):
    count = counts[:MOE_CLASSES, 0].astype(jnp.int32)
    ends = jnp.cumsum(count)
    starts = ends - count
    class_ids = jnp.arange(MOE_CLASSES, dtype=jnp.int32)

    def lookup(table, idx):
        return jnp.sum(jnp.where(idx[..., None] == class_ids, table, 0), axis=-1)

    pos = (lookup(starts, cls) + rank).reshape(TOKENS)

    first_blk = starts // SORT_BLOCK
    n_items = jnp.where(count > 0, (ends - 1) // SORT_BLOCK - first_blk + 1, 0)
    item_end = jnp.cumsum(n_items)
    item_start = item_end - n_items
    k = jnp.arange(MAX_ITEMS, dtype=jnp.int32)
    valid = k < item_end[-1]
    kc = jnp.minimum(k, item_end[-1] - 1)
    icls = jnp.sum((item_end[None, :] <= kc[:, None]).astype(jnp.int32), axis=1)
    blk = lookup(first_blk, icls) + (kc - lookup(item_start, icls))
    first = jnp.concatenate([jnp.ones((1,), jnp.int32), (blk[1:] != blk[:-1]).astype(jnp.int32)])
    last = jnp.concatenate([(blk[1:] != blk[:-1]) | ~valid[1:], jnp.ones((1,), bool)]).astype(jnp.int32)
    group = icls // MOE_PAIRS
    elo = group * MOE_EPG + lookup(jnp.asarray(PAIR_LO * MOE_GROUPS, jnp.int32), icls)
    ehi = group * MOE_EPG + lookup(jnp.asarray(PAIR_HI * MOE_GROUPS, jnp.int32), icls)
    return pos, (blk, icls, elo, ehi, first * valid, last * valid, valid.astype(jnp.int32))


def _moe(x, nw, mod5, layer, w_route, b_route, w1, w3, w2):
    payload, cls, rank, counts = _moe_route(x, nw, mod5, layer, w_route, b_route)
    pos, items = _moe_plan(cls, rank, counts)
    tiles = (TOKENS, TILE_ROWS, TILE_LANES)
    sorted_payload = _moe_permute(pos, payload).reshape(TOKENS * TILE_ROWS, TILE_LANES)
    y_sorted = _moe_experts(items, sorted_payload, layer, w1, w3, w2)
    return _moe_combine(pos, x, mod5, layer, y_sorted.reshape(tiles))


def _pad_cols(w, width):
    return jnp.pad(w, ((0, 0), (0, width - w.shape[1])))


def kernel(x, c, ada_w, ada_b, norm_mix, norm_ffn, ssd_w_in, ssd_conv_w, ssd_conv_b, ssd_dt_bias,
           ssd_a_log, ssd_d, ssd_norm, ssd_w_out, dsa_w_in, dsa_q_norm, dsa_k_norm, dsa_w_out,
           moe_w_group, moe_b_group, moe_w_expert, moe_b_expert, moe_w1, moe_w3, moe_w2):
    depth = ada_w.shape[0]
    xt = x.reshape(TOKENS, D_MODEL)
    mod = _modulation(c, ada_w, ada_b)
    mod5 = mod.reshape(depth, BATCH, 6, 1, D_MODEL)

    head_of_col = jnp.arange(SSD_D_INNER, dtype=jnp.int32) // SSD_HEAD_DIM
    expand = (jnp.arange(SSD_DT_PAD, dtype=jnp.int32)[:, None] == head_of_col[None, :]).astype(BF16)

    for i in range(depth):
        j = i // 2
        nw_mix = norm_mix[i].reshape(1, D_MODEL)
        if i % 2 == 0:
            w_in = _pad_cols(ssd_w_in[j], SSD_PROJ_PAD).astype(BF16)
            z, xbc, dt = _inproj(
                xt, nw_mix, mod5, i, 1, 0, w_in,
                ((0, SSD_D_INNER), (SSD_D_INNER, SSD_D_INNER + SSD_CONV_DIM),
                 (SSD_D_INNER + SSD_CONV_DIM, SSD_PROJ_PAD)))
            xt = _ssd_mixer(
                z, xbc, dt, xt, ssd_conv_w[j], ssd_conv_b[j].reshape(1, SSD_CONV_DIM),
                _pad_cols(ssd_dt_bias[j].reshape(1, SSD_HEADS), SSD_DT_PAD),
                _pad_cols(ssd_a_log[j].reshape(1, SSD_HEADS), SSD_DT_PAD),
                jnp.repeat(ssd_d[j], SSD_HEAD_DIM).reshape(1, SSD_D_INNER),
                ssd_norm[j].reshape(1, SSD_D_INNER), expand, ssd_w_out[j].astype(BF16), mod5, i)
        else:
            w = dsa_w_in[j]
            w_in = jnp.concatenate(
                [_pad_cols(w[:, :DSA_KI_START + IDX_HEAD_DIM], DSA_WI_START),
                 _pad_cols(w[:, DSA_KI_START + IDX_HEAD_DIM:], 128)], axis=1).astype(BF16)
            q, k, v, qi, ki, wi = _dsa_inproj(
                xt, nw_mix, mod5, i, w_in, dsa_q_norm[j].reshape(1, ATT_HEAD_DIM),
                dsa_k_norm[j].reshape(1, ATT_HEAD_DIM))
            xt = _dsa_mixer(q, k, v, qi, ki, wi, xt, dsa_w_out[j].astype(BF16), mod5, i)

        w_route = _pad_cols(jnp.concatenate([moe_w_expert[i], moe_w_group[i]], axis=1), ROUTE_PAD).astype(BF16)
        b_route = _pad_cols(jnp.concatenate([moe_b_expert[i], moe_b_group[i]]).reshape(1, -1), ROUTE_PAD)
        xt = _moe(xt, norm_ffn[i].reshape(1, D_MODEL), mod5, i, w_route, b_route,
                  moe_w1, moe_w3, moe_w2)
    return xt.reshape(BATCH, SEQ, D_MODEL)
```

```python
import functools

import jax
import jax.numpy as jnp
from jax import lax
from jax.experimental import pallas as pl
from jax.experimental.pallas import tpu as pltpu

F32 = jnp.float32
BF16 = jnp.bfloat16

D_MODEL = 1024
BATCH = 8
SEQ = 2048
TOKENS = BATCH * SEQ
EPS = 1e-6

SSD_D_INNER = 2048
SSD_HEAD_DIM = 64
SSD_HEADS = 32
SSD_GROUPS = 8
SSD_HEADS_PER_GROUP = 4
SSD_STATE = 128
SSD_CONV = 4
SSD_CHUNK = 128
SSD_GN = SSD_GROUPS * SSD_STATE
SSD_CONV_DIM = SSD_D_INNER + 2 * SSD_GN
SSD_GROUP_W = SSD_HEADS_PER_GROUP * SSD_HEAD_DIM
SSD_DT_PAD = 128
SSD_PROJ_PAD = SSD_D_INNER + SSD_CONV_DIM + SSD_DT_PAD
CONV_HALO = 8

ATT_HEADS = 16
ATT_KV_HEADS = 4
ATT_Q_PER_KV = 4
ATT_HEAD_DIM = 64
IDX_HEADS = 8
IDX_HEAD_DIM = 64
TOPK = 256
Q_BLOCK = 128
DSA_Q = ATT_HEADS * ATT_HEAD_DIM
DSA_KV = ATT_KV_HEADS * ATT_HEAD_DIM
DSA_QI = IDX_HEADS * IDX_HEAD_DIM
DSA_KI_START = DSA_Q + 2 * DSA_KV + DSA_QI
DSA_WI_START = DSA_KI_START + 128
DSA_PROJ_PAD = DSA_WI_START + 128

MOE_GROUPS = 4
MOE_EPG = 4
MOE_EXPERTS = 16
MOE_HIDDEN = 256
ROUTE_PAD = 128

VMEM_LIMIT = 56 * 1024 * 1024


def _sigmoid(v):
    return 1.0 / (1.0 + jnp.exp(-v))


def _silu(v):
    return v * _sigmoid(v)


def _split3(a):
    hi = a.astype(BF16)
    r = a - hi.astype(F32)
    mid = r.astype(BF16)
    lo = (r - mid.astype(F32)).astype(BF16)
    return hi, mid, lo


def _dot(a, b):
    return jnp.dot(a, b, preferred_element_type=F32)


def _dot_nt(a, b):
    return lax.dot_general(a, b, (((1,), (1,)), ((), ())), preferred_element_type=F32)


def _dot3_exact_rhs(a, m):
    hi, mid, lo = _split3(a)
    return _dot(hi, m) + _dot(mid, m) + _dot(lo, m)


def _dot3_exact_lhs(m, a):
    hi, mid, lo = _split3(a)
    return _dot(m, hi) + _dot(m, mid) + _dot(m, lo)


def _norm_mod(x, nw, scale, shift):
    ms = jnp.mean(x * x, axis=-1, keepdims=True)
    return x * lax.rsqrt(ms + EPS) * nw * (1.0 + scale) + shift


MOD_TN = 1536


def _mod_kernel(c_ref, w_ref, b_ref, o_ref):
    cond = _silu(c_ref[...]).astype(BF16)
    o_ref[...] = _dot(cond, w_ref[...].astype(BF16)) + b_ref[...]


def _modulation(c, ada_w, ada_b):
    depth = ada_w.shape[0]
    n = ada_w.shape[2]
    return pl.pallas_call(
        _mod_kernel,
        grid=(depth, n // MOD_TN),
        in_specs=[
            pl.BlockSpec((BATCH, D_MODEL), lambda i, j: (0, 0)),
            pl.BlockSpec((None, D_MODEL, MOD_TN), lambda i, j: (i, 0, j)),
            pl.BlockSpec((None, 1, MOD_TN), lambda i, j: (i, 0, j)),
        ],
        out_specs=pl.BlockSpec((None, BATCH, MOD_TN), lambda i, j: (i, 0, j)),
        out_shape=jax.ShapeDtypeStruct((depth, BATCH, n), F32),
        compiler_params=pltpu.CompilerParams(
            dimension_semantics=("arbitrary", "arbitrary"), vmem_limit_bytes=VMEM_LIMIT),
        name="adaln_mod",
    )(c, ada_w, ada_b.reshape(depth, 1, n))


def _mod_spec(layer, chunk, rows_per_batch_tile):
    return pl.BlockSpec((None, None, None, 1, D_MODEL),
                        lambda i, *_: (layer, i // rows_per_batch_tile, chunk, 0, 0))


INPROJ_TM = 256


def _inproj_kernel(x_ref, nw_ref, scale_ref, shift_ref, w_ref, *o_refs, col_slices):
    h = _norm_mod(x_ref[...], nw_ref[...], scale_ref[...], shift_ref[...]).astype(BF16)
    for n, (o_ref, (lo, hi)) in enumerate(zip(o_refs, col_slices)):
        r = _dot(h, w_ref[:, lo:hi])
        o_ref[...] = _silu(r) if n == 0 else r


def _inproj(x, nw, mod5, layer, scale_chunk, shift_chunk, w, col_slices):
    tm = INPROJ_TM
    tiles_per_batch = SEQ // tm
    n_pad = w.shape[1]
    return pl.pallas_call(
        functools.partial(_inproj_kernel, col_slices=col_slices),
        grid=(TOKENS // tm,),
        in_specs=[
            pl.BlockSpec((tm, D_MODEL), lambda i: (i, 0)),
            pl.BlockSpec((1, D_MODEL), lambda i: (0, 0)),
            _mod_spec(layer, scale_chunk, tiles_per_batch),
            _mod_spec(layer, shift_chunk, tiles_per_batch),
            pl.BlockSpec((D_MODEL, n_pad), lambda i: (0, 0)),
        ],
        out_specs=[pl.BlockSpec((tm, hi - lo), lambda i: (i, 0)) for lo, hi in col_slices],
        out_shape=[jax.ShapeDtypeStruct((TOKENS, hi - lo), F32) for lo, hi in col_slices],
        compiler_params=pltpu.CompilerParams(
            dimension_semantics=("arbitrary",), vmem_limit_bytes=VMEM_LIMIT),
        name="norm_inproj",
    )(x, nw, mod5, mod5, w)


CONV_COLS = 512


def _ssd_kernel(z_ref, xbc_ref, dt_ref, xres_ref, cw_ref, cb_ref, dtb_ref, alog_ref, de_ref, nw_ref,
                e_ref, wout_ref, gate_ref, o_ref, state_ref, ext_ref, act_ref, yn_ref):
    q = SSD_CHUNK
    c = pl.program_id(1)

    @pl.when(c == 0)
    def _():
        state_ref[...] = jnp.zeros_like(state_ref)
        ext_ref[0:CONV_HALO, :] = jnp.zeros((CONV_HALO, SSD_CONV_DIM), F32)

    @pl.when(c > 0)
    def _():
        ext_ref[0:CONV_HALO, :] = ext_ref[q:q + CONV_HALO, :]

    ext_ref[CONV_HALO:CONV_HALO + q, :] = xbc_ref[...]

    for s in range(SSD_CONV_DIM // CONV_COLS):
        cs = slice(s * CONV_COLS, (s + 1) * CONV_COLS)
        u = ext_ref[:, cs]
        acc = cw_ref[0:1, cs] * u
        for k in range(1, SSD_CONV):
            acc = pltpu.roll(acc, 1, axis=0) + cw_ref[k:k + 1, cs] * u
        act_ref[:, cs] = _silu(acc[CONV_HALO:CONV_HALO + q, :] + cb_ref[:, cs])

    dt_raw = dt_ref[...] + dtb_ref[...]
    dt = jnp.maximum(dt_raw, 0.0) + jnp.log1p(jnp.exp(-jnp.abs(dt_raw)))
    a = dt * (-jnp.exp(alog_ref[...]) * LOG2E)
    row = lax.broadcasted_iota(jnp.int32, (q, q), 0)
    col = lax.broadcasted_iota(jnp.int32, (q, q), 1)
    tril = row >= col
    acs = _dot3_exact_lhs(tril.astype(BF16), a)
    acs_t = acs.T
    expand = e_ref[...]
    acs_e = _dot3_exact_rhs(acs, expand)
    dt_e = _dot3_exact_rhs(dt, expand)
    tot_e = acs_e[q - 1:q, :]
    decay_from_start = jnp.exp2(acs_e)
    decay_to_end = jnp.exp2(tot_e - acs_e)
    chunk_decay = jnp.exp2(tot_e)

    lane_head = lax.broadcasted_iota(jnp.int32, (q, SSD_GROUP_W), 1) // SSD_HEAD_DIM
    for g in range(SSD_GROUPS):
        xs = act_ref[:, g * SSD_GROUP_W:(g + 1) * SSD_GROUP_W]
        gs = slice(g * SSD_GROUP_W, (g + 1) * SSD_GROUP_W)
        bm = act_ref[:, SSD_D_INNER + g * SSD_STATE:SSD_D_INNER + (g + 1) * SSD_STATE]
        cm = act_ref[:, SSD_D_INNER + SSD_GN + g * SSD_STATE:SSD_D_INNER + SSD_GN + (g + 1) * SSD_STATE]
        bm_t = bm.T.astype(BF16)
        cm_b = cm.astype(BF16)
        cb = _dot(cm_b, bm_t)
        xd = xs * dt_e[:, gs]
        ms = []
        xds = []
        for j in range(SSD_HEADS_PER_GROUP):
            h = g * SSD_HEADS_PER_GROUP + j
            seg = acs[:, h:h + 1] - acs_t[h:h + 1, :]
            dec = jnp.exp2(jnp.where(tril, seg, -jnp.inf))
            ms.append((cb * dec).astype(BF16))
            xds.append(jnp.where(lane_head == j, xd, 0.0).astype(BF16))
        y_diag = _dot(jnp.concatenate(ms, axis=1), jnp.concatenate(xds, axis=0))
        prev = state_ref[g]
        y_off = _dot(cm_b, prev.astype(BF16)) * decay_from_start[:, gs]
        state_ref[g] = prev * chunk_decay[:, gs] + _dot(bm_t, (xd * decay_to_end[:, gs]).astype(BF16))
        y = y_diag + y_off + xs * de_ref[:, gs]
        y = y * z_ref[:, gs]
        y = y * lax.rsqrt(jnp.mean(y * y, axis=-1, keepdims=True) + EPS) * nw_ref[:, gs]
        yn_ref[:, gs] = y.astype(BF16)

    out = _dot(yn_ref[...], wout_ref[...])
    o_ref[...] = xres_ref[...] + gate_ref[...] * out


def _ssd_mixer(z, xbc, dt, x, conv_w, conv_b, dt_bias, a_log, d_e, norm_w, expand, w_out, mod5, layer):
    q = SSD_CHUNK
    nc = SEQ // q
    tok = lambda w: pl.BlockSpec((q, w), lambda b, c: (b * nc + c, 0))
    full = lambda r, w: pl.BlockSpec((r, w), lambda b, c: (0, 0))
    return pl.pallas_call(
        _ssd_kernel,
        grid=(BATCH, nc),
        in_specs=[
            tok(SSD_D_INNER), tok(SSD_CONV_DIM), tok(SSD_DT_PAD), tok(D_MODEL),
            full(SSD_CONV, SSD_CONV_DIM), full(1, SSD_CONV_DIM), full(1, SSD_DT_PAD), full(1, SSD_DT_PAD),
            full(1, SSD_D_INNER), full(1, SSD_D_INNER), full(SSD_DT_PAD, SSD_D_INNER),
            full(SSD_D_INNER, D_MODEL),
            pl.BlockSpec((None, None, None, 1, D_MODEL), lambda b, c: (layer, b, 2, 0, 0)),
        ],
        out_specs=tok(D_MODEL),
        out_shape=jax.ShapeDtypeStruct((TOKENS, D_MODEL), F32),
        scratch_shapes=[
            pltpu.VMEM((SSD_GROUPS, SSD_STATE, SSD_GROUP_W), F32),
            pltpu.VMEM((q + CONV_HALO, SSD_CONV_DIM), F32),
            pltpu.VMEM((q, SSD_CONV_DIM), F32),
            pltpu.VMEM((q, SSD_D_INNER), BF16),
        ],
        compiler_params=pltpu.CompilerParams(
            dimension_semantics=("arbitrary", "arbitrary"), vmem_limit_bytes=VMEM_LIMIT),
        name="ssd_mixer",
    )(z, xbc, dt, x, conv_w, conv_b, dt_bias, a_log, d_e, norm_w, expand, w_out, mod5)


DSA_KEY_TILE = 256
DSA_CLASSES = 8
V_EXT = 2 * ATT_HEAD_DIM
DSA_BLOCKS_PER_CLASS = (SEQ // Q_BLOCK) // DSA_CLASSES
N_BISECT = 12
F32_MIN = float(jnp.finfo(jnp.float32).min)
LOG2E = 1.4426950408889634


def _count(mask):
    return jnp.sum(jnp.where(mask, 1.0, 0.0), axis=-1, keepdims=True)


def _select_topk(score_ref, q_pos, n_keys):
    kf = float(TOPK)
    small = (q_pos + 1) <= TOPK
    sc = score_ref[...]
    hi0 = jnp.max(sc, axis=-1, keepdims=True)
    lo0 = jnp.min(jnp.where(sc == -jnp.inf, jnp.inf, sc), axis=-1, keepdims=True)

    def bisect(_, carry):
        lo, hi = carry
        mid = lo + 0.5 * (hi - lo)
        ok = _count(score_ref[...] >= mid) >= kf
        return jnp.where(ok, mid, lo), jnp.where(ok, hi, mid)

    _, hi = lax.fori_loop(0, N_BISECT, bisect, (lo0, hi0))

    v0 = jnp.max(jnp.where(sc <= hi, sc, -jnp.inf), axis=-1, keepdims=True)
    c0 = _count(sc >= v0)
    pend0 = jnp.where((c0 >= kf) | small, 0.0, 1.0)

    def walk_cond(carry):
        return (carry[2] > 0.0) & (carry[3] < n_keys)

    def walk(carry):
        v, pend, _, it = carry
        s = score_ref[...]
        v2 = jnp.max(jnp.where(s < v, s, -jnp.inf), axis=-1, keepdims=True)
        c2 = _count(s >= v2)
        v = jnp.where(pend > 0.0, v2, v)
        pend = jnp.where(c2 >= kf, 0.0, pend)
        return v, pend, jnp.max(pend), it + 1

    v, _, _, _ = lax.while_loop(walk_cond, walk, (v0, pend0, jnp.max(pend0), jnp.int32(0)))
    thr = jnp.where(small, F32_MIN, v)

    key_pos = lax.broadcasted_iota(jnp.int32, (Q_BLOCK, n_keys), 1)
    gt = sc > thr
    eq = sc == thr
    need = kf - _count(gt)
    tie = jnp.where(_count(eq) > need, 1.0, 0.0)

    def search_cut():
        def body(_, carry):
            lo, hi = carry
            mid = (lo + hi) >> 1
            s = score_ref[...]
            ok = _count((s == thr) & (key_pos <= mid)) >= need
            return jnp.where(ok, lo, mid), jnp.where(ok, mid, hi)

        init = (jnp.full((Q_BLOCK, 1), -1, jnp.int32), jnp.full((Q_BLOCK, 1), n_keys - 1, jnp.int32))
        return lax.fori_loop(0, (n_keys - 1).bit_length() + 1, body, init)[1]

    cut = lax.cond(jnp.max(tie) > 0.0, search_cut, lambda: jnp.full((Q_BLOCK, 1), n_keys - 1, jnp.int32))
    score_ref[...] = jnp.where(gt | (eq & (key_pos <= cut)), 0.0, -jnp.inf)


def _dsa_inproj_kernel(x_ref, nw_ref, scale_ref, shift_ref, w_ref, qn_ref, kn_ref, seg_ref, segt_ref,
                       q_ref, k_ref, v_ref, qi_ref, ki_ref, wi_ref):
    hd = ATT_HEAD_DIM
    h = _norm_mod(x_ref[...], nw_ref[...], scale_ref[...], shift_ref[...]).astype(BF16)

    def head_norm(t, w):
        width = t.shape[1]
        ss = _dot((t * t).astype(BF16), seg_ref[0:width, :])
        r = lax.rsqrt(ss * (1.0 / hd) + EPS)
        r_hi = r.astype(BF16)
        r_lo = (r - r_hi.astype(F32)).astype(BF16)
        return t * (_dot(r_hi, segt_ref[:, 0:width]) + _dot(r_lo, segt_ref[:, 0:width])) * w

    q = head_norm(_dot(h, w_ref[:, 0:DSA_Q]), qn_ref[...] * (hd ** -0.5 * LOG2E))
    for n in range(ATT_HEADS):
        q_ref[n] = q[:, n * hd:(n + 1) * hd].astype(BF16)
    kv = _dot(h, w_ref[:, DSA_Q:DSA_Q + 2 * DSA_KV])
    k = head_norm(kv[:, 0:DSA_KV], kn_ref[...])
    for n in range(ATT_KV_HEADS):
        k_ref[n] = k[:, n * hd:(n + 1) * hd].astype(BF16)
        v_ref[n] = jnp.concatenate([kv[:, DSA_KV + n * hd:DSA_KV + (n + 1) * hd],
                                    jnp.ones((kv.shape[0], V_EXT - hd), F32)], axis=1).astype(BF16)
    qi = _dot(h, w_ref[:, DSA_Q + 2 * DSA_KV:DSA_KI_START])
    for n in range(IDX_HEADS):
        qi_ref[n] = qi[:, n * IDX_HEAD_DIM:(n + 1) * IDX_HEAD_DIM].astype(BF16)
    ki_ref[...] = _dot(h, w_ref[:, DSA_KI_START:DSA_KI_START + IDX_HEAD_DIM]).astype(BF16)
    wi_ref[...] = _dot(h, w_ref[:, DSA_WI_START:DSA_WI_START + IDX_HEADS]) * ((IDX_HEADS * IDX_HEAD_DIM) ** -0.5)


def _dsa_inproj(x, nw, mod5, layer, w, q_norm, k_norm):
    tm = INPROJ_TM
    tiles_per_batch = SEQ // tm
    heads = lambda n: pl.BlockSpec((n, tm, ATT_HEAD_DIM), lambda i: (0, i, 0))
    head_of = jnp.arange(DSA_Q, dtype=jnp.int32) // ATT_HEAD_DIM
    seg = (head_of[:, None] == jnp.arange(128, dtype=jnp.int32)[None, :]).astype(BF16)
    q_norm = jnp.tile(q_norm, (1, ATT_HEADS))
    k_norm = jnp.tile(k_norm, (1, ATT_KV_HEADS))
    return pl.pallas_call(
        _dsa_inproj_kernel,
        grid=(TOKENS // tm,),
        in_specs=[
            pl.BlockSpec((tm, D_MODEL), lambda i: (i, 0)),
            pl.BlockSpec((1, D_MODEL), lambda i: (0, 0)),
            _mod_spec(layer, 1, tiles_per_batch),
            _mod_spec(layer, 0, tiles_per_batch),
            pl.BlockSpec((D_MODEL, DSA_PROJ_PAD), lambda i: (0, 0)),
            pl.BlockSpec((1, DSA_Q), lambda i: (0, 0)),
            pl.BlockSpec((1, DSA_KV), lambda i: (0, 0)),
            pl.BlockSpec((DSA_Q, 128), lambda i: (0, 0)),
            pl.BlockSpec((128, DSA_Q), lambda i: (0, 0)),
        ],
        out_specs=[heads(ATT_HEADS), heads(ATT_KV_HEADS),
                   pl.BlockSpec((ATT_KV_HEADS, tm, V_EXT), lambda i: (0, i, 0)), heads(IDX_HEADS),
                   pl.BlockSpec((tm, IDX_HEAD_DIM), lambda i: (i, 0)),
                   pl.BlockSpec((tm, IDX_HEADS), lambda i: (i, 0))],
        out_shape=[jax.ShapeDtypeStruct((ATT_HEADS, TOKENS, ATT_HEAD_DIM), BF16),
                   jax.ShapeDtypeStruct((ATT_KV_HEADS, TOKENS, ATT_HEAD_DIM), BF16),
                   jax.ShapeDtypeStruct((ATT_KV_HEADS, TOKENS, V_EXT), BF16),
                   jax.ShapeDtypeStruct((IDX_HEADS, TOKENS, IDX_HEAD_DIM), BF16),
                   jax.ShapeDtypeStruct((TOKENS, IDX_HEAD_DIM), BF16),
                   jax.ShapeDtypeStruct((TOKENS, IDX_HEADS), F32)],
        compiler_params=pltpu.CompilerParams(
            dimension_semantics=("arbitrary",), vmem_limit_bytes=VMEM_LIMIT),
        name="dsa_inproj",
    )(x, nw, mod5, mod5, w, q_norm, k_norm, seg, seg.T)


def _dsa_kernel(q_ref, k_ref, v_ref, qi_ref, ki_ref, wi_ref, xres_ref, wout_ref, gate_ref, o_ref,
                score_ref, ocat_ref, *, n_keys, first_block):
    hd = ATT_HEAD_DIM
    q_pos = (first_block + pl.program_id(1)) * Q_BLOCK + lax.broadcasted_iota(jnp.int32, (Q_BLOCK, 1), 0)

    wi = wi_ref[...]
    qi = qi_ref[...].reshape(IDX_HEADS * Q_BLOCK, IDX_HEAD_DIM)
    for kt in range(n_keys // DSA_KEY_TILE):
        ks = slice(kt * DSA_KEY_TILE, (kt + 1) * DSA_KEY_TILE)
        raw = _dot_nt(qi, ki_ref[ks, :])
        acc = jnp.zeros((Q_BLOCK, DSA_KEY_TILE), F32)
        for n in range(IDX_HEADS):
            acc = acc + wi[:, n:n + 1] * jnp.maximum(raw[n * Q_BLOCK:(n + 1) * Q_BLOCK, :], 0.0)
        key_pos = kt * DSA_KEY_TILE + lax.broadcasted_iota(jnp.int32, (Q_BLOCK, DSA_KEY_TILE), 1)
        score_ref[:, ks] = jnp.where(key_pos <= q_pos, acc, -jnp.inf)

    if n_keys > TOPK:
        _select_topk(score_ref, q_pos, n_keys)
        bias = score_ref[...][None, :, :]
    else:
        bias = jnp.where(score_ref[...] == -jnp.inf, -jnp.inf, 0.0)[None, :, :]

    for n in range(ATT_KV_HEADS):
        q4 = q_ref[n * ATT_Q_PER_KV:(n + 1) * ATT_Q_PER_KV].reshape(ATT_Q_PER_KV * Q_BLOCK, hd)
        s = _dot_nt(q4, k_ref[n]).reshape(ATT_Q_PER_KV, Q_BLOCK, n_keys) + bias
        p = jnp.exp2(s - jnp.max(s, axis=-1, keepdims=True))
        o = _dot(p.reshape(ATT_Q_PER_KV * Q_BLOCK, n_keys).astype(BF16), v_ref[n])
        o = o[:, 0:hd] * (1.0 / o[:, hd:hd + 1])
        for g in range(ATT_Q_PER_KV):
            col = (n * ATT_Q_PER_KV + g) * hd
            ocat_ref[:, col:col + hd] = o[g * Q_BLOCK:(g + 1) * Q_BLOCK, :]
    out = _dot(ocat_ref[...].astype(BF16), wout_ref[...])
    o_ref[...] = xres_ref[...] + gate_ref[...] * out


def _dsa_mixer(q, k, v, qi, ki, wi, x, w_out, mod5, layer):
    nb = SEQ // Q_BLOCK
    k4 = k.reshape(ATT_KV_HEADS, BATCH, SEQ, ATT_HEAD_DIM)
    v4 = v.reshape(ATT_KV_HEADS, BATCH, SEQ, V_EXT)
    ki3 = ki.reshape(BATCH, SEQ, IDX_HEAD_DIM)
    for cls in range(DSA_CLASSES):
        n_keys = (cls + 1) * (SEQ // DSA_CLASSES)
        first_block = cls * DSA_BLOCKS_PER_CLASS
        row = lambda b, i, fb=first_block: b * nb + fb + i
        heads = lambda n: pl.BlockSpec((n, Q_BLOCK, ATT_HEAD_DIM), lambda b, i: (0, row(b, i), 0))
        keys = lambda width: pl.BlockSpec((ATT_KV_HEADS, None, n_keys, width), lambda b, i: (0, b, 0, 0))
        x = pl.pallas_call(
            functools.partial(_dsa_kernel, n_keys=n_keys, first_block=first_block),
            grid=(BATCH, DSA_BLOCKS_PER_CLASS),
            in_specs=[
                heads(ATT_HEADS), keys(ATT_HEAD_DIM), keys(V_EXT), heads(IDX_HEADS),
                pl.BlockSpec((None, n_keys, IDX_HEAD_DIM), lambda b, i: (b, 0, 0)),
                pl.BlockSpec((Q_BLOCK, IDX_HEADS), lambda b, i: (row(b, i), 0)),
                pl.BlockSpec((Q_BLOCK, D_MODEL), lambda b, i: (row(b, i), 0)),
                pl.BlockSpec((DSA_Q, D_MODEL), lambda b, i: (0, 0)),
                pl.BlockSpec((None, None, None, 1, D_MODEL), lambda b, i: (layer, b, 2, 0, 0)),
            ],
            out_specs=pl.BlockSpec((Q_BLOCK, D_MODEL), lambda b, i: (row(b, i), 0)),
            out_shape=jax.ShapeDtypeStruct((TOKENS, D_MODEL), F32),
            scratch_shapes=[
                pltpu.VMEM((Q_BLOCK, n_keys), F32),
                pltpu.VMEM((Q_BLOCK, D_MODEL), F32),
            ],
            input_output_aliases={6: 0},
            compiler_params=pltpu.CompilerParams(
                dimension_semantics=("arbitrary", "arbitrary"), vmem_limit_bytes=VMEM_LIMIT),
            name=f"dsa_mixer_c{cls}",
        )(q, k4, v4, qi, ki3, wi, x, w_out, mod5)
    return x


MOE_PAIRS = MOE_EPG * (MOE_EPG - 1) // 2
MOE_CLASSES = MOE_GROUPS * MOE_PAIRS
PAIR_LO = (0, 0, 0, 1, 1, 2)
PAIR_HI = (1, 2, 3, 2, 3, 3)
ROUTE_TM = 1024
META_W = 128
META_CLASS, META_RANK, META_WLO, META_WHI = 0, 1, 2, 3
TILE_ROWS, TILE_LANES = 8, 128
H_WORDS = D_MODEL // 2
H_SUBLANES = H_WORDS // TILE_LANES
SORT_BLOCK = 256
N_SORT_BLOCKS = TOKENS // SORT_BLOCK
MAX_ITEMS = N_SORT_BLOCKS + MOE_CLASSES
PERMUTE_TM = 512
COMBINE_TM = 512


def _route_kernel(x_ref, nw_ref, scale_ref, shift_ref, wrt_ref, brt_ref, tri_ref, pay_ref, cls_ref, rank_ref,
                  cnt_ref, carry_ref):
    tm = ROUTE_TM

    @pl.when(pl.program_id(0) == 0)
    def _():
        carry_ref[...] = jnp.zeros_like(carry_ref)

    h = _norm_mod(x_ref[...], nw_ref[...], scale_ref[...], shift_ref[...])
    logits = _dot_nt(wrt_ref[...], h.astype(BF16)) + brt_ref[...]
    sub = lax.broadcasted_iota(jnp.int32, logits.shape, 0)
    neg = -jnp.inf
    big = jnp.int32(ROUTE_PAD)
    is_group = (sub >= MOE_EXPERTS) & (sub < MOE_EXPERTS + MOE_GROUPS)
    gl = jnp.where(is_group, logits, neg)
    g_max = jnp.max(gl, axis=0, keepdims=True)
    g_idx = jnp.min(jnp.where(gl == g_max, sub - MOE_EXPERTS, big), axis=0, keepdims=True)
    g_val = 1.0 / jnp.sum(jnp.exp(gl - g_max), axis=0, keepdims=True)
    in_group = (sub < MOE_EXPERTS) & ((sub // MOE_EPG) == g_idx)
    el = jnp.where(in_group, logits, neg)
    m1 = jnp.max(el, axis=0, keepdims=True)
    i1 = jnp.min(jnp.where(el == m1, sub, big), axis=0, keepdims=True)
    el2 = jnp.where(sub == i1, neg, el)
    m2 = jnp.max(el2, axis=0, keepdims=True)
    i2 = jnp.min(jnp.where(el2 == m2, sub, big), axis=0, keepdims=True)
    r = jnp.exp(m2 - m1)
    w_top1 = g_val / (1.0 + r)
    w_top2 = g_val * r / (1.0 + r)

    lo = jnp.minimum(i1, i2) - g_idx * MOE_EPG
    hi = jnp.maximum(i1, i2) - g_idx * MOE_EPG
    pair = (lo * (2 * MOE_EPG - 1 - lo)) // 2 + (hi - lo - 1)
    cls = g_idx * MOE_PAIRS + pair
    w_lo = jnp.where(i1 < i2, w_top1, w_top2)
    w_hi = jnp.where(i1 < i2, w_top2, w_top1)

    onehot = sub == cls
    before = _dot(onehot.astype(BF16), tri_ref[...]) + carry_ref[...]
    rank = jnp.sum(jnp.where(onehot, before, 0.0), axis=0, keepdims=True)
    carry_ref[...] += jnp.sum(jnp.where(onehot, 1.0, 0.0), axis=1, keepdims=True)
    cnt_ref[...] = carry_ref[...]
    cls_ref[...] = cls
    rank_ref[...] = rank.astype(jnp.int32)

    words = pltpu.pack_elementwise([h[:, 0:H_WORDS], h[:, H_WORDS:D_MODEL]], packed_dtype=BF16)
    for s in range(H_SUBLANES):
        pay_ref[pl.ds(s, tm, stride=TILE_ROWS), :] = words[:, s * TILE_LANES:(s + 1) * TILE_LANES]
    sub8 = lax.broadcasted_iota(jnp.int32, (TILE_ROWS, tm), 0)
    rec = jnp.where(sub8 == META_CLASS, cls.astype(F32),
                    jnp.where(sub8 == META_RANK, rank,
                              jnp.where(sub8 == META_WLO, w_lo, jnp.where(sub8 == META_WHI, w_hi, 0.0))))
    rec = jnp.concatenate([rec, jnp.zeros((META_W - TILE_ROWS, tm), F32)], axis=0)
    for b in range(tm // META_W):
        meta = rec[:, b * META_W:(b + 1) * META_W].T
        pay_ref[pl.ds(b * META_W * TILE_ROWS + H_SUBLANES, META_W, stride=TILE_ROWS), :] = (
            lax.bitcast_convert_type(meta, jnp.int32))
    for s in range(H_SUBLANES + 1, TILE_ROWS):
        pay_ref[pl.ds(s, tm, stride=TILE_ROWS), :] = jnp.zeros((tm, TILE_LANES), jnp.int32)


def _moe_route(x, nw, mod5, layer, w_route, b_route):
    tm = ROUTE_TM
    tiles_per_batch = SEQ // tm
    tri = (jnp.arange(tm, dtype=jnp.int32)[:, None] < jnp.arange(tm, dtype=jnp.int32)[None, :]).astype(BF16)
    return pl.pallas_call(
        _route_kernel,
        grid=(TOKENS // tm,),
        in_specs=[
            pl.BlockSpec((tm, D_MODEL), lambda i: (i, 0)),
            pl.BlockSpec((1, D_MODEL), lambda i: (0, 0)),
            _mod_spec(layer, 4, tiles_per_batch),
            _mod_spec(layer, 3, tiles_per_batch),
            pl.BlockSpec((ROUTE_PAD, D_MODEL), lambda i: (0, 0)),
            pl.BlockSpec((ROUTE_PAD, 1), lambda i: (0, 0)),
            pl.BlockSpec((tm, tm), lambda i: (0, 0)),
        ],
        out_specs=[pl.BlockSpec((tm * TILE_ROWS, TILE_LANES), lambda i: (i, 0)),
                   pl.BlockSpec((1, tm), lambda i: (0, i)),
                   pl.BlockSpec((1, tm), lambda i: (0, i)),
                   pl.BlockSpec((ROUTE_PAD, 1), lambda i: (0, 0))],
        out_shape=[jax.ShapeDtypeStruct((TOKENS * TILE_ROWS, TILE_LANES), jnp.int32),
                   jax.ShapeDtypeStruct((1, TOKENS), jnp.int32),
                   jax.ShapeDtypeStruct((1, TOKENS), jnp.int32),
                   jax.ShapeDtypeStruct((ROUTE_PAD, 1), F32)],
        scratch_shapes=[pltpu.VMEM((ROUTE_PAD, 1), F32)],
        compiler_params=pltpu.CompilerParams(
            dimension_semantics=("arbitrary",), vmem_limit_bytes=VMEM_LIMIT),
        name="moe_route",
    )(x, nw, mod5, mod5, w_route.T, b_route.reshape(ROUTE_PAD, 1), tri)


def _permute_kernel(pos_ref, src_ref, dst_hbm, stage_ref, sem):
    i = pl.program_id(0)
    slot = i % 2
    rows = PERMUTE_TM * TILE_ROWS

    def slot_wait(s):
        whole = stage_ref.at[pl.ds(s * rows, rows), :]
        pltpu.make_async_copy(whole, whole, sem.at[s]).wait()

    @pl.when(i >= 2)
    def _():
        slot_wait(slot)

    base = pl.multiple_of(slot * rows, rows)
    stage_ref[pl.ds(base, rows), :] = src_ref[...]

    def issue(r, _):
        pltpu.make_async_copy(stage_ref.at[pl.ds(base + r * TILE_ROWS, TILE_ROWS), :],
                              dst_hbm.at[pos_ref[i * PERMUTE_TM + r]], sem.at[slot]).start()
        return 0

    lax.fori_loop(0, PERMUTE_TM, issue, 0, unroll=16)

    @pl.when(i == pl.num_programs(0) - 1)
    def _():
        slot_wait(1 - slot)
        slot_wait(slot)


def _moe_permute(pos, payload):
    rows = PERMUTE_TM * TILE_ROWS
    return pl.pallas_call(
        _permute_kernel,
        grid_spec=pltpu.PrefetchScalarGridSpec(
            num_scalar_prefetch=1, grid=(TOKENS // PERMUTE_TM,),
            in_specs=[pl.BlockSpec((rows, TILE_LANES), lambda i, pos: (i, 0))],
            out_specs=pl.BlockSpec(memory_space=pl.ANY),
            scratch_shapes=[pltpu.VMEM((2 * rows, TILE_LANES), jnp.int32), pltpu.SemaphoreType.DMA((2,))]),
        out_shape=jax.ShapeDtypeStruct((TOKENS, TILE_ROWS, TILE_LANES), jnp.int32),
        compiler_params=pltpu.CompilerParams(
            dimension_semantics=("arbitrary",), vmem_limit_bytes=VMEM_LIMIT),
        name="moe_permute",
    )(pos, payload)


def _experts_kernel(blk_ref, cls_ref, elo_ref, ehi_ref, first_ref, last_ref, valid_ref, newlo_ref, newhi_ref,
                    pay_ref, w1lo_ref, w3lo_ref, w2lo_ref, w1hi_ref, w3hi_ref, w2hi_ref, o_ref,
                    acc_ref, win_ref, wout_ref):
    k = pl.program_id(0)
    rows = SORT_BLOCK

    @pl.when((valid_ref[k] == 1) & (newlo_ref[k] == 1))
    def _():
        win_ref[0] = w1lo_ref[...].astype(BF16)
        win_ref[1] = w3lo_ref[...].astype(BF16)
        wout_ref[0] = w2lo_ref[...].astype(BF16)

    @pl.when((valid_ref[k] == 1) & (newhi_ref[k] == 1))
    def _():
        win_ref[2] = w1hi_ref[...].astype(BF16)
        win_ref[3] = w3hi_ref[...].astype(BF16)
        wout_ref[1] = w2hi_ref[...].astype(BF16)

    @pl.when(valid_ref[k] == 1)
    def _():
        def sublane(s):
            return pay_ref[pl.ds(s, rows, stride=TILE_ROWS), :]

        halves = [[pltpu.unpack_elementwise(sublane(s), index=i, packed_dtype=BF16, unpacked_dtype=F32)
                   for s in range(H_SUBLANES)] for i in range(2)]
        hb = jnp.concatenate(halves[0] + halves[1], axis=1).astype(BF16)
        meta = lax.bitcast_convert_type(sublane(H_SUBLANES), F32)
        mine = meta[:, META_CLASS:META_CLASS + 1] == cls_ref[k].astype(F32)
        w_lo = jnp.where(mine, meta[:, META_WLO:META_WLO + 1], 0.0)
        w_hi = jnp.where(mine, meta[:, META_WHI:META_WHI + 1], 0.0)
        hid_lo = _silu(_dot(hb, win_ref[0])) * _dot(hb, win_ref[1]) * w_lo
        hid_hi = _silu(_dot(hb, win_ref[2])) * _dot(hb, win_ref[3]) * w_hi
        y = _dot(hid_lo.astype(BF16), wout_ref[0]) + _dot(hid_hi.astype(BF16), wout_ref[1])

        @pl.when(first_ref[k] == 1)
        def _():
            acc_ref[...] = y

        @pl.when(first_ref[k] == 0)
        def _():
            acc_ref[...] += y

        @pl.when(last_ref[k] == 1)
        def _():
            for s in range(TILE_ROWS):
                o_ref[pl.ds(s, rows, stride=TILE_ROWS), :] = acc_ref[:, s * TILE_LANES:(s + 1) * TILE_LANES]


def _moe_experts(items, payload_sorted, layer, w1, w3, w2):
    blk, cls, elo, ehi, first, last, valid, newlo, newhi = items
    tiles = pl.BlockSpec((SORT_BLOCK * TILE_ROWS, TILE_LANES), lambda k, blk, *_: (blk[k], 0))
    w_in = lambda which: pl.BlockSpec(
        (None, None, D_MODEL, MOE_HIDDEN), lambda k, blk, cls, elo, ehi, *_: (layer, (elo, ehi)[which][k], 0, 0))
    w_out = lambda which: pl.BlockSpec(
        (None, None, MOE_HIDDEN, D_MODEL), lambda k, blk, cls, elo, ehi, *_: (layer, (elo, ehi)[which][k], 0, 0))
    return pl.pallas_call(
        _experts_kernel,
        grid_spec=pltpu.PrefetchScalarGridSpec(
            num_scalar_prefetch=9, grid=(MAX_ITEMS,),
            in_specs=[tiles, w_in(0), w_in(0), w_out(0), w_in(1), w_in(1), w_out(1)],
            out_specs=tiles,
            scratch_shapes=[pltpu.VMEM((SORT_BLOCK, D_MODEL), F32),
                            pltpu.VMEM((4, D_MODEL, MOE_HIDDEN), BF16),
                            pltpu.VMEM((2, MOE_HIDDEN, D_MODEL), BF16)]),
        out_shape=jax.ShapeDtypeStruct((TOKENS * TILE_ROWS, TILE_LANES), F32),
        compiler_params=pltpu.CompilerParams(
            dimension_semantics=("arbitrary",), vmem_limit_bytes=VMEM_LIMIT),
        name="moe_experts",
    )(blk, cls, elo, ehi, first, last, valid, newlo, newhi, payload_sorted, w1, w3, w2, w1, w3, w2)


def _combine_kernel(pos_ref, x_ref, gate_ref, y_hbm, o_ref, buf_ref, sem):
    tm = COMBINE_TM
    i = pl.program_id(0)
    n = pl.num_programs(0)

    def gather(tile, slot):
        def issue(r, _):
            pltpu.make_async_copy(y_hbm.at[pos_ref[tile * tm + r]],
                                  buf_ref.at[pl.ds((slot * tm + r) * TILE_ROWS, TILE_ROWS), :], sem.at[slot]).start()
            return 0
        lax.fori_loop(0, tm, issue, 0, unroll=16)

    @pl.when(i == 0)
    def _():
        gather(0, 0)

    @pl.when(i + 1 < n)
    def _():
        gather(i + 1, (i + 1) % 2)

    slot = i % 2
    base = slot * tm * TILE_ROWS
    whole_slot = buf_ref.at[pl.ds(base, tm * TILE_ROWS), :]
    pltpu.make_async_copy(whole_slot, whole_slot, sem.at[slot]).wait()
    y = jnp.concatenate([buf_ref[pl.ds(base + s, tm, stride=TILE_ROWS), :] for s in range(TILE_ROWS)], axis=1)
    o_ref[...] = x_ref[...] + gate_ref[...] * y


def _moe_combine(pos, x, mod5, layer, y_sorted):
    tm = COMBINE_TM
    tiles_per_batch = SEQ // tm
    return pl.pallas_call(
        _combine_kernel,
        grid_spec=pltpu.PrefetchScalarGridSpec(
            num_scalar_prefetch=1, grid=(TOKENS // tm,),
            in_specs=[
                pl.BlockSpec((tm, D_MODEL), lambda i, pos: (i, 0)),
                pl.BlockSpec((None, None, None, 1, D_MODEL),
                             lambda i, pos: (layer, i // tiles_per_batch, 5, 0, 0)),
                pl.BlockSpec(memory_space=pl.ANY),
            ],
            out_specs=pl.BlockSpec((tm, D_MODEL), lambda i, pos: (i, 0)),
            scratch_shapes=[pltpu.VMEM((2 * tm * TILE_ROWS, TILE_LANES), F32), pltpu.SemaphoreType.DMA((2,))]),
        out_shape=jax.ShapeDtypeStruct((TOKENS, D_MODEL), F32),
        compiler_params=pltpu.CompilerParams(
            dimension_semantics=("arbitrary",), vmem_limit_bytes=VMEM_LIMIT),
        name="moe_combine",
    )(pos, x, mod5, y_sorted)


def _moe_plan(cls, rank, counts):
    count = counts[:MOE_CLASSES, 0].astype(jnp.int32)
    ends = jnp.cumsum(count)
    starts = ends - count
    class_ids = jnp.arange(MOE_CLASSES, dtype=jnp.int32)

    def lookup(table, idx):
        return jnp.sum(jnp.where(idx[..., None] == class_ids, table, 0), axis=-1)

    pos = (lookup(starts, cls) + rank).reshape(TOKENS)

    first_blk = starts // SORT_BLOCK
    n_items = jnp.where(count > 0, (ends - 1) // SORT_BLOCK - first_blk + 1, 0)
    item_end = jnp.cumsum(n_items)
    item_start = item_end - n_items
    k = jnp.arange(MAX_ITEMS, dtype=jnp.int32)
    valid = k < item_end[-1]
    kc = jnp.minimum(k, item_end[-1] - 1)
    icls = jnp.sum((item_end[None, :] <= kc[:, None]).astype(jnp.int32), axis=1)
    blk = lookup(first_blk, icls) + (kc - lookup(item_start, icls))
    first = jnp.concatenate([jnp.ones((1,), jnp.int32), (blk[1:] != blk[:-1]).astype(jnp.int32)])
    last = jnp.concatenate([(blk[1:] != blk[:-1]) | ~valid[1:], jnp.ones((1,), bool)]).astype(jnp.int32)
    group = icls // MOE_PAIRS
    elo = group * MOE_EPG + lookup(jnp.asarray(PAIR_LO * MOE_GROUPS, jnp.int32), icls)
    ehi = group * MOE_EPG + lookup(jnp.asarray(PAIR_HI * MOE_GROUPS, jnp.int32), icls)
    changed = lambda e: jnp.concatenate([jnp.ones((1,), jnp.int32), (e[1:] != e[:-1]).astype(jnp.int32)])
    return pos, (blk, icls, elo, ehi, first * valid, last * valid, valid.astype(jnp.int32),
                 changed(elo), changed(ehi))


def _moe(x, nw, mod5, layer, w_route, b_route, w1, w3, w2):
    payload, cls, rank, counts = _moe_route(x, nw, mod5, layer, w_route, b_route)
    pos, items = _moe_plan(cls, rank, counts)
    tiles = (TOKENS, TILE_ROWS, TILE_LANES)
    sorted_payload = _moe_permute(pos, payload).reshape(TOKENS * TILE_ROWS, TILE_LANES)
    y_sorted = _moe_experts(items, sorted_payload, layer, w1, w3, w2)
    return _moe_combine(pos, x, mod5, layer, y_sorted.reshape(tiles))


def _pad_cols(w, width):
    return jnp.pad(w, ((0, 0), (0, width - w.shape[1])))


def kernel(x, c, ada_w, ada_b, norm_mix, norm_ffn, ssd_w_in, ssd_conv_w, ssd_conv_b, ssd_dt_bias,
           ssd_a_log, ssd_d, ssd_norm, ssd_w_out, dsa_w_in, dsa_q_norm, dsa_k_norm, dsa_w_out,
           moe_w_group, moe_b_group, moe_w_expert, moe_b_expert, moe_w1, moe_w3, moe_w2):
    depth = ada_w.shape[0]
    xt = x.reshape(TOKENS, D_MODEL)
    mod = _modulation(c, ada_w, ada_b)
    mod5 = mod.reshape(depth, BATCH, 6, 1, D_MODEL)

    head_of_col = jnp.arange(SSD_D_INNER, dtype=jnp.int32) // SSD_HEAD_DIM
    expand = (jnp.arange(SSD_DT_PAD, dtype=jnp.int32)[:, None] == head_of_col[None, :]).astype(BF16)

    for i in range(depth):
        j = i // 2
        nw_mix = norm_mix[i].reshape(1, D_MODEL)
        if i % 2 == 0:
            w_in = _pad_cols(ssd_w_in[j], SSD_PROJ_PAD).astype(BF16)
            z, xbc, dt = _inproj(
                xt, nw_mix, mod5, i, 1, 0, w_in,
                ((0, SSD_D_INNER), (SSD_D_INNER, SSD_D_INNER + SSD_CONV_DIM),
                 (SSD_D_INNER + SSD_CONV_DIM, SSD_PROJ_PAD)))
            xt = _ssd_mixer(
                z, xbc, dt, xt, ssd_conv_w[j], ssd_conv_b[j].reshape(1, SSD_CONV_DIM),
                _pad_cols(ssd_dt_bias[j].reshape(1, SSD_HEADS), SSD_DT_PAD),
                _pad_cols(ssd_a_log[j].reshape(1, SSD_HEADS), SSD_DT_PAD),
                jnp.repeat(ssd_d[j], SSD_HEAD_DIM).reshape(1, SSD_D_INNER),
                ssd_norm[j].reshape(1, SSD_D_INNER), expand, ssd_w_out[j].astype(BF16), mod5, i)
        else:
            w = dsa_w_in[j]
            w_in = jnp.concatenate(
                [_pad_cols(w[:, :DSA_KI_START + IDX_HEAD_DIM], DSA_WI_START),
                 _pad_cols(w[:, DSA_KI_START + IDX_HEAD_DIM:], 128)], axis=1).astype(BF16)
            q, k, v, qi, ki, wi = _dsa_inproj(
                xt, nw_mix, mod5, i, w_in, dsa_q_norm[j].reshape(1, ATT_HEAD_DIM),
                dsa_k_norm[j].reshape(1, ATT_HEAD_DIM))
            xt = _dsa_mixer(q, k, v, qi, ki, wi, xt, dsa_w_out[j].astype(BF16), mod5, i)

        w_route = _pad_cols(jnp.concatenate([moe_w_expert[i], moe_w_group[i]], axis=1), ROUTE_PAD).astype(BF16)
        b_route = _pad_cols(jnp.concatenate([moe_b_expert[i], moe_b_group[i]]).reshape(1, -1), ROUTE_PAD)
        xt = _moe(xt, norm_ffn[i].reshape(1, D_MODEL), mod5, i, w_route, b_route,
                  moe_w1, moe_w3, moe_w2)
    return xt.reshape(BATCH, SEQ, D_MODEL)
```

```python
import functools

import jax
import jax.numpy as jnp
from jax import lax
from jax.experimental import pallas as pl
from jax.experimental.pallas import tpu as pltpu

F32 = jnp.float32
BF16 = jnp.bfloat16

D_MODEL = 1024
BATCH = 8
SEQ = 2048
TOKENS = BATCH * SEQ
EPS = 1e-6

SSD_D_INNER = 2048
SSD_HEAD_DIM = 64
SSD_HEADS = 32
SSD_GROUPS = 8
SSD_HEADS_PER_GROUP = 4
SSD_STATE = 128
SSD_CONV = 4
SSD_CHUNK = 128
SSD_GN = SSD_GROUPS * SSD_STATE
SSD_CONV_DIM = SSD_D_INNER + 2 * SSD_GN
SSD_GROUP_W = SSD_HEADS_PER_GROUP * SSD_HEAD_DIM
SSD_DT_PAD = 128
SSD_PROJ_PAD = SSD_D_INNER + SSD_CONV_DIM + SSD_DT_PAD
CONV_HALO = 8

ATT_HEADS = 16
ATT_KV_HEADS = 4
ATT_Q_PER_KV = 4
ATT_HEAD_DIM = 64
IDX_HEADS = 8
IDX_HEAD_DIM = 64
TOPK = 256
Q_BLOCK = 128
DSA_Q = ATT_HEADS * ATT_HEAD_DIM
DSA_KV = ATT_KV_HEADS * ATT_HEAD_DIM
DSA_QI = IDX_HEADS * IDX_HEAD_DIM
DSA_KI_START = DSA_Q + 2 * DSA_KV + DSA_QI
DSA_WI_START = DSA_KI_START + 128
DSA_PROJ_PAD = DSA_WI_START + 128

MOE_GROUPS = 4
MOE_EPG = 4
MOE_EXPERTS = 16
MOE_HIDDEN = 256
ROUTE_PAD = 128

VMEM_LIMIT = 56 * 1024 * 1024


def _sigmoid(v):
    return 1.0 / (1.0 + jnp.exp(-v))


def _silu(v):
    return v * _sigmoid(v)


def _split3(a):
    hi = a.astype(BF16)
    r = a - hi.astype(F32)
    mid = r.astype(BF16)
    lo = (r - mid.astype(F32)).astype(BF16)
    return hi, mid, lo


def _dot(a, b):
    return jnp.dot(a, b, preferred_element_type=F32)


def _dot_nt(a, b):
    return lax.dot_general(a, b, (((1,), (1,)), ((), ())), preferred_element_type=F32)


def _dot3_exact_rhs(a, m):
    hi, mid, lo = _split3(a)
    return _dot(hi, m) + _dot(mid, m) + _dot(lo, m)


def _dot3_exact_lhs(m, a):
    hi, mid, lo = _split3(a)
    return _dot(m, hi) + _dot(m, mid) + _dot(m, lo)


def _norm_mod(x, nw, scale, shift):
    ms = jnp.mean(x * x, axis=-1, keepdims=True)
    return x * lax.rsqrt(ms + EPS) * nw * (1.0 + scale) + shift


MOD_TN = 1536


def _mod_kernel(c_ref, w_ref, b_ref, o_ref):
    cond = _silu(c_ref[...]).astype(BF16)
    o_ref[...] = _dot(cond, w_ref[...].astype(BF16)) + b_ref[...]


def _modulation(c, ada_w, ada_b):
    depth = ada_w.shape[0]
    n = ada_w.shape[2]
    return pl.pallas_call(
        _mod_kernel,
        grid=(depth, n // MOD_TN),
        in_specs=[
            pl.BlockSpec((BATCH, D_MODEL), lambda i, j: (0, 0)),
            pl.BlockSpec((None, D_MODEL, MOD_TN), lambda i, j: (i, 0, j)),
            pl.BlockSpec((None, 1, MOD_TN), lambda i, j: (i, 0, j)),
        ],
        out_specs=pl.BlockSpec((None, BATCH, MOD_TN), lambda i, j: (i, 0, j)),
        out_shape=jax.ShapeDtypeStruct((depth, BATCH, n), F32),
        compiler_params=pltpu.CompilerParams(
            dimension_semantics=("arbitrary", "arbitrary"), vmem_limit_bytes=VMEM_LIMIT),
        name="adaln_mod",
    )(c, ada_w, ada_b.reshape(depth, 1, n))


def _mod_spec(layer, chunk, rows_per_batch_tile):
    return pl.BlockSpec((None, None, None, 1, D_MODEL),
                        lambda i, *_: (layer, i // rows_per_batch_tile, chunk, 0, 0))


INPROJ_TM = 256


def _inproj_kernel(x_ref, nw_ref, scale_ref, shift_ref, w_ref, *o_refs, col_slices):
    h = _norm_mod(x_ref[...], nw_ref[...], scale_ref[...], shift_ref[...]).astype(BF16)
    for n, (o_ref, (lo, hi)) in enumerate(zip(o_refs, col_slices)):
        r = _dot(h, w_ref[:, lo:hi])
        o_ref[...] = _silu(r) if n == 0 else r


def _inproj(x, nw, mod5, layer, scale_chunk, shift_chunk, w, col_slices):
    tm = INPROJ_TM
    tiles_per_batch = SEQ // tm
    n_pad = w.shape[1]
    return pl.pallas_call(
        functools.partial(_inproj_kernel, col_slices=col_slices),
        grid=(TOKENS // tm,),
        in_specs=[
            pl.BlockSpec((tm, D_MODEL), lambda i: (i, 0)),
            pl.BlockSpec((1, D_MODEL), lambda i: (0, 0)),
            _mod_spec(layer, scale_chunk, tiles_per_batch),
            _mod_spec(layer, shift_chunk, tiles_per_batch),
            pl.BlockSpec((D_MODEL, n_pad), lambda i: (0, 0)),
        ],
        out_specs=[pl.BlockSpec((tm, hi - lo), lambda i: (i, 0)) for lo, hi in col_slices],
        out_shape=[jax.ShapeDtypeStruct((TOKENS, hi - lo), F32) for lo, hi in col_slices],
        compiler_params=pltpu.CompilerParams(
            dimension_semantics=("arbitrary",), vmem_limit_bytes=VMEM_LIMIT),
        name="norm_inproj",
    )(x, nw, mod5, mod5, w)


CONV_COLS = 512


def _ssd_kernel(z_ref, xbc_ref, dt_ref, xres_ref, cw_ref, cb_ref, dtb_ref, alog_ref, de_ref, nw_ref,
                e_ref, wout_ref, gate_ref, o_ref, state_ref, ext_ref, act_ref, yn_ref):
    q = SSD_CHUNK
    c = pl.program_id(1)

    @pl.when(c == 0)
    def _():
        state_ref[...] = jnp.zeros_like(state_ref)
        ext_ref[0:CONV_HALO, :] = jnp.zeros((CONV_HALO, SSD_CONV_DIM), F32)

    @pl.when(c > 0)
    def _():
        ext_ref[0:CONV_HALO, :] = ext_ref[q:q + CONV_HALO, :]

    ext_ref[CONV_HALO:CONV_HALO + q, :] = xbc_ref[...]

    for s in range(SSD_CONV_DIM // CONV_COLS):
        cs = slice(s * CONV_COLS, (s + 1) * CONV_COLS)
        u = ext_ref[:, cs]
        acc = cw_ref[0:1, cs] * u
        for k in range(1, SSD_CONV):
            acc = pltpu.roll(acc, 1, axis=0) + cw_ref[k:k + 1, cs] * u
        act_ref[:, cs] = _silu(acc[CONV_HALO:CONV_HALO + q, :] + cb_ref[:, cs])

    dt_raw = dt_ref[...] + dtb_ref[...]
    dt = jnp.maximum(dt_raw, 0.0) + jnp.log1p(jnp.exp(-jnp.abs(dt_raw)))
    a = dt * (-jnp.exp(alog_ref[...]) * LOG2E)
    row = lax.broadcasted_iota(jnp.int32, (q, q), 0)
    col = lax.broadcasted_iota(jnp.int32, (q, q), 1)
    tril = row >= col
    acs = _dot3_exact_lhs(tril.astype(BF16), a)
    acs_t = acs.T
    expand = e_ref[...]
    acs_e = _dot3_exact_rhs(acs, expand)
    dt_e = _dot3_exact_rhs(dt, expand)
    tot_e = acs_e[q - 1:q, :]
    decay_from_start = jnp.exp2(acs_e)
    decay_to_end = jnp.exp2(tot_e - acs_e)
    chunk_decay = jnp.exp2(tot_e)

    lane_head = lax.broadcasted_iota(jnp.int32, (q, SSD_GROUP_W), 1) // SSD_HEAD_DIM
    for g in range(SSD_GROUPS):
        xs = act_ref[:, g * SSD_GROUP_W:(g + 1) * SSD_GROUP_W]
        gs = slice(g * SSD_GROUP_W, (g + 1) * SSD_GROUP_W)
        bm = act_ref[:, SSD_D_INNER + g * SSD_STATE:SSD_D_INNER + (g + 1) * SSD_STATE]
        cm = act_ref[:, SSD_D_INNER + SSD_GN + g * SSD_STATE:SSD_D_INNER + SSD_GN + (g + 1) * SSD_STATE]
        bm_t = bm.T.astype(BF16)
        cm_b = cm.astype(BF16)
        cb = _dot(cm_b, bm_t)
        xd = xs * dt_e[:, gs]
        ms = []
        xds = []
        for j in range(SSD_HEADS_PER_GROUP):
            h = g * SSD_HEADS_PER_GROUP + j
            seg = acs[:, h:h + 1] - acs_t[h:h + 1, :]
            dec = jnp.exp2(jnp.where(tril, seg, -jnp.inf))
            ms.append((cb * dec).astype(BF16))
            xds.append(jnp.where(lane_head == j, xd, 0.0).astype(BF16))
        y_diag = _dot(jnp.concatenate(ms, axis=1), jnp.concatenate(xds, axis=0))
        prev = state_ref[g]
        y_off = _dot(cm_b, prev.astype(BF16)) * decay_from_start[:, gs]
        state_ref[g] = prev * chunk_decay[:, gs] + _dot(bm_t, (xd * decay_to_end[:, gs]).astype(BF16))
        y = y_diag + y_off + xs * de_ref[:, gs]
        y = y * z_ref[:, gs]
        y = y * lax.rsqrt(jnp.mean(y * y, axis=-1, keepdims=True) + EPS) * nw_ref[:, gs]
        yn_ref[:, gs] = y.astype(BF16)

    out = _dot(yn_ref[...], wout_ref[...])
    o_ref[...] = xres_ref[...] + gate_ref[...] * out


def _ssd_mixer(z, xbc, dt, x, conv_w, conv_b, dt_bias, a_log, d_e, norm_w, expand, w_out, mod5, layer):
    q = SSD_CHUNK
    nc = SEQ // q
    tok = lambda w: pl.BlockSpec((q, w), lambda b, c: (b * nc + c, 0))
    full = lambda r, w: pl.BlockSpec((r, w), lambda b, c: (0, 0))
    return pl.pallas_call(
        _ssd_kernel,
        grid=(BATCH, nc),
        in_specs=[
            tok(SSD_D_INNER), tok(SSD_CONV_DIM), tok(SSD_DT_PAD), tok(D_MODEL),
            full(SSD_CONV, SSD_CONV_DIM), full(1, SSD_CONV_DIM), full(1, SSD_DT_PAD), full(1, SSD_DT_PAD),
            full(1, SSD_D_INNER), full(1, SSD_D_INNER), full(SSD_DT_PAD, SSD_D_INNER),
            full(SSD_D_INNER, D_MODEL),
            pl.BlockSpec((None, None, None, 1, D_MODEL), lambda b, c: (layer, b, 2, 0, 0)),
        ],
        out_specs=tok(D_MODEL),
        out_shape=jax.ShapeDtypeStruct((TOKENS, D_MODEL), F32),
        scratch_shapes=[
            pltpu.VMEM((SSD_GROUPS, SSD_STATE, SSD_GROUP_W), F32),
            pltpu.VMEM((q + CONV_HALO, SSD_CONV_DIM), F32),
            pltpu.VMEM((q, SSD_CONV_DIM), F32),
            pltpu.VMEM((q, SSD_D_INNER), BF16),
        ],
        compiler_params=pltpu.CompilerParams(
            dimension_semantics=("arbitrary", "arbitrary"), vmem_limit_bytes=VMEM_LIMIT),
        name="ssd_mixer",
    )(z, xbc, dt, x, conv_w, conv_b, dt_bias, a_log, d_e, norm_w, expand, w_out, mod5)


DSA_KEY_TILE = 256
DSA_CLASSES = 8
V_EXT = 2 * ATT_HEAD_DIM
DSA_BLOCKS_PER_CLASS = (SEQ // Q_BLOCK) // DSA_CLASSES
N_BISECT = 12
F32_MIN = float(jnp.finfo(jnp.float32).min)
LOG2E = 1.4426950408889634


def _count(mask):
    return jnp.sum(jnp.where(mask, 1.0, 0.0), axis=-1, keepdims=True)


def _select_topk(score_ref, q_pos, n_keys):
    kf = float(TOPK)
    small = (q_pos + 1) <= TOPK
    sc = score_ref[...]
    hi0 = jnp.max(sc, axis=-1, keepdims=True)
    lo0 = jnp.min(jnp.where(sc == -jnp.inf, jnp.inf, sc), axis=-1, keepdims=True)

    def bisect(_, carry):
        lo, hi = carry
        mid = lo + 0.5 * (hi - lo)
        ok = _count(score_ref[...] >= mid) >= kf
        return jnp.where(ok, mid, lo), jnp.where(ok, hi, mid)

    _, hi = lax.fori_loop(0, N_BISECT, bisect, (lo0, hi0))

    v0 = jnp.max(jnp.where(sc <= hi, sc, -jnp.inf), axis=-1, keepdims=True)
    c0 = _count(sc >= v0)
    pend0 = jnp.where((c0 >= kf) | small, 0.0, 1.0)

    def walk_cond(carry):
        return (carry[3] > 0.0) & (carry[4] < n_keys)

    def walk(carry):
        v, c, pend, _, it = carry
        s = score_ref[...]
        v2 = jnp.max(jnp.where(s < v, s, -jnp.inf), axis=-1, keepdims=True)
        c2 = _count(s >= v2)
        v = jnp.where(pend > 0.0, v2, v)
        c = jnp.where(pend > 0.0, c2, c)
        pend = jnp.where(c2 >= kf, 0.0, pend)
        return v, c, pend, jnp.max(pend), it + 1

    v, c, _, _, _ = lax.while_loop(walk_cond, walk, (v0, c0, pend0, jnp.max(pend0), jnp.int32(0)))
    thr = jnp.where(small, F32_MIN, v)
    any_tie = jnp.max(jnp.where(small, 0.0, c - kf)) > 0.0

    @pl.when(jnp.logical_not(any_tie))
    def _():
        score_ref[...] = jnp.where(score_ref[...] >= thr, 0.0, -jnp.inf)

    @pl.when(any_tie)
    def _():
        s = score_ref[...]
        key_pos = lax.broadcasted_iota(jnp.int32, (Q_BLOCK, n_keys), 1)
        gt = s > thr
        eq = s == thr
        need = kf - _count(gt)

        def body(_, carry):
            lo, hi = carry
            mid = (lo + hi) >> 1
            ok = _count((score_ref[...] == thr) & (key_pos <= mid)) >= need
            return jnp.where(ok, lo, mid), jnp.where(ok, mid, hi)

        init = (jnp.full((Q_BLOCK, 1), -1, jnp.int32), jnp.full((Q_BLOCK, 1), n_keys - 1, jnp.int32))
        cut = lax.fori_loop(0, (n_keys - 1).bit_length() + 1, body, init)[1]
        score_ref[...] = jnp.where(gt | (eq & (key_pos <= cut)), 0.0, -jnp.inf)


def _dsa_inproj_kernel(x_ref, nw_ref, scale_ref, shift_ref, w_ref, qn_ref, kn_ref, seg_ref, segt_ref,
                       q_ref, k_ref, v_ref, qi_ref, ki_ref, wi_ref):
    hd = ATT_HEAD_DIM
    h = _norm_mod(x_ref[...], nw_ref[...], scale_ref[...], shift_ref[...]).astype(BF16)

    def head_norm(t, w):
        width = t.shape[1]
        ss = _dot((t * t).astype(BF16), seg_ref[0:width, :])
        r = lax.rsqrt(ss * (1.0 / hd) + EPS)
        r_hi = r.astype(BF16)
        r_lo = (r - r_hi.astype(F32)).astype(BF16)
        return t * (_dot(r_hi, segt_ref[:, 0:width]) + _dot(r_lo, segt_ref[:, 0:width])) * w

    q = head_norm(_dot(h, w_ref[:, 0:DSA_Q]), qn_ref[...] * (hd ** -0.5 * LOG2E))
    for n in range(ATT_HEADS):
        q_ref[n] = q[:, n * hd:(n + 1) * hd].astype(BF16)
    kv = _dot(h, w_ref[:, DSA_Q:DSA_Q + 2 * DSA_KV])
    k = head_norm(kv[:, 0:DSA_KV], kn_ref[...])
    for n in range(ATT_KV_HEADS):
        k_ref[n] = k[:, n * hd:(n + 1) * hd].astype(BF16)
        v_ref[n] = jnp.concatenate([kv[:, DSA_KV + n * hd:DSA_KV + (n + 1) * hd],
                                    jnp.ones((kv.shape[0], V_EXT - hd), F32)], axis=1).astype(BF16)
    qi = _dot(h, w_ref[:, DSA_Q + 2 * DSA_KV:DSA_KI_START])
    for n in range(IDX_HEADS):
        qi_ref[n] = qi[:, n * IDX_HEAD_DIM:(n + 1) * IDX_HEAD_DIM].astype(BF16)
    ki_ref[...] = _dot(h, w_ref[:, DSA_KI_START:DSA_KI_START + IDX_HEAD_DIM]).astype(BF16)
    wi_ref[...] = _dot(h, w_ref[:, DSA_WI_START:DSA_WI_START + IDX_HEADS]) * ((IDX_HEADS * IDX_HEAD_DIM) ** -0.5)


def _dsa_inproj(x, nw, mod5, layer, w, q_norm, k_norm):
    tm = INPROJ_TM
    tiles_per_batch = SEQ // tm
    heads = lambda n: pl.BlockSpec((n, tm, ATT_HEAD_DIM), lambda i: (0, i, 0))
    head_of = jnp.arange(DSA_Q, dtype=jnp.int32) // ATT_HEAD_DIM
    seg = (head_of[:, None] == jnp.arange(128, dtype=jnp.int32)[None, :]).astype(BF16)
    q_norm = jnp.tile(q_norm, (1, ATT_HEADS))
    k_norm = jnp.tile(k_norm, (1, ATT_KV_HEADS))
    return pl.pallas_call(
        _dsa_inproj_kernel,
        grid=(TOKENS // tm,),
        in_specs=[
            pl.BlockSpec((tm, D_MODEL), lambda i: (i, 0)),
            pl.BlockSpec((1, D_MODEL), lambda i: (0, 0)),
            _mod_spec(layer, 1, tiles_per_batch),
            _mod_spec(layer, 0, tiles_per_batch),
            pl.BlockSpec((D_MODEL, DSA_PROJ_PAD), lambda i: (0, 0)),
            pl.BlockSpec((1, DSA_Q), lambda i: (0, 0)),
            pl.BlockSpec((1, DSA_KV), lambda i: (0, 0)),
            pl.BlockSpec((DSA_Q, 128), lambda i: (0, 0)),
            pl.BlockSpec((128, DSA_Q), lambda i: (0, 0)),
        ],
        out_specs=[heads(ATT_HEADS), heads(ATT_KV_HEADS),
                   pl.BlockSpec((ATT_KV_HEADS, tm, V_EXT), lambda i: (0, i, 0)), heads(IDX_HEADS),
                   pl.BlockSpec((tm, IDX_HEAD_DIM), lambda i: (i, 0)),
                   pl.BlockSpec((tm, IDX_HEADS), lambda i: (i, 0))],
        out_shape=[jax.ShapeDtypeStruct((ATT_HEADS, TOKENS, ATT_HEAD_DIM), BF16),
                   jax.ShapeDtypeStruct((ATT_KV_HEADS, TOKENS, ATT_HEAD_DIM), BF16),
                   jax.ShapeDtypeStruct((ATT_KV_HEADS, TOKENS, V_EXT), BF16),
                   jax.ShapeDtypeStruct((IDX_HEADS, TOKENS, IDX_HEAD_DIM), BF16),
                   jax.ShapeDtypeStruct((TOKENS, IDX_HEAD_DIM), BF16),
                   jax.ShapeDtypeStruct((TOKENS, IDX_HEADS), F32)],
        compiler_params=pltpu.CompilerParams(
            dimension_semantics=("arbitrary",), vmem_limit_bytes=VMEM_LIMIT),
        name="dsa_inproj",
    )(x, nw, mod5, mod5, w, q_norm, k_norm, seg, seg.T)


def _dsa_kernel(q_ref, k_ref, v_ref, qi_ref, ki_ref, wi_ref, xres_ref, wout_ref, gate_ref, o_ref,
                score_ref, ocat_ref, *, n_keys, first_block):
    hd = ATT_HEAD_DIM
    q_pos = (first_block + pl.program_id(1)) * Q_BLOCK + lax.broadcasted_iota(jnp.int32, (Q_BLOCK, 1), 0)

    wi = wi_ref[...]
    qi = qi_ref[...].reshape(IDX_HEADS * Q_BLOCK, IDX_HEAD_DIM)
    for kt in range(n_keys // DSA_KEY_TILE):
        ks = slice(kt * DSA_KEY_TILE, (kt + 1) * DSA_KEY_TILE)
        raw = _dot_nt(qi, ki_ref[ks, :])
        acc = jnp.zeros((Q_BLOCK, DSA_KEY_TILE), F32)
        for n in range(IDX_HEADS):
            acc = acc + wi[:, n:n + 1] * jnp.maximum(raw[n * Q_BLOCK:(n + 1) * Q_BLOCK, :], 0.0)
        key_pos = kt * DSA_KEY_TILE + lax.broadcasted_iota(jnp.int32, (Q_BLOCK, DSA_KEY_TILE), 1)
        score_ref[:, ks] = jnp.where(key_pos <= q_pos, acc, -jnp.inf)

    if n_keys > TOPK:
        _select_topk(score_ref, q_pos, n_keys)
        bias = score_ref[...][None, :, :]
    else:
        bias = jnp.where(score_ref[...] == -jnp.inf, -jnp.inf, 0.0)[None, :, :]

    for n in range(ATT_KV_HEADS):
        q4 = q_ref[n * ATT_Q_PER_KV:(n + 1) * ATT_Q_PER_KV].reshape(ATT_Q_PER_KV * Q_BLOCK, hd)
        s = _dot_nt(q4, k_ref[n]).reshape(ATT_Q_PER_KV, Q_BLOCK, n_keys) + bias
        p = jnp.exp2(s - jnp.max(s, axis=-1, keepdims=True))
        o = _dot(p.reshape(ATT_Q_PER_KV * Q_BLOCK, n_keys).astype(BF16), v_ref[n])
        o = o[:, 0:hd] * (1.0 / o[:, hd:hd + 1])
        for g in range(ATT_Q_PER_KV):
            col = (n * ATT_Q_PER_KV + g) * hd
            ocat_ref[:, col:col + hd] = o[g * Q_BLOCK:(g + 1) * Q_BLOCK, :]
    out = _dot(ocat_ref[...].astype(BF16), wout_ref[...])
    o_ref[...] = xres_ref[...] + gate_ref[...] * out


def _dsa_mixer(q, k, v, qi, ki, wi, x, w_out, mod5, layer):
    nb = SEQ // Q_BLOCK
    k4 = k.reshape(ATT_KV_HEADS, BATCH, SEQ, ATT_HEAD_DIM)
    v4 = v.reshape(ATT_KV_HEADS, BATCH, SEQ, V_EXT)
    ki3 = ki.reshape(BATCH, SEQ, IDX_HEAD_DIM)
    for cls in range(DSA_CLASSES):
        n_keys = (cls + 1) * (SEQ // DSA_CLASSES)
        first_block = cls * DSA_BLOCKS_PER_CLASS
        row = lambda b, i, fb=first_block: b * nb + fb + i
        heads = lambda n: pl.BlockSpec((n, Q_BLOCK, ATT_HEAD_DIM), lambda b, i: (0, row(b, i), 0))
        keys = lambda width: pl.BlockSpec((ATT_KV_HEADS, None, n_keys, width), lambda b, i: (0, b, 0, 0))
        x = pl.pallas_call(
            functools.partial(_dsa_kernel, n_keys=n_keys, first_block=first_block),
            grid=(BATCH, DSA_BLOCKS_PER_CLASS),
            in_specs=[
                heads(ATT_HEADS), keys(ATT_HEAD_DIM), keys(V_EXT), heads(IDX_HEADS),
                pl.BlockSpec((None, n_keys, IDX_HEAD_DIM), lambda b, i: (b, 0, 0)),
                pl.BlockSpec((Q_BLOCK, IDX_HEADS), lambda b, i: (row(b, i), 0)),
                pl.BlockSpec((Q_BLOCK, D_MODEL), lambda b, i: (row(b, i), 0)),
                pl.BlockSpec((DSA_Q, D_MODEL), lambda b, i: (0, 0)),
                pl.BlockSpec((None, None, None, 1, D_MODEL), lambda b, i: (layer, b, 2, 0, 0)),
            ],
            out_specs=pl.BlockSpec((Q_BLOCK, D_MODEL), lambda b, i: (row(b, i), 0)),
            out_shape=jax.ShapeDtypeStruct((TOKENS, D_MODEL), F32),
            scratch_shapes=[
                pltpu.VMEM((Q_BLOCK, n_keys), F32),
                pltpu.VMEM((Q_BLOCK, D_MODEL), F32),
            ],
            input_output_aliases={6: 0},
            compiler_params=pltpu.CompilerParams(
                dimension_semantics=("arbitrary", "arbitrary"), vmem_limit_bytes=VMEM_LIMIT),
            name=f"dsa_mixer_c{cls}",
        )(q, k4, v4, qi, ki3, wi, x, w_out, mod5)
    return x


MOE_PAIRS = MOE_EPG * (MOE_EPG - 1) // 2
MOE_CLASSES = MOE_GROUPS * MOE_PAIRS
PAIR_LO = (0, 0, 0, 1, 1, 2)
PAIR_HI = (1, 2, 3, 2, 3, 3)
ROUTE_TM = 1024
META_W = 128
META_CLASS, META_RANK, META_WLO, META_WHI = 0, 1, 2, 3
TILE_ROWS, TILE_LANES = 8, 128
H_WORDS = D_MODEL // 2
H_SUBLANES = H_WORDS // TILE_LANES
SORT_BLOCK = 256
N_SORT_BLOCKS = TOKENS // SORT_BLOCK
MAX_ITEMS = N_SORT_BLOCKS + MOE_CLASSES
PERMUTE_TM = 512
COMBINE_TM = 512


def _route_kernel(x_ref, nw_ref, scale_ref, shift_ref, wrt_ref, brt_ref, tri_ref, pay_ref, cls_ref, rank_ref,
                  cnt_ref, carry_ref):
    tm = ROUTE_TM

    @pl.when(pl.program_id(0) == 0)
    def _():
        carry_ref[...] = jnp.zeros_like(carry_ref)

    h = _norm_mod(x_ref[...], nw_ref[...], scale_ref[...], shift_ref[...])
    logits = _dot_nt(wrt_ref[...], h.astype(BF16)) + brt_ref[...]
    sub = lax.broadcasted_iota(jnp.int32, logits.shape, 0)
    neg = -jnp.inf
    big = jnp.int32(ROUTE_PAD)
    is_group = (sub >= MOE_EXPERTS) & (sub < MOE_EXPERTS + MOE_GROUPS)
    gl = jnp.where(is_group, logits, neg)
    g_max = jnp.max(gl, axis=0, keepdims=True)
    g_idx = jnp.min(jnp.where(gl == g_max, sub - MOE_EXPERTS, big), axis=0, keepdims=True)
    g_val = 1.0 / jnp.sum(jnp.exp(gl - g_max), axis=0, keepdims=True)
    in_group = (sub < MOE_EXPERTS) & ((sub // MOE_EPG) == g_idx)
    el = jnp.where(in_group, logits, neg)
    m1 = jnp.max(el, axis=0, keepdims=True)
    i1 = jnp.min(jnp.where(el == m1, sub, big), axis=0, keepdims=True)
    el2 = jnp.where(sub == i1, neg, el)
    m2 = jnp.max(el2, axis=0, keepdims=True)
    i2 = jnp.min(jnp.where(el2 == m2, sub, big), axis=0, keepdims=True)
    r = jnp.exp(m2 - m1)
    w_top1 = g_val / (1.0 + r)
    w_top2 = g_val * r / (1.0 + r)

    lo = jnp.minimum(i1, i2) - g_idx * MOE_EPG
    hi = jnp.maximum(i1, i2) - g_idx * MOE_EPG
    pair = (lo * (2 * MOE_EPG - 1 - lo)) // 2 + (hi - lo - 1)
    cls = g_idx * MOE_PAIRS + pair
    w_lo = jnp.where(i1 < i2, w_top1, w_top2)
    w_hi = jnp.where(i1 < i2, w_top2, w_top1)

    onehot = sub == cls
    before = _dot(onehot.astype(BF16), tri_ref[...]) + carry_ref[...]
    rank = jnp.sum(jnp.where(onehot, before, 0.0), axis=0, keepdims=True)
    carry_ref[...] += jnp.sum(jnp.where(onehot, 1.0, 0.0), axis=1, keepdims=True)
    cnt_ref[...] = carry_ref[...]
    cls_ref[...] = cls
    rank_ref[...] = rank.astype(jnp.int32)

    words = pltpu.pack_elementwise([h[:, 0:H_WORDS], h[:, H_WORDS:D_MODEL]], packed_dtype=BF16)
    for s in range(H_SUBLANES):
        pay_ref[pl.ds(s, tm, stride=TILE_ROWS), :] = words[:, s * TILE_LANES:(s + 1) * TILE_LANES]
    sub8 = lax.broadcasted_iota(jnp.int32, (TILE_ROWS, tm), 0)
    rec = jnp.where(sub8 == META_CLASS, cls.astype(F32),
                    jnp.where(sub8 == META_RANK, rank,
                              jnp.where(sub8 == META_WLO, w_lo, jnp.where(sub8 == META_WHI, w_hi, 0.0))))
    rec = jnp.concatenate([rec, jnp.zeros((META_W - TILE_ROWS, tm), F32)], axis=0)
    for b in range(tm // META_W):
        meta = rec[:, b * META_W:(b + 1) * META_W].T
        pay_ref[pl.ds(b * META_W * TILE_ROWS + H_SUBLANES, META_W, stride=TILE_ROWS), :] = (
            lax.bitcast_convert_type(meta, jnp.int32))
    for s in range(H_SUBLANES + 1, TILE_ROWS):
        pay_ref[pl.ds(s, tm, stride=TILE_ROWS), :] = jnp.zeros((tm, TILE_LANES), jnp.int32)


def _moe_route(x, nw, mod5, layer, w_route, b_route):
    tm = ROUTE_TM
    tiles_per_batch = SEQ // tm
    tri = (jnp.arange(tm, dtype=jnp.int32)[:, None] < jnp.arange(tm, dtype=jnp.int32)[None, :]).astype(BF16)
    return pl.pallas_call(
        _route_kernel,
        grid=(TOKENS // tm,),
        in_specs=[
            pl.BlockSpec((tm, D_MODEL), lambda i: (i, 0)),
            pl.BlockSpec((1, D_MODEL), lambda i: (0, 0)),
            _mod_spec(layer, 4, tiles_per_batch),
            _mod_spec(layer, 3, tiles_per_batch),
            pl.BlockSpec((ROUTE_PAD, D_MODEL), lambda i: (0, 0)),
            pl.BlockSpec((ROUTE_PAD, 1), lambda i: (0, 0)),
            pl.BlockSpec((tm, tm), lambda i: (0, 0)),
        ],
        out_specs=[pl.BlockSpec((tm * TILE_ROWS, TILE_LANES), lambda i: (i, 0)),
                   pl.BlockSpec((1, tm), lambda i: (0, i)),
                   pl.BlockSpec((1, tm), lambda i: (0, i)),
                   pl.BlockSpec((ROUTE_PAD, 1), lambda i: (0, 0))],
        out_shape=[jax.ShapeDtypeStruct((TOKENS * TILE_ROWS, TILE_LANES), jnp.int32),
                   jax.ShapeDtypeStruct((1, TOKENS), jnp.int32),
                   jax.ShapeDtypeStruct((1, TOKENS), jnp.int32),
                   jax.ShapeDtypeStruct((ROUTE_PAD, 1), F32)],
        scratch_shapes=[pltpu.VMEM((ROUTE_PAD, 1), F32)],
        compiler_params=pltpu.CompilerParams(
            dimension_semantics=("arbitrary",), vmem_limit_bytes=VMEM_LIMIT),
        name="moe_route",
    )(x, nw, mod5, mod5, w_route.T, b_route.reshape(ROUTE_PAD, 1), tri)


def _permute_kernel(pos_ref, src_ref, dst_hbm, stage_ref, sem):
    i = pl.program_id(0)
    slot = i % 2
    rows = PERMUTE_TM * TILE_ROWS

    def slot_wait(s):
        whole = stage_ref.at[pl.ds(s * rows, rows), :]
        pltpu.make_async_copy(whole, whole, sem.at[s]).wait()

    @pl.when(i >= 2)
    def _():
        slot_wait(slot)

    base = pl.multiple_of(slot * rows, rows)
    stage_ref[pl.ds(base, rows), :] = src_ref[...]

    def issue(r, _):
        pltpu.make_async_copy(stage_ref.at[pl.ds(base + r * TILE_ROWS, TILE_ROWS), :],
                              dst_hbm.at[pos_ref[i * PERMUTE_TM + r]], sem.at[slot]).start()
        return 0

    lax.fori_loop(0, PERMUTE_TM, issue, 0, unroll=16)

    @pl.when(i == pl.num_programs(0) - 1)
    def _():
        slot_wait(1 - slot)
        slot_wait(slot)


def _moe_permute(pos, payload):
    rows = PERMUTE_TM * TILE_ROWS
    return pl.pallas_call(
        _permute_kernel,
        grid_spec=pltpu.PrefetchScalarGridSpec(
            num_scalar_prefetch=1, grid=(TOKENS // PERMUTE_TM,),
            in_specs=[pl.BlockSpec((rows, TILE_LANES), lambda i, pos: (i, 0))],
            out_specs=pl.BlockSpec(memory_space=pl.ANY),
            scratch_shapes=[pltpu.VMEM((2 * rows, TILE_LANES), jnp.int32), pltpu.SemaphoreType.DMA((2,))]),
        out_shape=jax.ShapeDtypeStruct((TOKENS, TILE_ROWS, TILE_LANES), jnp.int32),
        compiler_params=pltpu.CompilerParams(
            dimension_semantics=("arbitrary",), vmem_limit_bytes=VMEM_LIMIT),
        name="moe_permute",
    )(pos, payload)


def _experts_kernel(blk_ref, cls_ref, elo_ref, ehi_ref, first_ref, last_ref, valid_ref,
                    pay_ref, w1lo_ref, w3lo_ref, w2lo_ref, w1hi_ref, w3hi_ref, w2hi_ref, o_ref, acc_ref):
    k = pl.program_id(0)
    rows = SORT_BLOCK

    @pl.when(valid_ref[k] == 1)
    def _():
        def sublane(s):
            return pay_ref[pl.ds(s, rows, stride=TILE_ROWS), :]

        halves = [[pltpu.unpack_elementwise(sublane(s), index=i, packed_dtype=BF16, unpacked_dtype=F32)
                   for s in range(H_SUBLANES)] for i in range(2)]
        hb = jnp.concatenate(halves[0] + halves[1], axis=1).astype(BF16)
        meta = lax.bitcast_convert_type(sublane(H_SUBLANES), F32)
        mine = meta[:, META_CLASS:META_CLASS + 1] == cls_ref[k].astype(F32)
        w_lo = jnp.where(mine, meta[:, META_WLO:META_WLO + 1], 0.0)
        w_hi = jnp.where(mine, meta[:, META_WHI:META_WHI + 1], 0.0)
        bf = lambda w_ref: w_ref[...].astype(BF16)
        hid_lo = _silu(_dot(hb, bf(w1lo_ref))) * _dot(hb, bf(w3lo_ref)) * w_lo
        hid_hi = _silu(_dot(hb, bf(w1hi_ref))) * _dot(hb, bf(w3hi_ref)) * w_hi
        y = _dot(hid_lo.astype(BF16), bf(w2lo_ref)) + _dot(hid_hi.astype(BF16), bf(w2hi_ref))

        @pl.when(first_ref[k] == 1)
        def _():
            acc_ref[...] = y

        @pl.when(first_ref[k] == 0)
        def _():
            acc_ref[...] += y

        @pl.when(last_ref[k] == 1)
        def _():
            for s in range(TILE_ROWS):
                o_ref[pl.ds(s, rows, stride=TILE_ROWS), :] = acc_ref[:, s * TILE_LANES:(s + 1) * TILE_LANES]


def _moe_experts(items, payload_sorted, layer, w1, w3, w2):
    blk, cls, elo, ehi, first, last, valid = items
    tiles = pl.BlockSpec((SORT_BLOCK * TILE_ROWS, TILE_LANES), lambda k, blk, *_: (blk[k], 0))
    w_in = lambda which: pl.BlockSpec(
        (None, None, D_MODEL, MOE_HIDDEN), lambda k, blk, cls, elo, ehi, *_: (layer, (elo, ehi)[which][k], 0, 0))
    w_out = lambda which: pl.BlockSpec(
        (None, None, MOE_HIDDEN, D_MODEL), lambda k, blk, cls, elo, ehi, *_: (layer, (elo, ehi)[which][k], 0, 0))
    return pl.pallas_call(
        _experts_kernel,
        grid_spec=pltpu.PrefetchScalarGridSpec(
            num_scalar_prefetch=7, grid=(MAX_ITEMS,),
            in_specs=[tiles, w_in(0), w_in(0), w_out(0), w_in(1), w_in(1), w_out(1)],
            out_specs=tiles,
            scratch_shapes=[pltpu.VMEM((SORT_BLOCK, D_MODEL), F32)]),
        out_shape=jax.ShapeDtypeStruct((TOKENS * TILE_ROWS, TILE_LANES), F32),
        compiler_params=pltpu.CompilerParams(
            dimension_semantics=("arbitrary",), vmem_limit_bytes=VMEM_LIMIT),
        name="moe_experts",
    )(blk, cls, elo, ehi, first, last, valid, payload_sorted, w1, w3, w2, w1, w3, w2)


def _combine_kernel(pos_ref, x_ref, gate_ref, y_hbm, o_ref, buf_ref, sem):
    tm = COMBINE_TM
    i = pl.program_id(0)
    n = pl.num_programs(0)

    def gather(tile, slot):
        def issue(r, _):
            pltpu.make_async_copy(y_hbm.at[pos_ref[tile * tm + r]],
                                  buf_ref.at[pl.ds((slot * tm + r) * TILE_ROWS, TILE_ROWS), :], sem.at[slot]).start()
            return 0
        lax.fori_loop(0, tm, issue, 0, unroll=16)

    @pl.when(i == 0)
    def _():
        gather(0, 0)

    @pl.when(i + 1 < n)
    def _():
        gather(i + 1, (i + 1) % 2)

    slot = i % 2
    base = slot * tm * TILE_ROWS
    whole_slot = buf_ref.at[pl.ds(base, tm * TILE_ROWS), :]
    pltpu.make_async_copy(whole_slot, whole_slot, sem.at[slot]).wait()
    y = jnp.concatenate([buf_ref[pl.ds(base + s, tm, stride=TILE_ROWS), :] for s in range(TILE_ROWS)], axis=1)
    o_ref[...] = x_ref[...] + gate_ref[...] * y


def _moe_combine(pos, x, mod5, layer, y_sorted):
    tm = COMBINE_TM
    tiles_per_batch = SEQ // tm
    return pl.pallas_call(
        _combine_kernel,
        grid_spec=pltpu.PrefetchScalarGridSpec(
            num_scalar_prefetch=1, grid=(TOKENS // tm,),
            in_specs=[
                pl.BlockSpec((tm, D_MODEL), lambda i, pos: (i, 0)),
                pl.BlockSpec((None, None, None, 1, D_MODEL),
                             lambda i, pos: (layer, i // tiles_per_batch, 5, 0, 0)),
                pl.BlockSpec(memory_space=pl.ANY),
            ],
            out_specs=pl.BlockSpec((tm, D_MODEL), lambda i, pos: (i, 0)),
            scratch_shapes=[pltpu.VMEM((2 * tm * TILE_ROWS, TILE_LANES), F32), pltpu.SemaphoreType.DMA((2,))]),
        out_shape=jax.ShapeDtypeStruct((TOKENS, D_MODEL), F32),
        compiler_params=pltpu.CompilerParams(
            dimension_semantics=("arbitrary",), vmem_limit_bytes=VMEM_LIMIT),
        name="moe_combine",
    )(pos, x, mod5, y_sorted)


def _moe_plan(cls, rank, counts):
    count = counts[:MOE_CLASSES, 0].astype(jnp.int32)
    ends = jnp.cumsum(count)
    starts = ends - count
    class_ids = jnp.arange(MOE_CLASSES, dtype=jnp.int32)

    def lookup(table, idx):
        return jnp.sum(jnp.where(idx[..., None] == class_ids, table, 0), axis=-1)

    pos = (lookup(starts, cls) + rank).reshape(TOKENS)

    first_blk = starts // SORT_BLOCK
    n_items = jnp.where(count > 0, (ends - 1) // SORT_BLOCK - first_blk + 1, 0)
    item_end = jnp.cumsum(n_items)
    item_start = item_end - n_items
    k = jnp.arange(MAX_ITEMS, dtype=jnp.int32)
    valid = k < item_end[-1]
    kc = jnp.minimum(k, item_end[-1] - 1)
    icls = jnp.sum((item_end[None, :] <= kc[:, None]).astype(jnp.int32), axis=1)
    blk = lookup(first_blk, icls) + (kc - lookup(item_start, icls))
    first = jnp.concatenate([jnp.ones((1,), jnp.int32), (blk[1:] != blk[:-1]).astype(jnp.int32)])
    last = jnp.concatenate([(blk[1:] != blk[:-1]) | ~valid[1:], jnp.ones((1,), bool)]).astype(jnp.int32)
    group = icls // MOE_PAIRS
    elo = group * MOE_EPG + lookup(jnp.asarray(PAIR_LO * MOE_GROUPS, jnp.int32), icls)
    ehi = group * MOE_EPG + lookup(jnp.asarray(PAIR_HI * MOE_GROUPS, jnp.int32), icls)
    return pos, (blk, icls, elo, ehi, first * valid, last * valid, valid.astype(jnp.int32))


def _moe(x, nw, mod5, layer, w_route, b_route, w1, w3, w2):
    payload, cls, rank, counts = _moe_route(x, nw, mod5, layer, w_route, b_route)
    pos, items = _moe_plan(cls, rank, counts)
    tiles = (TOKENS, TILE_ROWS, TILE_LANES)
    sorted_payload = _moe_permute(pos, payload).reshape(TOKENS * TILE_ROWS, TILE_LANES)
    y_sorted = _moe_experts(items, sorted_payload, layer, w1, w3, w2)
    return _moe_combine(pos, x, mod5, layer, y_sorted.reshape(tiles))


def _pad_cols(w, width):
    return jnp.pad(w, ((0, 0), (0, width - w.shape[1])))


def kernel(x, c, ada_w, ada_b, norm_mix, norm_ffn, ssd_w_in, ssd_conv_w, ssd_conv_b, ssd_dt_bias,
           ssd_a_log, ssd_d, ssd_norm, ssd_w_out, dsa_w_in, dsa_q_norm, dsa_k_norm, dsa_w_out,
           moe_w_group, moe_b_group, moe_w_expert, moe_b_expert, moe_w1, moe_w3, moe_w2):
    depth = ada_w.shape[0]
    xt = x.reshape(TOKENS, D_MODEL)
    mod = _modulation(c, ada_w, ada_b)
    mod5 = mod.reshape(depth, BATCH, 6, 1, D_MODEL)

    head_of_col = jnp.arange(SSD_D_INNER, dtype=jnp.int32) // SSD_HEAD_DIM
    expand = (jnp.arange(SSD_DT_PAD, dtype=jnp.int32)[:, None] == head_of_col[None, :]).astype(BF16)

    for i in range(depth):
        j = i // 2
        nw_mix = norm_mix[i].reshape(1, D_MODEL)
        if i % 2 == 0:
            w_in = _pad_cols(ssd_w_in[j], SSD_PROJ_PAD).astype(BF16)
            z, xbc, dt = _inproj(
                xt, nw_mix, mod5, i, 1, 0, w_in,
                ((0, SSD_D_INNER), (SSD_D_INNER, SSD_D_INNER + SSD_CONV_DIM),
                 (SSD_D_INNER + SSD_CONV_DIM, SSD_PROJ_PAD)))
            xt = _ssd_mixer(
                z, xbc, dt, xt, ssd_conv_w[j], ssd_conv_b[j].reshape(1, SSD_CONV_DIM),
                _pad_cols(ssd_dt_bias[j].reshape(1, SSD_HEADS), SSD_DT_PAD),
                _pad_cols(ssd_a_log[j].reshape(1, SSD_HEADS), SSD_DT_PAD),
                jnp.repeat(ssd_d[j], SSD_HEAD_DIM).reshape(1, SSD_D_INNER),
                ssd_norm[j].reshape(1, SSD_D_INNER), expand, ssd_w_out[j].astype(BF16), mod5, i)
        else:
            w = dsa_w_in[j]
            w_in = jnp.concatenate(
                [_pad_cols(w[:, :DSA_KI_START + IDX_HEAD_DIM], DSA_WI_START),
                 _pad_cols(w[:, DSA_KI_START + IDX_HEAD_DIM:], 128)], axis=1).astype(BF16)
            q, k, v, qi, ki, wi = _dsa_inproj(
                xt, nw_mix, mod5, i, w_in, dsa_q_norm[j].reshape(1, ATT_HEAD_DIM),
                dsa_k_norm[j].reshape(1, ATT_HEAD_DIM))
            xt = _dsa_mixer(q, k, v, qi, ki, wi, xt, dsa_w_out[j].astype(BF16), mod5, i)

        w_route = _pad_cols(jnp.concatenate([moe_w_expert[i], moe_w_group[i]], axis=1), ROUTE_PAD).astype(BF16)
        b_route = _pad_cols(jnp.concatenate([moe_b_expert[i], moe_b_group[i]]).reshape(1, -1), ROUTE_PAD)
        xt = _moe(xt, norm_ffn[i].reshape(1, D_MODEL), mod5, i, w_route, b_route,
                  moe_w1, moe_w3, moe_w2)
    return xt.reshape(BATCH, SEQ, D_MODEL)
```

```python
import functools

import jax
import jax.numpy as jnp
from jax import lax
from jax.experimental import pallas as pl
from jax.experimental.pallas import tpu as pltpu

F32 = jnp.float32
BF16 = jnp.bfloat16

D_MODEL = 1024
BATCH = 8
SEQ = 2048
TOKENS = BATCH * SEQ
EPS = 1e-6

SSD_D_INNER = 2048
SSD_HEAD_DIM = 64
SSD_HEADS = 32
SSD_GROUPS = 8
SSD_HEADS_PER_GROUP = 4
SSD_STATE = 128
SSD_CONV = 4
SSD_CHUNK = 128
SSD_GN = SSD_GROUPS * SSD_STATE
SSD_CONV_DIM = SSD_D_INNER + 2 * SSD_GN
SSD_GROUP_W = SSD_HEADS_PER_GROUP * SSD_HEAD_DIM
SSD_DT_PAD = 128
SSD_PROJ_PAD = SSD_D_INNER + SSD_CONV_DIM + SSD_DT_PAD
CONV_HALO = 8

ATT_HEADS = 16
ATT_KV_HEADS = 4
ATT_Q_PER_KV = 4
ATT_HEAD_DIM = 64
IDX_HEADS = 8
IDX_HEAD_DIM = 64
TOPK = 256
Q_BLOCK = 256
DSA_Q = ATT_HEADS * ATT_HEAD_DIM
DSA_KV = ATT_KV_HEADS * ATT_HEAD_DIM
DSA_QI = IDX_HEADS * IDX_HEAD_DIM
DSA_KI_START = DSA_Q + 2 * DSA_KV + DSA_QI
DSA_WI_START = DSA_KI_START + 128
DSA_PROJ_PAD = DSA_WI_START + 128

MOE_GROUPS = 4
MOE_EPG = 4
MOE_EXPERTS = 16
MOE_HIDDEN = 256
ROUTE_PAD = 128

VMEM_LIMIT = 56 * 1024 * 1024


def _sigmoid(v):
    return 1.0 / (1.0 + jnp.exp(-v))


def _silu(v):
    return v * _sigmoid(v)


def _split3(a):
    hi = a.astype(BF16)
    r = a - hi.astype(F32)
    mid = r.astype(BF16)
    lo = (r - mid.astype(F32)).astype(BF16)
    return hi, mid, lo


def _dot(a, b):
    return jnp.dot(a, b, preferred_element_type=F32)


def _dot_nt(a, b):
    return lax.dot_general(a, b, (((1,), (1,)), ((), ())), preferred_element_type=F32)


def _dot3_exact_rhs(a, m):
    hi, mid, lo = _split3(a)
    return _dot(hi, m) + _dot(mid, m) + _dot(lo, m)


def _dot3_exact_lhs(m, a):
    hi, mid, lo = _split3(a)
    return _dot(m, hi) + _dot(m, mid) + _dot(m, lo)


def _norm_mod(x, nw, scale, shift):
    ms = jnp.mean(x * x, axis=-1, keepdims=True)
    return x * lax.rsqrt(ms + EPS) * nw * (1.0 + scale) + shift


MOD_TN = 1536


def _mod_kernel(c_ref, w_ref, b_ref, o_ref):
    cond = _silu(c_ref[...]).astype(BF16)
    o_ref[...] = _dot(cond, w_ref[...].astype(BF16)) + b_ref[...]


def _modulation(c, ada_w, ada_b):
    depth = ada_w.shape[0]
    n = ada_w.shape[2]
    return pl.pallas_call(
        _mod_kernel,
        grid=(depth, n // MOD_TN),
        in_specs=[
            pl.BlockSpec((BATCH, D_MODEL), lambda i, j: (0, 0)),
            pl.BlockSpec((None, D_MODEL, MOD_TN), lambda i, j: (i, 0, j)),
            pl.BlockSpec((None, 1, MOD_TN), lambda i, j: (i, 0, j)),
        ],
        out_specs=pl.BlockSpec((None, BATCH, MOD_TN), lambda i, j: (i, 0, j)),
        out_shape=jax.ShapeDtypeStruct((depth, BATCH, n), F32),
        compiler_params=pltpu.CompilerParams(
            dimension_semantics=("arbitrary", "arbitrary"), vmem_limit_bytes=VMEM_LIMIT),
        name="adaln_mod",
    )(c, ada_w, ada_b.reshape(depth, 1, n))


def _mod_spec(layer, chunk, rows_per_batch_tile):
    return pl.BlockSpec((None, None, None, 1, D_MODEL),
                        lambda i, *_: (layer, i // rows_per_batch_tile, chunk, 0, 0))


INPROJ_TM = 256


def _inproj_kernel(x_ref, nw_ref, scale_ref, shift_ref, w_ref, *o_refs, col_slices):
    h = _norm_mod(x_ref[...], nw_ref[...], scale_ref[...], shift_ref[...]).astype(BF16)
    for n, (o_ref, (lo, hi)) in enumerate(zip(o_refs, col_slices)):
        r = _dot(h, w_ref[:, lo:hi])
        o_ref[...] = _silu(r) if n == 0 else r


def _inproj(x, nw, mod5, layer, scale_chunk, shift_chunk, w, col_slices):
    tm = INPROJ_TM
    tiles_per_batch = SEQ // tm
    n_pad = w.shape[1]
    return pl.pallas_call(
        functools.partial(_inproj_kernel, col_slices=col_slices),
        grid=(TOKENS // tm,),
        in_specs=[
            pl.BlockSpec((tm, D_MODEL), lambda i: (i, 0)),
            pl.BlockSpec((1, D_MODEL), lambda i: (0, 0)),
            _mod_spec(layer, scale_chunk, tiles_per_batch),
            _mod_spec(layer, shift_chunk, tiles_per_batch),
            pl.BlockSpec((D_MODEL, n_pad), lambda i: (0, 0)),
        ],
        out_specs=[pl.BlockSpec((tm, hi - lo), lambda i: (i, 0)) for lo, hi in col_slices],
        out_shape=[jax.ShapeDtypeStruct((TOKENS, hi - lo), F32) for lo, hi in col_slices],
        compiler_params=pltpu.CompilerParams(
            dimension_semantics=("arbitrary",), vmem_limit_bytes=VMEM_LIMIT),
        name="norm_inproj",
    )(x, nw, mod5, mod5, w)


CONV_COLS = 512


def _ssd_kernel(z_ref, xbc_ref, dt_ref, xres_ref, cw_ref, cb_ref, dtb_ref, alog_ref, de_ref, nw_ref,
                e_ref, wout_ref, gate_ref, o_ref, state_ref, ext_ref, act_ref, yn_ref):
    q = SSD_CHUNK
    c = pl.program_id(1)

    @pl.when(c == 0)
    def _():
        state_ref[...] = jnp.zeros_like(state_ref)
        ext_ref[0:CONV_HALO, :] = jnp.zeros((CONV_HALO, SSD_CONV_DIM), F32)

    @pl.when(c > 0)
    def _():
        ext_ref[0:CONV_HALO, :] = ext_ref[q:q + CONV_HALO, :]

    ext_ref[CONV_HALO:CONV_HALO + q, :] = xbc_ref[...]

    for s in range(SSD_CONV_DIM // CONV_COLS):
        cs = slice(s * CONV_COLS, (s + 1) * CONV_COLS)
        u = ext_ref[:, cs]
        acc = cw_ref[0:1, cs] * u
        for k in range(1, SSD_CONV):
            acc = pltpu.roll(acc, 1, axis=0) + cw_ref[k:k + 1, cs] * u
        act_ref[:, cs] = _silu(acc[CONV_HALO:CONV_HALO + q, :] + cb_ref[:, cs])

    dt_raw = dt_ref[...] + dtb_ref[...]
    dt = jnp.maximum(dt_raw, 0.0) + jnp.log1p(jnp.exp(-jnp.abs(dt_raw)))
    a = dt * (-jnp.exp(alog_ref[...]) * LOG2E)
    row = lax.broadcasted_iota(jnp.int32, (q, q), 0)
    col = lax.broadcasted_iota(jnp.int32, (q, q), 1)
    tril = row >= col
    acs = _dot3_exact_lhs(tril.astype(BF16), a)
    acs_t = acs.T
    expand = e_ref[...]
    acs_e = _dot3_exact_rhs(acs, expand)
    dt_e = _dot3_exact_rhs(dt, expand)
    tot_e = acs_e[q - 1:q, :]
    decay_from_start = jnp.exp2(acs_e)
    decay_to_end = jnp.exp2(tot_e - acs_e)
    chunk_decay = jnp.exp2(tot_e)

    lane_head = lax.broadcasted_iota(jnp.int32, (q, SSD_GROUP_W), 1) // SSD_HEAD_DIM
    for g in range(SSD_GROUPS):
        xs = act_ref[:, g * SSD_GROUP_W:(g + 1) * SSD_GROUP_W]
        gs = slice(g * SSD_GROUP_W, (g + 1) * SSD_GROUP_W)
        bm = act_ref[:, SSD_D_INNER + g * SSD_STATE:SSD_D_INNER + (g + 1) * SSD_STATE]
        cm = act_ref[:, SSD_D_INNER + SSD_GN + g * SSD_STATE:SSD_D_INNER + SSD_GN + (g + 1) * SSD_STATE]
        bm_t = bm.T.astype(BF16)
        cm_b = cm.astype(BF16)
        cb = _dot(cm_b, bm_t)
        xd = xs * dt_e[:, gs]
        ms = []
        xds = []
        for j in range(SSD_HEADS_PER_GROUP):
            h = g * SSD_HEADS_PER_GROUP + j
            seg = acs[:, h:h + 1] - acs_t[h:h + 1, :]
            dec = jnp.exp2(jnp.where(tril, seg, -jnp.inf))
            ms.append((cb * dec).astype(BF16))
            xds.append(jnp.where(lane_head == j, xd, 0.0).astype(BF16))
        y_diag = _dot(jnp.concatenate(ms, axis=1), jnp.concatenate(xds, axis=0))
        prev = state_ref[g]
        y_off = _dot(cm_b, prev.astype(BF16)) * decay_from_start[:, gs]
        state_ref[g] = prev * chunk_decay[:, gs] + _dot(bm_t, (xd * decay_to_end[:, gs]).astype(BF16))
        y = y_diag + y_off + xs * de_ref[:, gs]
        y = y * z_ref[:, gs]
        y = y * lax.rsqrt(jnp.mean(y * y, axis=-1, keepdims=True) + EPS) * nw_ref[:, gs]
        yn_ref[:, gs] = y.astype(BF16)

    out = _dot(yn_ref[...], wout_ref[...])
    o_ref[...] = xres_ref[...] + gate_ref[...] * out


def _ssd_mixer(z, xbc, dt, x, conv_w, conv_b, dt_bias, a_log, d_e, norm_w, expand, w_out, mod5, layer):
    q = SSD_CHUNK
    nc = SEQ // q
    tok = lambda w: pl.BlockSpec((q, w), lambda b, c: (b * nc + c, 0))
    full = lambda r, w: pl.BlockSpec((r, w), lambda b, c: (0, 0))
    return pl.pallas_call(
        _ssd_kernel,
        grid=(BATCH, nc),
        in_specs=[
            tok(SSD_D_INNER), tok(SSD_CONV_DIM), tok(SSD_DT_PAD), tok(D_MODEL),
            full(SSD_CONV, SSD_CONV_DIM), full(1, SSD_CONV_DIM), full(1, SSD_DT_PAD), full(1, SSD_DT_PAD),
            full(1, SSD_D_INNER), full(1, SSD_D_INNER), full(SSD_DT_PAD, SSD_D_INNER),
            full(SSD_D_INNER, D_MODEL),
            pl.BlockSpec((None, None, None, 1, D_MODEL), lambda b, c: (layer, b, 2, 0, 0)),
        ],
        out_specs=tok(D_MODEL),
        out_shape=jax.ShapeDtypeStruct((TOKENS, D_MODEL), F32),
        scratch_shapes=[
            pltpu.VMEM((SSD_GROUPS, SSD_STATE, SSD_GROUP_W), F32),
            pltpu.VMEM((q + CONV_HALO, SSD_CONV_DIM), F32),
            pltpu.VMEM((q, SSD_CONV_DIM), F32),
            pltpu.VMEM((q, SSD_D_INNER), BF16),
        ],
        compiler_params=pltpu.CompilerParams(
            dimension_semantics=("arbitrary", "arbitrary"), vmem_limit_bytes=VMEM_LIMIT),
        name="ssd_mixer",
    )(z, xbc, dt, x, conv_w, conv_b, dt_bias, a_log, d_e, norm_w, expand, w_out, mod5)


DSA_KEY_TILE = 256
DSA_CLASSES = 8
V_EXT = 2 * ATT_HEAD_DIM
DSA_BLOCKS_PER_CLASS = (SEQ // Q_BLOCK) // DSA_CLASSES
N_BISECT = 12
F32_MIN = float(jnp.finfo(jnp.float32).min)
LOG2E = 1.4426950408889634


def _count(mask):
    return jnp.sum(jnp.where(mask, 1.0, 0.0), axis=-1, keepdims=True)


def _select_topk(score_ref, q_pos, n_keys):
    kf = float(TOPK)
    small = (q_pos + 1) <= TOPK
    sc = score_ref[...]
    hi0 = jnp.max(sc, axis=-1, keepdims=True)
    lo0 = jnp.min(jnp.where(sc == -jnp.inf, jnp.inf, sc), axis=-1, keepdims=True)

    def bisect(_, carry):
        lo, hi = carry
        mid = lo + 0.5 * (hi - lo)
        ok = _count(score_ref[...] >= mid) >= kf
        return jnp.where(ok, mid, lo), jnp.where(ok, hi, mid)

    _, hi = lax.fori_loop(0, N_BISECT, bisect, (lo0, hi0))

    v0 = jnp.max(jnp.where(sc <= hi, sc, -jnp.inf), axis=-1, keepdims=True)
    c0 = _count(sc >= v0)
    pend0 = jnp.where((c0 >= kf) | small, 0.0, 1.0)

    def walk_cond(carry):
        return (carry[3] > 0.0) & (carry[4] < n_keys)

    def walk(carry):
        v, c, pend, _, it = carry
        s = score_ref[...]
        v2 = jnp.max(jnp.where(s < v, s, -jnp.inf), axis=-1, keepdims=True)
        c2 = _count(s >= v2)
        v = jnp.where(pend > 0.0, v2, v)
        c = jnp.where(pend > 0.0, c2, c)
        pend = jnp.where(c2 >= kf, 0.0, pend)
        return v, c, pend, jnp.max(pend), it + 1

    v, c, _, _, _ = lax.while_loop(walk_cond, walk, (v0, c0, pend0, jnp.max(pend0), jnp.int32(0)))
    thr = jnp.where(small, F32_MIN, v)
    any_tie = jnp.max(jnp.where(small, 0.0, c - kf)) > 0.0

    @pl.when(jnp.logical_not(any_tie))
    def _():
        score_ref[...] = jnp.where(score_ref[...] >= thr, 0.0, -jnp.inf)

    @pl.when(any_tie)
    def _():
        s = score_ref[...]
        key_pos = lax.broadcasted_iota(jnp.int32, (Q_BLOCK, n_keys), 1)
        gt = s > thr
        eq = s == thr
        need = kf - _count(gt)

        def body(_, carry):
            lo, hi = carry
            mid = (lo + hi) >> 1
            ok = _count((score_ref[...] == thr) & (key_pos <= mid)) >= need
            return jnp.where(ok, lo, mid), jnp.where(ok, mid, hi)

        init = (jnp.full((Q_BLOCK, 1), -1, jnp.int32), jnp.full((Q_BLOCK, 1), n_keys - 1, jnp.int32))
        cut = lax.fori_loop(0, (n_keys - 1).bit_length() + 1, body, init)[1]
        score_ref[...] = jnp.where(gt | (eq & (key_pos <= cut)), 0.0, -jnp.inf)


def _dsa_inproj_kernel(x_ref, nw_ref, scale_ref, shift_ref, w_ref, qn_ref, kn_ref, seg_ref, segt_ref,
                       q_ref, k_ref, v_ref, qi_ref, ki_ref, wi_ref):
    hd = ATT_HEAD_DIM
    h = _norm_mod(x_ref[...], nw_ref[...], scale_ref[...], shift_ref[...]).astype(BF16)

    def head_norm(t, w):
        width = t.shape[1]
        ss = _dot((t * t).astype(BF16), seg_ref[0:width, :])
        r = lax.rsqrt(ss * (1.0 / hd) + EPS)
        r_hi = r.astype(BF16)
        r_lo = (r - r_hi.astype(F32)).astype(BF16)
        return t * (_dot(r_hi, segt_ref[:, 0:width]) + _dot(r_lo, segt_ref[:, 0:width])) * w

    q = head_norm(_dot(h, w_ref[:, 0:DSA_Q]), qn_ref[...] * (hd ** -0.5 * LOG2E))
    for n in range(ATT_HEADS):
        q_ref[n] = q[:, n * hd:(n + 1) * hd].astype(BF16)
    kv = _dot(h, w_ref[:, DSA_Q:DSA_Q + 2 * DSA_KV])
    k = head_norm(kv[:, 0:DSA_KV], kn_ref[...])
    for n in range(ATT_KV_HEADS):
        k_ref[n] = k[:, n * hd:(n + 1) * hd].astype(BF16)
        v_ref[n] = jnp.concatenate([kv[:, DSA_KV + n * hd:DSA_KV + (n + 1) * hd],
                                    jnp.ones((kv.shape[0], V_EXT - hd), F32)], axis=1).astype(BF16)
    qi = _dot(h, w_ref[:, DSA_Q + 2 * DSA_KV:DSA_KI_START])
    for n in range(IDX_HEADS):
        qi_ref[n] = qi[:, n * IDX_HEAD_DIM:(n + 1) * IDX_HEAD_DIM].astype(BF16)
    ki_ref[...] = _dot(h, w_ref[:, DSA_KI_START:DSA_KI_START + IDX_HEAD_DIM]).astype(BF16)
    wi_ref[...] = _dot(h, w_ref[:, DSA_WI_START:DSA_WI_START + IDX_HEADS]) * ((IDX_HEADS * IDX_HEAD_DIM) ** -0.5)


def _dsa_inproj(x, nw, mod5, layer, w, q_norm, k_norm):
    tm = INPROJ_TM
    tiles_per_batch = SEQ // tm
    heads = lambda n: pl.BlockSpec((n, tm, ATT_HEAD_DIM), lambda i: (0, i, 0))
    head_of = jnp.arange(DSA_Q, dtype=jnp.int32) // ATT_HEAD_DIM
    seg = (head_of[:, None] == jnp.arange(128, dtype=jnp.int32)[None, :]).astype(BF16)
    q_norm = jnp.tile(q_norm, (1, ATT_HEADS))
    k_norm = jnp.tile(k_norm, (1, ATT_KV_HEADS))
    return pl.pallas_call(
        _dsa_inproj_kernel,
        grid=(TOKENS // tm,),
        in_specs=[
            pl.BlockSpec((tm, D_MODEL), lambda i: (i, 0)),
            pl.BlockSpec((1, D_MODEL), lambda i: (0, 0)),
            _mod_spec(layer, 1, tiles_per_batch),
            _mod_spec(layer, 0, tiles_per_batch),
            pl.BlockSpec((D_MODEL, DSA_PROJ_PAD), lambda i: (0, 0)),
            pl.BlockSpec((1, DSA_Q), lambda i: (0, 0)),
            pl.BlockSpec((1, DSA_KV), lambda i: (0, 0)),
            pl.BlockSpec((DSA_Q, 128), lambda i: (0, 0)),
            pl.BlockSpec((128, DSA_Q), lambda i: (0, 0)),
        ],
        out_specs=[heads(ATT_HEADS), heads(ATT_KV_HEADS),
                   pl.BlockSpec((ATT_KV_HEADS, tm, V_EXT), lambda i: (0, i, 0)), heads(IDX_HEADS),
                   pl.BlockSpec((tm, IDX_HEAD_DIM), lambda i: (i, 0)),
                   pl.BlockSpec((tm, IDX_HEADS), lambda i: (i, 0))],
        out_shape=[jax.ShapeDtypeStruct((ATT_HEADS, TOKENS, ATT_HEAD_DIM), BF16),
                   jax.ShapeDtypeStruct((ATT_KV_HEADS, TOKENS, ATT_HEAD_DIM), BF16),
                   jax.ShapeDtypeStruct((ATT_KV_HEADS, TOKENS, V_EXT), BF16),
                   jax.ShapeDtypeStruct((IDX_HEADS, TOKENS, IDX_HEAD_DIM), BF16),
                   jax.ShapeDtypeStruct((TOKENS, IDX_HEAD_DIM), BF16),
                   jax.ShapeDtypeStruct((TOKENS, IDX_HEADS), F32)],
        compiler_params=pltpu.CompilerParams(
            dimension_semantics=("arbitrary",), vmem_limit_bytes=VMEM_LIMIT),
        name="dsa_inproj",
    )(x, nw, mod5, mod5, w, q_norm, k_norm, seg, seg.T)


def _dsa_kernel(q_ref, k_ref, v_ref, qi_ref, ki_ref, wi_ref, xres_ref, wout_ref, gate_ref, o_ref,
                score_ref, ocat_ref, *, n_keys, first_block):
    hd = ATT_HEAD_DIM
    q_pos = (first_block + pl.program_id(1)) * Q_BLOCK + lax.broadcasted_iota(jnp.int32, (Q_BLOCK, 1), 0)

    wi = wi_ref[...]
    qi = qi_ref[...].reshape(IDX_HEADS * Q_BLOCK, IDX_HEAD_DIM)
    for kt in range(n_keys // DSA_KEY_TILE):
        ks = slice(kt * DSA_KEY_TILE, (kt + 1) * DSA_KEY_TILE)
        raw = _dot_nt(qi, ki_ref[ks, :])
        acc = jnp.zeros((Q_BLOCK, DSA_KEY_TILE), F32)
        for n in range(IDX_HEADS):
            acc = acc + wi[:, n:n + 1] * jnp.maximum(raw[n * Q_BLOCK:(n + 1) * Q_BLOCK, :], 0.0)
        key_pos = kt * DSA_KEY_TILE + lax.broadcasted_iota(jnp.int32, (Q_BLOCK, DSA_KEY_TILE), 1)
        score_ref[:, ks] = jnp.where(key_pos <= q_pos, acc, -jnp.inf)

    if n_keys > TOPK:
        _select_topk(score_ref, q_pos, n_keys)
        bias = score_ref[...][None, :, :]
    else:
        bias = jnp.where(score_ref[...] == -jnp.inf, -jnp.inf, 0.0)[None, :, :]

    for n in range(ATT_KV_HEADS):
        q4 = q_ref[n * ATT_Q_PER_KV:(n + 1) * ATT_Q_PER_KV].reshape(ATT_Q_PER_KV * Q_BLOCK, hd)
        s = _dot_nt(q4, k_ref[n]).reshape(ATT_Q_PER_KV, Q_BLOCK, n_keys) + bias
        p = jnp.exp2(s - jnp.max(s, axis=-1, keepdims=True))
        o = _dot(p.reshape(ATT_Q_PER_KV * Q_BLOCK, n_keys).astype(BF16), v_ref[n])
        o = o[:, 0:hd] * (1.0 / o[:, hd:hd + 1])
        for g in range(ATT_Q_PER_KV):
            col = (n * ATT_Q_PER_KV + g) * hd
            ocat_ref[:, col:col + hd] = o[g * Q_BLOCK:(g + 1) * Q_BLOCK, :]
    out = _dot(ocat_ref[...].astype(BF16), wout_ref[...])
    o_ref[...] = xres_ref[...] + gate_ref[...] * out


def _dsa_mixer(q, k, v, qi, ki, wi, x, w_out, mod5, layer):
    nb = SEQ // Q_BLOCK
    k4 = k.reshape(ATT_KV_HEADS, BATCH, SEQ, ATT_HEAD_DIM)
    v4 = v.reshape(ATT_KV_HEADS, BATCH, SEQ, V_EXT)
    ki3 = ki.reshape(BATCH, SEQ, IDX_HEAD_DIM)
    for cls in range(DSA_CLASSES):
        n_keys = (cls + 1) * (SEQ // DSA_CLASSES)
        first_block = cls * DSA_BLOCKS_PER_CLASS
        row = lambda b, i, fb=first_block: b * nb + fb + i
        heads = lambda n: pl.BlockSpec((n, Q_BLOCK, ATT_HEAD_DIM), lambda b, i: (0, row(b, i), 0))
        keys = lambda width: pl.BlockSpec((ATT_KV_HEADS, None, n_keys, width), lambda b, i: (0, b, 0, 0))
        x = pl.pallas_call(
            functools.partial(_dsa_kernel, n_keys=n_keys, first_block=first_block),
            grid=(BATCH, DSA_BLOCKS_PER_CLASS),
            in_specs=[
                heads(ATT_HEADS), keys(ATT_HEAD_DIM), keys(V_EXT), heads(IDX_HEADS),
                pl.BlockSpec((None, n_keys, IDX_HEAD_DIM), lambda b, i: (b, 0, 0)),
                pl.BlockSpec((Q_BLOCK, IDX_HEADS), lambda b, i: (row(b, i), 0)),
                pl.BlockSpec((Q_BLOCK, D_MODEL), lambda b, i: (row(b, i), 0)),
                pl.BlockSpec((DSA_Q, D_MODEL), lambda b, i: (0, 0)),
                pl.BlockSpec((None, None, None, 1, D_MODEL), lambda b, i: (layer, b, 2, 0, 0)),
            ],
            out_specs=pl.BlockSpec((Q_BLOCK, D_MODEL), lambda b, i: (row(b, i), 0)),
            out_shape=jax.ShapeDtypeStruct((TOKENS, D_MODEL), F32),
            scratch_shapes=[
                pltpu.VMEM((Q_BLOCK, n_keys), F32),
                pltpu.VMEM((Q_BLOCK, D_MODEL), F32),
            ],
            input_output_aliases={6: 0},
            compiler_params=pltpu.CompilerParams(
                dimension_semantics=("arbitrary", "arbitrary"), vmem_limit_bytes=VMEM_LIMIT),
            name=f"dsa_mixer_c{cls}",
        )(q, k4, v4, qi, ki3, wi, x, w_out, mod5)
    return x


MOE_PAIRS = MOE_EPG * (MOE_EPG - 1) // 2
MOE_CLASSES = MOE_GROUPS * MOE_PAIRS
PAIR_LO = (0, 0, 0, 1, 1, 2)
PAIR_HI = (1, 2, 3, 2, 3, 3)
ROUTE_TM = 1024
META_W = 128
META_CLASS, META_RANK, META_WLO, META_WHI = 0, 1, 2, 3
TILE_ROWS, TILE_LANES = 8, 128
H_WORDS = D_MODEL // 2
H_SUBLANES = H_WORDS // TILE_LANES
SORT_BLOCK = 256
N_SORT_BLOCKS = TOKENS // SORT_BLOCK
MAX_ITEMS = N_SORT_BLOCKS + MOE_CLASSES
PERMUTE_TM = 512
COMBINE_TM = 512


def _route_kernel(x_ref, nw_ref, scale_ref, shift_ref, wrt_ref, brt_ref, tri_ref, pay_ref, cls_ref, rank_ref,
                  cnt_ref, carry_ref):
    tm = ROUTE_TM

    @pl.when(pl.program_id(0) == 0)
    def _():
        carry_ref[...] = jnp.zeros_like(carry_ref)

    h = _norm_mod(x_ref[...], nw_ref[...], scale_ref[...], shift_ref[...])
    logits = _dot_nt(wrt_ref[...], h.astype(BF16)) + brt_ref[...]
    sub = lax.broadcasted_iota(jnp.int32, logits.shape, 0)
    neg = -jnp.inf
    big = jnp.int32(ROUTE_PAD)
    is_group = (sub >= MOE_EXPERTS) & (sub < MOE_EXPERTS + MOE_GROUPS)
    gl = jnp.where(is_group, logits, neg)
    g_max = jnp.max(gl, axis=0, keepdims=True)
    g_idx = jnp.min(jnp.where(gl == g_max, sub - MOE_EXPERTS, big), axis=0, keepdims=True)
    g_val = 1.0 / jnp.sum(jnp.exp(gl - g_max), axis=0, keepdims=True)
    in_group = (sub < MOE_EXPERTS) & ((sub // MOE_EPG) == g_idx)
    el = jnp.where(in_group, logits, neg)
    m1 = jnp.max(el, axis=0, keepdims=True)
    i1 = jnp.min(jnp.where(el == m1, sub, big), axis=0, keepdims=True)
    el2 = jnp.where(sub == i1, neg, el)
    m2 = jnp.max(el2, axis=0, keepdims=True)
    i2 = jnp.min(jnp.where(el2 == m2, sub, big), axis=0, keepdims=True)
    r = jnp.exp(m2 - m1)
    w_top1 = g_val / (1.0 + r)
    w_top2 = g_val * r / (1.0 + r)

    lo = jnp.minimum(i1, i2) - g_idx * MOE_EPG
    hi = jnp.maximum(i1, i2) - g_idx * MOE_EPG
    pair = (lo * (2 * MOE_EPG - 1 - lo)) // 2 + (hi - lo - 1)
    cls = g_idx * MOE_PAIRS + pair
    w_lo = jnp.where(i1 < i2, w_top1, w_top2)
    w_hi = jnp.where(i1 < i2, w_top2, w_top1)

    onehot = sub == cls
    before = _dot(onehot.astype(BF16), tri_ref[...]) + carry_ref[...]
    rank = jnp.sum(jnp.where(onehot, before, 0.0), axis=0, keepdims=True)
    carry_ref[...] += jnp.sum(jnp.where(onehot, 1.0, 0.0), axis=1, keepdims=True)
    cnt_ref[...] = carry_ref[...]
    cls_ref[...] = cls
    rank_ref[...] = rank.astype(jnp.int32)

    words = pltpu.pack_elementwise([h[:, 0:H_WORDS], h[:, H_WORDS:D_MODEL]], packed_dtype=BF16)
    for s in range(H_SUBLANES):
        pay_ref[pl.ds(s, tm, stride=TILE_ROWS), :] = words[:, s * TILE_LANES:(s + 1) * TILE_LANES]
    sub8 = lax.broadcasted_iota(jnp.int32, (TILE_ROWS, tm), 0)
    rec = jnp.where(sub8 == META_CLASS, cls.astype(F32),
                    jnp.where(sub8 == META_RANK, rank,
                              jnp.where(sub8 == META_WLO, w_lo, jnp.where(sub8 == META_WHI, w_hi, 0.0))))
    rec = jnp.concatenate([rec, jnp.zeros((META_W - TILE_ROWS, tm), F32)], axis=0)
    for b in range(tm // META_W):
        meta = rec[:, b * META_W:(b + 1) * META_W].T
        pay_ref[pl.ds(b * META_W * TILE_ROWS + H_SUBLANES, META_W, stride=TILE_ROWS), :] = (
            lax.bitcast_convert_type(meta, jnp.int32))
    for s in range(H_SUBLANES + 1, TILE_ROWS):
        pay_ref[pl.ds(s, tm, stride=TILE_ROWS), :] = jnp.zeros((tm, TILE_LANES), jnp.int32)


def _moe_route(x, nw, mod5, layer, w_route, b_route):
    tm = ROUTE_TM
    tiles_per_batch = SEQ // tm
    tri = (jnp.arange(tm, dtype=jnp.int32)[:, None] < jnp.arange(tm, dtype=jnp.int32)[None, :]).astype(BF16)
    return pl.pallas_call(
        _route_kernel,
        grid=(TOKENS // tm,),
        in_specs=[
            pl.BlockSpec((tm, D_MODEL), lambda i: (i, 0)),
            pl.BlockSpec((1, D_MODEL), lambda i: (0, 0)),
            _mod_spec(layer, 4, tiles_per_batch),
            _mod_spec(layer, 3, tiles_per_batch),
            pl.BlockSpec((ROUTE_PAD, D_MODEL), lambda i: (0, 0)),
            pl.BlockSpec((ROUTE_PAD, 1), lambda i: (0, 0)),
            pl.BlockSpec((tm, tm), lambda i: (0, 0)),
        ],
        out_specs=[pl.BlockSpec((tm * TILE_ROWS, TILE_LANES), lambda i: (i, 0)),
                   pl.BlockSpec((1, tm), lambda i: (0, i)),
                   pl.BlockSpec((1, tm), lambda i: (0, i)),
                   pl.BlockSpec((ROUTE_PAD, 1), lambda i: (0, 0))],
        out_shape=[jax.ShapeDtypeStruct((TOKENS * TILE_ROWS, TILE_LANES), jnp.int32),
                   jax.ShapeDtypeStruct((1, TOKENS), jnp.int32),
                   jax.ShapeDtypeStruct((1, TOKENS), jnp.int32),
                   jax.ShapeDtypeStruct((ROUTE_PAD, 1), F32)],
        scratch_shapes=[pltpu.VMEM((ROUTE_PAD, 1), F32)],
        compiler_params=pltpu.CompilerParams(
            dimension_semantics=("arbitrary",), vmem_limit_bytes=VMEM_LIMIT),
        name="moe_route",
    )(x, nw, mod5, mod5, w_route.T, b_route.reshape(ROUTE_PAD, 1), tri)


def _permute_kernel(pos_ref, src_ref, dst_hbm, stage_ref, sem):
    i = pl.program_id(0)
    slot = i % 2
    rows = PERMUTE_TM * TILE_ROWS

    def slot_wait(s):
        whole = stage_ref.at[pl.ds(s * rows, rows), :]
        pltpu.make_async_copy(whole, whole, sem.at[s]).wait()

    @pl.when(i >= 2)
    def _():
        slot_wait(slot)

    base = pl.multiple_of(slot * rows, rows)
    stage_ref[pl.ds(base, rows), :] = src_ref[...]

    def issue(r, _):
        pltpu.make_async_copy(stage_ref.at[pl.ds(base + r * TILE_ROWS, TILE_ROWS), :],
                              dst_hbm.at[pos_ref[i * PERMUTE_TM + r]], sem.at[slot]).start()
        return 0

    lax.fori_loop(0, PERMUTE_TM, issue, 0, unroll=16)

    @pl.when(i == pl.num_programs(0) - 1)
    def _():
        slot_wait(1 - slot)
        slot_wait(slot)


def _moe_permute(pos, payload):
    rows = PERMUTE_TM * TILE_ROWS
    return pl.pallas_call(
        _permute_kernel,
        grid_spec=pltpu.PrefetchScalarGridSpec(
            num_scalar_prefetch=1, grid=(TOKENS // PERMUTE_TM,),
            in_specs=[pl.BlockSpec((rows, TILE_LANES), lambda i, pos: (i, 0))],
            out_specs=pl.BlockSpec(memory_space=pl.ANY),
            scratch_shapes=[pltpu.VMEM((2 * rows, TILE_LANES), jnp.int32), pltpu.SemaphoreType.DMA((2,))]),
        out_shape=jax.ShapeDtypeStruct((TOKENS, TILE_ROWS, TILE_LANES), jnp.int32),
        compiler_params=pltpu.CompilerParams(
            dimension_semantics=("arbitrary",), vmem_limit_bytes=VMEM_LIMIT),
        name="moe_permute",
    )(pos, payload)


def _experts_kernel(blk_ref, cls_ref, elo_ref, ehi_ref, first_ref, last_ref, valid_ref,
                    pay_ref, w1lo_ref, w3lo_ref, w2lo_ref, w1hi_ref, w3hi_ref, w2hi_ref, o_ref, acc_ref):
    k = pl.program_id(0)
    rows = SORT_BLOCK

    @pl.when(valid_ref[k] == 1)
    def _():
        def sublane(s):
            return pay_ref[pl.ds(s, rows, stride=TILE_ROWS), :]

        halves = [[pltpu.unpack_elementwise(sublane(s), index=i, packed_dtype=BF16, unpacked_dtype=F32)
                   for s in range(H_SUBLANES)] for i in range(2)]
        hb = jnp.concatenate(halves[0] + halves[1], axis=1).astype(BF16)
        meta = lax.bitcast_convert_type(sublane(H_SUBLANES), F32)
        mine = meta[:, META_CLASS:META_CLASS + 1] == cls_ref[k].astype(F32)
        w_lo = jnp.where(mine, meta[:, META_WLO:META_WLO + 1], 0.0)
        w_hi = jnp.where(mine, meta[:, META_WHI:META_WHI + 1], 0.0)
        bf = lambda w_ref: w_ref[...].astype(BF16)
        hid_lo = _silu(_dot(hb, bf(w1lo_ref))) * _dot(hb, bf(w3lo_ref)) * w_lo
        hid_hi = _silu(_dot(hb, bf(w1hi_ref))) * _dot(hb, bf(w3hi_ref)) * w_hi
        y = _dot(hid_lo.astype(BF16), bf(w2lo_ref)) + _dot(hid_hi.astype(BF16), bf(w2hi_ref))

        @pl.when(first_ref[k] == 1)
        def _():
            acc_ref[...] = y

        @pl.when(first_ref[k] == 0)
        def _():
            acc_ref[...] += y

        @pl.when(last_ref[k] == 1)
        def _():
            for s in range(TILE_ROWS):
                o_ref[pl.ds(s, rows, stride=TILE_ROWS), :] = acc_ref[:, s * TILE_LANES:(s + 1) * TILE_LANES]


def _moe_experts(items, payload_sorted, layer, w1, w3, w2):
    blk, cls, elo, ehi, first, last, valid = items
    tiles = pl.BlockSpec((SORT_BLOCK * TILE_ROWS, TILE_LANES), lambda k, blk, *_: (blk[k], 0))
    w_in = lambda which: pl.BlockSpec(
        (None, None, D_MODEL, MOE_HIDDEN), lambda k, blk, cls, elo, ehi, *_: (layer, (elo, ehi)[which][k], 0, 0))
    w_out = lambda which: pl.BlockSpec(
        (None, None, MOE_HIDDEN, D_MODEL), lambda k, blk, cls, elo, ehi, *_: (layer, (elo, ehi)[which][k], 0, 0))
    return pl.pallas_call(
        _experts_kernel,
        grid_spec=pltpu.PrefetchScalarGridSpec(
            num_scalar_prefetch=7, grid=(MAX_ITEMS,),
            in_specs=[tiles, w_in(0), w_in(0), w_out(0), w_in(1), w_in(1), w_out(1)],
            out_specs=tiles,
            scratch_shapes=[pltpu.VMEM((SORT_BLOCK, D_MODEL), F32)]),
        out_shape=jax.ShapeDtypeStruct((TOKENS * TILE_ROWS, TILE_LANES), F32),
        compiler_params=pltpu.CompilerParams(
            dimension_semantics=("arbitrary",), vmem_limit_bytes=VMEM_LIMIT),
        name="moe_experts",
    )(blk, cls, elo, ehi, first, last, valid, payload_sorted, w1, w3, w2, w1, w3, w2)


def _combine_kernel(pos_ref, x_ref, gate_ref, y_hbm, o_ref, buf_ref, sem):
    tm = COMBINE_TM
    i = pl.program_id(0)
    n = pl.num_programs(0)

    def gather(tile, slot):
        def issue(r, _):
            pltpu.make_async_copy(y_hbm.at[pos_ref[tile * tm + r]],
                                  buf_ref.at[pl.ds((slot * tm + r) * TILE_ROWS, TILE_ROWS), :], sem.at[slot]).start()
            return 0
        lax.fori_loop(0, tm, issue, 0, unroll=16)

    @pl.when(i == 0)
    def _():
        gather(0, 0)

    @pl.when(i + 1 < n)
    def _():
        gather(i + 1, (i + 1) % 2)

    slot = i % 2
    base = slot * tm * TILE_ROWS
    whole_slot = buf_ref.at[pl.ds(base, tm * TILE_ROWS), :]
    pltpu.make_async_copy(whole_slot, whole_slot, sem.at[slot]).wait()
    y = jnp.concatenate([buf_ref[pl.ds(base + s, tm, stride=TILE_ROWS), :] for s in range(TILE_ROWS)], axis=1)
    o_ref[...] = x_ref[...] + gate_ref[...] * y


def _moe_combine(pos, x, mod5, layer, y_sorted):
    tm = COMBINE_TM
    tiles_per_batch = SEQ // tm
    return pl.pallas_call(
        _combine_kernel,
        grid_spec=pltpu.PrefetchScalarGridSpec(
            num_scalar_prefetch=1, grid=(TOKENS // tm,),
            in_specs=[
                pl.BlockSpec((tm, D_MODEL), lambda i, pos: (i, 0)),
                pl.BlockSpec((None, None, None, 1, D_MODEL),
                             lambda i, pos: (layer, i // tiles_per_batch, 5, 0, 0)),
                pl.BlockSpec(memory_space=pl.ANY),
            ],
            out_specs=pl.BlockSpec((tm, D_MODEL), lambda i, pos: (i, 0)),
            scratch_shapes=[pltpu.VMEM((2 * tm * TILE_ROWS, TILE_LANES), F32), pltpu.SemaphoreType.DMA((2,))]),
        out_shape=jax.ShapeDtypeStruct((TOKENS, D_MODEL), F32),
        compiler_params=pltpu.CompilerParams(
            dimension_semantics=("arbitrary",), vmem_limit_bytes=VMEM_LIMIT),
        name="moe_combine",
    )(pos, x, mod5, y_sorted)


def _moe_plan(cls, rank, counts):
    count = counts[:MOE_CLASSES, 0].astype(jnp.int32)
    ends = jnp.cumsum(count)
    starts = ends - count
    class_ids = jnp.arange(MOE_CLASSES, dtype=jnp.int32)

    def lookup(table, idx):
        return jnp.sum(jnp.where(idx[..., None] == class_ids, table, 0), axis=-1)

    pos = (lookup(starts, cls) + rank).reshape(TOKENS)

    first_blk = starts // SORT_BLOCK
    n_items = jnp.where(count > 0, (ends - 1) // SORT_BLOCK - first_blk + 1, 0)
    item_end = jnp.cumsum(n_items)
    item_start = item_end - n_items
    k = jnp.arange(MAX_ITEMS, dtype=jnp.int32)
    valid = k < item_end[-1]
    kc = jnp.minimum(k, item_end[-1] - 1)
    icls = jnp.sum((item_end[None, :] <= kc[:, None]).astype(jnp.int32), axis=1)
    blk = lookup(first_blk, icls) + (kc - lookup(item_start, icls))
    first = jnp.concatenate([jnp.ones((1,), jnp.int32), (blk[1:] != blk[:-1]).astype(jnp.int32)])
    last = jnp.concatenate([(blk[1:] != blk[:-1]) | ~valid[1:], jnp.ones((1,), bool)]).astype(jnp.int32)
    group = icls // MOE_PAIRS
    elo = group * MOE_EPG + lookup(jnp.asarray(PAIR_LO * MOE_GROUPS, jnp.int32), icls)
    ehi = group * MOE_EPG + lookup(jnp.asarray(PAIR_HI * MOE_GROUPS, jnp.int32), icls)
    return pos, (blk, icls, elo, ehi, first * valid, last * valid, valid.astype(jnp.int32))


def _moe(x, nw, mod5, layer, w_route, b_route, w1, w3, w2):
    payload, cls, rank, counts = _moe_route(x, nw, mod5, layer, w_route, b_route)
    pos, items = _moe_plan(cls, rank, counts)
    tiles = (TOKENS, TILE_ROWS, TILE_LANES)
    sorted_payload = _moe_permute(pos, payload).reshape(TOKENS * TILE_ROWS, TILE_LANES)
    y_sorted = _moe_experts(items, sorted_payload, layer, w1, w3, w2)
    return _moe_combine(pos, x, mod5, layer, y_sorted.reshape(tiles))


def _pad_cols(w, width):
    return jnp.pad(w, ((0, 0), (0, width - w.shape[1])))


def kernel(x, c, ada_w, ada_b, norm_mix, norm_ffn, ssd_w_in, ssd_conv_w, ssd_conv_b, ssd_dt_bias,
           ssd_a_log, ssd_d, ssd_norm, ssd_w_out, dsa_w_in, dsa_q_norm, dsa_k_norm, dsa_w_out,
           moe_w_group, moe_b_group, moe_w_expert, moe_b_expert, moe_w1, moe_w3, moe_w2):
    depth = ada_w.shape[0]
    xt = x.reshape(TOKENS, D_MODEL)
    mod = _modulation(c, ada_w, ada_b)
    mod5 = mod.reshape(depth, BATCH, 6, 1, D_MODEL)

    head_of_col = jnp.arange(SSD_D_INNER, dtype=jnp.int32) // SSD_HEAD_DIM
    expand = (jnp.arange(SSD_DT_PAD, dtype=jnp.int32)[:, None] == head_of_col[None, :]).astype(BF16)

    for i in range(depth):
        j = i // 2
        nw_mix = norm_mix[i].reshape(1, D_MODEL)
        if i % 2 == 0:
            w_in = _pad_cols(ssd_w_in[j], SSD_PROJ_PAD).astype(BF16)
            z, xbc, dt = _inproj(
                xt, nw_mix, mod5, i, 1, 0, w_in,
                ((0, SSD_D_INNER), (SSD_D_INNER, SSD_D_INNER + SSD_CONV_DIM),
                 (SSD_D_INNER + SSD_CONV_DIM, SSD_PROJ_PAD)))
            xt = _ssd_mixer(
                z, xbc, dt, xt, ssd_conv_w[j], ssd_conv_b[j].reshape(1, SSD_CONV_DIM),
                _pad_cols(ssd_dt_bias[j].reshape(1, SSD_HEADS), SSD_DT_PAD),
                _pad_cols(ssd_a_log[j].reshape(1, SSD_HEADS), SSD_DT_PAD),
                jnp.repeat(ssd_d[j], SSD_HEAD_DIM).reshape(1, SSD_D_INNER),
                ssd_norm[j].reshape(1, SSD_D_INNER), expand, ssd_w_out[j].astype(BF16), mod5, i)
        else:
            w = dsa_w_in[j]
            w_in = jnp.concatenate(
                [_pad_cols(w[:, :DSA_KI_START + IDX_HEAD_DIM], DSA_WI_START),
                 _pad_cols(w[:, DSA_KI_START + IDX_HEAD_DIM:], 128)], axis=1).astype(BF16)
            q, k, v, qi, ki, wi = _dsa_inproj(
                xt, nw_mix, mod5, i, w_in, dsa_q_norm[j].reshape(1, ATT_HEAD_DIM),
                dsa_k_norm[j].reshape(1, ATT_HEAD_DIM))
            xt = _dsa_mixer(q, k, v, qi, ki, wi, xt, dsa_w_out[j].astype(BF16), mod5, i)

        w_route = _pad_cols(jnp.concatenate([moe_w_expert[i], moe_w_group[i]], axis=1), ROUTE_PAD).astype(BF16)
        b_route = _pad_cols(jnp.concatenate([moe_b_expert[i], moe_b_group[i]]).reshape(1, -1), ROUTE_PAD)
        xt = _moe(xt, norm_ffn[i].reshape(1, D_MODEL), mod5, i, w_route, b_route,
                  moe_w1, moe_w3, moe_w2)
    return xt.reshape(BATCH, SEQ, D_MODEL)
```

```python
import functools

import jax
import jax.numpy as jnp
from jax import lax
from jax.experimental import pallas as pl
from jax.experimental.pallas import tpu as pltpu

F32 = jnp.float32
BF16 = jnp.bfloat16

D_MODEL = 1024
BATCH = 8
SEQ = 2048
TOKENS = BATCH * SEQ
EPS = 1e-6

SSD_D_INNER = 2048
SSD_HEAD_DIM = 64
SSD_HEADS = 32
SSD_GROUPS = 8
SSD_HEADS_PER_GROUP = 4
SSD_STATE = 128
SSD_CONV = 4
SSD_CHUNK = 128
SSD_GN = SSD_GROUPS * SSD_STATE
SSD_CONV_DIM = SSD_D_INNER + 2 * SSD_GN
SSD_GROUP_W = SSD_HEADS_PER_GROUP * SSD_HEAD_DIM
SSD_DT_PAD = 128
CONV_HALO = 8

ATT_HEADS = 16
ATT_KV_HEADS = 4
ATT_Q_PER_KV = 4
ATT_HEAD_DIM = 64
IDX_HEADS = 8
IDX_HEAD_DIM = 64
TOPK = 256
Q_BLOCK = 256
DSA_Q = ATT_HEADS * ATT_HEAD_DIM
DSA_KV = ATT_KV_HEADS * ATT_HEAD_DIM
DSA_QI = IDX_HEADS * IDX_HEAD_DIM
DSA_KI_START = DSA_Q + 2 * DSA_KV + DSA_QI
DSA_WI_START = DSA_KI_START + 128
DSA_PROJ_PAD = DSA_WI_START + 128

MOE_GROUPS = 4
MOE_EPG = 4
MOE_EXPERTS = 16
MOE_HIDDEN = 256
ROUTE_PAD = 128

VMEM_LIMIT = 56 * 1024 * 1024


def _sigmoid(v):
    return 1.0 / (1.0 + jnp.exp(-v))


def _silu(v):
    return v * _sigmoid(v)


def _split3(a):
    hi = a.astype(BF16)
    r = a - hi.astype(F32)
    mid = r.astype(BF16)
    lo = (r - mid.astype(F32)).astype(BF16)
    return hi, mid, lo


def _dot(a, b):
    return jnp.dot(a, b, preferred_element_type=F32)


def _dot_nt(a, b):
    return lax.dot_general(a, b, (((1,), (1,)), ((), ())), preferred_element_type=F32)


def _dot3_exact_rhs(a, m):
    hi, mid, lo = _split3(a)
    return _dot(hi, m) + _dot(mid, m) + _dot(lo, m)


def _dot3_exact_lhs(m, a):
    hi, mid, lo = _split3(a)
    return _dot(m, hi) + _dot(m, mid) + _dot(m, lo)


def _norm_mod(x, nw, scale, shift):
    ms = jnp.mean(x * x, axis=-1, keepdims=True)
    return x * lax.rsqrt(ms + EPS) * nw * (1.0 + scale) + shift


MOD_TN = 1536


def _mod_kernel(c_ref, w_ref, b_ref, o_ref):
    cond = _silu(c_ref[...]).astype(BF16)
    o_ref[...] = _dot(cond, w_ref[...].astype(BF16)) + b_ref[...]


def _modulation(c, ada_w, ada_b):
    depth = ada_w.shape[0]
    n = ada_w.shape[2]
    return pl.pallas_call(
        _mod_kernel,
        grid=(depth, n // MOD_TN),
        in_specs=[
            pl.BlockSpec((BATCH, D_MODEL), lambda i, j: (0, 0)),
            pl.BlockSpec((None, D_MODEL, MOD_TN), lambda i, j: (i, 0, j)),
            pl.BlockSpec((None, 1, MOD_TN), lambda i, j: (i, 0, j)),
        ],
        out_specs=pl.BlockSpec((None, BATCH, MOD_TN), lambda i, j: (i, 0, j)),
        out_shape=jax.ShapeDtypeStruct((depth, BATCH, n), F32),
        compiler_params=pltpu.CompilerParams(
            dimension_semantics=("arbitrary", "arbitrary"), vmem_limit_bytes=VMEM_LIMIT),
        name="adaln_mod",
    )(c, ada_w, ada_b.reshape(depth, 1, n))


def _mod_spec(layer, chunk, rows_per_batch_tile):
    return pl.BlockSpec((None, None, None, 1, D_MODEL),
                        lambda i, *_: (layer, i // rows_per_batch_tile, chunk, 0, 0))


INPROJ_TM = 256
DSA_INPROJ_TM = 512


def _inproj_kernel(x_ref, nw_ref, scale_ref, shift_ref, w_ref, wdt_ref, z_ref, xbc_ref, dt_ref):
    h = _norm_mod(x_ref[...], nw_ref[...], scale_ref[...], shift_ref[...]).astype(BF16)
    z_ref[...] = _silu(_dot(h, w_ref[:, 0:SSD_D_INNER]))
    xbc_ref[...] = _dot(h, w_ref[:, SSD_D_INNER:SSD_D_INNER + SSD_CONV_DIM])
    dt_ref[...] = _dot(h, wdt_ref[...])


def _inproj(x, nw, mod5, layer, scale_chunk, shift_chunk, w, w_dt):
    tm = INPROJ_TM
    tiles_per_batch = SEQ // tm
    widths = (SSD_D_INNER, SSD_CONV_DIM, SSD_DT_PAD)
    return pl.pallas_call(
        _inproj_kernel,
        grid=(TOKENS // tm,),
        in_specs=[
            pl.BlockSpec((tm, D_MODEL), lambda i: (i, 0)),
            pl.BlockSpec((1, D_MODEL), lambda i: (0, 0)),
            _mod_spec(layer, scale_chunk, tiles_per_batch),
            _mod_spec(layer, shift_chunk, tiles_per_batch),
            pl.BlockSpec((D_MODEL, SSD_D_INNER + SSD_CONV_DIM), lambda i: (0, 0)),
            pl.BlockSpec((D_MODEL, SSD_DT_PAD), lambda i: (0, 0)),
        ],
        out_specs=[pl.BlockSpec((tm, width), lambda i: (i, 0)) for width in widths],
        out_shape=[jax.ShapeDtypeStruct((TOKENS, width), F32) for width in widths],
        compiler_params=pltpu.CompilerParams(
            dimension_semantics=("arbitrary",), vmem_limit_bytes=VMEM_LIMIT),
        name="norm_inproj",
    )(x, nw, mod5, mod5, w, w_dt)


CONV_COLS = 512


def _ssd_kernel(z_ref, xbc_ref, dt_ref, xres_ref, cw_ref, cb_ref, dtb_ref, alog_ref, de_ref, nw_ref,
                e_ref, wout_ref, gate_ref, o_ref, state_ref, ext_ref, act_ref, yn_ref):
    q = SSD_CHUNK
    c = pl.program_id(1)

    @pl.when(c == 0)
    def _():
        state_ref[...] = jnp.zeros_like(state_ref)
        ext_ref[0:CONV_HALO, :] = jnp.zeros((CONV_HALO, SSD_CONV_DIM), F32)

    @pl.when(c > 0)
    def _():
        ext_ref[0:CONV_HALO, :] = ext_ref[q:q + CONV_HALO, :]

    ext_ref[CONV_HALO:CONV_HALO + q, :] = xbc_ref[...]

    for s in range(SSD_CONV_DIM // CONV_COLS):
        cs = slice(s * CONV_COLS, (s + 1) * CONV_COLS)
        u = ext_ref[:, cs]
        acc = cw_ref[0:1, cs] * u
        for k in range(1, SSD_CONV):
            acc = pltpu.roll(acc, 1, axis=0) + cw_ref[k:k + 1, cs] * u
        act_ref[:, cs] = _silu(acc[CONV_HALO:CONV_HALO + q, :] + cb_ref[:, cs])

    dt_raw = dt_ref[...] + dtb_ref[...]
    dt = jnp.maximum(dt_raw, 0.0) + jnp.log1p(jnp.exp(-jnp.abs(dt_raw)))
    a = dt * (-jnp.exp(alog_ref[...]) * LOG2E)
    row = lax.broadcasted_iota(jnp.int32, (q, q), 0)
    col = lax.broadcasted_iota(jnp.int32, (q, q), 1)
    tril = row >= col
    acs = _dot3_exact_lhs(tril.astype(BF16), a)
    acs_t = acs.T
    expand = e_ref[...]
    acs_e = _dot3_exact_rhs(acs, expand)
    dt_e = _dot3_exact_rhs(dt, expand)
    tot_e = acs_e[q - 1:q, :]
    decay_from_start = jnp.exp2(acs_e)
    decay_to_end = jnp.exp2(tot_e - acs_e)
    chunk_decay = jnp.exp2(tot_e)

    lane_head = lax.broadcasted_iota(jnp.int32, (q, SSD_GROUP_W), 1) // SSD_HEAD_DIM
    for g in range(SSD_GROUPS):
        xs = act_ref[:, g * SSD_GROUP_W:(g + 1) * SSD_GROUP_W]
        gs = slice(g * SSD_GROUP_W, (g + 1) * SSD_GROUP_W)
        bm = act_ref[:, SSD_D_INNER + g * SSD_STATE:SSD_D_INNER + (g + 1) * SSD_STATE]
        cm = act_ref[:, SSD_D_INNER + SSD_GN + g * SSD_STATE:SSD_D_INNER + SSD_GN + (g + 1) * SSD_STATE]
        bm_t = bm.T.astype(BF16)
        cm_b = cm.astype(BF16)
        cb = _dot(cm_b, bm_t)
        xd = xs * dt_e[:, gs]
        ms = []
        xds = []
        for j in range(SSD_HEADS_PER_GROUP):
            h = g * SSD_HEADS_PER_GROUP + j
            seg = acs[:, h:h + 1] - acs_t[h:h + 1, :]
            dec = jnp.exp2(jnp.where(tril, seg, -jnp.inf))
            ms.append((cb * dec).astype(BF16))
            xds.append(jnp.where(lane_head == j, xd, 0.0).astype(BF16))
        y_diag = _dot(jnp.concatenate(ms, axis=1), jnp.concatenate(xds, axis=0))
        prev = state_ref[g]
        y_off = _dot(cm_b, prev.astype(BF16)) * decay_from_start[:, gs]
        state_ref[g] = prev * chunk_decay[:, gs] + _dot(bm_t, (xd * decay_to_end[:, gs]).astype(BF16))
        y = y_diag + y_off + xs * de_ref[:, gs]
        y = y * z_ref[:, gs]
        y = y * lax.rsqrt(jnp.mean(y * y, axis=-1, keepdims=True) + EPS) * nw_ref[:, gs]
        yn_ref[:, gs] = y.astype(BF16)

    out = _dot(yn_ref[...], wout_ref[...])
    o_ref[...] = xres_ref[...] + gate_ref[...] * out


def _ssd_mixer(z, xbc, dt, x, conv_w, conv_b, dt_bias, a_log, d_e, norm_w, expand, w_out, mod5, layer):
    q = SSD_CHUNK
    nc = SEQ // q
    tok = lambda w: pl.BlockSpec((q, w), lambda b, c: (b * nc + c, 0))
    full = lambda r, w: pl.BlockSpec((r, w), lambda b, c: (0, 0))
    return pl.pallas_call(
        _ssd_kernel,
        grid=(BATCH, nc),
        in_specs=[
            tok(SSD_D_INNER), tok(SSD_CONV_DIM), tok(SSD_DT_PAD), tok(D_MODEL),
            full(SSD_CONV, SSD_CONV_DIM), full(1, SSD_CONV_DIM), full(1, SSD_DT_PAD), full(1, SSD_DT_PAD),
            full(1, SSD_D_INNER), full(1, SSD_D_INNER), full(SSD_DT_PAD, SSD_D_INNER),
            full(SSD_D_INNER, D_MODEL),
            pl.BlockSpec((None, None, None, 1, D_MODEL), lambda b, c: (layer, b, 2, 0, 0)),
        ],
        out_specs=tok(D_MODEL),
        out_shape=jax.ShapeDtypeStruct((TOKENS, D_MODEL), F32),
        scratch_shapes=[
            pltpu.VMEM((SSD_GROUPS, SSD_STATE, SSD_GROUP_W), F32),
            pltpu.VMEM((q + CONV_HALO, SSD_CONV_DIM), F32),
            pltpu.VMEM((q, SSD_CONV_DIM), F32),
            pltpu.VMEM((q, SSD_D_INNER), BF16),
        ],
        compiler_params=pltpu.CompilerParams(
            dimension_semantics=("arbitrary", "arbitrary"), vmem_limit_bytes=VMEM_LIMIT),
        name="ssd_mixer",
    )(z, xbc, dt, x, conv_w, conv_b, dt_bias, a_log, d_e, norm_w, expand, w_out, mod5)


DSA_KEY_TILE = 256
DSA_CLASSES = 8
V_EXT = 2 * ATT_HEAD_DIM
DSA_BLOCKS_PER_CLASS = (SEQ // Q_BLOCK) // DSA_CLASSES
N_BISECT = 12
F32_MIN = float(jnp.finfo(jnp.float32).min)
LOG2E = 1.4426950408889634


def _count(mask):
    return jnp.sum(jnp.where(mask, 1.0, 0.0), axis=-1, keepdims=True)


def _select_topk(score_ref, q_pos, n_keys):
    kf = float(TOPK)
    small = (q_pos + 1) <= TOPK
    sc = score_ref[...]
    hi0 = jnp.max(sc, axis=-1, keepdims=True)
    lo0 = jnp.min(jnp.where(sc == -jnp.inf, jnp.inf, sc), axis=-1, keepdims=True)

    def bisect(_, carry):
        lo, hi = carry
        mid = lo + 0.5 * (hi - lo)
        ok = _count(score_ref[...] >= mid) >= kf
        return jnp.where(ok, mid, lo), jnp.where(ok, hi, mid)

    _, hi = lax.fori_loop(0, N_BISECT, bisect, (lo0, hi0))

    v0 = jnp.max(jnp.where(sc <= hi, sc, -jnp.inf), axis=-1, keepdims=True)
    c0 = _count(sc >= v0)
    pend0 = jnp.where((c0 >= kf) | small, 0.0, 1.0)

    def walk_cond(carry):
        return (carry[3] > 0.0) & (carry[4] < n_keys)

    def walk(carry):
        v, c, pend, _, it = carry
        s = score_ref[...]
        v2 = jnp.max(jnp.where(s < v, s, -jnp.inf), axis=-1, keepdims=True)
        c2 = _count(s >= v2)
        v = jnp.where(pend > 0.0, v2, v)
        c = jnp.where(pend > 0.0, c2, c)
        pend = jnp.where(c2 >= kf, 0.0, pend)
        return v, c, pend, jnp.max(pend), it + 1

    v, c, _, _, _ = lax.while_loop(walk_cond, walk, (v0, c0, pend0, jnp.max(pend0), jnp.int32(0)))
    thr = jnp.where(small, F32_MIN, v)
    any_tie = jnp.max(jnp.where(small, 0.0, c - kf)) > 0.0

    @pl.when(jnp.logical_not(any_tie))
    def _():
        score_ref[...] = jnp.where(score_ref[...] >= thr, 0.0, -jnp.inf)

    @pl.when(any_tie)
    def _():
        s = score_ref[...]
        key_pos = lax.broadcasted_iota(jnp.int32, (Q_BLOCK, n_keys), 1)
        gt = s > thr
        eq = s == thr
        need = kf - _count(gt)

        def body(_, carry):
            lo, hi = carry
            mid = (lo + hi) >> 1
            ok = _count((score_ref[...] == thr) & (key_pos <= mid)) >= need
            return jnp.where(ok, lo, mid), jnp.where(ok, mid, hi)

        init = (jnp.full((Q_BLOCK, 1), -1, jnp.int32), jnp.full((Q_BLOCK, 1), n_keys - 1, jnp.int32))
        cut = lax.fori_loop(0, (n_keys - 1).bit_length() + 1, body, init)[1]
        score_ref[...] = jnp.where(gt | (eq & (key_pos <= cut)), 0.0, -jnp.inf)


def _dsa_inproj_kernel(x_ref, nw_ref, scale_ref, shift_ref, w_ref, qn_ref, kn_ref, seg_ref, segt_ref,
                       q_ref, k_ref, v_ref, qi_ref, ki_ref, wi_ref):
    hd = ATT_HEAD_DIM
    h = _norm_mod(x_ref[...], nw_ref[...], scale_ref[...], shift_ref[...]).astype(BF16)

    def head_norm(t, w):
        width = t.shape[1]
        ss = _dot((t * t).astype(BF16), seg_ref[0:width, :])
        r = lax.rsqrt(ss * (1.0 / hd) + EPS)
        r_hi = r.astype(BF16)
        r_lo = (r - r_hi.astype(F32)).astype(BF16)
        return t * (_dot(r_hi, segt_ref[:, 0:width]) + _dot(r_lo, segt_ref[:, 0:width])) * w

    q = head_norm(_dot(h, w_ref[:, 0:DSA_Q]), qn_ref[...] * (hd ** -0.5 * LOG2E))
    for n in range(ATT_HEADS):
        q_ref[n] = q[:, n * hd:(n + 1) * hd].astype(BF16)
    kv = _dot(h, w_ref[:, DSA_Q:DSA_Q + 2 * DSA_KV])
    k = head_norm(kv[:, 0:DSA_KV], kn_ref[...])
    for n in range(ATT_KV_HEADS):
        k_ref[n] = k[:, n * hd:(n + 1) * hd].astype(BF16)
        v_ref[n] = jnp.concatenate([kv[:, DSA_KV + n * hd:DSA_KV + (n + 1) * hd],
                                    jnp.ones((kv.shape[0], V_EXT - hd), F32)], axis=1).astype(BF16)
    qi = _dot(h, w_ref[:, DSA_Q + 2 * DSA_KV:DSA_KI_START])
    for n in range(IDX_HEADS):
        qi_ref[n] = qi[:, n * IDX_HEAD_DIM:(n + 1) * IDX_HEAD_DIM].astype(BF16)
    ki_ref[...] = _dot(h, w_ref[:, DSA_KI_START:DSA_KI_START + IDX_HEAD_DIM]).astype(BF16)
    wi_ref[...] = _dot(h, w_ref[:, DSA_WI_START:DSA_WI_START + IDX_HEADS]) * ((IDX_HEADS * IDX_HEAD_DIM) ** -0.5)


def _dsa_inproj(x, nw, mod5, layer, w, q_norm, k_norm):
    tm = DSA_INPROJ_TM
    tiles_per_batch = SEQ // tm
    heads = lambda n: pl.BlockSpec((n, tm, ATT_HEAD_DIM), lambda i: (0, i, 0))
    head_of = jnp.arange(DSA_Q, dtype=jnp.int32) // ATT_HEAD_DIM
    seg = (head_of[:, None] == jnp.arange(128, dtype=jnp.int32)[None, :]).astype(BF16)
    q_norm = jnp.tile(q_norm, (1, ATT_HEADS))
    k_norm = jnp.tile(k_norm, (1, ATT_KV_HEADS))
    return pl.pallas_call(
        _dsa_inproj_kernel,
        grid=(TOKENS // tm,),
        in_specs=[
            pl.BlockSpec((tm, D_MODEL), lambda i: (i, 0)),
            pl.BlockSpec((1, D_MODEL), lambda i: (0, 0)),
            _mod_spec(layer, 1, tiles_per_batch),
            _mod_spec(layer, 0, tiles_per_batch),
            pl.BlockSpec((D_MODEL, DSA_PROJ_PAD), lambda i: (0, 0)),
            pl.BlockSpec((1, DSA_Q), lambda i: (0, 0)),
            pl.BlockSpec((1, DSA_KV), lambda i: (0, 0)),
            pl.BlockSpec((DSA_Q, 128), lambda i: (0, 0)),
            pl.BlockSpec((128, DSA_Q), lambda i: (0, 0)),
        ],
        out_specs=[heads(ATT_HEADS), heads(ATT_KV_HEADS),
                   pl.BlockSpec((ATT_KV_HEADS, tm, V_EXT), lambda i: (0, i, 0)), heads(IDX_HEADS),
                   pl.BlockSpec((tm, IDX_HEAD_DIM), lambda i: (i, 0)),
                   pl.BlockSpec((tm, IDX_HEADS), lambda i: (i, 0))],
        out_shape=[jax.ShapeDtypeStruct((ATT_HEADS, TOKENS, ATT_HEAD_DIM), BF16),
                   jax.ShapeDtypeStruct((ATT_KV_HEADS, TOKENS, ATT_HEAD_DIM), BF16),
                   jax.ShapeDtypeStruct((ATT_KV_HEADS, TOKENS, V_EXT), BF16),
                   jax.ShapeDtypeStruct((IDX_HEADS, TOKENS, IDX_HEAD_DIM), BF16),
                   jax.ShapeDtypeStruct((TOKENS, IDX_HEAD_DIM), BF16),
                   jax.ShapeDtypeStruct((TOKENS, IDX_HEADS), F32)],
        compiler_params=pltpu.CompilerParams(
            dimension_semantics=("arbitrary",), vmem_limit_bytes=VMEM_LIMIT),
        name="dsa_inproj",
    )(x, nw, mod5, mod5, w, q_norm, k_norm, seg, seg.T)


def _dsa_kernel(q_ref, k_ref, v_ref, qi_ref, ki_ref, wi_ref, xres_ref, wout_ref, gate_ref, o_ref,
                score_ref, ocat_ref, *, n_keys, first_block):
    hd = ATT_HEAD_DIM
    q_pos = (first_block + pl.program_id(1)) * Q_BLOCK + lax.broadcasted_iota(jnp.int32, (Q_BLOCK, 1), 0)

    wi = wi_ref[...]
    qi = qi_ref[...].reshape(IDX_HEADS * Q_BLOCK, IDX_HEAD_DIM)
    for kt in range(n_keys // DSA_KEY_TILE):
        ks = slice(kt * DSA_KEY_TILE, (kt + 1) * DSA_KEY_TILE)
        raw = _dot_nt(qi, ki_ref[ks, :])
        acc = jnp.zeros((Q_BLOCK, DSA_KEY_TILE), F32)
        for n in range(IDX_HEADS):
            acc = acc + wi[:, n:n + 1] * jnp.maximum(raw[n * Q_BLOCK:(n + 1) * Q_BLOCK, :], 0.0)
        key_pos = kt * DSA_KEY_TILE + lax.broadcasted_iota(jnp.int32, (Q_BLOCK, DSA_KEY_TILE), 1)
        score_ref[:, ks] = jnp.where(key_pos <= q_pos, acc, -jnp.inf)

    if n_keys > TOPK:
        _select_topk(score_ref, q_pos, n_keys)
        bias = score_ref[...][None, :, :]
    else:
        bias = jnp.where(score_ref[...] == -jnp.inf, -jnp.inf, 0.0)[None, :, :]

    for n in range(ATT_KV_HEADS):
        q4 = q_ref[n * ATT_Q_PER_KV:(n + 1) * ATT_Q_PER_KV].reshape(ATT_Q_PER_KV * Q_BLOCK, hd)
        s = _dot_nt(q4, k_ref[n]).reshape(ATT_Q_PER_KV, Q_BLOCK, n_keys) + bias
        p = jnp.exp2(s - jnp.max(s, axis=-1, keepdims=True))
        o = _dot(p.reshape(ATT_Q_PER_KV * Q_BLOCK, n_keys).astype(BF16), v_ref[n])
        o = o[:, 0:hd] * (1.0 / o[:, hd:hd + 1])
        for g in range(ATT_Q_PER_KV):
            col = (n * ATT_Q_PER_KV + g) * hd
            ocat_ref[:, col:col + hd] = o[g * Q_BLOCK:(g + 1) * Q_BLOCK, :]
    out = _dot(ocat_ref[...].astype(BF16), wout_ref[...])
    o_ref[...] = xres_ref[...] + gate_ref[...] * out


def _dsa_mixer(q, k, v, qi, ki, wi, x, w_out, mod5, layer):
    nb = SEQ // Q_BLOCK
    k4 = k.reshape(ATT_KV_HEADS, BATCH, SEQ, ATT_HEAD_DIM)
    v4 = v.reshape(ATT_KV_HEADS, BATCH, SEQ, V_EXT)
    ki3 = ki.reshape(BATCH, SEQ, IDX_HEAD_DIM)
    for cls in range(DSA_CLASSES):
        n_keys = (cls + 1) * (SEQ // DSA_CLASSES)
        first_block = cls * DSA_BLOCKS_PER_CLASS
        row = lambda b, i, fb=first_block: b * nb + fb + i
        heads = lambda n: pl.BlockSpec((n, Q_BLOCK, ATT_HEAD_DIM), lambda b, i: (0, row(b, i), 0))
        keys = lambda width: pl.BlockSpec((ATT_KV_HEADS, None, n_keys, width), lambda b, i: (0, b, 0, 0))
        x = pl.pallas_call(
            functools.partial(_dsa_kernel, n_keys=n_keys, first_block=first_block),
            grid=(BATCH, DSA_BLOCKS_PER_CLASS),
            in_specs=[
                heads(ATT_HEADS), keys(ATT_HEAD_DIM), keys(V_EXT), heads(IDX_HEADS),
                pl.BlockSpec((None, n_keys, IDX_HEAD_DIM), lambda b, i: (b, 0, 0)),
                pl.BlockSpec((Q_BLOCK, IDX_HEADS), lambda b, i: (row(b, i), 0)),
                pl.BlockSpec((Q_BLOCK, D_MODEL), lambda b, i: (row(b, i), 0)),
                pl.BlockSpec((DSA_Q, D_MODEL), lambda b, i: (0, 0)),
                pl.BlockSpec((None, None, None, 1, D_MODEL), lambda b, i: (layer, b, 2, 0, 0)),
            ],
            out_specs=pl.BlockSpec((Q_BLOCK, D_MODEL), lambda b, i: (row(b, i), 0)),
            out_shape=jax.ShapeDtypeStruct((TOKENS, D_MODEL), F32),
            scratch_shapes=[
                pltpu.VMEM((Q_BLOCK, n_keys), F32),
                pltpu.VMEM((Q_BLOCK, D_MODEL), F32),
            ],
            input_output_aliases={6: 0},
            compiler_params=pltpu.CompilerParams(
                dimension_semantics=("arbitrary", "arbitrary"), vmem_limit_bytes=VMEM_LIMIT),
            name=f"dsa_mixer_c{cls}",
        )(q, k4, v4, qi, ki3, wi, x, w_out, mod5)
    return x


MOE_PAIRS = MOE_EPG * (MOE_EPG - 1) // 2
MOE_CLASSES = MOE_GROUPS * MOE_PAIRS
PAIR_LO = (0, 0, 0, 1, 1, 2)
PAIR_HI = (1, 2, 3, 2, 3, 3)
ROUTE_TM = 1024
META_W = 128
META_CLASS, META_RANK, META_WLO, META_WHI = 0, 1, 2, 3
TILE_ROWS, TILE_LANES = 8, 128
H_WORDS = D_MODEL // 2
H_SUBLANES = H_WORDS // TILE_LANES
SORT_BLOCK = 256
N_SORT_BLOCKS = TOKENS // SORT_BLOCK
MAX_ITEMS = N_SORT_BLOCKS + MOE_CLASSES
PERMUTE_TM = 512
COMBINE_TM = 512


def _route_kernel(x_ref, nw_ref, scale_ref, shift_ref, wrt_ref, brt_ref, tri_ref, pay_ref, cls_ref, rank_ref,
                  cnt_ref, carry_ref):
    tm = ROUTE_TM

    @pl.when(pl.program_id(0) == 0)
    def _():
        carry_ref[...] = jnp.zeros_like(carry_ref)

    h = _norm_mod(x_ref[...], nw_ref[...], scale_ref[...], shift_ref[...])
    logits = _dot_nt(wrt_ref[...], h.astype(BF16)) + brt_ref[...]
    sub = lax.broadcasted_iota(jnp.int32, logits.shape, 0)
    neg = -jnp.inf
    big = jnp.int32(ROUTE_PAD)
    is_group = (sub >= MOE_EXPERTS) & (sub < MOE_EXPERTS + MOE_GROUPS)
    gl = jnp.where(is_group, logits, neg)
    g_max = jnp.max(gl, axis=0, keepdims=True)
    g_idx = jnp.min(jnp.where(gl == g_max, sub - MOE_EXPERTS, big), axis=0, keepdims=True)
    g_val = 1.0 / jnp.sum(jnp.exp(gl - g_max), axis=0, keepdims=True)
    in_group = (sub < MOE_EXPERTS) & ((sub // MOE_EPG) == g_idx)
    el = jnp.where(in_group, logits, neg)
    m1 = jnp.max(el, axis=0, keepdims=True)
    i1 = jnp.min(jnp.where(el == m1, sub, big), axis=0, keepdims=True)
    el2 = jnp.where(sub == i1, neg, el)
    m2 = jnp.max(el2, axis=0, keepdims=True)
    i2 = jnp.min(jnp.where(el2 == m2, sub, big), axis=0, keepdims=True)
    r = jnp.exp(m2 - m1)
    w_top1 = g_val / (1.0 + r)
    w_top2 = g_val * r / (1.0 + r)

    lo = jnp.minimum(i1, i2) - g_idx * MOE_EPG
    hi = jnp.maximum(i1, i2) - g_idx * MOE_EPG
    pair = (lo * (2 * MOE_EPG - 1 - lo)) // 2 + (hi - lo - 1)
    cls = g_idx * MOE_PAIRS + pair
    w_lo = jnp.where(i1 < i2, w_top1, w_top2)
    w_hi = jnp.where(i1 < i2, w_top2, w_top1)

    onehot = sub == cls
    before = _dot(onehot.astype(BF16), tri_ref[...]) + carry_ref[...]
    rank = jnp.sum(jnp.where(onehot, before, 0.0), axis=0, keepdims=True)
    carry_ref[...] += jnp.sum(jnp.where(onehot, 1.0, 0.0), axis=1, keepdims=True)
    cnt_ref[...] = carry_ref[...]
    cls_ref[...] = cls
    rank_ref[...] = rank.astype(jnp.int32)

    words = pltpu.pack_elementwise([h[:, 0:H_WORDS], h[:, H_WORDS:D_MODEL]], packed_dtype=BF16)
    for s in range(H_SUBLANES):
        pay_ref[pl.ds(s, tm, stride=TILE_ROWS), :] = words[:, s * TILE_LANES:(s + 1) * TILE_LANES]
    sub8 = lax.broadcasted_iota(jnp.int32, (TILE_ROWS, tm), 0)
    rec = jnp.where(sub8 == META_CLASS, cls.astype(F32),
                    jnp.where(sub8 == META_RANK, rank,
                              jnp.where(sub8 == META_WLO, w_lo, jnp.where(sub8 == META_WHI, w_hi, 0.0))))
    rec = jnp.concatenate([rec, jnp.zeros((META_W - TILE_ROWS, tm), F32)], axis=0)
    for b in range(tm // META_W):
        meta = rec[:, b * META_W:(b + 1) * META_W].T
        pay_ref[pl.ds(b * META_W * TILE_ROWS + H_SUBLANES, META_W, stride=TILE_ROWS), :] = (
            lax.bitcast_convert_type(meta, jnp.int32))
    for s in range(H_SUBLANES + 1, TILE_ROWS):
        pay_ref[pl.ds(s, tm, stride=TILE_ROWS), :] = jnp.zeros((tm, TILE_LANES), jnp.int32)


def _moe_route(x, nw, mod5, layer, w_route, b_route):
    tm = ROUTE_TM
    tiles_per_batch = SEQ // tm
    tri = (jnp.arange(tm, dtype=jnp.int32)[:, None] < jnp.arange(tm, dtype=jnp.int32)[None, :]).astype(BF16)
    return pl.pallas_call(
        _route_kernel,
        grid=(TOKENS // tm,),
        in_specs=[
            pl.BlockSpec((tm, D_MODEL), lambda i: (i, 0)),
            pl.BlockSpec((1, D_MODEL), lambda i: (0, 0)),
            _mod_spec(layer, 4, tiles_per_batch),
            _mod_spec(layer, 3, tiles_per_batch),
            pl.BlockSpec((ROUTE_PAD, D_MODEL), lambda i: (0, 0)),
            pl.BlockSpec((ROUTE_PAD, 1), lambda i: (0, 0)),
            pl.BlockSpec((tm, tm), lambda i: (0, 0)),
        ],
        out_specs=[pl.BlockSpec((tm * TILE_ROWS, TILE_LANES), lambda i: (i, 0)),
                   pl.BlockSpec((1, tm), lambda i: (0, i)),
                   pl.BlockSpec((1, tm), lambda i: (0, i)),
                   pl.BlockSpec((ROUTE_PAD, 1), lambda i: (0, 0))],
        out_shape=[jax.ShapeDtypeStruct((TOKENS * TILE_ROWS, TILE_LANES), jnp.int32),
                   jax.ShapeDtypeStruct((1, TOKENS), jnp.int32),
                   jax.ShapeDtypeStruct((1, TOKENS), jnp.int32),
                   jax.ShapeDtypeStruct((ROUTE_PAD, 1), F32)],
        scratch_shapes=[pltpu.VMEM((ROUTE_PAD, 1), F32)],
        compiler_params=pltpu.CompilerParams(
            dimension_semantics=("arbitrary",), vmem_limit_bytes=VMEM_LIMIT),
        name="moe_route",
    )(x, nw, mod5, mod5, w_route.T, b_route.reshape(ROUTE_PAD, 1), tri)


def _permute_kernel(pos_ref, src_ref, dst_hbm, stage_ref, sem):
    i = pl.program_id(0)
    slot = i % 2
    rows = PERMUTE_TM * TILE_ROWS

    def slot_wait(s):
        whole = stage_ref.at[pl.ds(s * rows, rows), :]
        pltpu.make_async_copy(whole, whole, sem.at[s]).wait()

    @pl.when(i >= 2)
    def _():
        slot_wait(slot)

    base = pl.multiple_of(slot * rows, rows)
    stage_ref[pl.ds(base, rows), :] = src_ref[...]

    def issue(r, _):
        pltpu.make_async_copy(stage_ref.at[pl.ds(base + r * TILE_ROWS, TILE_ROWS), :],
                              dst_hbm.at[pos_ref[i * PERMUTE_TM + r]], sem.at[slot]).start()
        return 0

    lax.fori_loop(0, PERMUTE_TM, issue, 0, unroll=16)

    @pl.when(i == pl.num_programs(0) - 1)
    def _():
        slot_wait(1 - slot)
        slot_wait(slot)


def _moe_permute(pos, payload):
    rows = PERMUTE_TM * TILE_ROWS
    return pl.pallas_call(
        _permute_kernel,
        grid_spec=pltpu.PrefetchScalarGridSpec(
            num_scalar_prefetch=1, grid=(TOKENS // PERMUTE_TM,),
            in_specs=[pl.BlockSpec((rows, TILE_LANES), lambda i, pos: (i, 0))],
            out_specs=pl.BlockSpec(memory_space=pl.ANY),
            scratch_shapes=[pltpu.VMEM((2 * rows, TILE_LANES), jnp.int32), pltpu.SemaphoreType.DMA((2,))]),
        out_shape=jax.ShapeDtypeStruct((TOKENS, TILE_ROWS, TILE_LANES), jnp.int32),
        compiler_params=pltpu.CompilerParams(
            dimension_semantics=("arbitrary",), vmem_limit_bytes=VMEM_LIMIT),
        name="moe_permute",
    )(pos, payload)


def _experts_kernel(blk_ref, cls_ref, elo_ref, ehi_ref, first_ref, last_ref, valid_ref,
                    pay_ref, w1lo_ref, w3lo_ref, w2lo_ref, w1hi_ref, w3hi_ref, w2hi_ref, o_ref, acc_ref):
    k = pl.program_id(0)
    rows = SORT_BLOCK

    @pl.when(valid_ref[k] == 1)
    def _():
        def sublane(s):
            return pay_ref[pl.ds(s, rows, stride=TILE_ROWS), :]

        halves = [[pltpu.unpack_elementwise(sublane(s), index=i, packed_dtype=BF16, unpacked_dtype=F32)
                   for s in range(H_SUBLANES)] for i in range(2)]
        hb = jnp.concatenate(halves[0] + halves[1], axis=1).astype(BF16)
        meta = lax.bitcast_convert_type(sublane(H_SUBLANES), F32)
        mine = meta[:, META_CLASS:META_CLASS + 1] == cls_ref[k].astype(F32)
        w_lo = jnp.where(mine, meta[:, META_WLO:META_WLO + 1], 0.0)
        w_hi = jnp.where(mine, meta[:, META_WHI:META_WHI + 1], 0.0)
        bf = lambda w_ref: w_ref[...].astype(BF16)
        hid_lo = _silu(_dot(hb, bf(w1lo_ref))) * _dot(hb, bf(w3lo_ref)) * w_lo
        hid_hi = _silu(_dot(hb, bf(w1hi_ref))) * _dot(hb, bf(w3hi_ref)) * w_hi
        y = _dot(hid_lo.astype(BF16), bf(w2lo_ref)) + _dot(hid_hi.astype(BF16), bf(w2hi_ref))

        @pl.when(first_ref[k] == 1)
        def _():
            acc_ref[...] = y

        @pl.when(first_ref[k] == 0)
        def _():
            acc_ref[...] += y

        @pl.when(last_ref[k] == 1)
        def _():
            for s in range(TILE_ROWS):
                o_ref[pl.ds(s, rows, stride=TILE_ROWS), :] = acc_ref[:, s * TILE_LANES:(s + 1) * TILE_LANES]


def _moe_experts(items, payload_sorted, layer, w1, w3, w2):
    blk, cls, elo, ehi, first, last, valid = items
    tiles = pl.BlockSpec((SORT_BLOCK * TILE_ROWS, TILE_LANES), lambda k, blk, *_: (blk[k], 0))
    w_in = lambda which: pl.BlockSpec(
        (None, None, D_MODEL, MOE_HIDDEN), lambda k, blk, cls, elo, ehi, *_: (layer, (elo, ehi)[which][k], 0, 0))
    w_out = lambda which: pl.BlockSpec(
        (None, None, MOE_HIDDEN, D_MODEL), lambda k, blk, cls, elo, ehi, *_: (layer, (elo, ehi)[which][k], 0, 0))
    return pl.pallas_call(
        _experts_kernel,
        grid_spec=pltpu.PrefetchScalarGridSpec(
            num_scalar_prefetch=7, grid=(MAX_ITEMS,),
            in_specs=[tiles, w_in(0), w_in(0), w_out(0), w_in(1), w_in(1), w_out(1)],
            out_specs=tiles,
            scratch_shapes=[pltpu.VMEM((SORT_BLOCK, D_MODEL), F32)]),
        out_shape=jax.ShapeDtypeStruct((TOKENS * TILE_ROWS, TILE_LANES), F32),
        compiler_params=pltpu.CompilerParams(
            dimension_semantics=("arbitrary",), vmem_limit_bytes=VMEM_LIMIT),
        name="moe_experts",
    )(blk, cls, elo, ehi, first, last, valid, payload_sorted, w1, w3, w2, w1, w3, w2)


def _combine_kernel(pos_ref, x_ref, gate_ref, y_hbm, o_ref, buf_ref, sem):
    tm = COMBINE_TM
    i = pl.program_id(0)
    n = pl.num_programs(0)

    def gather(tile, slot):
        def issue(r, _):
            pltpu.make_async_copy(y_hbm.at[pos_ref[tile * tm + r]],
                                  buf_ref.at[pl.ds((slot * tm + r) * TILE_ROWS, TILE_ROWS), :], sem.at[slot]).start()
            return 0
        lax.fori_loop(0, tm, issue, 0, unroll=16)

    @pl.when(i == 0)
    def _():
        gather(0, 0)

    @pl.when(i + 1 < n)
    def _():
        gather(i + 1, (i + 1) % 2)

    slot = i % 2
    base = slot * tm * TILE_ROWS
    whole_slot = buf_ref.at[pl.ds(base, tm * TILE_ROWS), :]
    pltpu.make_async_copy(whole_slot, whole_slot, sem.at[slot]).wait()
    y = jnp.concatenate([buf_ref[pl.ds(base + s, tm, stride=TILE_ROWS), :] for s in range(TILE_ROWS)], axis=1)
    o_ref[...] = x_ref[...] + gate_ref[...] * y


def _moe_combine(pos, x, mod5, layer, y_sorted):
    tm = COMBINE_TM
    tiles_per_batch = SEQ // tm
    return pl.pallas_call(
        _combine_kernel,
        grid_spec=pltpu.PrefetchScalarGridSpec(
            num_scalar_prefetch=1, grid=(TOKENS // tm,),
            in_specs=[
                pl.BlockSpec((tm, D_MODEL), lambda i, pos: (i, 0)),
                pl.BlockSpec((None, None, None, 1, D_MODEL),
                             lambda i, pos: (layer, i // tiles_per_batch, 5, 0, 0)),
                pl.BlockSpec(memory_space=pl.ANY),
            ],
            out_specs=pl.BlockSpec((tm, D_MODEL), lambda i, pos: (i, 0)),
            scratch_shapes=[pltpu.VMEM((2 * tm * TILE_ROWS, TILE_LANES), F32), pltpu.SemaphoreType.DMA((2,))]),
        out_shape=jax.ShapeDtypeStruct((TOKENS, D_MODEL), F32),
        compiler_params=pltpu.CompilerParams(
            dimension_semantics=("arbitrary",), vmem_limit_bytes=VMEM_LIMIT),
        name="moe_combine",
    )(pos, x, mod5, y_sorted)


def _moe_plan(cls, rank, counts):
    count = counts[:MOE_CLASSES, 0].astype(jnp.int32)
    ends = jnp.cumsum(count)
    starts = ends - count
    class_ids = jnp.arange(MOE_CLASSES, dtype=jnp.int32)

    def lookup(table, idx):
        return jnp.sum(jnp.where(idx[..., None] == class_ids, table, 0), axis=-1)

    pos = (lookup(starts, cls) + rank).reshape(TOKENS)

    first_blk = starts // SORT_BLOCK
    n_items = jnp.where(count > 0, (ends - 1) // SORT_BLOCK - first_blk + 1, 0)
    item_end = jnp.cumsum(n_items)
    item_start = item_end - n_items
    k = jnp.arange(MAX_ITEMS, dtype=jnp.int32)
    valid = k < item_end[-1]
    kc = jnp.minimum(k, item_end[-1] - 1)
    icls = jnp.sum((item_end[None, :] <= kc[:, None]).astype(jnp.int32), axis=1)
    blk = lookup(first_blk, icls) + (kc - lookup(item_start, icls))
    first = jnp.concatenate([jnp.ones((1,), jnp.int32), (blk[1:] != blk[:-1]).astype(jnp.int32)])
    last = jnp.concatenate([(blk[1:] != blk[:-1]) | ~valid[1:], jnp.ones((1,), bool)]).astype(jnp.int32)
    group = icls // MOE_PAIRS
    elo = group * MOE_EPG + lookup(jnp.asarray(PAIR_LO * MOE_GROUPS, jnp.int32), icls)
    ehi = group * MOE_EPG + lookup(jnp.asarray(PAIR_HI * MOE_GROUPS, jnp.int32), icls)
    return pos, (blk, icls, elo, ehi, first * valid, last * valid, valid.astype(jnp.int32))


def _moe(x, nw, mod5, layer, w_route, b_route, w1, w3, w2):
    payload, cls, rank, counts = _moe_route(x, nw, mod5, layer, w_route, b_route)
    pos, items = _moe_plan(cls, rank, counts)
    tiles = (TOKENS, TILE_ROWS, TILE_LANES)
    sorted_payload = _moe_permute(pos, payload).reshape(TOKENS * TILE_ROWS, TILE_LANES)
    y_sorted = _moe_experts(items, sorted_payload, layer, w1, w3, w2)
    return _moe_combine(pos, x, mod5, layer, y_sorted.reshape(tiles))


def _pad_cols(w, width):
    return jnp.pad(w, ((0, 0), (0, width - w.shape[1])))


def kernel(x, c, ada_w, ada_b, norm_mix, norm_ffn, ssd_w_in, ssd_conv_w, ssd_conv_b, ssd_dt_bias,
           ssd_a_log, ssd_d, ssd_norm, ssd_w_out, dsa_w_in, dsa_q_norm, dsa_k_norm, dsa_w_out,
           moe_w_group, moe_b_group, moe_w_expert, moe_b_expert, moe_w1, moe_w3, moe_w2):
    depth = ada_w.shape[0]
    xt = x.reshape(TOKENS, D_MODEL)
    mod = _modulation(c, ada_w, ada_b)
    mod5 = mod.reshape(depth, BATCH, 6, 1, D_MODEL)

    head_of_col = jnp.arange(SSD_D_INNER, dtype=jnp.int32) // SSD_HEAD_DIM
    expand = (jnp.arange(SSD_DT_PAD, dtype=jnp.int32)[:, None] == head_of_col[None, :]).astype(BF16)

    for i in range(depth):
        j = i // 2
        nw_mix = norm_mix[i].reshape(1, D_MODEL)
        if i % 2 == 0:
            n_main = SSD_D_INNER + SSD_CONV_DIM
            z, xbc, dt = _inproj(
                xt, nw_mix, mod5, i, 1, 0, ssd_w_in[j][:, :n_main].astype(BF16),
                _pad_cols(ssd_w_in[j][:, n_main:], SSD_DT_PAD).astype(BF16))
            xt = _ssd_mixer(
                z, xbc, dt, xt, ssd_conv_w[j], ssd_conv_b[j].reshape(1, SSD_CONV_DIM),
                _pad_cols(ssd_dt_bias[j].reshape(1, SSD_HEADS), SSD_DT_PAD),
                _pad_cols(ssd_a_log[j].reshape(1, SSD_HEADS), SSD_DT_PAD),
                jnp.repeat(ssd_d[j], SSD_HEAD_DIM).reshape(1, SSD_D_INNER),
                ssd_norm[j].reshape(1, SSD_D_INNER), expand, ssd_w_out[j].astype(BF16), mod5, i)
        else:
            w = dsa_w_in[j]
            w_in = jnp.concatenate(
                [_pad_cols(w[:, :DSA_KI_START + IDX_HEAD_DIM], DSA_WI_START),
                 _pad_cols(w[:, DSA_KI_START + IDX_HEAD_DIM:], 128)], axis=1).astype(BF16)
            q, k, v, qi, ki, wi = _dsa_inproj(
                xt, nw_mix, mod5, i, w_in, dsa_q_norm[j].reshape(1, ATT_HEAD_DIM),
                dsa_k_norm[j].reshape(1, ATT_HEAD_DIM))
            xt = _dsa_mixer(q, k, v, qi, ki, wi, xt, dsa_w_out[j].astype(BF16), mod5, i)

        w_route = _pad_cols(jnp.concatenate([moe_w_expert[i], moe_w_group[i]], axis=1), ROUTE_PAD).astype(BF16)
        b_route = _pad_cols(jnp.concatenate([moe_b_expert[i], moe_b_group[i]]).reshape(1, -1), ROUTE_PAD)
        xt = _moe(xt, norm_ffn[i].reshape(1, D_MODEL), mod5, i, w_route, b_route,
                  moe_w1, moe_w3, moe_w2)
    return xt.reshape(BATCH, SEQ, D_MODEL)
```

```python
import functools

import jax
import jax.numpy as jnp
from jax import lax
from jax.experimental import pallas as pl
from jax.experimental.pallas import tpu as pltpu

F32 = jnp.float32
BF16 = jnp.bfloat16

D_MODEL = 1024
BATCH = 8
SEQ = 2048
TOKENS = BATCH * SEQ
EPS = 1e-6

SSD_D_INNER = 2048
SSD_HEAD_DIM = 64
SSD_HEADS = 32
SSD_GROUPS = 8
SSD_HEADS_PER_GROUP = 4
SSD_STATE = 128
SSD_CONV = 4
SSD_CHUNK = 128
SSD_GN = SSD_GROUPS * SSD_STATE
SSD_CONV_DIM = SSD_D_INNER + 2 * SSD_GN
SSD_GROUP_W = SSD_HEADS_PER_GROUP * SSD_HEAD_DIM
SSD_DT_PAD = 128
CONV_HALO = 8

ATT_HEADS = 16
ATT_KV_HEADS = 4
ATT_Q_PER_KV = 4
ATT_HEAD_DIM = 64
IDX_HEADS = 8
IDX_HEAD_DIM = 64
TOPK = 256
Q_BLOCK = 256
DSA_Q = ATT_HEADS * ATT_HEAD_DIM
DSA_KV = ATT_KV_HEADS * ATT_HEAD_DIM
DSA_QI = IDX_HEADS * IDX_HEAD_DIM
DSA_KI_START = DSA_Q + 2 * DSA_KV + DSA_QI
DSA_WI_START = DSA_KI_START + 128
DSA_PROJ_PAD = DSA_WI_START + 128

MOE_GROUPS = 4
MOE_EPG = 4
MOE_EXPERTS = 16
MOE_HIDDEN = 256
ROUTE_PAD = 128

VMEM_LIMIT = 56 * 1024 * 1024


def _sigmoid(v):
    return 1.0 / (1.0 + jnp.exp(-v))


def _silu(v):
    return v * _sigmoid(v)


def _split3(a):
    hi = a.astype(BF16)
    r = a - hi.astype(F32)
    mid = r.astype(BF16)
    lo = (r - mid.astype(F32)).astype(BF16)
    return hi, mid, lo


def _dot(a, b):
    return jnp.dot(a, b, preferred_element_type=F32)


def _dot_nt(a, b):
    return lax.dot_general(a, b, (((1,), (1,)), ((), ())), preferred_element_type=F32)


def _dot3_exact_rhs(a, m):
    hi, mid, lo = _split3(a)
    return _dot(hi, m) + _dot(mid, m) + _dot(lo, m)


def _dot3_exact_lhs(m, a):
    hi, mid, lo = _split3(a)
    return _dot(m, hi) + _dot(m, mid) + _dot(m, lo)


def _norm_mod(x, nw, scale, shift):
    ms = jnp.mean(x * x, axis=-1, keepdims=True)
    return x * lax.rsqrt(ms + EPS) * nw * (1.0 + scale) + shift


MOD_TN = 1536


def _mod_kernel(c_ref, w_ref, b_ref, o_ref):
    cond = _silu(c_ref[...]).astype(BF16)
    o_ref[...] = _dot(cond, w_ref[...].astype(BF16)) + b_ref[...]


def _modulation(c, ada_w, ada_b):
    depth = ada_w.shape[0]
    n = ada_w.shape[2]
    return pl.pallas_call(
        _mod_kernel,
        grid=(depth, n // MOD_TN),
        in_specs=[
            pl.BlockSpec((BATCH, D_MODEL), lambda i, j: (0, 0)),
            pl.BlockSpec((None, D_MODEL, MOD_TN), lambda i, j: (i, 0, j)),
            pl.BlockSpec((None, 1, MOD_TN), lambda i, j: (i, 0, j)),
        ],
        out_specs=pl.BlockSpec((None, BATCH, MOD_TN), lambda i, j: (i, 0, j)),
        out_shape=jax.ShapeDtypeStruct((depth, BATCH, n), F32),
        compiler_params=pltpu.CompilerParams(
            dimension_semantics=("arbitrary", "arbitrary"), vmem_limit_bytes=VMEM_LIMIT),
        name="adaln_mod",
    )(c, ada_w, ada_b.reshape(depth, 1, n))


def _mod_spec(layer, chunk, rows_per_batch_tile):
    return pl.BlockSpec((None, None, None, 1, D_MODEL),
                        lambda i, *_: (layer, i // rows_per_batch_tile, chunk, 0, 0))


INPROJ_TM = 256
DSA_INPROJ_TM = 512


def _inproj_kernel(x_ref, nw_ref, scale_ref, shift_ref, w_ref, z_ref, xbc_ref, dt_ref):
    h = _norm_mod(x_ref[...], nw_ref[...], scale_ref[...], shift_ref[...]).astype(BF16)
    n_main = SSD_D_INNER + SSD_CONV_DIM
    z_ref[...] = _silu(_dot(h, w_ref[:, 0:SSD_D_INNER]))
    xbc_ref[...] = _dot(h, w_ref[:, SSD_D_INNER:n_main])
    dt_ref[:, 0:SSD_HEADS] = _dot(h, w_ref[:, n_main:n_main + SSD_HEADS])
    dt_ref[:, SSD_HEADS:SSD_DT_PAD] = jnp.zeros((dt_ref.shape[0], SSD_DT_PAD - SSD_HEADS), F32)


def _inproj(x, nw, mod5, layer, scale_chunk, shift_chunk, w):
    tm = INPROJ_TM
    tiles_per_batch = SEQ // tm
    widths = (SSD_D_INNER, SSD_CONV_DIM, SSD_DT_PAD)
    return pl.pallas_call(
        _inproj_kernel,
        grid=(TOKENS // tm,),
        in_specs=[
            pl.BlockSpec((tm, D_MODEL), lambda i: (i, 0)),
            pl.BlockSpec((1, D_MODEL), lambda i: (0, 0)),
            _mod_spec(layer, scale_chunk, tiles_per_batch),
            _mod_spec(layer, shift_chunk, tiles_per_batch),
            pl.BlockSpec((D_MODEL, SSD_D_INNER + SSD_CONV_DIM + SSD_HEADS), lambda i: (0, 0)),
        ],
        out_specs=[pl.BlockSpec((tm, width), lambda i: (i, 0)) for width in widths],
        out_shape=[jax.ShapeDtypeStruct((TOKENS, width), F32) for width in widths],
        compiler_params=pltpu.CompilerParams(
            dimension_semantics=("arbitrary",), vmem_limit_bytes=VMEM_LIMIT),
        name="norm_inproj",
    )(x, nw, mod5, mod5, w)


CONV_COLS = 512


def _ssd_kernel(z_ref, xbc_ref, dt_ref, xres_ref, cw_ref, cb_ref, dtb_ref, alog_ref, de_ref, nw_ref,
                e_ref, wout_ref, gate_ref, o_ref, state_ref, ext_ref, act_ref, yn_ref):
    q = SSD_CHUNK
    c = pl.program_id(1)

    @pl.when(c == 0)
    def _():
        state_ref[...] = jnp.zeros_like(state_ref)
        ext_ref[0:CONV_HALO, :] = jnp.zeros((CONV_HALO, SSD_CONV_DIM), F32)

    @pl.when(c > 0)
    def _():
        ext_ref[0:CONV_HALO, :] = ext_ref[q:q + CONV_HALO, :]

    ext_ref[CONV_HALO:CONV_HALO + q, :] = xbc_ref[...]

    for s in range(SSD_CONV_DIM // CONV_COLS):
        cs = slice(s * CONV_COLS, (s + 1) * CONV_COLS)
        u = ext_ref[:, cs]
        acc = cw_ref[0:1, cs] * u
        for k in range(1, SSD_CONV):
            acc = pltpu.roll(acc, 1, axis=0) + cw_ref[k:k + 1, cs] * u
        act_ref[:, cs] = _silu(acc[CONV_HALO:CONV_HALO + q, :] + cb_ref[:, cs])

    dt_raw = dt_ref[...] + dtb_ref[...]
    dt = jnp.maximum(dt_raw, 0.0) + jnp.log1p(jnp.exp(-jnp.abs(dt_raw)))
    a = dt * (-jnp.exp(alog_ref[...]) * LOG2E)
    row = lax.broadcasted_iota(jnp.int32, (q, q), 0)
    col = lax.broadcasted_iota(jnp.int32, (q, q), 1)
    tril = row >= col
    acs = _dot3_exact_lhs(tril.astype(BF16), a)
    acs_t = acs.T
    expand = e_ref[...]
    acs_e = _dot3_exact_rhs(acs, expand)
    dt_e = _dot3_exact_rhs(dt, expand)
    tot_e = acs_e[q - 1:q, :]
    decay_from_start = jnp.exp2(acs_e)
    decay_to_end = jnp.exp2(tot_e - acs_e)
    chunk_decay = jnp.exp2(tot_e)

    lane_head = lax.broadcasted_iota(jnp.int32, (q, SSD_GROUP_W), 1) // SSD_HEAD_DIM
    for g in range(SSD_GROUPS):
        xs = act_ref[:, g * SSD_GROUP_W:(g + 1) * SSD_GROUP_W]
        gs = slice(g * SSD_GROUP_W, (g + 1) * SSD_GROUP_W)
        bm = act_ref[:, SSD_D_INNER + g * SSD_STATE:SSD_D_INNER + (g + 1) * SSD_STATE]
        cm = act_ref[:, SSD_D_INNER + SSD_GN + g * SSD_STATE:SSD_D_INNER + SSD_GN + (g + 1) * SSD_STATE]
        bm_t = bm.T.astype(BF16)
        cm_b = cm.astype(BF16)
        cb = _dot(cm_b, bm_t)
        xd = xs * dt_e[:, gs]
        ms = []
        xds = []
        for j in range(SSD_HEADS_PER_GROUP):
            h = g * SSD_HEADS_PER_GROUP + j
            seg = acs[:, h:h + 1] - acs_t[h:h + 1, :]
            dec = jnp.exp2(jnp.where(tril, seg, -jnp.inf))
            ms.append((cb * dec).astype(BF16))
            xds.append(jnp.where(lane_head == j, xd, 0.0).astype(BF16))
        y_diag = _dot(jnp.concatenate(ms, axis=1), jnp.concatenate(xds, axis=0))
        prev = state_ref[g]
        y_off = _dot(cm_b, prev.astype(BF16)) * decay_from_start[:, gs]
        state_ref[g] = prev * chunk_decay[:, gs] + _dot(bm_t, (xd * decay_to_end[:, gs]).astype(BF16))
        y = y_diag + y_off + xs * de_ref[:, gs]
        y = y * z_ref[:, gs]
        y = y * lax.rsqrt(jnp.mean(y * y, axis=-1, keepdims=True) + EPS) * nw_ref[:, gs]
        yn_ref[:, gs] = y.astype(BF16)

    out = _dot(yn_ref[...], wout_ref[...])
    o_ref[...] = xres_ref[...] + gate_ref[...] * out


def _ssd_mixer(z, xbc, dt, x, conv_w, conv_b, dt_bias, a_log, d_e, norm_w, expand, w_out, mod5, layer):
    q = SSD_CHUNK
    nc = SEQ // q
    tok = lambda w: pl.BlockSpec((q, w), lambda b, c: (b * nc + c, 0))
    full = lambda r, w: pl.BlockSpec((r, w), lambda b, c: (0, 0))
    return pl.pallas_call(
        _ssd_kernel,
        grid=(BATCH, nc),
        in_specs=[
            tok(SSD_D_INNER), tok(SSD_CONV_DIM), tok(SSD_DT_PAD), tok(D_MODEL),
            full(SSD_CONV, SSD_CONV_DIM), full(1, SSD_CONV_DIM), full(1, SSD_DT_PAD), full(1, SSD_DT_PAD),
            full(1, SSD_D_INNER), full(1, SSD_D_INNER), full(SSD_DT_PAD, SSD_D_INNER),
            full(SSD_D_INNER, D_MODEL),
            pl.BlockSpec((None, None, None, 1, D_MODEL), lambda b, c: (layer, b, 2, 0, 0)),
        ],
        out_specs=tok(D_MODEL),
        out_shape=jax.ShapeDtypeStruct((TOKENS, D_MODEL), F32),
        scratch_shapes=[
            pltpu.VMEM((SSD_GROUPS, SSD_STATE, SSD_GROUP_W), F32),
            pltpu.VMEM((q + CONV_HALO, SSD_CONV_DIM), F32),
            pltpu.VMEM((q, SSD_CONV_DIM), F32),
            pltpu.VMEM((q, SSD_D_INNER), BF16),
        ],
        compiler_params=pltpu.CompilerParams(
            dimension_semantics=("arbitrary", "arbitrary"), vmem_limit_bytes=VMEM_LIMIT),
        name="ssd_mixer",
    )(z, xbc, dt, x, conv_w, conv_b, dt_bias, a_log, d_e, norm_w, expand, w_out, mod5)


DSA_KEY_TILE = 256
DSA_CLASSES = 8
V_EXT = 2 * ATT_HEAD_DIM
DSA_BLOCKS_PER_CLASS = (SEQ // Q_BLOCK) // DSA_CLASSES
N_BISECT = 12
F32_MIN = float(jnp.finfo(jnp.float32).min)
LOG2E = 1.4426950408889634


def _count(mask):
    return jnp.sum(jnp.where(mask, 1.0, 0.0), axis=-1, keepdims=True)


def _select_topk(score_ref, q_pos, n_keys):
    kf = float(TOPK)
    small = (q_pos + 1) <= TOPK
    sc = score_ref[...]
    hi0 = jnp.max(sc, axis=-1, keepdims=True)
    lo0 = jnp.min(jnp.where(sc == -jnp.inf, jnp.inf, sc), axis=-1, keepdims=True)

    def bisect(_, carry):
        lo, hi = carry
        mid = lo + 0.5 * (hi - lo)
        ok = _count(score_ref[...] >= mid) >= kf
        return jnp.where(ok, mid, lo), jnp.where(ok, hi, mid)

    _, hi = lax.fori_loop(0, N_BISECT, bisect, (lo0, hi0))

    v0 = jnp.max(jnp.where(sc <= hi, sc, -jnp.inf), axis=-1, keepdims=True)
    c0 = _count(sc >= v0)
    pend0 = jnp.where((c0 >= kf) | small, 0.0, 1.0)

    def walk_cond(carry):
        return (carry[3] > 0.0) & (carry[4] < n_keys)

    def walk(carry):
        v, c, pend, _, it = carry
        s = score_ref[...]
        v2 = jnp.max(jnp.where(s < v, s, -jnp.inf), axis=-1, keepdims=True)
        c2 = _count(s >= v2)
        v = jnp.where(pend > 0.0, v2, v)
        c = jnp.where(pend > 0.0, c2, c)
        pend = jnp.where(c2 >= kf, 0.0, pend)
        return v, c, pend, jnp.max(pend), it + 1

    v, c, _, _, _ = lax.while_loop(walk_cond, walk, (v0, c0, pend0, jnp.max(pend0), jnp.int32(0)))
    thr = jnp.where(small, F32_MIN, v)
    any_tie = jnp.max(jnp.where(small, 0.0, c - kf)) > 0.0

    @pl.when(jnp.logical_not(any_tie))
    def _():
        score_ref[...] = jnp.where(score_ref[...] >= thr, 0.0, -jnp.inf)

    @pl.when(any_tie)
    def _():
        s = score_ref[...]
        key_pos = lax.broadcasted_iota(jnp.int32, (Q_BLOCK, n_keys), 1)
        gt = s > thr
        eq = s == thr
        need = kf - _count(gt)

        def body(_, carry):
            lo, hi = carry
            mid = (lo + hi) >> 1
            ok = _count((score_ref[...] == thr) & (key_pos <= mid)) >= need
            return jnp.where(ok, lo, mid), jnp.where(ok, mid, hi)

        init = (jnp.full((Q_BLOCK, 1), -1, jnp.int32), jnp.full((Q_BLOCK, 1), n_keys - 1, jnp.int32))
        cut = lax.fori_loop(0, (n_keys - 1).bit_length() + 1, body, init)[1]
        score_ref[...] = jnp.where(gt | (eq & (key_pos <= cut)), 0.0, -jnp.inf)


def _dsa_inproj_kernel(x_ref, nw_ref, scale_ref, shift_ref, w_ref, qn_ref, kn_ref, seg_ref, segt_ref,
                       q_ref, k_ref, v_ref, qi_ref, ki_ref, wi_ref):
    hd = ATT_HEAD_DIM
    h = _norm_mod(x_ref[...], nw_ref[...], scale_ref[...], shift_ref[...]).astype(BF16)

    def head_norm(t, w):
        width = t.shape[1]
        ss = _dot((t * t).astype(BF16), seg_ref[0:width, :])
        r = lax.rsqrt(ss * (1.0 / hd) + EPS)
        r_hi = r.astype(BF16)
        r_lo = (r - r_hi.astype(F32)).astype(BF16)
        return t * (_dot(r_hi, segt_ref[:, 0:width]) + _dot(r_lo, segt_ref[:, 0:width])) * w

    q = head_norm(_dot(h, w_ref[:, 0:DSA_Q]), qn_ref[...] * (hd ** -0.5 * LOG2E))
    for n in range(ATT_HEADS):
        q_ref[n] = q[:, n * hd:(n + 1) * hd].astype(BF16)
    kv = _dot(h, w_ref[:, DSA_Q:DSA_Q + 2 * DSA_KV])
    k = head_norm(kv[:, 0:DSA_KV], kn_ref[...])
    for n in range(ATT_KV_HEADS):
        k_ref[n] = k[:, n * hd:(n + 1) * hd].astype(BF16)
        v_ref[n] = jnp.concatenate([kv[:, DSA_KV + n * hd:DSA_KV + (n + 1) * hd],
                                    jnp.ones((kv.shape[0], V_EXT - hd), F32)], axis=1).astype(BF16)
    qi = _dot(h, w_ref[:, DSA_Q + 2 * DSA_KV:DSA_KI_START])
    for n in range(IDX_HEADS):
        qi_ref[n] = qi[:, n * IDX_HEAD_DIM:(n + 1) * IDX_HEAD_DIM].astype(BF16)
    ki_ref[...] = _dot(h, w_ref[:, DSA_KI_START:DSA_KI_START + IDX_HEAD_DIM]).astype(BF16)
    wi_ref[...] = _dot(h, w_ref[:, DSA_WI_START:DSA_WI_START + IDX_HEADS]) * ((IDX_HEADS * IDX_HEAD_DIM) ** -0.5)


def _dsa_inproj(x, nw, mod5, layer, w, q_norm, k_norm):
    tm = DSA_INPROJ_TM
    tiles_per_batch = SEQ // tm
    heads = lambda n: pl.BlockSpec((n, tm, ATT_HEAD_DIM), lambda i: (0, i, 0))
    head_of = jnp.arange(DSA_Q, dtype=jnp.int32) // ATT_HEAD_DIM
    seg = (head_of[:, None] == jnp.arange(128, dtype=jnp.int32)[None, :]).astype(BF16)
    q_norm = jnp.tile(q_norm, (1, ATT_HEADS))
    k_norm = jnp.tile(k_norm, (1, ATT_KV_HEADS))
    return pl.pallas_call(
        _dsa_inproj_kernel,
        grid=(TOKENS // tm,),
        in_specs=[
            pl.BlockSpec((tm, D_MODEL), lambda i: (i, 0)),
            pl.BlockSpec((1, D_MODEL), lambda i: (0, 0)),
            _mod_spec(layer, 1, tiles_per_batch),
            _mod_spec(layer, 0, tiles_per_batch),
            pl.BlockSpec((D_MODEL, DSA_PROJ_PAD), lambda i: (0, 0)),
            pl.BlockSpec((1, DSA_Q), lambda i: (0, 0)),
            pl.BlockSpec((1, DSA_KV), lambda i: (0, 0)),
            pl.BlockSpec((DSA_Q, 128), lambda i: (0, 0)),
            pl.BlockSpec((128, DSA_Q), lambda i: (0, 0)),
        ],
        out_specs=[heads(ATT_HEADS), heads(ATT_KV_HEADS),
                   pl.BlockSpec((ATT_KV_HEADS, tm, V_EXT), lambda i: (0, i, 0)), heads(IDX_HEADS),
                   pl.BlockSpec((tm, IDX_HEAD_DIM), lambda i: (i, 0)),
                   pl.BlockSpec((tm, IDX_HEADS), lambda i: (i, 0))],
        out_shape=[jax.ShapeDtypeStruct((ATT_HEADS, TOKENS, ATT_HEAD_DIM), BF16),
                   jax.ShapeDtypeStruct((ATT_KV_HEADS, TOKENS, ATT_HEAD_DIM), BF16),
                   jax.ShapeDtypeStruct((ATT_KV_HEADS, TOKENS, V_EXT), BF16),
                   jax.ShapeDtypeStruct((IDX_HEADS, TOKENS, IDX_HEAD_DIM), BF16),
                   jax.ShapeDtypeStruct((TOKENS, IDX_HEAD_DIM), BF16),
                   jax.ShapeDtypeStruct((TOKENS, IDX_HEADS), F32)],
        compiler_params=pltpu.CompilerParams(
            dimension_semantics=("arbitrary",), vmem_limit_bytes=VMEM_LIMIT),
        name="dsa_inproj",
    )(x, nw, mod5, mod5, w, q_norm, k_norm, seg, seg.T)


def _dsa_kernel(q_ref, k_ref, v_ref, qi_ref, ki_ref, wi_ref, xres_ref, wout_ref, gate_ref, o_ref,
                score_ref, ocat_ref, *, n_keys, first_block):
    hd = ATT_HEAD_DIM
    q_pos = (first_block + pl.program_id(1)) * Q_BLOCK + lax.broadcasted_iota(jnp.int32, (Q_BLOCK, 1), 0)

    wi = wi_ref[...]
    qi = qi_ref[...].reshape(IDX_HEADS * Q_BLOCK, IDX_HEAD_DIM)
    for kt in range(n_keys // DSA_KEY_TILE):
        ks = slice(kt * DSA_KEY_TILE, (kt + 1) * DSA_KEY_TILE)
        raw = _dot_nt(qi, ki_ref[ks, :])
        acc = jnp.zeros((Q_BLOCK, DSA_KEY_TILE), F32)
        for n in range(IDX_HEADS):
            acc = acc + wi[:, n:n + 1] * jnp.maximum(raw[n * Q_BLOCK:(n + 1) * Q_BLOCK, :], 0.0)
        key_pos = kt * DSA_KEY_TILE + lax.broadcasted_iota(jnp.int32, (Q_BLOCK, DSA_KEY_TILE), 1)
        score_ref[:, ks] = jnp.where(key_pos <= q_pos, acc, -jnp.inf)

    if n_keys > TOPK:
        _select_topk(score_ref, q_pos, n_keys)
        bias = score_ref[...][None, :, :]
    else:
        bias = jnp.where(score_ref[...] == -jnp.inf, -jnp.inf, 0.0)[None, :, :]

    for n in range(ATT_KV_HEADS):
        q4 = q_ref[n * ATT_Q_PER_KV:(n + 1) * ATT_Q_PER_KV].reshape(ATT_Q_PER_KV * Q_BLOCK, hd)
        s = _dot_nt(q4, k_ref[n]).reshape(ATT_Q_PER_KV, Q_BLOCK, n_keys) + bias
        p = jnp.exp2(s - jnp.max(s, axis=-1, keepdims=True))
        o = _dot(p.reshape(ATT_Q_PER_KV * Q_BLOCK, n_keys).astype(BF16), v_ref[n])
        o = o[:, 0:hd] * (1.0 / o[:, hd:hd + 1])
        for g in range(ATT_Q_PER_KV):
            col = (n * ATT_Q_PER_KV + g) * hd
            ocat_ref[:, col:col + hd] = o[g * Q_BLOCK:(g + 1) * Q_BLOCK, :]
    out = _dot(ocat_ref[...].astype(BF16), wout_ref[...])
    o_ref[...] = xres_ref[...] + gate_ref[...] * out


def _dsa_mixer(q, k, v, qi, ki, wi, x, w_out, mod5, layer):
    nb = SEQ // Q_BLOCK
    k4 = k.reshape(ATT_KV_HEADS, BATCH, SEQ, ATT_HEAD_DIM)
    v4 = v.reshape(ATT_KV_HEADS, BATCH, SEQ, V_EXT)
    ki3 = ki.reshape(BATCH, SEQ, IDX_HEAD_DIM)
    for cls in range(DSA_CLASSES):
        n_keys = (cls + 1) * (SEQ // DSA_CLASSES)
        first_block = cls * DSA_BLOCKS_PER_CLASS
        row = lambda b, i, fb=first_block: b * nb + fb + i
        heads = lambda n: pl.BlockSpec((n, Q_BLOCK, ATT_HEAD_DIM), lambda b, i: (0, row(b, i), 0))
        keys = lambda width: pl.BlockSpec((ATT_KV_HEADS, None, n_keys, width), lambda b, i: (0, b, 0, 0))
        x = pl.pallas_call(
            functools.partial(_dsa_kernel, n_keys=n_keys, first_block=first_block),
            grid=(BATCH, DSA_BLOCKS_PER_CLASS),
            in_specs=[
                heads(ATT_HEADS), keys(ATT_HEAD_DIM), keys(V_EXT), heads(IDX_HEADS),
                pl.BlockSpec((None, n_keys, IDX_HEAD_DIM), lambda b, i: (b, 0, 0)),
                pl.BlockSpec((Q_BLOCK, IDX_HEADS), lambda b, i: (row(b, i), 0)),
                pl.BlockSpec((Q_BLOCK, D_MODEL), lambda b, i: (row(b, i), 0)),
                pl.BlockSpec((DSA_Q, D_MODEL), lambda b, i: (0, 0)),
                pl.BlockSpec((None, None, None, 1, D_MODEL), lambda b, i: (layer, b, 2, 0, 0)),
            ],
            out_specs=pl.BlockSpec((Q_BLOCK, D_MODEL), lambda b, i: (row(b, i), 0)),
            out_shape=jax.ShapeDtypeStruct((TOKENS, D_MODEL), F32),
            scratch_shapes=[
                pltpu.VMEM((Q_BLOCK, n_keys), F32),
                pltpu.VMEM((Q_BLOCK, D_MODEL), F32),
            ],
            input_output_aliases={6: 0},
            compiler_params=pltpu.CompilerParams(
                dimension_semantics=("arbitrary", "arbitrary"), vmem_limit_bytes=VMEM_LIMIT),
            name=f"dsa_mixer_c{cls}",
        )(q, k4, v4, qi, ki3, wi, x, w_out, mod5)
    return x


MOE_PAIRS = MOE_EPG * (MOE_EPG - 1) // 2
MOE_CLASSES = MOE_GROUPS * MOE_PAIRS
PAIR_LO = (0, 0, 0, 1, 1, 2)
PAIR_HI = (1, 2, 3, 2, 3, 3)
ROUTE_TM = 1024
META_W = 128
META_CLASS, META_RANK, META_WLO, META_WHI = 0, 1, 2, 3
TILE_ROWS, TILE_LANES = 8, 128
H_WORDS = D_MODEL // 2
H_SUBLANES = H_WORDS // TILE_LANES
SORT_BLOCK = 256
N_SORT_BLOCKS = TOKENS // SORT_BLOCK
MAX_ITEMS = N_SORT_BLOCKS + MOE_CLASSES
PERMUTE_TM = 512
COMBINE_TM = 512


def _route_kernel(x_ref, nw_ref, scale_ref, shift_ref, wrt_ref, brt_ref, tri_ref, pay_ref, cls_ref, rank_ref,
                  cnt_ref, carry_ref):
    tm = ROUTE_TM

    @pl.when(pl.program_id(0) == 0)
    def _():
        carry_ref[...] = jnp.zeros_like(carry_ref)

    h = _norm_mod(x_ref[...], nw_ref[...], scale_ref[...], shift_ref[...])
    logits = _dot_nt(wrt_ref[...], h.astype(BF16)) + brt_ref[...]
    sub = lax.broadcasted_iota(jnp.int32, logits.shape, 0)
    neg = -jnp.inf
    big = jnp.int32(ROUTE_PAD)
    is_group = (sub >= MOE_EXPERTS) & (sub < MOE_EXPERTS + MOE_GROUPS)
    gl = jnp.where(is_group, logits, neg)
    g_max = jnp.max(gl, axis=0, keepdims=True)
    g_idx = jnp.min(jnp.where(gl == g_max, sub - MOE_EXPERTS, big), axis=0, keepdims=True)
    g_val = 1.0 / jnp.sum(jnp.exp(gl - g_max), axis=0, keepdims=True)
    in_group = (sub < MOE_EXPERTS) & ((sub // MOE_EPG) == g_idx)
    el = jnp.where(in_group, logits, neg)
    m1 = jnp.max(el, axis=0, keepdims=True)
    i1 = jnp.min(jnp.where(el == m1, sub, big), axis=0, keepdims=True)
    el2 = jnp.where(sub == i1, neg, el)
    m2 = jnp.max(el2, axis=0, keepdims=True)
    i2 = jnp.min(jnp.where(el2 == m2, sub, big), axis=0, keepdims=True)
    r = jnp.exp(m2 - m1)
    w_top1 = g_val / (1.0 + r)
    w_top2 = g_val * r / (1.0 + r)

    lo = jnp.minimum(i1, i2) - g_idx * MOE_EPG
    hi = jnp.maximum(i1, i2) - g_idx * MOE_EPG
    pair = (lo * (2 * MOE_EPG - 1 - lo)) // 2 + (hi - lo - 1)
    cls = g_idx * MOE_PAIRS + pair
    w_lo = jnp.where(i1 < i2, w_top1, w_top2)
    w_hi = jnp.where(i1 < i2, w_top2, w_top1)

    onehot = sub == cls
    before = _dot(onehot.astype(BF16), tri_ref[...]) + carry_ref[...]
    rank = jnp.sum(jnp.where(onehot, before, 0.0), axis=0, keepdims=True)
    carry_ref[...] += jnp.sum(jnp.where(onehot, 1.0, 0.0), axis=1, keepdims=True)
    cnt_ref[...] = carry_ref[...]
    cls_ref[...] = cls
    rank_ref[...] = rank.astype(jnp.int32)

    words = pltpu.pack_elementwise([h[:, 0:H_WORDS], h[:, H_WORDS:D_MODEL]], packed_dtype=BF16)
    for s in range(H_SUBLANES):
        pay_ref[pl.ds(s, tm, stride=TILE_ROWS), :] = words[:, s * TILE_LANES:(s + 1) * TILE_LANES]
    sub8 = lax.broadcasted_iota(jnp.int32, (TILE_ROWS, tm), 0)
    rec = jnp.where(sub8 == META_CLASS, cls.astype(F32),
                    jnp.where(sub8 == META_RANK, rank,
                              jnp.where(sub8 == META_WLO, w_lo, jnp.where(sub8 == META_WHI, w_hi, 0.0))))
    rec = jnp.concatenate([rec, jnp.zeros((META_W - TILE_ROWS, tm), F32)], axis=0)
    for b in range(tm // META_W):
        meta = rec[:, b * META_W:(b + 1) * META_W].T
        pay_ref[pl.ds(b * META_W * TILE_ROWS + H_SUBLANES, META_W, stride=TILE_ROWS), :] = (
            lax.bitcast_convert_type(meta, jnp.int32))
    for s in range(H_SUBLANES + 1, TILE_ROWS):
        pay_ref[pl.ds(s, tm, stride=TILE_ROWS), :] = jnp.zeros((tm, TILE_LANES), jnp.int32)


def _moe_route(x, nw, mod5, layer, w_route, b_route):
    tm = ROUTE_TM
    tiles_per_batch = SEQ // tm
    tri = (jnp.arange(tm, dtype=jnp.int32)[:, None] < jnp.arange(tm, dtype=jnp.int32)[None, :]).astype(BF16)
    return pl.pallas_call(
        _route_kernel,
        grid=(TOKENS // tm,),
        in_specs=[
            pl.BlockSpec((tm, D_MODEL), lambda i: (i, 0)),
            pl.BlockSpec((1, D_MODEL), lambda i: (0, 0)),
            _mod_spec(layer, 4, tiles_per_batch),
            _mod_spec(layer, 3, tiles_per_batch),
            pl.BlockSpec((ROUTE_PAD, D_MODEL), lambda i: (0, 0)),
            pl.BlockSpec((ROUTE_PAD, 1), lambda i: (0, 0)),
            pl.BlockSpec((tm, tm), lambda i: (0, 0)),
        ],
        out_specs=[pl.BlockSpec((tm * TILE_ROWS, TILE_LANES), lambda i: (i, 0)),
                   pl.BlockSpec((1, tm), lambda i: (0, i)),
                   pl.BlockSpec((1, tm), lambda i: (0, i)),
                   pl.BlockSpec((ROUTE_PAD, 1), lambda i: (0, 0))],
        out_shape=[jax.ShapeDtypeStruct((TOKENS * TILE_ROWS, TILE_LANES), jnp.int32),
                   jax.ShapeDtypeStruct((1, TOKENS), jnp.int32),
                   jax.ShapeDtypeStruct((1, TOKENS), jnp.int32),
                   jax.ShapeDtypeStruct((ROUTE_PAD, 1), F32)],
        scratch_shapes=[pltpu.VMEM((ROUTE_PAD, 1), F32)],
        compiler_params=pltpu.CompilerParams(
            dimension_semantics=("arbitrary",), vmem_limit_bytes=VMEM_LIMIT),
        name="moe_route",
    )(x, nw, mod5, mod5, w_route.T, b_route.reshape(ROUTE_PAD, 1), tri)


def _permute_kernel(pos_ref, src_ref, dst_hbm, stage_ref, sem):
    i = pl.program_id(0)
    slot = i % 2
    rows = PERMUTE_TM * TILE_ROWS

    def slot_wait(s):
        whole = stage_ref.at[pl.ds(s * rows, rows), :]
        pltpu.make_async_copy(whole, whole, sem.at[s]).wait()

    @pl.when(i >= 2)
    def _():
        slot_wait(slot)

    base = pl.multiple_of(slot * rows, rows)
    stage_ref[pl.ds(base, rows), :] = src_ref[...]

    def issue(r, _):
        pltpu.make_async_copy(stage_ref.at[pl.ds(base + r * TILE_ROWS, TILE_ROWS), :],
                              dst_hbm.at[pos_ref[i * PERMUTE_TM + r]], sem.at[slot]).start()
        return 0

    lax.fori_loop(0, PERMUTE_TM, issue, 0, unroll=16)

    @pl.when(i == pl.num_programs(0) - 1)
    def _():
        slot_wait(1 - slot)
        slot_wait(slot)


def _moe_permute(pos, payload):
    rows = PERMUTE_TM * TILE_ROWS
    return pl.pallas_call(
        _permute_kernel,
        grid_spec=pltpu.PrefetchScalarGridSpec(
            num_scalar_prefetch=1, grid=(TOKENS // PERMUTE_TM,),
            in_specs=[pl.BlockSpec((rows, TILE_LANES), lambda i, pos: (i, 0))],
            out_specs=pl.BlockSpec(memory_space=pl.ANY),
            scratch_shapes=[pltpu.VMEM((2 * rows, TILE_LANES), jnp.int32), pltpu.SemaphoreType.DMA((2,))]),
        out_shape=jax.ShapeDtypeStruct((TOKENS, TILE_ROWS, TILE_LANES), jnp.int32),
        compiler_params=pltpu.CompilerParams(
            dimension_semantics=("arbitrary",), vmem_limit_bytes=VMEM_LIMIT),
        name="moe_permute",
    )(pos, payload)


def _experts_kernel(blk_ref, cls_ref, elo_ref, ehi_ref, first_ref, last_ref, valid_ref,
                    pay_ref, w1lo_ref, w3lo_ref, w2lo_ref, w1hi_ref, w3hi_ref, w2hi_ref, o_ref, acc_ref):
    k = pl.program_id(0)
    rows = SORT_BLOCK

    @pl.when(valid_ref[k] == 1)
    def _():
        def sublane(s):
            return pay_ref[pl.ds(s, rows, stride=TILE_ROWS), :]

        halves = [[pltpu.unpack_elementwise(sublane(s), index=i, packed_dtype=BF16, unpacked_dtype=F32)
                   for s in range(H_SUBLANES)] for i in range(2)]
        hb = jnp.concatenate(halves[0] + halves[1], axis=1).astype(BF16)
        meta = lax.bitcast_convert_type(sublane(H_SUBLANES), F32)
        mine = meta[:, META_CLASS:META_CLASS + 1] == cls_ref[k].astype(F32)
        w_lo = jnp.where(mine, meta[:, META_WLO:META_WLO + 1], 0.0)
        w_hi = jnp.where(mine, meta[:, META_WHI:META_WHI + 1], 0.0)
        bf = lambda w_ref: w_ref[...].astype(BF16)
        hid_lo = _silu(_dot(hb, bf(w1lo_ref))) * _dot(hb, bf(w3lo_ref)) * w_lo
        hid_hi = _silu(_dot(hb, bf(w1hi_ref))) * _dot(hb, bf(w3hi_ref)) * w_hi
        y = _dot(hid_lo.astype(BF16), bf(w2lo_ref)) + _dot(hid_hi.astype(BF16), bf(w2hi_ref))

        @pl.when(first_ref[k] == 1)
        def _():
            acc_ref[...] = y

        @pl.when(first_ref[k] == 0)
        def _():
            acc_ref[...] += y

        @pl.when(last_ref[k] == 1)
        def _():
            for s in range(TILE_ROWS):
                o_ref[pl.ds(s, rows, stride=TILE_ROWS), :] = acc_ref[:, s * TILE_LANES:(s + 1) * TILE_LANES]


def _moe_experts(items, payload_sorted, layer, w1, w3, w2):
    blk, cls, elo, ehi, first, last, valid = items
    tiles = pl.BlockSpec((SORT_BLOCK * TILE_ROWS, TILE_LANES), lambda k, blk, *_: (blk[k], 0))
    w_in = lambda which: pl.BlockSpec(
        (None, None, D_MODEL, MOE_HIDDEN), lambda k, blk, cls, elo, ehi, *_: (layer, (elo, ehi)[which][k], 0, 0))
    w_out = lambda which: pl.BlockSpec(
        (None, None, MOE_HIDDEN, D_MODEL), lambda k, blk, cls, elo, ehi, *_: (layer, (elo, ehi)[which][k], 0, 0))
    return pl.pallas_call(
        _experts_kernel,
        grid_spec=pltpu.PrefetchScalarGridSpec(
            num_scalar_prefetch=7, grid=(MAX_ITEMS,),
            in_specs=[tiles, w_in(0), w_in(0), w_out(0), w_in(1), w_in(1), w_out(1)],
            out_specs=tiles,
            scratch_shapes=[pltpu.VMEM((SORT_BLOCK, D_MODEL), F32)]),
        out_shape=jax.ShapeDtypeStruct((TOKENS * TILE_ROWS, TILE_LANES), F32),
        compiler_params=pltpu.CompilerParams(
            dimension_semantics=("arbitrary",), vmem_limit_bytes=VMEM_LIMIT),
        name="moe_experts",
    )(blk, cls, elo, ehi, first, last, valid, payload_sorted, w1, w3, w2, w1, w3, w2)


def _combine_kernel(pos_ref, x_ref, gate_ref, y_hbm, o_ref, buf_ref, sem):
    tm = COMBINE_TM
    i = pl.program_id(0)
    n = pl.num_programs(0)

    def gather(tile, slot):
        def issue(r, _):
            pltpu.make_async_copy(y_hbm.at[pos_ref[tile * tm + r]],
                                  buf_ref.at[pl.ds((slot * tm + r) * TILE_ROWS, TILE_ROWS), :], sem.at[slot]).start()
            return 0
        lax.fori_loop(0, tm, issue, 0, unroll=16)

    @pl.when(i == 0)
    def _():
        gather(0, 0)

    @pl.when(i + 1 < n)
    def _():
        gather(i + 1, (i + 1) % 2)

    slot = i % 2
    base = slot * tm * TILE_ROWS
    whole_slot = buf_ref.at[pl.ds(base, tm * TILE_ROWS), :]
    pltpu.make_async_copy(whole_slot, whole_slot, sem.at[slot]).wait()
    y = jnp.concatenate([buf_ref[pl.ds(base + s, tm, stride=TILE_ROWS), :] for s in range(TILE_ROWS)], axis=1)
    o_ref[...] = x_ref[...] + gate_ref[...] * y


def _moe_combine(pos, x, mod5, layer, y_sorted):
    tm = COMBINE_TM
    tiles_per_batch = SEQ // tm
    return pl.pallas_call(
        _combine_kernel,
        grid_spec=pltpu.PrefetchScalarGridSpec(
            num_scalar_prefetch=1, grid=(TOKENS // tm,),
            in_specs=[
                pl.BlockSpec((tm, D_MODEL), lambda i, pos: (i, 0)),
                pl.BlockSpec((None, None, None, 1, D_MODEL),
                             lambda i, pos: (layer, i // tiles_per_batch, 5, 0, 0)),
                pl.BlockSpec(memory_space=pl.ANY),
            ],
            out_specs=pl.BlockSpec((tm, D_MODEL), lambda i, pos: (i, 0)),
            scratch_shapes=[pltpu.VMEM((2 * tm * TILE_ROWS, TILE_LANES), F32), pltpu.SemaphoreType.DMA((2,))]),
        out_shape=jax.ShapeDtypeStruct((TOKENS, D_MODEL), F32),
        compiler_params=pltpu.CompilerParams(
            dimension_semantics=("arbitrary",), vmem_limit_bytes=VMEM_LIMIT),
        name="moe_combine",
    )(pos, x, mod5, y_sorted)


def _moe_plan(cls, rank, counts):
    count = counts[:MOE_CLASSES, 0].astype(jnp.int32)
    ends = jnp.cumsum(count)
    starts = ends - count
    class_ids = jnp.arange(MOE_CLASSES, dtype=jnp.int32)

    def lookup(table, idx):
        return jnp.sum(jnp.where(idx[..., None] == class_ids, table, 0), axis=-1)

    pos = (lookup(starts, cls) + rank).reshape(TOKENS)

    first_blk = starts // SORT_BLOCK
    n_items = jnp.where(count > 0, (ends - 1) // SORT_BLOCK - first_blk + 1, 0)
    item_end = jnp.cumsum(n_items)
    item_start = item_end - n_items
    k = jnp.arange(MAX_ITEMS, dtype=jnp.int32)
    valid = k < item_end[-1]
    kc = jnp.minimum(k, item_end[-1] - 1)
    icls = jnp.sum((item_end[None, :] <= kc[:, None]).astype(jnp.int32), axis=1)
    blk = lookup(first_blk, icls) + (kc - lookup(item_start, icls))
    first = jnp.concatenate([jnp.ones((1,), jnp.int32), (blk[1:] != blk[:-1]).astype(jnp.int32)])
    last = jnp.concatenate([(blk[1:] != blk[:-1]) | ~valid[1:], jnp.ones((1,), bool)]).astype(jnp.int32)
    group = icls // MOE_PAIRS
    elo = group * MOE_EPG + lookup(jnp.asarray(PAIR_LO * MOE_GROUPS, jnp.int32), icls)
    ehi = group * MOE_EPG + lookup(jnp.asarray(PAIR_HI * MOE_GROUPS, jnp.int32), icls)
    return pos, (blk, icls, elo, ehi, first * valid, last * valid, valid.astype(jnp.int32))


def _moe(x, nw, mod5, layer, w_route, b_route, w1, w3, w2):
    payload, cls, rank, counts = _moe_route(x, nw, mod5, layer, w_route, b_route)
    pos, items = _moe_plan(cls, rank, counts)
    tiles = (TOKENS, TILE_ROWS, TILE_LANES)
    sorted_payload = _moe_permute(pos, payload).reshape(TOKENS * TILE_ROWS, TILE_LANES)
    y_sorted = _moe_experts(items, sorted_payload, layer, w1, w3, w2)
    return _moe_combine(pos, x, mod5, layer, y_sorted.reshape(tiles))


def _pad_cols(w, width):
    return jnp.pad(w, ((0, 0), (0, width - w.shape[1])))


def kernel(x, c, ada_w, ada_b, norm_mix, norm_ffn, ssd_w_in, ssd_conv_w, ssd_conv_b, ssd_dt_bias,
           ssd_a_log, ssd_d, ssd_norm, ssd_w_out, dsa_w_in, dsa_q_norm, dsa_k_norm, dsa_w_out,
           moe_w_group, moe_b_group, moe_w_expert, moe_b_expert, moe_w1, moe_w3, moe_w2):
    depth = ada_w.shape[0]
    xt = x.reshape(TOKENS, D_MODEL)
    mod = _modulation(c, ada_w, ada_b)
    mod5 = mod.reshape(depth, BATCH, 6, 1, D_MODEL)

    head_of_col = jnp.arange(SSD_D_INNER, dtype=jnp.int32) // SSD_HEAD_DIM
    expand = (jnp.arange(SSD_DT_PAD, dtype=jnp.int32)[:, None] == head_of_col[None, :]).astype(BF16)

    for i in range(depth):
        j = i // 2
        nw_mix = norm_mix[i].reshape(1, D_MODEL)
        if i % 2 == 0:
            z, xbc, dt = _inproj(xt, nw_mix, mod5, i, 1, 0, ssd_w_in[j].astype(BF16))
            xt = _ssd_mixer(
                z, xbc, dt, xt, ssd_conv_w[j], ssd_conv_b[j].reshape(1, SSD_CONV_DIM),
                _pad_cols(ssd_dt_bias[j].reshape(1, SSD_HEADS), SSD_DT_PAD),
                _pad_cols(ssd_a_log[j].reshape(1, SSD_HEADS), SSD_DT_PAD),
                jnp.repeat(ssd_d[j], SSD_HEAD_DIM).reshape(1, SSD_D_INNER),
                ssd_norm[j].reshape(1, SSD_D_INNER), expand, ssd_w_out[j].astype(BF16), mod5, i)
        else:
            w = dsa_w_in[j]
            w_in = jnp.concatenate(
                [_pad_cols(w[:, :DSA_KI_START + IDX_HEAD_DIM], DSA_WI_START),
                 _pad_cols(w[:, DSA_KI_START + IDX_HEAD_DIM:], 128)], axis=1).astype(BF16)
            q, k, v, qi, ki, wi = _dsa_inproj(
                xt, nw_mix, mod5, i, w_in, dsa_q_norm[j].reshape(1, ATT_HEAD_DIM),
                dsa_k_norm[j].reshape(1, ATT_HEAD_DIM))
            xt = _dsa_mixer(q, k, v, qi, ki, wi, xt, dsa_w_out[j].astype(BF16), mod5, i)

        w_route = _pad_cols(jnp.concatenate([moe_w_expert[i], moe_w_group[i]], axis=1), ROUTE_PAD).astype(BF16)
        b_route = _pad_cols(jnp.concatenate([moe_b_expert[i], moe_b_group[i]]).reshape(1, -1), ROUTE_PAD)
        xt = _moe(xt, norm_ffn[i].reshape(1, D_MODEL), mod5, i, w_route, b_route,
                  moe_w1, moe_w3, moe_w2)
    return xt.reshape(BATCH, SEQ, D_MODEL)
```

```python
import functools

import jax
import jax.numpy as jnp
from jax import lax
from jax.experimental import pallas as pl
from jax.experimental.pallas import tpu as pltpu

F32 = jnp.float32
BF16 = jnp.bfloat16

D_MODEL = 1024
BATCH = 8
SEQ = 2048
TOKENS = BATCH * SEQ
EPS = 1e-6

SSD_D_INNER = 2048
SSD_HEAD_DIM = 64
SSD_HEADS = 32
SSD_GROUPS = 8
SSD_HEADS_PER_GROUP = 4
SSD_STATE = 128
SSD_CONV = 4
SSD_CHUNK = 128
SSD_GN = SSD_GROUPS * SSD_STATE
SSD_CONV_DIM = SSD_D_INNER + 2 * SSD_GN
SSD_GROUP_W = SSD_HEADS_PER_GROUP * SSD_HEAD_DIM
SSD_DT_PAD = 128
CONV_HALO = 8

ATT_HEADS = 16
ATT_KV_HEADS = 4
ATT_Q_PER_KV = 4
ATT_HEAD_DIM = 64
IDX_HEADS = 8
IDX_HEAD_DIM = 64
TOPK = 256
Q_BLOCK = 256
DSA_Q = ATT_HEADS * ATT_HEAD_DIM
DSA_KV = ATT_KV_HEADS * ATT_HEAD_DIM
DSA_QI = IDX_HEADS * IDX_HEAD_DIM
DSA_KI_START = DSA_Q + 2 * DSA_KV + DSA_QI
DSA_WI_START = DSA_KI_START + 128
DSA_PROJ_PAD = DSA_WI_START + 128

MOE_GROUPS = 4
MOE_EPG = 4
MOE_EXPERTS = 16
MOE_HIDDEN = 256
ROUTE_PAD = 128

VMEM_LIMIT = 56 * 1024 * 1024


def _sigmoid(v):
    return 1.0 / (1.0 + jnp.exp(-v))


def _silu(v):
    return v * _sigmoid(v)


def _split3(a):
    hi = a.astype(BF16)
    r = a - hi.astype(F32)
    mid = r.astype(BF16)
    lo = (r - mid.astype(F32)).astype(BF16)
    return hi, mid, lo


def _dot(a, b):
    return jnp.dot(a, b, preferred_element_type=F32)


def _dot_nt(a, b):
    return lax.dot_general(a, b, (((1,), (1,)), ((), ())), preferred_element_type=F32)


def _dot3_exact_rhs(a, m):
    hi, mid, lo = _split3(a)
    return _dot(hi, m) + _dot(mid, m) + _dot(lo, m)


def _dot3_exact_lhs(m, a):
    hi, mid, lo = _split3(a)
    return _dot(m, hi) + _dot(m, mid) + _dot(m, lo)


def _norm_mod(x, nw, scale, shift):
    ms = jnp.mean(x * x, axis=-1, keepdims=True)
    return x * lax.rsqrt(ms + EPS) * nw * (1.0 + scale) + shift


MOD_TN = 1536


def _mod_kernel(c_ref, w_ref, b_ref, o_ref):
    cond = _silu(c_ref[...]).astype(BF16)
    o_ref[...] = _dot(cond, w_ref[...].astype(BF16)) + b_ref[...]


def _modulation(c, ada_w, ada_b):
    depth = ada_w.shape[0]
    n = ada_w.shape[2]
    return pl.pallas_call(
        _mod_kernel,
        grid=(depth, n // MOD_TN),
        in_specs=[
            pl.BlockSpec((BATCH, D_MODEL), lambda i, j: (0, 0)),
            pl.BlockSpec((None, D_MODEL, MOD_TN), lambda i, j: (i, 0, j)),
            pl.BlockSpec((None, 1, MOD_TN), lambda i, j: (i, 0, j)),
        ],
        out_specs=pl.BlockSpec((None, BATCH, MOD_TN), lambda i, j: (i, 0, j)),
        out_shape=jax.ShapeDtypeStruct((depth, BATCH, n), F32),
        compiler_params=pltpu.CompilerParams(
            dimension_semantics=("arbitrary", "arbitrary"), vmem_limit_bytes=VMEM_LIMIT),
        name="adaln_mod",
    )(c, ada_w, ada_b.reshape(depth, 1, n))


def _mod_spec(layer, chunk, rows_per_batch_tile):
    return pl.BlockSpec((None, None, None, 1, D_MODEL),
                        lambda i, *_: (layer, i // rows_per_batch_tile, chunk, 0, 0))


INPROJ_TM = 512
DSA_INPROJ_TM = 512


def _inproj_kernel(x_ref, nw_ref, scale_ref, shift_ref, w_ref, z_ref, xbc_ref, dt_ref):
    h = _norm_mod(x_ref[...], nw_ref[...], scale_ref[...], shift_ref[...]).astype(BF16)
    n_main = SSD_D_INNER + SSD_CONV_DIM
    z_ref[...] = _silu(_dot(h, w_ref[:, 0:SSD_D_INNER]))
    xbc_ref[...] = _dot(h, w_ref[:, SSD_D_INNER:n_main])
    dt_ref[:, 0:SSD_HEADS] = _dot(h, w_ref[:, n_main:n_main + SSD_HEADS])
    dt_ref[:, SSD_HEADS:SSD_DT_PAD] = jnp.zeros((dt_ref.shape[0], SSD_DT_PAD - SSD_HEADS), F32)


def _inproj(x, nw, mod5, layer, scale_chunk, shift_chunk, w):
    tm = INPROJ_TM
    tiles_per_batch = SEQ // tm
    widths = (SSD_D_INNER, SSD_CONV_DIM, SSD_DT_PAD)
    return pl.pallas_call(
        _inproj_kernel,
        grid=(TOKENS // tm,),
        in_specs=[
            pl.BlockSpec((tm, D_MODEL), lambda i: (i, 0)),
            pl.BlockSpec((1, D_MODEL), lambda i: (0, 0)),
            _mod_spec(layer, scale_chunk, tiles_per_batch),
            _mod_spec(layer, shift_chunk, tiles_per_batch),
            pl.BlockSpec((D_MODEL, SSD_D_INNER + SSD_CONV_DIM + SSD_HEADS), lambda i: (0, 0),
                         pipeline_mode=pl.Buffered(1)),
        ],
        out_specs=[pl.BlockSpec((tm, width), lambda i: (i, 0)) for width in widths],
        out_shape=[jax.ShapeDtypeStruct((TOKENS, width), F32) for width in widths],
        compiler_params=pltpu.CompilerParams(
            dimension_semantics=("arbitrary",), vmem_limit_bytes=VMEM_LIMIT),
        name="norm_inproj",
    )(x, nw, mod5, mod5, w)


CONV_COLS = 512


def _ssd_kernel(z_ref, xbc_ref, dt_ref, xres_ref, cw_ref, cb_ref, dtb_ref, alog_ref, de_ref, nw_ref,
                e_ref, wout_ref, gate_ref, o_ref, state_ref, ext_ref, act_ref, yn_ref):
    q = SSD_CHUNK
    c = pl.program_id(1)

    @pl.when(c == 0)
    def _():
        state_ref[...] = jnp.zeros_like(state_ref)
        ext_ref[0:CONV_HALO, :] = jnp.zeros((CONV_HALO, SSD_CONV_DIM), F32)

    @pl.when(c > 0)
    def _():
        ext_ref[0:CONV_HALO, :] = ext_ref[q:q + CONV_HALO, :]

    ext_ref[CONV_HALO:CONV_HALO + q, :] = xbc_ref[...]

    for s in range(SSD_CONV_DIM // CONV_COLS):
        cs = slice(s * CONV_COLS, (s + 1) * CONV_COLS)
        u = ext_ref[:, cs]
        acc = cw_ref[0:1, cs] * u
        for k in range(1, SSD_CONV):
            acc = pltpu.roll(acc, 1, axis=0) + cw_ref[k:k + 1, cs] * u
        act_ref[:, cs] = _silu(acc[CONV_HALO:CONV_HALO + q, :] + cb_ref[:, cs])

    dt_raw = dt_ref[...] + dtb_ref[...]
    dt = jnp.maximum(dt_raw, 0.0) + jnp.log1p(jnp.exp(-jnp.abs(dt_raw)))
    a = dt * (-jnp.exp(alog_ref[...]) * LOG2E)
    row = lax.broadcasted_iota(jnp.int32, (q, q), 0)
    col = lax.broadcasted_iota(jnp.int32, (q, q), 1)
    tril = row >= col
    acs = _dot3_exact_lhs(tril.astype(BF16), a)
    acs_t = acs.T
    expand = e_ref[...]
    acs_e = _dot3_exact_rhs(acs, expand)
    dt_e = _dot3_exact_rhs(dt, expand)
    tot_e = acs_e[q - 1:q, :]
    decay_from_start = jnp.exp2(acs_e)
    decay_to_end = jnp.exp2(tot_e - acs_e)
    chunk_decay = jnp.exp2(tot_e)

    lane_head = lax.broadcasted_iota(jnp.int32, (q, SSD_GROUP_W), 1) // SSD_HEAD_DIM
    for g in range(SSD_GROUPS):
        xs = act_ref[:, g * SSD_GROUP_W:(g + 1) * SSD_GROUP_W]
        gs = slice(g * SSD_GROUP_W, (g + 1) * SSD_GROUP_W)
        bm = act_ref[:, SSD_D_INNER + g * SSD_STATE:SSD_D_INNER + (g + 1) * SSD_STATE]
        cm = act_ref[:, SSD_D_INNER + SSD_GN + g * SSD_STATE:SSD_D_INNER + SSD_GN + (g + 1) * SSD_STATE]
        bm_t = bm.T.astype(BF16)
        cm_b = cm.astype(BF16)
        cb = _dot(cm_b, bm_t)
        xd = xs * dt_e[:, gs]
        ms = []
        xds = []
        for j in range(SSD_HEADS_PER_GROUP):
            h = g * SSD_HEADS_PER_GROUP + j
            seg = acs[:, h:h + 1] - acs_t[h:h + 1, :]
            dec = jnp.exp2(jnp.where(tril, seg, -jnp.inf))
            ms.append((cb * dec).astype(BF16))
            xds.append(jnp.where(lane_head == j, xd, 0.0).astype(BF16))
        y_diag = _dot(jnp.concatenate(ms, axis=1), jnp.concatenate(xds, axis=0))
        prev = state_ref[g]
        y_off = _dot(cm_b, prev.astype(BF16)) * decay_from_start[:, gs]
        state_ref[g] = prev * chunk_decay[:, gs] + _dot(bm_t, (xd * decay_to_end[:, gs]).astype(BF16))
        y = y_diag + y_off + xs * de_ref[:, gs]
        y = y * z_ref[:, gs]
        y = y * lax.rsqrt(jnp.mean(y * y, axis=-1, keepdims=True) + EPS) * nw_ref[:, gs]
        yn_ref[:, gs] = y.astype(BF16)

    out = _dot(yn_ref[...], wout_ref[...])
    o_ref[...] = xres_ref[...] + gate_ref[...] * out


def _ssd_mixer(z, xbc, dt, x, conv_w, conv_b, dt_bias, a_log, d_e, norm_w, expand, w_out, mod5, layer):
    q = SSD_CHUNK
    nc = SEQ // q
    tok = lambda w: pl.BlockSpec((q, w), lambda b, c: (b * nc + c, 0))
    full = lambda r, w: pl.BlockSpec((r, w), lambda b, c: (0, 0))
    return pl.pallas_call(
        _ssd_kernel,
        grid=(BATCH, nc),
        in_specs=[
            tok(SSD_D_INNER), tok(SSD_CONV_DIM), tok(SSD_DT_PAD), tok(D_MODEL),
            full(SSD_CONV, SSD_CONV_DIM), full(1, SSD_CONV_DIM), full(1, SSD_DT_PAD), full(1, SSD_DT_PAD),
            full(1, SSD_D_INNER), full(1, SSD_D_INNER), full(SSD_DT_PAD, SSD_D_INNER),
            full(SSD_D_INNER, D_MODEL),
            pl.BlockSpec((None, None, None, 1, D_MODEL), lambda b, c: (layer, b, 2, 0, 0)),
        ],
        out_specs=tok(D_MODEL),
        out_shape=jax.ShapeDtypeStruct((TOKENS, D_MODEL), F32),
        scratch_shapes=[
            pltpu.VMEM((SSD_GROUPS, SSD_STATE, SSD_GROUP_W), F32),
            pltpu.VMEM((q + CONV_HALO, SSD_CONV_DIM), F32),
            pltpu.VMEM((q, SSD_CONV_DIM), F32),
            pltpu.VMEM((q, SSD_D_INNER), BF16),
        ],
        compiler_params=pltpu.CompilerParams(
            dimension_semantics=("arbitrary", "arbitrary"), vmem_limit_bytes=VMEM_LIMIT),
        name="ssd_mixer",
    )(z, xbc, dt, x, conv_w, conv_b, dt_bias, a_log, d_e, norm_w, expand, w_out, mod5)


DSA_KEY_TILE = 256
DSA_CLASSES = 8
V_EXT = 2 * ATT_HEAD_DIM
DSA_BLOCKS_PER_CLASS = (SEQ // Q_BLOCK) // DSA_CLASSES
N_BISECT = 12
F32_MIN = float(jnp.finfo(jnp.float32).min)
LOG2E = 1.4426950408889634


def _count(mask):
    return jnp.sum(jnp.where(mask, 1.0, 0.0), axis=-1, keepdims=True)


def _select_topk(score_ref, q_pos, n_keys):
    kf = float(TOPK)
    small = (q_pos + 1) <= TOPK
    sc = score_ref[...]
    hi0 = jnp.max(sc, axis=-1, keepdims=True)
    lo0 = jnp.min(jnp.where(sc == -jnp.inf, jnp.inf, sc), axis=-1, keepdims=True)

    def bisect(_, carry):
        lo, hi = carry
        mid = lo + 0.5 * (hi - lo)
        ok = _count(score_ref[...] >= mid) >= kf
        return jnp.where(ok, mid, lo), jnp.where(ok, hi, mid)

    _, hi = lax.fori_loop(0, N_BISECT, bisect, (lo0, hi0))

    v0 = jnp.max(jnp.where(sc <= hi, sc, -jnp.inf), axis=-1, keepdims=True)
    c0 = _count(sc >= v0)
    pend0 = jnp.where((c0 >= kf) | small, 0.0, 1.0)

    def walk_cond(carry):
        return (carry[3] > 0.0) & (carry[4] < n_keys)

    def walk(carry):
        v, c, pend, _, it = carry
        s = score_ref[...]
        v2 = jnp.max(jnp.where(s < v, s, -jnp.inf), axis=-1, keepdims=True)
        c2 = _count(s >= v2)
        v = jnp.where(pend > 0.0, v2, v)
        c = jnp.where(pend > 0.0, c2, c)
        pend = jnp.where(c2 >= kf, 0.0, pend)
        return v, c, pend, jnp.max(pend), it + 1

    v, c, _, _, _ = lax.while_loop(walk_cond, walk, (v0, c0, pend0, jnp.max(pend0), jnp.int32(0)))
    thr = jnp.where(small, F32_MIN, v)
    any_tie = jnp.max(jnp.where(small, 0.0, c - kf)) > 0.0

    @pl.when(jnp.logical_not(any_tie))
    def _():
        score_ref[...] = jnp.where(score_ref[...] >= thr, 0.0, -jnp.inf)

    @pl.when(any_tie)
    def _():
        s = score_ref[...]
        key_pos = lax.broadcasted_iota(jnp.int32, (Q_BLOCK, n_keys), 1)
        gt = s > thr
        eq = s == thr
        need = kf - _count(gt)

        def body(_, carry):
            lo, hi = carry
            mid = (lo + hi) >> 1
            ok = _count((score_ref[...] == thr) & (key_pos <= mid)) >= need
            return jnp.where(ok, lo, mid), jnp.where(ok, mid, hi)

        init = (jnp.full((Q_BLOCK, 1), -1, jnp.int32), jnp.full((Q_BLOCK, 1), n_keys - 1, jnp.int32))
        cut = lax.fori_loop(0, (n_keys - 1).bit_length() + 1, body, init)[1]
        score_ref[...] = jnp.where(gt | (eq & (key_pos <= cut)), 0.0, -jnp.inf)


def _dsa_inproj_kernel(x_ref, nw_ref, scale_ref, shift_ref, w_ref, qn_ref, kn_ref, seg_ref, segt_ref,
                       q_ref, k_ref, v_ref, qi_ref, ki_ref, wi_ref):
    hd = ATT_HEAD_DIM
    h = _norm_mod(x_ref[...], nw_ref[...], scale_ref[...], shift_ref[...]).astype(BF16)

    def head_norm(t, w):
        width = t.shape[1]
        ss = _dot((t * t).astype(BF16), seg_ref[0:width, :])
        r = lax.rsqrt(ss * (1.0 / hd) + EPS)
        r_hi = r.astype(BF16)
        r_lo = (r - r_hi.astype(F32)).astype(BF16)
        return t * (_dot(r_hi, segt_ref[:, 0:width]) + _dot(r_lo, segt_ref[:, 0:width])) * w

    q = head_norm(_dot(h, w_ref[:, 0:DSA_Q]), qn_ref[...] * (hd ** -0.5 * LOG2E))
    for n in range(ATT_HEADS):
        q_ref[n] = q[:, n * hd:(n + 1) * hd].astype(BF16)
    kv = _dot(h, w_ref[:, DSA_Q:DSA_Q + 2 * DSA_KV])
    k = head_norm(kv[:, 0:DSA_KV], kn_ref[...])
    for n in range(ATT_KV_HEADS):
        k_ref[n] = k[:, n * hd:(n + 1) * hd].astype(BF16)
        v_ref[n] = jnp.concatenate([kv[:, DSA_KV + n * hd:DSA_KV + (n + 1) * hd],
                                    jnp.ones((kv.shape[0], V_EXT - hd), F32)], axis=1).astype(BF16)
    qi = _dot(h, w_ref[:, DSA_Q + 2 * DSA_KV:DSA_KI_START])
    for n in range(IDX_HEADS):
        qi_ref[n] = qi[:, n * IDX_HEAD_DIM:(n + 1) * IDX_HEAD_DIM].astype(BF16)
    ki_ref[...] = _dot(h, w_ref[:, DSA_KI_START:DSA_KI_START + IDX_HEAD_DIM]).astype(BF16)
    wi_ref[...] = _dot(h, w_ref[:, DSA_WI_START:DSA_WI_START + IDX_HEADS]) * ((IDX_HEADS * IDX_HEAD_DIM) ** -0.5)


def _dsa_inproj(x, nw, mod5, layer, w, q_norm, k_norm):
    tm = DSA_INPROJ_TM
    tiles_per_batch = SEQ // tm
    heads = lambda n: pl.BlockSpec((n, tm, ATT_HEAD_DIM), lambda i: (0, i, 0))
    head_of = jnp.arange(DSA_Q, dtype=jnp.int32) // ATT_HEAD_DIM
    seg = (head_of[:, None] == jnp.arange(128, dtype=jnp.int32)[None, :]).astype(BF16)
    q_norm = jnp.tile(q_norm, (1, ATT_HEADS))
    k_norm = jnp.tile(k_norm, (1, ATT_KV_HEADS))
    return pl.pallas_call(
        _dsa_inproj_kernel,
        grid=(TOKENS // tm,),
        in_specs=[
            pl.BlockSpec((tm, D_MODEL), lambda i: (i, 0)),
            pl.BlockSpec((1, D_MODEL), lambda i: (0, 0)),
            _mod_spec(layer, 1, tiles_per_batch),
            _mod_spec(layer, 0, tiles_per_batch),
            pl.BlockSpec((D_MODEL, DSA_PROJ_PAD), lambda i: (0, 0)),
            pl.BlockSpec((1, DSA_Q), lambda i: (0, 0)),
            pl.BlockSpec((1, DSA_KV), lambda i: (0, 0)),
            pl.BlockSpec((DSA_Q, 128), lambda i: (0, 0)),
            pl.BlockSpec((128, DSA_Q), lambda i: (0, 0)),
        ],
        out_specs=[heads(ATT_HEADS), heads(ATT_KV_HEADS),
                   pl.BlockSpec((ATT_KV_HEADS, tm, V_EXT), lambda i: (0, i, 0)), heads(IDX_HEADS),
                   pl.BlockSpec((tm, IDX_HEAD_DIM), lambda i: (i, 0)),
                   pl.BlockSpec((tm, IDX_HEADS), lambda i: (i, 0))],
        out_shape=[jax.ShapeDtypeStruct((ATT_HEADS, TOKENS, ATT_HEAD_DIM), BF16),
                   jax.ShapeDtypeStruct((ATT_KV_HEADS, TOKENS, ATT_HEAD_DIM), BF16),
                   jax.ShapeDtypeStruct((ATT_KV_HEADS, TOKENS, V_EXT), BF16),
                   jax.ShapeDtypeStruct((IDX_HEADS, TOKENS, IDX_HEAD_DIM), BF16),
                   jax.ShapeDtypeStruct((TOKENS, IDX_HEAD_DIM), BF16),
                   jax.ShapeDtypeStruct((TOKENS, IDX_HEADS), F32)],
        compiler_params=pltpu.CompilerParams(
            dimension_semantics=("arbitrary",), vmem_limit_bytes=VMEM_LIMIT),
        name="dsa_inproj",
    )(x, nw, mod5, mod5, w, q_norm, k_norm, seg, seg.T)


def _dsa_kernel(q_ref, k_ref, v_ref, qi_ref, ki_ref, wi_ref, xres_ref, wout_ref, gate_ref, o_ref,
                score_ref, ocat_ref, *, n_keys, first_block):
    hd = ATT_HEAD_DIM
    q_pos = (first_block + pl.program_id(1)) * Q_BLOCK + lax.broadcasted_iota(jnp.int32, (Q_BLOCK, 1), 0)

    wi = wi_ref[...]
    qi = qi_ref[...].reshape(IDX_HEADS * Q_BLOCK, IDX_HEAD_DIM)
    for kt in range(n_keys // DSA_KEY_TILE):
        ks = slice(kt * DSA_KEY_TILE, (kt + 1) * DSA_KEY_TILE)
        raw = _dot_nt(qi, ki_ref[ks, :])
        acc = jnp.zeros((Q_BLOCK, DSA_KEY_TILE), F32)
        for n in range(IDX_HEADS):
            acc = acc + wi[:, n:n + 1] * jnp.maximum(raw[n * Q_BLOCK:(n + 1) * Q_BLOCK, :], 0.0)
        key_pos = kt * DSA_KEY_TILE + lax.broadcasted_iota(jnp.int32, (Q_BLOCK, DSA_KEY_TILE), 1)
        score_ref[:, ks] = jnp.where(key_pos <= q_pos, acc, -jnp.inf)

    if n_keys > TOPK:
        _select_topk(score_ref, q_pos, n_keys)
        bias = score_ref[...][None, :, :]
    else:
        bias = jnp.where(score_ref[...] == -jnp.inf, -jnp.inf, 0.0)[None, :, :]

    for n in range(ATT_KV_HEADS):
        q4 = q_ref[n * ATT_Q_PER_KV:(n + 1) * ATT_Q_PER_KV].reshape(ATT_Q_PER_KV * Q_BLOCK, hd)
        s = _dot_nt(q4, k_ref[n]).reshape(ATT_Q_PER_KV, Q_BLOCK, n_keys) + bias
        p = jnp.exp2(s - jnp.max(s, axis=-1, keepdims=True))
        o = _dot(p.reshape(ATT_Q_PER_KV * Q_BLOCK, n_keys).astype(BF16), v_ref[n])
        o = o[:, 0:hd] * (1.0 / o[:, hd:hd + 1])
        for g in range(ATT_Q_PER_KV):
            col = (n * ATT_Q_PER_KV + g) * hd
            ocat_ref[:, col:col + hd] = o[g * Q_BLOCK:(g + 1) * Q_BLOCK, :]
    out = _dot(ocat_ref[...].astype(BF16), wout_ref[...])
    o_ref[...] = xres_ref[...] + gate_ref[...] * out


def _dsa_mixer(q, k, v, qi, ki, wi, x, w_out, mod5, layer):
    nb = SEQ // Q_BLOCK
    k4 = k.reshape(ATT_KV_HEADS, BATCH, SEQ, ATT_HEAD_DIM)
    v4 = v.reshape(ATT_KV_HEADS, BATCH, SEQ, V_EXT)
    ki3 = ki.reshape(BATCH, SEQ, IDX_HEAD_DIM)
    for cls in range(DSA_CLASSES):
        n_keys = (cls + 1) * (SEQ // DSA_CLASSES)
        first_block = cls * DSA_BLOCKS_PER_CLASS
        row = lambda b, i, fb=first_block: b * nb + fb + i
        heads = lambda n: pl.BlockSpec((n, Q_BLOCK, ATT_HEAD_DIM), lambda b, i: (0, row(b, i), 0))
        keys = lambda width: pl.BlockSpec((ATT_KV_HEADS, None, n_keys, width), lambda b, i: (0, b, 0, 0))
        x = pl.pallas_call(
            functools.partial(_dsa_kernel, n_keys=n_keys, first_block=first_block),
            grid=(BATCH, DSA_BLOCKS_PER_CLASS),
            in_specs=[
                heads(ATT_HEADS), keys(ATT_HEAD_DIM), keys(V_EXT), heads(IDX_HEADS),
                pl.BlockSpec((None, n_keys, IDX_HEAD_DIM), lambda b, i: (b, 0, 0)),
                pl.BlockSpec((Q_BLOCK, IDX_HEADS), lambda b, i: (row(b, i), 0)),
                pl.BlockSpec((Q_BLOCK, D_MODEL), lambda b, i: (row(b, i), 0)),
                pl.BlockSpec((DSA_Q, D_MODEL), lambda b, i: (0, 0)),
                pl.BlockSpec((None, None, None, 1, D_MODEL), lambda b, i: (layer, b, 2, 0, 0)),
            ],
            out_specs=pl.BlockSpec((Q_BLOCK, D_MODEL), lambda b, i: (row(b, i), 0)),
            out_shape=jax.ShapeDtypeStruct((TOKENS, D_MODEL), F32),
            scratch_shapes=[
                pltpu.VMEM((Q_BLOCK, n_keys), F32),
                pltpu.VMEM((Q_BLOCK, D_MODEL), F32),
            ],
            input_output_aliases={6: 0},
            compiler_params=pltpu.CompilerParams(
                dimension_semantics=("arbitrary", "arbitrary"), vmem_limit_bytes=VMEM_LIMIT),
            name=f"dsa_mixer_c{cls}",
        )(q, k4, v4, qi, ki3, wi, x, w_out, mod5)
    return x


MOE_PAIRS = MOE_EPG * (MOE_EPG - 1) // 2
MOE_CLASSES = MOE_GROUPS * MOE_PAIRS
PAIR_LO = (0, 0, 0, 1, 1, 2)
PAIR_HI = (1, 2, 3, 2, 3, 3)
ROUTE_TM = 1024
META_W = 128
META_CLASS, META_RANK, META_WLO, META_WHI = 0, 1, 2, 3
TILE_ROWS, TILE_LANES = 8, 128
H_WORDS = D_MODEL // 2
H_SUBLANES = H_WORDS // TILE_LANES
SORT_BLOCK = 256
N_SORT_BLOCKS = TOKENS // SORT_BLOCK
MAX_ITEMS = N_SORT_BLOCKS + MOE_CLASSES
PERMUTE_TM = 512
COMBINE_TM = 512


def _route_kernel(x_ref, nw_ref, scale_ref, shift_ref, wrt_ref, brt_ref, tri_ref, pay_ref, cls_ref, rank_ref,
                  cnt_ref, carry_ref):
    tm = ROUTE_TM

    @pl.when(pl.program_id(0) == 0)
    def _():
        carry_ref[...] = jnp.zeros_like(carry_ref)

    h = _norm_mod(x_ref[...], nw_ref[...], scale_ref[...], shift_ref[...])
    logits = _dot_nt(wrt_ref[...], h.astype(BF16)) + brt_ref[...]
    sub = lax.broadcasted_iota(jnp.int32, logits.shape, 0)
    neg = -jnp.inf
    big = jnp.int32(ROUTE_PAD)
    is_group = (sub >= MOE_EXPERTS) & (sub < MOE_EXPERTS + MOE_GROUPS)
    gl = jnp.where(is_group, logits, neg)
    g_max = jnp.max(gl, axis=0, keepdims=True)
    g_idx = jnp.min(jnp.where(gl == g_max, sub - MOE_EXPERTS, big), axis=0, keepdims=True)
    g_val = 1.0 / jnp.sum(jnp.exp(gl - g_max), axis=0, keepdims=True)
    in_group = (sub < MOE_EXPERTS) & ((sub // MOE_EPG) == g_idx)
    el = jnp.where(in_group, logits, neg)
    m1 = jnp.max(el, axis=0, keepdims=True)
    i1 = jnp.min(jnp.where(el == m1, sub, big), axis=0, keepdims=True)
    el2 = jnp.where(sub == i1, neg, el)
    m2 = jnp.max(el2, axis=0, keepdims=True)
    i2 = jnp.min(jnp.where(el2 == m2, sub, big), axis=0, keepdims=True)
    r = jnp.exp(m2 - m1)
    w_top1 = g_val / (1.0 + r)
    w_top2 = g_val * r / (1.0 + r)

    lo = jnp.minimum(i1, i2) - g_idx * MOE_EPG
    hi = jnp.maximum(i1, i2) - g_idx * MOE_EPG
    pair = (lo * (2 * MOE_EPG - 1 - lo)) // 2 + (hi - lo - 1)
    cls = g_idx * MOE_PAIRS + pair
    w_lo = jnp.where(i1 < i2, w_top1, w_top2)
    w_hi = jnp.where(i1 < i2, w_top2, w_top1)

    onehot = sub == cls
    before = _dot(onehot.astype(BF16), tri_ref[...]) + carry_ref[...]
    rank = jnp.sum(jnp.where(onehot, before, 0.0), axis=0, keepdims=True)
    carry_ref[...] += jnp.sum(jnp.where(onehot, 1.0, 0.0), axis=1, keepdims=True)
    cnt_ref[...] = carry_ref[...]
    cls_ref[...] = cls
    rank_ref[...] = rank.astype(jnp.int32)

    words = pltpu.pack_elementwise([h[:, 0:H_WORDS], h[:, H_WORDS:D_MODEL]], packed_dtype=BF16)
    for s in range(H_SUBLANES):
        pay_ref[pl.ds(s, tm, stride=TILE_ROWS), :] = words[:, s * TILE_LANES:(s + 1) * TILE_LANES]
    sub8 = lax.broadcasted_iota(jnp.int32, (TILE_ROWS, tm), 0)
    rec = jnp.where(sub8 == META_CLASS, cls.astype(F32),
                    jnp.where(sub8 == META_RANK, rank,
                              jnp.where(sub8 == META_WLO, w_lo, jnp.where(sub8 == META_WHI, w_hi, 0.0))))
    rec = jnp.concatenate([rec, jnp.zeros((META_W - TILE_ROWS, tm), F32)], axis=0)
    for b in range(tm // META_W):
        meta = rec[:, b * META_W:(b + 1) * META_W].T
        pay_ref[pl.ds(b * META_W * TILE_ROWS + H_SUBLANES, META_W, stride=TILE_ROWS), :] = (
            lax.bitcast_convert_type(meta, jnp.int32))
    for s in range(H_SUBLANES + 1, TILE_ROWS):
        pay_ref[pl.ds(s, tm, stride=TILE_ROWS), :] = jnp.zeros((tm, TILE_LANES), jnp.int32)


def _moe_route(x, nw, mod5, layer, w_route, b_route):
    tm = ROUTE_TM
    tiles_per_batch = SEQ // tm
    tri = (jnp.arange(tm, dtype=jnp.int32)[:, None] < jnp.arange(tm, dtype=jnp.int32)[None, :]).astype(BF16)
    return pl.pallas_call(
        _route_kernel,
        grid=(TOKENS // tm,),
        in_specs=[
            pl.BlockSpec((tm, D_MODEL), lambda i: (i, 0)),
            pl.BlockSpec((1, D_MODEL), lambda i: (0, 0)),
            _mod_spec(layer, 4, tiles_per_batch),
            _mod_spec(layer, 3, tiles_per_batch),
            pl.BlockSpec((ROUTE_PAD, D_MODEL), lambda i: (0, 0)),
            pl.BlockSpec((ROUTE_PAD, 1), lambda i: (0, 0)),
            pl.BlockSpec((tm, tm), lambda i: (0, 0)),
        ],
        out_specs=[pl.BlockSpec((tm * TILE_ROWS, TILE_LANES), lambda i: (i, 0)),
                   pl.BlockSpec((1, tm), lambda i: (0, i)),
                   pl.BlockSpec((1, tm), lambda i: (0, i)),
                   pl.BlockSpec((ROUTE_PAD, 1), lambda i: (0, 0))],
        out_shape=[jax.ShapeDtypeStruct((TOKENS * TILE_ROWS, TILE_LANES), jnp.int32),
                   jax.ShapeDtypeStruct((1, TOKENS), jnp.int32),
                   jax.ShapeDtypeStruct((1, TOKENS), jnp.int32),
                   jax.ShapeDtypeStruct((ROUTE_PAD, 1), F32)],
        scratch_shapes=[pltpu.VMEM((ROUTE_PAD, 1), F32)],
        compiler_params=pltpu.CompilerParams(
            dimension_semantics=("arbitrary",), vmem_limit_bytes=VMEM_LIMIT),
        name="moe_route",
    )(x, nw, mod5, mod5, w_route.T, b_route.reshape(ROUTE_PAD, 1), tri)


def _permute_kernel(pos_ref, src_ref, dst_hbm, stage_ref, sem):
    i = pl.program_id(0)
    slot = i % 2
    rows = PERMUTE_TM * TILE_ROWS

    def slot_wait(s):
        whole = stage_ref.at[pl.ds(s * rows, rows), :]
        pltpu.make_async_copy(whole, whole, sem.at[s]).wait()

    @pl.when(i >= 2)
    def _():
        slot_wait(slot)

    base = pl.multiple_of(slot * rows, rows)
    stage_ref[pl.ds(base, rows), :] = src_ref[...]

    def issue(r, _):
        pltpu.make_async_copy(stage_ref.at[pl.ds(base + r * TILE_ROWS, TILE_ROWS), :],
                              dst_hbm.at[pos_ref[i * PERMUTE_TM + r]], sem.at[slot]).start()
        return 0

    lax.fori_loop(0, PERMUTE_TM, issue, 0, unroll=16)

    @pl.when(i == pl.num_programs(0) - 1)
    def _():
        slot_wait(1 - slot)
        slot_wait(slot)


def _moe_permute(pos, payload):
    rows = PERMUTE_TM * TILE_ROWS
    return pl.pallas_call(
        _permute_kernel,
        grid_spec=pltpu.PrefetchScalarGridSpec(
            num_scalar_prefetch=1, grid=(TOKENS // PERMUTE_TM,),
            in_specs=[pl.BlockSpec((rows, TILE_LANES), lambda i, pos: (i, 0))],
            out_specs=pl.BlockSpec(memory_space=pl.ANY),
            scratch_shapes=[pltpu.VMEM((2 * rows, TILE_LANES), jnp.int32), pltpu.SemaphoreType.DMA((2,))]),
        out_shape=jax.ShapeDtypeStruct((TOKENS, TILE_ROWS, TILE_LANES), jnp.int32),
        compiler_params=pltpu.CompilerParams(
            dimension_semantics=("arbitrary",), vmem_limit_bytes=VMEM_LIMIT),
        name="moe_permute",
    )(pos, payload)


def _experts_kernel(blk_ref, cls_ref, elo_ref, ehi_ref, first_ref, last_ref, valid_ref,
                    pay_ref, w1lo_ref, w3lo_ref, w2lo_ref, w1hi_ref, w3hi_ref, w2hi_ref, o_ref, acc_ref):
    k = pl.program_id(0)
    rows = SORT_BLOCK

    @pl.when(valid_ref[k] == 1)
    def _():
        def sublane(s):
            return pay_ref[pl.ds(s, rows, stride=TILE_ROWS), :]

        halves = [[pltpu.unpack_elementwise(sublane(s), index=i, packed_dtype=BF16, unpacked_dtype=F32)
                   for s in range(H_SUBLANES)] for i in range(2)]
        hb = jnp.concatenate(halves[0] + halves[1], axis=1).astype(BF16)
        meta = lax.bitcast_convert_type(sublane(H_SUBLANES), F32)
        mine = meta[:, META_CLASS:META_CLASS + 1] == cls_ref[k].astype(F32)
        w_lo = jnp.where(mine, meta[:, META_WLO:META_WLO + 1], 0.0)
        w_hi = jnp.where(mine, meta[:, META_WHI:META_WHI + 1], 0.0)
        bf = lambda w_ref: w_ref[...].astype(BF16)
        hid_lo = _silu(_dot(hb, bf(w1lo_ref))) * _dot(hb, bf(w3lo_ref)) * w_lo
        hid_hi = _silu(_dot(hb, bf(w1hi_ref))) * _dot(hb, bf(w3hi_ref)) * w_hi
        y = _dot(hid_lo.astype(BF16), bf(w2lo_ref)) + _dot(hid_hi.astype(BF16), bf(w2hi_ref))

        @pl.when(first_ref[k] == 1)
        def _():
            acc_ref[...] = y

        @pl.when(first_ref[k] == 0)
        def _():
            acc_ref[...] += y

        @pl.when(last_ref[k] == 1)
        def _():
            for s in range(TILE_ROWS):
                o_ref[pl.ds(s, rows, stride=TILE_ROWS), :] = acc_ref[:, s * TILE_LANES:(s + 1) * TILE_LANES]


def _moe_experts(items, payload_sorted, layer, w1, w3, w2):
    blk, cls, elo, ehi, first, last, valid = items
    tiles = pl.BlockSpec((SORT_BLOCK * TILE_ROWS, TILE_LANES), lambda k, blk, *_: (blk[k], 0))
    w_in = lambda which: pl.BlockSpec(
        (None, None, D_MODEL, MOE_HIDDEN), lambda k, blk, cls, elo, ehi, *_: (layer, (elo, ehi)[which][k], 0, 0))
    w_out = lambda which: pl.BlockSpec(
        (None, None, MOE_HIDDEN, D_MODEL), lambda k, blk, cls, elo, ehi, *_: (layer, (elo, ehi)[which][k], 0, 0))
    return pl.pallas_call(
        _experts_kernel,
        grid_spec=pltpu.PrefetchScalarGridSpec(
            num_scalar_prefetch=7, grid=(MAX_ITEMS,),
            in_specs=[tiles, w_in(0), w_in(0), w_out(0), w_in(1), w_in(1), w_out(1)],
            out_specs=tiles,
            scratch_shapes=[pltpu.VMEM((SORT_BLOCK, D_MODEL), F32)]),
        out_shape=jax.ShapeDtypeStruct((TOKENS * TILE_ROWS, TILE_LANES), F32),
        compiler_params=pltpu.CompilerParams(
            dimension_semantics=("arbitrary",), vmem_limit_bytes=VMEM_LIMIT),
        name="moe_experts",
    )(blk, cls, elo, ehi, first, last, valid, payload_sorted, w1, w3, w2, w1, w3, w2)


def _combine_kernel(pos_ref, x_ref, gate_ref, y_hbm, o_ref, buf_ref, sem):
    tm = COMBINE_TM
    i = pl.program_id(0)
    n = pl.num_programs(0)

    def gather(tile, slot):
        def issue(r, _):
            pltpu.make_async_copy(y_hbm.at[pos_ref[tile * tm + r]],
                                  buf_ref.at[pl.ds((slot * tm + r) * TILE_ROWS, TILE_ROWS), :], sem.at[slot]).start()
            return 0
        lax.fori_loop(0, tm, issue, 0, unroll=16)

    @pl.when(i == 0)
    def _():
        gather(0, 0)

    @pl.when(i + 1 < n)
    def _():
        gather(i + 1, (i + 1) % 2)

    slot = i % 2
    base = slot * tm * TILE_ROWS
    whole_slot = buf_ref.at[pl.ds(base, tm * TILE_ROWS), :]
    pltpu.make_async_copy(whole_slot, whole_slot, sem.at[slot]).wait()
    y = jnp.concatenate([buf_ref[pl.ds(base + s, tm, stride=TILE_ROWS), :] for s in range(TILE_ROWS)], axis=1)
    o_ref[...] = x_ref[...] + gate_ref[...] * y


def _moe_combine(pos, x, mod5, layer, y_sorted):
    tm = COMBINE_TM
    tiles_per_batch = SEQ // tm
    return pl.pallas_call(
        _combine_kernel,
        grid_spec=pltpu.PrefetchScalarGridSpec(
            num_scalar_prefetch=1, grid=(TOKENS // tm,),
            in_specs=[
                pl.BlockSpec((tm, D_MODEL), lambda i, pos: (i, 0)),
                pl.BlockSpec((None, None, None, 1, D_MODEL),
                             lambda i, pos: (layer, i // tiles_per_batch, 5, 0, 0)),
                pl.BlockSpec(memory_space=pl.ANY),
            ],
            out_specs=pl.BlockSpec((tm, D_MODEL), lambda i, pos: (i, 0)),
            scratch_shapes=[pltpu.VMEM((2 * tm * TILE_ROWS, TILE_LANES), F32), pltpu.SemaphoreType.DMA((2,))]),
        out_shape=jax.ShapeDtypeStruct((TOKENS, D_MODEL), F32),
        compiler_params=pltpu.CompilerParams(
            dimension_semantics=("arbitrary",), vmem_limit_bytes=VMEM_LIMIT),
        name="moe_combine",
    )(pos, x, mod5, y_sorted)


def _moe_plan(cls, rank, counts):
    count = counts[:MOE_CLASSES, 0].astype(jnp.int32)
    ends = jnp.cumsum(count)
    starts = ends - count
    class_ids = jnp.arange(MOE_CLASSES, dtype=jnp.int32)

    def lookup(table, idx):
        return jnp.sum(jnp.where(idx[..., None] == class_ids, table, 0), axis=-1)

    pos = (lookup(starts, cls) + rank).reshape(TOKENS)

    first_blk = starts // SORT_BLOCK
    n_items = jnp.where(count > 0, (ends - 1) // SORT_BLOCK - first_blk + 1, 0)
    item_end = jnp.cumsum(n_items)
    item_start = item_end - n_items
    k = jnp.arange(MAX_ITEMS, dtype=jnp.int32)
    valid = k < item_end[-1]
    kc = jnp.minimum(k, item_end[-1] - 1)
    icls = jnp.sum((item_end[None, :] <= kc[:, None]).astype(jnp.int32), axis=1)
    blk = lookup(first_blk, icls) + (kc - lookup(item_start, icls))
    first = jnp.concatenate([jnp.ones((1,), jnp.int32), (blk[1:] != blk[:-1]).astype(jnp.int32)])
    last = jnp.concatenate([(blk[1:] != blk[:-1]) | ~valid[1:], jnp.ones((1,), bool)]).astype(jnp.int32)
    group = icls // MOE_PAIRS
    elo = group * MOE_EPG + lookup(jnp.asarray(PAIR_LO * MOE_GROUPS, jnp.int32), icls)
    ehi = group * MOE_EPG + lookup(jnp.asarray(PAIR_HI * MOE_GROUPS, jnp.int32), icls)
    return pos, (blk, icls, elo, ehi, first * valid, last * valid, valid.astype(jnp.int32))


def _moe(x, nw, mod5, layer, w_route, b_route, w1, w3, w2):
    payload, cls, rank, counts = _moe_route(x, nw, mod5, layer, w_route, b_route)
    pos, items = _moe_plan(cls, rank, counts)
    tiles = (TOKENS, TILE_ROWS, TILE_LANES)
    sorted_payload = _moe_permute(pos, payload).reshape(TOKENS * TILE_ROWS, TILE_LANES)
    y_sorted = _moe_experts(items, sorted_payload, layer, w1, w3, w2)
    return _moe_combine(pos, x, mod5, layer, y_sorted.reshape(tiles))


def _pad_cols(w, width):
    return jnp.pad(w, ((0, 0), (0, width - w.shape[1])))


def kernel(x, c, ada_w, ada_b, norm_mix, norm_ffn, ssd_w_in, ssd_conv_w, ssd_conv_b, ssd_dt_bias,
           ssd_a_log, ssd_d, ssd_norm, ssd_w_out, dsa_w_in, dsa_q_norm, dsa_k_norm, dsa_w_out,
           moe_w_group, moe_b_group, moe_w_expert, moe_b_expert, moe_w1, moe_w3, moe_w2):
    depth = ada_w.shape[0]
    xt = x.reshape(TOKENS, D_MODEL)
    mod = _modulation(c, ada_w, ada_b)
    mod5 = mod.reshape(depth, BATCH, 6, 1, D_MODEL)

    head_of_col = jnp.arange(SSD_D_INNER, dtype=jnp.int32) // SSD_HEAD_DIM
    expand = (jnp.arange(SSD_DT_PAD, dtype=jnp.int32)[:, None] == head_of_col[None, :]).astype(BF16)

    for i in range(depth):
        j = i // 2
        nw_mix = norm_mix[i].reshape(1, D_MODEL)
        if i % 2 == 0:
            z, xbc, dt = _inproj(xt, nw_mix, mod5, i, 1, 0, ssd_w_in[j].astype(BF16))
            xt = _ssd_mixer(
                z, xbc, dt, xt, ssd_conv_w[j], ssd_conv_b[j].reshape(1, SSD_CONV_DIM),
                _pad_cols(ssd_dt_bias[j].reshape(1, SSD_HEADS), SSD_DT_PAD),
                _pad_cols(ssd_a_log[j].reshape(1, SSD_HEADS), SSD_DT_PAD),
                jnp.repeat(ssd_d[j], SSD_HEAD_DIM).reshape(1, SSD_D_INNER),
                ssd_norm[j].reshape(1, SSD_D_INNER), expand, ssd_w_out[j].astype(BF16), mod5, i)
        else:
            w = dsa_w_in[j]
            w_in = jnp.concatenate(
                [_pad_cols(w[:, :DSA_KI_START + IDX_HEAD_DIM], DSA_WI_START),
                 _pad_cols(w[:, DSA_KI_START + IDX_HEAD_DIM:], 128)], axis=1).astype(BF16)
            q, k, v, qi, ki, wi = _dsa_inproj(
                xt, nw_mix, mod5, i, w_in, dsa_q_norm[j].reshape(1, ATT_HEAD_DIM),
                dsa_k_norm[j].reshape(1, ATT_HEAD_DIM))
            xt = _dsa_mixer(q, k, v, qi, ki, wi, xt, dsa_w_out[j].astype(BF16), mod5, i)

        w_route = _pad_cols(jnp.concatenate([moe_w_expert[i], moe_w_group[i]], axis=1), ROUTE_PAD).astype(BF16)
        b_route = _pad_cols(jnp.concatenate([moe_b_expert[i], moe_b_group[i]]).reshape(1, -1), ROUTE_PAD)
        xt = _moe(xt, norm_ffn[i].reshape(1, D_MODEL), mod5, i, w_route, b_route,
                  moe_w1, moe_w3, moe_w2)
    return xt.reshape(BATCH, SEQ, D_MODEL)
```

```python
import functools

import jax
import jax.numpy as jnp
from jax import lax
from jax.experimental import pallas as pl
from jax.experimental.pallas import tpu as pltpu

F32 = jnp.float32
BF16 = jnp.bfloat16

D_MODEL = 1024
BATCH = 8
SEQ = 2048
TOKENS = BATCH * SEQ
EPS = 1e-6

SSD_D_INNER = 2048
SSD_HEAD_DIM = 64
SSD_HEADS = 32
SSD_GROUPS = 8
SSD_HEADS_PER_GROUP = 4
SSD_STATE = 128
SSD_CONV = 4
SSD_CHUNK = 128
SSD_GN = SSD_GROUPS * SSD_STATE
SSD_CONV_DIM = SSD_D_INNER + 2 * SSD_GN
SSD_GROUP_W = SSD_HEADS_PER_GROUP * SSD_HEAD_DIM
SSD_DT_PAD = 128
CONV_HALO = 8

ATT_HEADS = 16
ATT_KV_HEADS = 4
ATT_Q_PER_KV = 4
ATT_HEAD_DIM = 64
IDX_HEADS = 8
IDX_HEAD_DIM = 64
TOPK = 256
Q_BLOCK = 256
DSA_Q = ATT_HEADS * ATT_HEAD_DIM
DSA_KV = ATT_KV_HEADS * ATT_HEAD_DIM
DSA_QI = IDX_HEADS * IDX_HEAD_DIM
DSA_KI_START = DSA_Q + 2 * DSA_KV + DSA_QI
DSA_WI_START = DSA_KI_START + 128
DSA_PROJ_PAD = DSA_WI_START + 128

MOE_GROUPS = 4
MOE_EPG = 4
MOE_EXPERTS = 16
MOE_HIDDEN = 256
ROUTE_PAD = 128

VMEM_LIMIT = 56 * 1024 * 1024


def _sigmoid(v):
    return 1.0 / (1.0 + jnp.exp(-v))


def _silu(v):
    return v * _sigmoid(v)


def _split3(a):
    hi = a.astype(BF16)
    r = a - hi.astype(F32)
    mid = r.astype(BF16)
    lo = (r - mid.astype(F32)).astype(BF16)
    return hi, mid, lo


def _dot(a, b):
    return jnp.dot(a, b, preferred_element_type=F32)


def _dot_nt(a, b):
    return lax.dot_general(a, b, (((1,), (1,)), ((), ())), preferred_element_type=F32)


def _dot3_exact_rhs(a, m):
    hi, mid, lo = _split3(a)
    return _dot(hi, m) + _dot(mid, m) + _dot(lo, m)


def _dot3_exact_lhs(m, a):
    hi, mid, lo = _split3(a)
    return _dot(m, hi) + _dot(m, mid) + _dot(m, lo)


def _norm_mod(x, nw, scale, shift):
    ms = jnp.mean(x * x, axis=-1, keepdims=True)
    return x * lax.rsqrt(ms + EPS) * nw * (1.0 + scale) + shift


MOD_TN = 1536


def _mod_kernel(c_ref, w_ref, b_ref, o_ref):
    cond = _silu(c_ref[...]).astype(BF16)
    o_ref[...] = _dot(cond, w_ref[...].astype(BF16)) + b_ref[...]


def _modulation(c, ada_w, ada_b):
    depth = ada_w.shape[0]
    n = ada_w.shape[2]
    return pl.pallas_call(
        _mod_kernel,
        grid=(depth, n // MOD_TN),
        in_specs=[
            pl.BlockSpec((BATCH, D_MODEL), lambda i, j: (0, 0)),
            pl.BlockSpec((None, D_MODEL, MOD_TN), lambda i, j: (i, 0, j)),
            pl.BlockSpec((None, 1, MOD_TN), lambda i, j: (i, 0, j)),
        ],
        out_specs=pl.BlockSpec((None, BATCH, MOD_TN), lambda i, j: (i, 0, j)),
        out_shape=jax.ShapeDtypeStruct((depth, BATCH, n), F32),
        compiler_params=pltpu.CompilerParams(
            dimension_semantics=("arbitrary", "arbitrary"), vmem_limit_bytes=VMEM_LIMIT),
        name="adaln_mod",
    )(c, ada_w, ada_b.reshape(depth, 1, n))


def _mod_spec(layer, chunk, rows_per_batch_tile):
    return pl.BlockSpec((None, None, None, 1, D_MODEL),
                        lambda i, *_: (layer, i // rows_per_batch_tile, chunk, 0, 0))


INPROJ_TM = 256
DSA_INPROJ_TM = 512


def _inproj_kernel(x_ref, nw_ref, scale_ref, shift_ref, w_ref, z_ref, xbc_ref, dt_ref):
    h = _norm_mod(x_ref[...], nw_ref[...], scale_ref[...], shift_ref[...]).astype(BF16)
    n_main = SSD_D_INNER + SSD_CONV_DIM
    z_ref[...] = _silu(_dot(h, w_ref[:, 0:SSD_D_INNER]))
    xbc_ref[...] = _dot(h, w_ref[:, SSD_D_INNER:n_main])
    dt_ref[:, 0:SSD_HEADS] = _dot(h, w_ref[:, n_main:n_main + SSD_HEADS])
    dt_ref[:, SSD_HEADS:SSD_DT_PAD] = jnp.zeros((dt_ref.shape[0], SSD_DT_PAD - SSD_HEADS), F32)


def _inproj(x, nw, mod5, layer, scale_chunk, shift_chunk, w):
    tm = INPROJ_TM
    tiles_per_batch = SEQ // tm
    widths = (SSD_D_INNER, SSD_CONV_DIM, SSD_DT_PAD)
    return pl.pallas_call(
        _inproj_kernel,
        grid=(TOKENS // tm,),
        in_specs=[
            pl.BlockSpec((tm, D_MODEL), lambda i: (i, 0)),
            pl.BlockSpec((1, D_MODEL), lambda i: (0, 0)),
            _mod_spec(layer, scale_chunk, tiles_per_batch),
            _mod_spec(layer, shift_chunk, tiles_per_batch),
            pl.BlockSpec((D_MODEL, SSD_D_INNER + SSD_CONV_DIM + SSD_HEADS), lambda i: (0, 0)),
        ],
        out_specs=[pl.BlockSpec((tm, width), lambda i: (i, 0)) for width in widths],
        out_shape=[jax.ShapeDtypeStruct((TOKENS, width), F32) for width in widths],
        compiler_params=pltpu.CompilerParams(
            dimension_semantics=("arbitrary",), vmem_limit_bytes=VMEM_LIMIT),
        name="norm_inproj",
    )(x, nw, mod5, mod5, w)


CONV_COLS = 512


def _ssd_kernel(z_ref, xbc_ref, dt_ref, xres_ref, cw_ref, cb_ref, dtb_ref, alog_ref, de_ref, nw_ref,
                e_ref, wout_ref, gate_ref, o_ref, state_ref, ext_ref, act_ref, yn_ref):
    q = SSD_CHUNK
    c = pl.program_id(1)

    @pl.when(c == 0)
    def _():
        state_ref[...] = jnp.zeros_like(state_ref)
        ext_ref[0:CONV_HALO, :] = jnp.zeros((CONV_HALO, SSD_CONV_DIM), F32)

    @pl.when(c > 0)
    def _():
        ext_ref[0:CONV_HALO, :] = ext_ref[q:q + CONV_HALO, :]

    ext_ref[CONV_HALO:CONV_HALO + q, :] = xbc_ref[...]

    for s in range(SSD_CONV_DIM // CONV_COLS):
        cs = slice(s * CONV_COLS, (s + 1) * CONV_COLS)
        u = ext_ref[:, cs]
        acc = cw_ref[0:1, cs] * u
        for k in range(1, SSD_CONV):
            acc = pltpu.roll(acc, 1, axis=0) + cw_ref[k:k + 1, cs] * u
        act_ref[:, cs] = _silu(acc[CONV_HALO:CONV_HALO + q, :] + cb_ref[:, cs])

    dt_raw = dt_ref[...] + dtb_ref[...]
    dt = jnp.maximum(dt_raw, 0.0) + jnp.log1p(jnp.exp(-jnp.abs(dt_raw)))
    a = dt * (-jnp.exp(alog_ref[...]) * LOG2E)
    row = lax.broadcasted_iota(jnp.int32, (q, q), 0)
    col = lax.broadcasted_iota(jnp.int32, (q, q), 1)
    tril = row >= col
    acs = _dot3_exact_lhs(tril.astype(BF16), a)
    acs_t = acs.T
    expand = e_ref[...]
    acs_e = _dot3_exact_rhs(acs, expand)
    dt_e = _dot3_exact_rhs(dt, expand)
    tot_e = acs_e[q - 1:q, :]
    decay_from_start = jnp.exp2(acs_e)
    decay_to_end = jnp.exp2(tot_e - acs_e)
    chunk_decay = jnp.exp2(tot_e)

    lane_head = lax.broadcasted_iota(jnp.int32, (q, SSD_GROUP_W), 1) // SSD_HEAD_DIM
    for g in range(SSD_GROUPS):
        xs = act_ref[:, g * SSD_GROUP_W:(g + 1) * SSD_GROUP_W]
        gs = slice(g * SSD_GROUP_W, (g + 1) * SSD_GROUP_W)
        bm = act_ref[:, SSD_D_INNER + g * SSD_STATE:SSD_D_INNER + (g + 1) * SSD_STATE]
        cm = act_ref[:, SSD_D_INNER + SSD_GN + g * SSD_STATE:SSD_D_INNER + SSD_GN + (g + 1) * SSD_STATE]
        bm_t = bm.T.astype(BF16)
        cm_b = cm.astype(BF16)
        cb = _dot(cm_b, bm_t)
        xd = xs * dt_e[:, gs]
        ms = []
        xds = []
        for j in range(SSD_HEADS_PER_GROUP):
            h = g * SSD_HEADS_PER_GROUP + j
            seg = acs[:, h:h + 1] - acs_t[h:h + 1, :]
            dec = jnp.exp2(jnp.where(tril, seg, -jnp.inf))
            ms.append((cb * dec).astype(BF16))
            xds.append(jnp.where(lane_head == j, xd, 0.0).astype(BF16))
        y_diag = _dot(jnp.concatenate(ms, axis=1), jnp.concatenate(xds, axis=0))
        prev = state_ref[g]
        y_off = _dot(cm_b, prev.astype(BF16)) * decay_from_start[:, gs]
        state_ref[g] = prev * chunk_decay[:, gs] + _dot(bm_t, (xd * decay_to_end[:, gs]).astype(BF16))
        y = y_diag + y_off + xs * de_ref[:, gs]
        y = y * z_ref[:, gs]
        y = y * lax.rsqrt(jnp.mean(y * y, axis=-1, keepdims=True) + EPS) * nw_ref[:, gs]
        yn_ref[:, gs] = y.astype(BF16)

    out = _dot(yn_ref[...], wout_ref[...])
    o_ref[...] = xres_ref[...] + gate_ref[...] * out


def _ssd_mixer(z, xbc, dt, x, conv_w, conv_b, dt_bias, a_log, d_e, norm_w, expand, w_out, mod5, layer):
    q = SSD_CHUNK
    nc = SEQ // q
    tok = lambda w: pl.BlockSpec((q, w), lambda b, c: (b * nc + c, 0))
    full = lambda r, w: pl.BlockSpec((r, w), lambda b, c: (0, 0))
    return pl.pallas_call(
        _ssd_kernel,
        grid=(BATCH, nc),
        in_specs=[
            tok(SSD_D_INNER), tok(SSD_CONV_DIM), tok(SSD_DT_PAD), tok(D_MODEL),
            full(SSD_CONV, SSD_CONV_DIM), full(1, SSD_CONV_DIM), full(1, SSD_DT_PAD), full(1, SSD_DT_PAD),
            full(1, SSD_D_INNER), full(1, SSD_D_INNER), full(SSD_DT_PAD, SSD_D_INNER),
            full(SSD_D_INNER, D_MODEL),
            pl.BlockSpec((None, None, None, 1, D_MODEL), lambda b, c: (layer, b, 2, 0, 0)),
        ],
        out_specs=tok(D_MODEL),
        out_shape=jax.ShapeDtypeStruct((TOKENS, D_MODEL), F32),
        scratch_shapes=[
            pltpu.VMEM((SSD_GROUPS, SSD_STATE, SSD_GROUP_W), F32),
            pltpu.VMEM((q + CONV_HALO, SSD_CONV_DIM), F32),
            pltpu.VMEM((q, SSD_CONV_DIM), F32),
            pltpu.VMEM((q, SSD_D_INNER), BF16),
        ],
        compiler_params=pltpu.CompilerParams(
            dimension_semantics=("arbitrary", "arbitrary"), vmem_limit_bytes=VMEM_LIMIT),
        name="ssd_mixer",
    )(z, xbc, dt, x, conv_w, conv_b, dt_bias, a_log, d_e, norm_w, expand, w_out, mod5)


DSA_KEY_TILE = 256
DSA_CLASSES = 8
V_EXT = 2 * ATT_HEAD_DIM
DSA_BLOCKS_PER_CLASS = (SEQ // Q_BLOCK) // DSA_CLASSES
N_BISECT = 14
F32_MIN = float(jnp.finfo(jnp.float32).min)
LOG2E = 1.4426950408889634


def _count(mask):
    return jnp.sum(jnp.where(mask, 1.0, 0.0), axis=-1, keepdims=True)


def _select_topk(score_ref, q_pos, n_keys):
    kf = float(TOPK)
    small = (q_pos + 1) <= TOPK
    sc = score_ref[...]
    hi0 = jnp.max(sc, axis=-1, keepdims=True)
    lo0 = jnp.min(jnp.where(sc == -jnp.inf, jnp.inf, sc), axis=-1, keepdims=True)

    def bisect(_, carry):
        lo, hi = carry
        mid = lo + 0.5 * (hi - lo)
        ok = _count(score_ref[...] >= mid) >= kf
        return jnp.where(ok, mid, lo), jnp.where(ok, hi, mid)

    _, hi = lax.fori_loop(0, N_BISECT, bisect, (lo0, hi0))

    v0 = jnp.max(jnp.where(sc <= hi, sc, -jnp.inf), axis=-1, keepdims=True)
    c0 = _count(sc >= v0)
    pend0 = jnp.where((c0 >= kf) | small, 0.0, 1.0)

    def walk_cond(carry):
        return (carry[3] > 0.0) & (carry[4] < n_keys)

    def walk(carry):
        v, c, pend, _, it = carry
        s = score_ref[...]
        v2 = jnp.max(jnp.where(s < v, s, -jnp.inf), axis=-1, keepdims=True)
        c2 = _count(s >= v2)
        v = jnp.where(pend > 0.0, v2, v)
        c = jnp.where(pend > 0.0, c2, c)
        pend = jnp.where(c2 >= kf, 0.0, pend)
        return v, c, pend, jnp.max(pend), it + 1

    v, c, _, _, _ = lax.while_loop(walk_cond, walk, (v0, c0, pend0, jnp.max(pend0), jnp.int32(0)))
    thr = jnp.where(small, F32_MIN, v)
    any_tie = jnp.max(jnp.where(small, 0.0, c - kf)) > 0.0

    @pl.when(jnp.logical_not(any_tie))
    def _():
        score_ref[...] = jnp.where(score_ref[...] >= thr, 0.0, -jnp.inf)

    @pl.when(any_tie)
    def _():
        s = score_ref[...]
        key_pos = lax.broadcasted_iota(jnp.int32, (Q_BLOCK, n_keys), 1)
        gt = s > thr
        eq = s == thr
        need = kf - _count(gt)

        def body(_, carry):
            lo, hi = carry
            mid = (lo + hi) >> 1
            ok = _count((score_ref[...] == thr) & (key_pos <= mid)) >= need
            return jnp.where(ok, lo, mid), jnp.where(ok, mid, hi)

        init = (jnp.full((Q_BLOCK, 1), -1, jnp.int32), jnp.full((Q_BLOCK, 1), n_keys - 1, jnp.int32))
        cut = lax.fori_loop(0, (n_keys - 1).bit_length() + 1, body, init)[1]
        score_ref[...] = jnp.where(gt | (eq & (key_pos <= cut)), 0.0, -jnp.inf)


def _dsa_inproj_kernel(x_ref, nw_ref, scale_ref, shift_ref, w_ref, qn_ref, kn_ref, seg_ref, segt_ref,
                       q_ref, k_ref, v_ref, qi_ref, ki_ref, wi_ref):
    hd = ATT_HEAD_DIM
    h = _norm_mod(x_ref[...], nw_ref[...], scale_ref[...], shift_ref[...]).astype(BF16)

    def head_norm(t, w):
        width = t.shape[1]
        ss = _dot((t * t).astype(BF16), seg_ref[0:width, :])
        r = lax.rsqrt(ss * (1.0 / hd) + EPS)
        r_hi = r.astype(BF16)
        r_lo = (r - r_hi.astype(F32)).astype(BF16)
        return t * (_dot(r_hi, segt_ref[:, 0:width]) + _dot(r_lo, segt_ref[:, 0:width])) * w

    q = head_norm(_dot(h, w_ref[:, 0:DSA_Q]), qn_ref[...] * (hd ** -0.5 * LOG2E))
    for n in range(ATT_HEADS):
        q_ref[n] = q[:, n * hd:(n + 1) * hd].astype(BF16)
    kv = _dot(h, w_ref[:, DSA_Q:DSA_Q + 2 * DSA_KV])
    k = head_norm(kv[:, 0:DSA_KV], kn_ref[...])
    for n in range(ATT_KV_HEADS):
        k_ref[n] = k[:, n * hd:(n + 1) * hd].astype(BF16)
        v_ref[n] = jnp.concatenate([kv[:, DSA_KV + n * hd:DSA_KV + (n + 1) * hd],
                                    jnp.ones((kv.shape[0], V_EXT - hd), F32)], axis=1).astype(BF16)
    qi = _dot(h, w_ref[:, DSA_Q + 2 * DSA_KV:DSA_KI_START])
    for n in range(IDX_HEADS):
        qi_ref[n] = qi[:, n * IDX_HEAD_DIM:(n + 1) * IDX_HEAD_DIM].astype(BF16)
    ki_ref[...] = _dot(h, w_ref[:, DSA_KI_START:DSA_KI_START + IDX_HEAD_DIM]).astype(BF16)
    wi_ref[...] = _dot(h, w_ref[:, DSA_WI_START:DSA_WI_START + IDX_HEADS]) * ((IDX_HEADS * IDX_HEAD_DIM) ** -0.5)


def _dsa_inproj(x, nw, mod5, layer, w, q_norm, k_norm):
    tm = DSA_INPROJ_TM
    tiles_per_batch = SEQ // tm
    heads = lambda n: pl.BlockSpec((n, tm, ATT_HEAD_DIM), lambda i: (0, i, 0))
    head_of = jnp.arange(DSA_Q, dtype=jnp.int32) // ATT_HEAD_DIM
    seg = (head_of[:, None] == jnp.arange(128, dtype=jnp.int32)[None, :]).astype(BF16)
    q_norm = jnp.tile(q_norm, (1, ATT_HEADS))
    k_norm = jnp.tile(k_norm, (1, ATT_KV_HEADS))
    return pl.pallas_call(
        _dsa_inproj_kernel,
        grid=(TOKENS // tm,),
        in_specs=[
            pl.BlockSpec((tm, D_MODEL), lambda i: (i, 0)),
            pl.BlockSpec((1, D_MODEL), lambda i: (0, 0)),
            _mod_spec(layer, 1, tiles_per_batch),
            _mod_spec(layer, 0, tiles_per_batch),
            pl.BlockSpec((D_MODEL, DSA_PROJ_PAD), lambda i: (0, 0)),
            pl.BlockSpec((1, DSA_Q), lambda i: (0, 0)),
            pl.BlockSpec((1, DSA_KV), lambda i: (0, 0)),
            pl.BlockSpec((DSA_Q, 128), lambda i: (0, 0)),
            pl.BlockSpec((128, DSA_Q), lambda i: (0, 0)),
        ],
        out_specs=[heads(ATT_HEADS), heads(ATT_KV_HEADS),
                   pl.BlockSpec((ATT_KV_HEADS, tm, V_EXT), lambda i: (0, i, 0)), heads(IDX_HEADS),
                   pl.BlockSpec((tm, IDX_HEAD_DIM), lambda i: (i, 0)),
                   pl.BlockSpec((tm, IDX_HEADS), lambda i: (i, 0))],
        out_shape=[jax.ShapeDtypeStruct((ATT_HEADS, TOKENS, ATT_HEAD_DIM), BF16),
                   jax.ShapeDtypeStruct((ATT_KV_HEADS, TOKENS, ATT_HEAD_DIM), BF16),
                   jax.ShapeDtypeStruct((ATT_KV_HEADS, TOKENS, V_EXT), BF16),
                   jax.ShapeDtypeStruct((IDX_HEADS, TOKENS, IDX_HEAD_DIM), BF16),
                   jax.ShapeDtypeStruct((TOKENS, IDX_HEAD_DIM), BF16),
                   jax.ShapeDtypeStruct((TOKENS, IDX_HEADS), F32)],
        compiler_params=pltpu.CompilerParams(
            dimension_semantics=("arbitrary",), vmem_limit_bytes=VMEM_LIMIT),
        name="dsa_inproj",
    )(x, nw, mod5, mod5, w, q_norm, k_norm, seg, seg.T)


def _dsa_kernel(q_ref, k_ref, v_ref, qi_ref, ki_ref, wi_ref, xres_ref, wout_ref, gate_ref, o_ref,
                score_ref, ocat_ref, *, n_keys, first_block):
    hd = ATT_HEAD_DIM
    q_pos = (first_block + pl.program_id(1)) * Q_BLOCK + lax.broadcasted_iota(jnp.int32, (Q_BLOCK, 1), 0)

    wi = wi_ref[...]
    qi = qi_ref[...].reshape(IDX_HEADS * Q_BLOCK, IDX_HEAD_DIM)
    for kt in range(n_keys // DSA_KEY_TILE):
        ks = slice(kt * DSA_KEY_TILE, (kt + 1) * DSA_KEY_TILE)
        raw = _dot_nt(qi, ki_ref[ks, :])
        acc = jnp.zeros((Q_BLOCK, DSA_KEY_TILE), F32)
        for n in range(IDX_HEADS):
            acc = acc + wi[:, n:n + 1] * jnp.maximum(raw[n * Q_BLOCK:(n + 1) * Q_BLOCK, :], 0.0)
        key_pos = kt * DSA_KEY_TILE + lax.broadcasted_iota(jnp.int32, (Q_BLOCK, DSA_KEY_TILE), 1)
        score_ref[:, ks] = jnp.where(key_pos <= q_pos, acc, -jnp.inf)

    if n_keys > TOPK:
        _select_topk(score_ref, q_pos, n_keys)
        bias = score_ref[...][None, :, :]
    else:
        bias = jnp.where(score_ref[...] == -jnp.inf, -jnp.inf, 0.0)[None, :, :]

    for n in range(ATT_KV_HEADS):
        q4 = q_ref[n * ATT_Q_PER_KV:(n + 1) * ATT_Q_PER_KV].reshape(ATT_Q_PER_KV * Q_BLOCK, hd)
        s = _dot_nt(q4, k_ref[n]).reshape(ATT_Q_PER_KV, Q_BLOCK, n_keys) + bias
        p = jnp.exp2(s - jnp.max(s, axis=-1, keepdims=True))
        o = _dot(p.reshape(ATT_Q_PER_KV * Q_BLOCK, n_keys).astype(BF16), v_ref[n])
        o = o[:, 0:hd] * (1.0 / o[:, hd:hd + 1])
        for g in range(ATT_Q_PER_KV):
            col = (n * ATT_Q_PER_KV + g) * hd
            ocat_ref[:, col:col + hd] = o[g * Q_BLOCK:(g + 1) * Q_BLOCK, :]
    out = _dot(ocat_ref[...].astype(BF16), wout_ref[...])
    o_ref[...] = xres_ref[...] + gate_ref[...] * out


def _dsa_mixer(q, k, v, qi, ki, wi, x, w_out, mod5, layer):
    nb = SEQ // Q_BLOCK
    k4 = k.reshape(ATT_KV_HEADS, BATCH, SEQ, ATT_HEAD_DIM)
    v4 = v.reshape(ATT_KV_HEADS, BATCH, SEQ, V_EXT)
    ki3 = ki.reshape(BATCH, SEQ, IDX_HEAD_DIM)
    for cls in range(DSA_CLASSES):
        n_keys = (cls + 1) * (SEQ // DSA_CLASSES)
        first_block = cls * DSA_BLOCKS_PER_CLASS
        row = lambda b, i, fb=first_block: b * nb + fb + i
        heads = lambda n: pl.BlockSpec((n, Q_BLOCK, ATT_HEAD_DIM), lambda b, i: (0, row(b, i), 0))
        keys = lambda width: pl.BlockSpec((ATT_KV_HEADS, None, n_keys, width), lambda b, i: (0, b, 0, 0))
        x = pl.pallas_call(
            functools.partial(_dsa_kernel, n_keys=n_keys, first_block=first_block),
            grid=(BATCH, DSA_BLOCKS_PER_CLASS),
            in_specs=[
                heads(ATT_HEADS), keys(ATT_HEAD_DIM), keys(V_EXT), heads(IDX_HEADS),
                pl.BlockSpec((None, n_keys, IDX_HEAD_DIM), lambda b, i: (b, 0, 0)),
                pl.BlockSpec((Q_BLOCK, IDX_HEADS), lambda b, i: (row(b, i), 0)),
                pl.BlockSpec((Q_BLOCK, D_MODEL), lambda b, i: (row(b, i), 0)),
                pl.BlockSpec((DSA_Q, D_MODEL), lambda b, i: (0, 0)),
                pl.BlockSpec((None, None, None, 1, D_MODEL), lambda b, i: (layer, b, 2, 0, 0)),
            ],
            out_specs=pl.BlockSpec((Q_BLOCK, D_MODEL), lambda b, i: (row(b, i), 0)),
            out_shape=jax.ShapeDtypeStruct((TOKENS, D_MODEL), F32),
            scratch_shapes=[
                pltpu.VMEM((Q_BLOCK, n_keys), F32),
                pltpu.VMEM((Q_BLOCK, D_MODEL), F32),
            ],
            input_output_aliases={6: 0},
            compiler_params=pltpu.CompilerParams(
                dimension_semantics=("arbitrary", "arbitrary"), vmem_limit_bytes=VMEM_LIMIT),
            name=f"dsa_mixer_c{cls}",
        )(q, k4, v4, qi, ki3, wi, x, w_out, mod5)
    return x


MOE_PAIRS = MOE_EPG * (MOE_EPG - 1) // 2
MOE_CLASSES = MOE_GROUPS * MOE_PAIRS
PAIR_LO = (0, 0, 0, 1, 1, 2)
PAIR_HI = (1, 2, 3, 2, 3, 3)
ROUTE_TM = 1024
META_W = 128
META_CLASS, META_RANK, META_WLO, META_WHI = 0, 1, 2, 3
TILE_ROWS, TILE_LANES = 8, 128
H_WORDS = D_MODEL // 2
H_SUBLANES = H_WORDS // TILE_LANES
SORT_BLOCK = 256
N_SORT_BLOCKS = TOKENS // SORT_BLOCK
MAX_ITEMS = N_SORT_BLOCKS + MOE_CLASSES
PERMUTE_TM = 512
COMBINE_TM = 512


def _route_kernel(x_ref, nw_ref, scale_ref, shift_ref, wrt_ref, brt_ref, tri_ref, pay_ref, cls_ref, rank_ref,
                  cnt_ref, carry_ref):
    tm = ROUTE_TM

    @pl.when(pl.program_id(0) == 0)
    def _():
        carry_ref[...] = jnp.zeros_like(carry_ref)

    h = _norm_mod(x_ref[...], nw_ref[...], scale_ref[...], shift_ref[...])
    logits = _dot_nt(wrt_ref[...], h.astype(BF16)) + brt_ref[...]
    sub = lax.broadcasted_iota(jnp.int32, logits.shape, 0)
    neg = -jnp.inf
    big = jnp.int32(ROUTE_PAD)
    is_group = (sub >= MOE_EXPERTS) & (sub < MOE_EXPERTS + MOE_GROUPS)
    gl = jnp.where(is_group, logits, neg)
    g_max = jnp.max(gl, axis=0, keepdims=True)
    g_idx = jnp.min(jnp.where(gl == g_max, sub - MOE_EXPERTS, big), axis=0, keepdims=True)
    g_val = 1.0 / jnp.sum(jnp.exp(gl - g_max), axis=0, keepdims=True)
    in_group = (sub < MOE_EXPERTS) & ((sub // MOE_EPG) == g_idx)
    el = jnp.where(in_group, logits, neg)
    m1 = jnp.max(el, axis=0, keepdims=True)
    i1 = jnp.min(jnp.where(el == m1, sub, big), axis=0, keepdims=True)
    el2 = jnp.where(sub == i1, neg, el)
    m2 = jnp.max(el2, axis=0, keepdims=True)
    i2 = jnp.min(jnp.where(el2 == m2, sub, big), axis=0, keepdims=True)
    r = jnp.exp(m2 - m1)
    w_top1 = g_val / (1.0 + r)
    w_top2 = g_val * r / (1.0 + r)

    lo = jnp.minimum(i1, i2) - g_idx * MOE_EPG
    hi = jnp.maximum(i1, i2) - g_idx * MOE_EPG
    pair = (lo * (2 * MOE_EPG - 1 - lo)) // 2 + (hi - lo - 1)
    cls = g_idx * MOE_PAIRS + pair
    w_lo = jnp.where(i1 < i2, w_top1, w_top2)
    w_hi = jnp.where(i1 < i2, w_top2, w_top1)

    onehot = sub == cls
    before = _dot(onehot.astype(BF16), tri_ref[...]) + carry_ref[...]
    rank = jnp.sum(jnp.where(onehot, before, 0.0), axis=0, keepdims=True)
    carry_ref[...] += jnp.sum(jnp.where(onehot, 1.0, 0.0), axis=1, keepdims=True)
    cnt_ref[...] = carry_ref[...]
    cls_ref[...] = cls
    rank_ref[...] = rank.astype(jnp.int32)

    words = pltpu.pack_elementwise([h[:, 0:H_WORDS], h[:, H_WORDS:D_MODEL]], packed_dtype=BF16)
    for s in range(H_SUBLANES):
        pay_ref[pl.ds(s, tm, stride=TILE_ROWS), :] = words[:, s * TILE_LANES:(s + 1) * TILE_LANES]
    sub8 = lax.broadcasted_iota(jnp.int32, (TILE_ROWS, tm), 0)
    rec = jnp.where(sub8 == META_CLASS, cls.astype(F32),
                    jnp.where(sub8 == META_RANK, rank,
                              jnp.where(sub8 == META_WLO, w_lo, jnp.where(sub8 == META_WHI, w_hi, 0.0))))
    rec = jnp.concatenate([rec, jnp.zeros((META_W - TILE_ROWS, tm), F32)], axis=0)
    for b in range(tm // META_W):
        meta = rec[:, b * META_W:(b + 1) * META_W].T
        pay_ref[pl.ds(b * META_W * TILE_ROWS + H_SUBLANES, META_W, stride=TILE_ROWS), :] = (
            lax.bitcast_convert_type(meta, jnp.int32))
    for s in range(H_SUBLANES + 1, TILE_ROWS):
        pay_ref[pl.ds(s, tm, stride=TILE_ROWS), :] = jnp.zeros((tm, TILE_LANES), jnp.int32)


def _moe_route(x, nw, mod5, layer, w_route, b_route):
    tm = ROUTE_TM
    tiles_per_batch = SEQ // tm
    tri = (jnp.arange(tm, dtype=jnp.int32)[:, None] < jnp.arange(tm, dtype=jnp.int32)[None, :]).astype(BF16)
    return pl.pallas_call(
        _route_kernel,
        grid=(TOKENS // tm,),
        in_specs=[
            pl.BlockSpec((tm, D_MODEL), lambda i: (i, 0)),
            pl.BlockSpec((1, D_MODEL), lambda i: (0, 0)),
            _mod_spec(layer, 4, tiles_per_batch),
            _mod_spec(layer, 3, tiles_per_batch),
            pl.BlockSpec((ROUTE_PAD, D_MODEL), lambda i: (0, 0)),
            pl.BlockSpec((ROUTE_PAD, 1), lambda i: (0, 0)),
            pl.BlockSpec((tm, tm), lambda i: (0, 0)),
        ],
        out_specs=[pl.BlockSpec((tm * TILE_ROWS, TILE_LANES), lambda i: (i, 0)),
                   pl.BlockSpec((1, tm), lambda i: (0, i)),
                   pl.BlockSpec((1, tm), lambda i: (0, i)),
                   pl.BlockSpec((ROUTE_PAD, 1), lambda i: (0, 0))],
        out_shape=[jax.ShapeDtypeStruct((TOKENS * TILE_ROWS, TILE_LANES), jnp.int32),
                   jax.ShapeDtypeStruct((1, TOKENS), jnp.int32),
                   jax.ShapeDtypeStruct((1, TOKENS), jnp.int32),
                   jax.ShapeDtypeStruct((ROUTE_PAD, 1), F32)],
        scratch_shapes=[pltpu.VMEM((ROUTE_PAD, 1), F32)],
        compiler_params=pltpu.CompilerParams(
            dimension_semantics=("arbitrary",), vmem_limit_bytes=VMEM_LIMIT),
        name="moe_route",
    )(x, nw, mod5, mod5, w_route.T, b_route.reshape(ROUTE_PAD, 1), tri)


def _permute_kernel(pos_ref, src_ref, dst_hbm, stage_ref, sem):
    i = pl.program_id(0)
    slot = i % 2
    rows = PERMUTE_TM * TILE_ROWS

    def slot_wait(s):
        whole = stage_ref.at[pl.ds(s * rows, rows), :]
        pltpu.make_async_copy(whole, whole, sem.at[s]).wait()

    @pl.when(i >= 2)
    def _():
        slot_wait(slot)

    base = pl.multiple_of(slot * rows, rows)
    stage_ref[pl.ds(base, rows), :] = src_ref[...]

    def issue(r, _):
        pltpu.make_async_copy(stage_ref.at[pl.ds(base + r * TILE_ROWS, TILE_ROWS), :],
                              dst_hbm.at[pos_ref[i * PERMUTE_TM + r]], sem.at[slot]).start()
        return 0

    lax.fori_loop(0, PERMUTE_TM, issue, 0, unroll=16)

    @pl.when(i == pl.num_programs(0) - 1)
    def _():
        slot_wait(1 - slot)
        slot_wait(slot)


def _moe_permute(pos, payload):
    rows = PERMUTE_TM * TILE_ROWS
    return pl.pallas_call(
        _permute_kernel,
        grid_spec=pltpu.PrefetchScalarGridSpec(
            num_scalar_prefetch=1, grid=(TOKENS // PERMUTE_TM,),
            in_specs=[pl.BlockSpec((rows, TILE_LANES), lambda i, pos: (i, 0))],
            out_specs=pl.BlockSpec(memory_space=pl.ANY),
            scratch_shapes=[pltpu.VMEM((2 * rows, TILE_LANES), jnp.int32), pltpu.SemaphoreType.DMA((2,))]),
        out_shape=jax.ShapeDtypeStruct((TOKENS, TILE_ROWS, TILE_LANES), jnp.int32),
        compiler_params=pltpu.CompilerParams(
            dimension_semantics=("arbitrary",), vmem_limit_bytes=VMEM_LIMIT),
        name="moe_permute",
    )(pos, payload)


def _experts_kernel(blk_ref, cls_ref, elo_ref, ehi_ref, first_ref, last_ref, valid_ref,
                    pay_ref, w1lo_ref, w3lo_ref, w2lo_ref, w1hi_ref, w3hi_ref, w2hi_ref, o_ref, acc_ref):
    k = pl.program_id(0)
    rows = SORT_BLOCK

    @pl.when(valid_ref[k] == 1)
    def _():
        def sublane(s):
            return pay_ref[pl.ds(s, rows, stride=TILE_ROWS), :]

        halves = [[pltpu.unpack_elementwise(sublane(s), index=i, packed_dtype=BF16, unpacked_dtype=F32)
                   for s in range(H_SUBLANES)] for i in range(2)]
        hb = jnp.concatenate(halves[0] + halves[1], axis=1).astype(BF16)
        meta = lax.bitcast_convert_type(sublane(H_SUBLANES), F32)
        mine = meta[:, META_CLASS:META_CLASS + 1] == cls_ref[k].astype(F32)
        w_lo = jnp.where(mine, meta[:, META_WLO:META_WLO + 1], 0.0)
        w_hi = jnp.where(mine, meta[:, META_WHI:META_WHI + 1], 0.0)
        bf = lambda w_ref: w_ref[...].astype(BF16)
        hid_lo = _silu(_dot(hb, bf(w1lo_ref))) * _dot(hb, bf(w3lo_ref)) * w_lo
        hid_hi = _silu(_dot(hb, bf(w1hi_ref))) * _dot(hb, bf(w3hi_ref)) * w_hi
        y = _dot(hid_lo.astype(BF16), bf(w2lo_ref)) + _dot(hid_hi.astype(BF16), bf(w2hi_ref))

        @pl.when(first_ref[k] == 1)
        def _():
            acc_ref[...] = y

        @pl.when(first_ref[k] == 0)
        def _():
            acc_ref[...] += y

        @pl.when(last_ref[k] == 1)
        def _():
            for s in range(TILE_ROWS):
                o_ref[pl.ds(s, rows, stride=TILE_ROWS), :] = acc_ref[:, s * TILE_LANES:(s + 1) * TILE_LANES]


def _moe_experts(items, payload_sorted, layer, w1, w3, w2):
    blk, cls, elo, ehi, first, last, valid = items
    tiles = pl.BlockSpec((SORT_BLOCK * TILE_ROWS, TILE_LANES), lambda k, blk, *_: (blk[k], 0))
    w_in = lambda which: pl.BlockSpec(
        (None, None, D_MODEL, MOE_HIDDEN), lambda k, blk, cls, elo, ehi, *_: (layer, (elo, ehi)[which][k], 0, 0))
    w_out = lambda which: pl.BlockSpec(
        (None, None, MOE_HIDDEN, D_MODEL), lambda k, blk, cls, elo, ehi, *_: (layer, (elo, ehi)[which][k], 0, 0))
    return pl.pallas_call(
        _experts_kernel,
        grid_spec=pltpu.PrefetchScalarGridSpec(
            num_scalar_prefetch=7, grid=(MAX_ITEMS,),
            in_specs=[tiles, w_in(0), w_in(0), w_out(0), w_in(1), w_in(1), w_out(1)],
            out_specs=tiles,
            scratch_shapes=[pltpu.VMEM((SORT_BLOCK, D_MODEL), F32)]),
        out_shape=jax.ShapeDtypeStruct((TOKENS * TILE_ROWS, TILE_LANES), F32),
        compiler_params=pltpu.CompilerParams(
            dimension_semantics=("arbitrary",), vmem_limit_bytes=VMEM_LIMIT),
        name="moe_experts",
    )(blk, cls, elo, ehi, first, last, valid, payload_sorted, w1, w3, w2, w1, w3, w2)


def _combine_kernel(pos_ref, x_ref, gate_ref, y_hbm, o_ref, buf_ref, sem):
    tm = COMBINE_TM
    i = pl.program_id(0)
    n = pl.num_programs(0)

    def gather(tile, slot):
        def issue(r, _):
            pltpu.make_async_copy(y_hbm.at[pos_ref[tile * tm + r]],
                                  buf_ref.at[pl.ds((slot * tm + r) * TILE_ROWS, TILE_ROWS), :], sem.at[slot]).start()
            return 0
        lax.fori_loop(0, tm, issue, 0, unroll=16)

    @pl.when(i == 0)
    def _():
        gather(0, 0)

    @pl.when(i + 1 < n)
    def _():
        gather(i + 1, (i + 1) % 2)

    slot = i % 2
    base = slot * tm * TILE_ROWS
    whole_slot = buf_ref.at[pl.ds(base, tm * TILE_ROWS), :]
    pltpu.make_async_copy(whole_slot, whole_slot, sem.at[slot]).wait()
    y = jnp.concatenate([buf_ref[pl.ds(base + s, tm, stride=TILE_ROWS), :] for s in range(TILE_ROWS)], axis=1)
    o_ref[...] = x_ref[...] + gate_ref[...] * y


def _moe_combine(pos, x, mod5, layer, y_sorted):
    tm = COMBINE_TM
    tiles_per_batch = SEQ // tm
    return pl.pallas_call(
        _combine_kernel,
        grid_spec=pltpu.PrefetchScalarGridSpec(
            num_scalar_prefetch=1, grid=(TOKENS // tm,),
            in_specs=[
                pl.BlockSpec((tm, D_MODEL), lambda i, pos: (i, 0)),
                pl.BlockSpec((None, None, None, 1, D_MODEL),
                             lambda i, pos: (layer, i // tiles_per_batch, 5, 0, 0)),
                pl.BlockSpec(memory_space=pl.ANY),
            ],
            out_specs=pl.BlockSpec((tm, D_MODEL), lambda i, pos: (i, 0)),
            scratch_shapes=[pltpu.VMEM((2 * tm * TILE_ROWS, TILE_LANES), F32), pltpu.SemaphoreType.DMA((2,))]),
        out_shape=jax.ShapeDtypeStruct((TOKENS, D_MODEL), F32),
        compiler_params=pltpu.CompilerParams(
            dimension_semantics=("arbitrary",), vmem_limit_bytes=VMEM_LIMIT),
        name="moe_combine",
    )(pos, x, mod5, y_sorted)


def _moe_plan(cls, rank, counts):
    count = counts[:MOE_CLASSES, 0].astype(jnp.int32)
    ends = jnp.cumsum(count)
    starts = ends - count
    class_ids = jnp.arange(MOE_CLASSES, dtype=jnp.int32)

    def lookup(table, idx):
        return jnp.sum(jnp.where(idx[..., None] == class_ids, table, 0), axis=-1)

    pos = (lookup(starts, cls) + rank).reshape(TOKENS)

    first_blk = starts // SORT_BLOCK
    n_items = jnp.where(count > 0, (ends - 1) // SORT_BLOCK - first_blk + 1, 0)
    item_end = jnp.cumsum(n_items)
    item_start = item_end - n_items
    k = jnp.arange(MAX_ITEMS, dtype=jnp.int32)
    valid = k < item_end[-1]
    kc = jnp.minimum(k, item_end[-1] - 1)
    icls = jnp.sum((item_end[None, :] <= kc[:, None]).astype(jnp.int32), axis=1)
    blk = lookup(first_blk, icls) + (kc - lookup(item_start, icls))
    first = jnp.concatenate([jnp.ones((1,), jnp.int32), (blk[1:] != blk[:-1]).astype(jnp.int32)])
    last = jnp.concatenate([(blk[1:] != blk[:-1]) | ~valid[1:], jnp.ones((1,), bool)]).astype(jnp.int32)
    group = icls // MOE_PAIRS
    elo = group * MOE_EPG + lookup(jnp.asarray(PAIR_LO * MOE_GROUPS, jnp.int32), icls)
    ehi = group * MOE_EPG + lookup(jnp.asarray(PAIR_HI * MOE_GROUPS, jnp.int32), icls)
    return pos, (blk, icls, elo, ehi, first * valid, last * valid, valid.astype(jnp.int32))


def _moe(x, nw, mod5, layer, w_route, b_route, w1, w3, w2):
    payload, cls, rank, counts = _moe_route(x, nw, mod5, layer, w_route, b_route)
    pos, items = _moe_plan(cls, rank, counts)
    tiles = (TOKENS, TILE_ROWS, TILE_LANES)
    sorted_payload = _moe_permute(pos, payload).reshape(TOKENS * TILE_ROWS, TILE_LANES)
    y_sorted = _moe_experts(items, sorted_payload, layer, w1, w3, w2)
    return _moe_combine(pos, x, mod5, layer, y_sorted.reshape(tiles))


def _pad_cols(w, width):
    return jnp.pad(w, ((0, 0), (0, width - w.shape[1])))


def kernel(x, c, ada_w, ada_b, norm_mix, norm_ffn, ssd_w_in, ssd_conv_w, ssd_conv_b, ssd_dt_bias,
           ssd_a_log, ssd_d, ssd_norm, ssd_w_out, dsa_w_in, dsa_q_norm, dsa_k_norm, dsa_w_out,
           moe_w_group, moe_b_group, moe_w_expert, moe_b_expert, moe_w1, moe_w3, moe_w2):
    depth = ada_w.shape[0]
    xt = x.reshape(TOKENS, D_MODEL)
    mod = _modulation(c, ada_w, ada_b)
    mod5 = mod.reshape(depth, BATCH, 6, 1, D_MODEL)

    head_of_col = jnp.arange(SSD_D_INNER, dtype=jnp.int32) // SSD_HEAD_DIM
    expand = (jnp.arange(SSD_DT_PAD, dtype=jnp.int32)[:, None] == head_of_col[None, :]).astype(BF16)

    for i in range(depth):
        j = i // 2
        nw_mix = norm_mix[i].reshape(1, D_MODEL)
        if i % 2 == 0:
            z, xbc, dt = _inproj(xt, nw_mix, mod5, i, 1, 0, ssd_w_in[j].astype(BF16))
            xt = _ssd_mixer(
                z, xbc, dt, xt, ssd_conv_w[j], ssd_conv_b[j].reshape(1, SSD_CONV_DIM),
                _pad_cols(ssd_dt_bias[j].reshape(1, SSD_HEADS), SSD_DT_PAD),
                _pad_cols(ssd_a_log[j].reshape(1, SSD_HEADS), SSD_DT_PAD),
                jnp.repeat(ssd_d[j], SSD_HEAD_DIM).reshape(1, SSD_D_INNER),
                ssd_norm[j].reshape(1, SSD_D_INNER), expand, ssd_w_out[j].astype(BF16), mod5, i)
        else:
            w = dsa_w_in[j]
            w_in = jnp.concatenate(
                [_pad_cols(w[:, :DSA_KI_START + IDX_HEAD_DIM], DSA_WI_START),
                 _pad_cols(w[:, DSA_KI_START + IDX_HEAD_DIM:], 128)], axis=1).astype(BF16)
            q, k, v, qi, ki, wi = _dsa_inproj(
                xt, nw_mix, mod5, i, w_in, dsa_q_norm[j].reshape(1, ATT_HEAD_DIM),
                dsa_k_norm[j].reshape(1, ATT_HEAD_DIM))
            xt = _dsa_mixer(q, k, v, qi, ki, wi, xt, dsa_w_out[j].astype(BF16), mod5, i)

        w_route = _pad_cols(jnp.concatenate([moe_w_expert[i], moe_w_group[i]], axis=1), ROUTE_PAD).astype(BF16)
        b_route = _pad_cols(jnp.concatenate([moe_b_expert[i], moe_b_group[i]]).reshape(1, -1), ROUTE_PAD)
        xt = _moe(xt, norm_ffn[i].reshape(1, D_MODEL), mod5, i, w_route, b_route,
                  moe_w1, moe_w3, moe_w2)
    return xt.reshape(BATCH, SEQ, D_MODEL)
```

```python
import functools

import jax
import jax.numpy as jnp
from jax import lax
from jax.experimental import pallas as pl
from jax.experimental.pallas import tpu as pltpu

F32 = jnp.float32
BF16 = jnp.bfloat16

D_MODEL = 1024
BATCH = 8
SEQ = 2048
TOKENS = BATCH * SEQ
EPS = 1e-6

SSD_D_INNER = 2048
SSD_HEAD_DIM = 64
SSD_HEADS = 32
SSD_GROUPS = 8
SSD_HEADS_PER_GROUP = 4
SSD_STATE = 128
SSD_CONV = 4
SSD_CHUNK = 128
SSD_GN = SSD_GROUPS * SSD_STATE
SSD_CONV_DIM = SSD_D_INNER + 2 * SSD_GN
SSD_GROUP_W = SSD_HEADS_PER_GROUP * SSD_HEAD_DIM
SSD_DT_PAD = 128
CONV_HALO = 8

ATT_HEADS = 16
ATT_KV_HEADS = 4
ATT_Q_PER_KV = 4
ATT_HEAD_DIM = 64
IDX_HEADS = 8
IDX_HEAD_DIM = 64
TOPK = 256
Q_BLOCK = 256
DSA_Q = ATT_HEADS * ATT_HEAD_DIM
DSA_KV = ATT_KV_HEADS * ATT_HEAD_DIM
DSA_QI = IDX_HEADS * IDX_HEAD_DIM
DSA_KI_START = DSA_Q + 2 * DSA_KV + DSA_QI
DSA_WI_START = DSA_KI_START + 128
DSA_PROJ_PAD = DSA_WI_START + 128

MOE_GROUPS = 4
MOE_EPG = 4
MOE_EXPERTS = 16
MOE_HIDDEN = 256
ROUTE_PAD = 128

VMEM_LIMIT = 56 * 1024 * 1024


def _sigmoid(v):
    return 1.0 / (1.0 + jnp.exp(-v))


def _silu(v):
    return v * _sigmoid(v)


def _split3(a):
    hi = a.astype(BF16)
    r = a - hi.astype(F32)
    mid = r.astype(BF16)
    lo = (r - mid.astype(F32)).astype(BF16)
    return hi, mid, lo


def _dot(a, b):
    return jnp.dot(a, b, preferred_element_type=F32)


def _dot_nt(a, b):
    return lax.dot_general(a, b, (((1,), (1,)), ((), ())), preferred_element_type=F32)


def _dot3_exact_rhs(a, m):
    hi, mid, lo = _split3(a)
    return _dot(hi, m) + _dot(mid, m) + _dot(lo, m)


def _dot3_exact_lhs(m, a):
    hi, mid, lo = _split3(a)
    return _dot(m, hi) + _dot(m, mid) + _dot(m, lo)


def _norm_mod(x, nw, scale, shift):
    ms = jnp.mean(x * x, axis=-1, keepdims=True)
    return x * lax.rsqrt(ms + EPS) * nw * (1.0 + scale) + shift


MOD_TN = 1536


def _mod_kernel(c_ref, w_ref, b_ref, o_ref):
    cond = _silu(c_ref[...]).astype(BF16)
    o_ref[...] = _dot(cond, w_ref[...].astype(BF16)) + b_ref[...]


def _modulation(c, ada_w, ada_b):
    depth = ada_w.shape[0]
    n = ada_w.shape[2]
    return pl.pallas_call(
        _mod_kernel,
        grid=(depth, n // MOD_TN),
        in_specs=[
            pl.BlockSpec((BATCH, D_MODEL), lambda i, j: (0, 0)),
            pl.BlockSpec((None, D_MODEL, MOD_TN), lambda i, j: (i, 0, j)),
            pl.BlockSpec((None, 1, MOD_TN), lambda i, j: (i, 0, j)),
        ],
        out_specs=pl.BlockSpec((None, BATCH, MOD_TN), lambda i, j: (i, 0, j)),
        out_shape=jax.ShapeDtypeStruct((depth, BATCH, n), F32),
        compiler_params=pltpu.CompilerParams(
            dimension_semantics=("arbitrary", "arbitrary"), vmem_limit_bytes=VMEM_LIMIT),
        name="adaln_mod",
    )(c, ada_w, ada_b.reshape(depth, 1, n))


def _mod_spec(layer, chunk, rows_per_batch_tile):
    return pl.BlockSpec((None, None, None, 1, D_MODEL),
                        lambda i, *_: (layer, i // rows_per_batch_tile, chunk, 0, 0))


INPROJ_TM = 256
DSA_INPROJ_TM = 512


def _inproj_kernel(x_ref, nw_ref, scale_ref, shift_ref, w_ref, z_ref, xbc_ref, dt_ref):
    h = _norm_mod(x_ref[...], nw_ref[...], scale_ref[...], shift_ref[...]).astype(BF16)
    n_main = SSD_D_INNER + SSD_CONV_DIM
    z_ref[...] = _silu(_dot(h, w_ref[:, 0:SSD_D_INNER]))
    xbc_ref[...] = _dot(h, w_ref[:, SSD_D_INNER:n_main])
    dt_ref[:, 0:SSD_HEADS] = _dot(h, w_ref[:, n_main:n_main + SSD_HEADS])
    dt_ref[:, SSD_HEADS:SSD_DT_PAD] = jnp.zeros((dt_ref.shape[0], SSD_DT_PAD - SSD_HEADS), F32)


def _inproj(x, nw, mod5, layer, scale_chunk, shift_chunk, w):
    tm = INPROJ_TM
    tiles_per_batch = SEQ // tm
    widths = (SSD_D_INNER, SSD_CONV_DIM, SSD_DT_PAD)
    return pl.pallas_call(
        _inproj_kernel,
        grid=(TOKENS // tm,),
        in_specs=[
            pl.BlockSpec((tm, D_MODEL), lambda i: (i, 0)),
            pl.BlockSpec((1, D_MODEL), lambda i: (0, 0)),
            _mod_spec(layer, scale_chunk, tiles_per_batch),
            _mod_spec(layer, shift_chunk, tiles_per_batch),
            pl.BlockSpec((D_MODEL, SSD_D_INNER + SSD_CONV_DIM + SSD_HEADS), lambda i: (0, 0)),
        ],
        out_specs=[pl.BlockSpec((tm, width), lambda i: (i, 0)) for width in widths],
        out_shape=[jax.ShapeDtypeStruct((TOKENS, width), F32) for width in widths],
        compiler_params=pltpu.CompilerParams(
            dimension_semantics=("arbitrary",), vmem_limit_bytes=VMEM_LIMIT),
        name="norm_inproj",
    )(x, nw, mod5, mod5, w)


CONV_COLS = 512


def _ssd_kernel(z_ref, xbc_ref, dt_ref, xres_ref, cw_ref, cb_ref, dtb_ref, alog_ref, de_ref, nw_ref,
                e_ref, wout_ref, gate_ref, o_ref, state_ref, ext_ref, act_ref, yn_ref):
    q = SSD_CHUNK
    c = pl.program_id(1)

    @pl.when(c == 0)
    def _():
        state_ref[...] = jnp.zeros_like(state_ref)
        ext_ref[0:CONV_HALO, :] = jnp.zeros((CONV_HALO, SSD_CONV_DIM), F32)

    @pl.when(c > 0)
    def _():
        ext_ref[0:CONV_HALO, :] = ext_ref[q:q + CONV_HALO, :]

    ext_ref[CONV_HALO:CONV_HALO + q, :] = xbc_ref[...]

    for s in range(SSD_CONV_DIM // CONV_COLS):
        cs = slice(s * CONV_COLS, (s + 1) * CONV_COLS)
        u = ext_ref[:, cs]
        acc = cw_ref[0:1, cs] * u
        for k in range(1, SSD_CONV):
            acc = pltpu.roll(acc, 1, axis=0) + cw_ref[k:k + 1, cs] * u
        act_ref[:, cs] = _silu(acc[CONV_HALO:CONV_HALO + q, :] + cb_ref[:, cs])

    dt_raw = dt_ref[...] + dtb_ref[...]
    dt = jnp.maximum(dt_raw, 0.0) + jnp.log1p(jnp.exp(-jnp.abs(dt_raw)))
    a = dt * (-jnp.exp(alog_ref[...]) * LOG2E)
    row = lax.broadcasted_iota(jnp.int32, (q, q), 0)
    col = lax.broadcasted_iota(jnp.int32, (q, q), 1)
    tril = row >= col
    acs = _dot3_exact_lhs(tril.astype(BF16), a)
    acs_t = acs.T
    expand = e_ref[...]
    acs_e = _dot3_exact_rhs(acs, expand)
    dt_e = _dot3_exact_rhs(dt, expand)
    tot_e = acs_e[q - 1:q, :]
    decay_from_start = jnp.exp2(acs_e)
    decay_to_end = jnp.exp2(tot_e - acs_e)
    chunk_decay = jnp.exp2(tot_e)

    lane_head = lax.broadcasted_iota(jnp.int32, (q, SSD_GROUP_W), 1) // SSD_HEAD_DIM
    for g in range(SSD_GROUPS):
        xs = act_ref[:, g * SSD_GROUP_W:(g + 1) * SSD_GROUP_W]
        gs = slice(g * SSD_GROUP_W, (g + 1) * SSD_GROUP_W)
        bm = act_ref[:, SSD_D_INNER + g * SSD_STATE:SSD_D_INNER + (g + 1) * SSD_STATE]
        cm = act_ref[:, SSD_D_INNER + SSD_GN + g * SSD_STATE:SSD_D_INNER + SSD_GN + (g + 1) * SSD_STATE]
        bm_t = bm.T.astype(BF16)
        cm_b = cm.astype(BF16)
        cb = _dot(cm_b, bm_t)
        xd = xs * dt_e[:, gs]
        ms = []
        xds = []
        for j in range(SSD_HEADS_PER_GROUP):
            h = g * SSD_HEADS_PER_GROUP + j
            seg = acs[:, h:h + 1] - acs_t[h:h + 1, :]
            dec = jnp.exp2(jnp.where(tril, seg, -jnp.inf))
            ms.append((cb * dec).astype(BF16))
            xds.append(jnp.where(lane_head == j, xd, 0.0).astype(BF16))
        y_diag = _dot(jnp.concatenate(ms, axis=1), jnp.concatenate(xds, axis=0))
        prev = state_ref[g]
        y_off = _dot(cm_b, prev.astype(BF16)) * decay_from_start[:, gs]
        state_ref[g] = prev * chunk_decay[:, gs] + _dot(bm_t, (xd * decay_to_end[:, gs]).astype(BF16))
        y = y_diag + y_off + xs * de_ref[:, gs]
        y = y * z_ref[:, gs]
        y = y * lax.rsqrt(jnp.mean(y * y, axis=-1, keepdims=True) + EPS) * nw_ref[:, gs]
        yn_ref[:, gs] = y.astype(BF16)

    out = _dot(yn_ref[...], wout_ref[...])
    o_ref[...] = xres_ref[...] + gate_ref[...] * out


def _ssd_mixer(z, xbc, dt, x, conv_w, conv_b, dt_bias, a_log, d_e, norm_w, expand, w_out, mod5, layer):
    q = SSD_CHUNK
    nc = SEQ // q
    tok = lambda w: pl.BlockSpec((q, w), lambda b, c: (b * nc + c, 0))
    full = lambda r, w: pl.BlockSpec((r, w), lambda b, c: (0, 0))
    return pl.pallas_call(
        _ssd_kernel,
        grid=(BATCH, nc),
        in_specs=[
            tok(SSD_D_INNER), tok(SSD_CONV_DIM), tok(SSD_DT_PAD), tok(D_MODEL),
            full(SSD_CONV, SSD_CONV_DIM), full(1, SSD_CONV_DIM), full(1, SSD_DT_PAD), full(1, SSD_DT_PAD),
            full(1, SSD_D_INNER), full(1, SSD_D_INNER), full(SSD_DT_PAD, SSD_D_INNER),
            full(SSD_D_INNER, D_MODEL),
            pl.BlockSpec((None, None, None, 1, D_MODEL), lambda b, c: (layer, b, 2, 0, 0)),
        ],
        out_specs=tok(D_MODEL),
        out_shape=jax.ShapeDtypeStruct((TOKENS, D_MODEL), F32),
        scratch_shapes=[
            pltpu.VMEM((SSD_GROUPS, SSD_STATE, SSD_GROUP_W), F32),
            pltpu.VMEM((q + CONV_HALO, SSD_CONV_DIM), F32),
            pltpu.VMEM((q, SSD_CONV_DIM), F32),
            pltpu.VMEM((q, SSD_D_INNER), BF16),
        ],
        compiler_params=pltpu.CompilerParams(
            dimension_semantics=("arbitrary", "arbitrary"), vmem_limit_bytes=VMEM_LIMIT),
        name="ssd_mixer",
    )(z, xbc, dt, x, conv_w, conv_b, dt_bias, a_log, d_e, norm_w, expand, w_out, mod5)


DSA_KEY_TILE = 256
DSA_CLASSES = 8
V_EXT = 2 * ATT_HEAD_DIM
DSA_BLOCKS_PER_CLASS = (SEQ // Q_BLOCK) // DSA_CLASSES
DSA_WIDE_KEYS = 1024
N_BISECT = 12
F32_MIN = float(jnp.finfo(jnp.float32).min)
LOG2E = 1.4426950408889634


def _count(mask):
    return jnp.sum(jnp.where(mask, 1.0, 0.0), axis=-1, keepdims=True)


def _select_topk(score_ref, q_pos, n_keys):
    kf = float(TOPK)
    small = (q_pos + 1) <= TOPK
    sc = score_ref[...]
    hi0 = jnp.max(sc, axis=-1, keepdims=True)
    lo0 = jnp.min(jnp.where(sc == -jnp.inf, jnp.inf, sc), axis=-1, keepdims=True)

    def bisect(_, carry):
        lo, hi = carry
        mid = lo + 0.5 * (hi - lo)
        ok = _count(score_ref[...] >= mid) >= kf
        return jnp.where(ok, mid, lo), jnp.where(ok, hi, mid)

    _, hi = lax.fori_loop(0, N_BISECT, bisect, (lo0, hi0))

    v0 = jnp.max(jnp.where(sc <= hi, sc, -jnp.inf), axis=-1, keepdims=True)
    c0 = _count(sc >= v0)
    pend0 = jnp.where((c0 >= kf) | small, 0.0, 1.0)

    def walk_cond(carry):
        return (carry[3] > 0.0) & (carry[4] < n_keys)

    def walk(carry):
        v, c, pend, _, it = carry
        s = score_ref[...]
        v2 = jnp.max(jnp.where(s < v, s, -jnp.inf), axis=-1, keepdims=True)
        c2 = _count(s >= v2)
        v = jnp.where(pend > 0.0, v2, v)
        c = jnp.where(pend > 0.0, c2, c)
        pend = jnp.where(c2 >= kf, 0.0, pend)
        return v, c, pend, jnp.max(pend), it + 1

    v, c, _, _, _ = lax.while_loop(walk_cond, walk, (v0, c0, pend0, jnp.max(pend0), jnp.int32(0)))
    thr = jnp.where(small, F32_MIN, v)
    any_tie = jnp.max(jnp.where(small, 0.0, c - kf)) > 0.0

    @pl.when(jnp.logical_not(any_tie))
    def _():
        score_ref[...] = jnp.where(score_ref[...] >= thr, 0.0, -jnp.inf)

    @pl.when(any_tie)
    def _():
        s = score_ref[...]
        key_pos = lax.broadcasted_iota(jnp.int32, (Q_BLOCK, n_keys), 1)
        gt = s > thr
        eq = s == thr
        need = kf - _count(gt)

        def body(_, carry):
            lo, hi = carry
            mid = (lo + hi) >> 1
            ok = _count((score_ref[...] == thr) & (key_pos <= mid)) >= need
            return jnp.where(ok, lo, mid), jnp.where(ok, mid, hi)

        init = (jnp.full((Q_BLOCK, 1), -1, jnp.int32), jnp.full((Q_BLOCK, 1), n_keys - 1, jnp.int32))
        cut = lax.fori_loop(0, (n_keys - 1).bit_length() + 1, body, init)[1]
        score_ref[...] = jnp.where(gt | (eq & (key_pos <= cut)), 0.0, -jnp.inf)


def _dsa_inproj_kernel(x_ref, nw_ref, scale_ref, shift_ref, w_ref, qn_ref, kn_ref, seg_ref, segt_ref,
                       q_ref, k_ref, v_ref, qi_ref, ki_ref, wi_ref):
    hd = ATT_HEAD_DIM
    h = _norm_mod(x_ref[...], nw_ref[...], scale_ref[...], shift_ref[...]).astype(BF16)

    def head_norm(t, w):
        width = t.shape[1]
        ss = _dot((t * t).astype(BF16), seg_ref[0:width, :])
        r = lax.rsqrt(ss * (1.0 / hd) + EPS)
        r_hi = r.astype(BF16)
        r_lo = (r - r_hi.astype(F32)).astype(BF16)
        return t * (_dot(r_hi, segt_ref[:, 0:width]) + _dot(r_lo, segt_ref[:, 0:width])) * w

    q = head_norm(_dot(h, w_ref[:, 0:DSA_Q]), qn_ref[...] * (hd ** -0.5 * LOG2E))
    for n in range(ATT_HEADS):
        q_ref[n] = q[:, n * hd:(n + 1) * hd].astype(BF16)
    kv = _dot(h, w_ref[:, DSA_Q:DSA_Q + 2 * DSA_KV])
    k = head_norm(kv[:, 0:DSA_KV], kn_ref[...])
    for n in range(ATT_KV_HEADS):
        k_ref[n] = k[:, n * hd:(n + 1) * hd].astype(BF16)
        v_ref[n] = jnp.concatenate([kv[:, DSA_KV + n * hd:DSA_KV + (n + 1) * hd],
                                    jnp.ones((kv.shape[0], V_EXT - hd), F32)], axis=1).astype(BF16)
    qi = _dot(h, w_ref[:, DSA_Q + 2 * DSA_KV:DSA_KI_START])
    for n in range(IDX_HEADS):
        qi_ref[n] = qi[:, n * IDX_HEAD_DIM:(n + 1) * IDX_HEAD_DIM].astype(BF16)
    ki_ref[...] = _dot(h, w_ref[:, DSA_KI_START:DSA_KI_START + IDX_HEAD_DIM]).astype(BF16)
    wi_ref[...] = _dot(h, w_ref[:, DSA_WI_START:DSA_WI_START + IDX_HEADS]) * ((IDX_HEADS * IDX_HEAD_DIM) ** -0.5)


def _dsa_inproj(x, nw, mod5, layer, w, q_norm, k_norm):
    tm = DSA_INPROJ_TM
    tiles_per_batch = SEQ // tm
    heads = lambda n: pl.BlockSpec((n, tm, ATT_HEAD_DIM), lambda i: (0, i, 0))
    head_of = jnp.arange(DSA_Q, dtype=jnp.int32) // ATT_HEAD_DIM
    seg = (head_of[:, None] == jnp.arange(128, dtype=jnp.int32)[None, :]).astype(BF16)
    q_norm = jnp.tile(q_norm, (1, ATT_HEADS))
    k_norm = jnp.tile(k_norm, (1, ATT_KV_HEADS))
    return pl.pallas_call(
        _dsa_inproj_kernel,
        grid=(TOKENS // tm,),
        in_specs=[
            pl.BlockSpec((tm, D_MODEL), lambda i: (i, 0)),
            pl.BlockSpec((1, D_MODEL), lambda i: (0, 0)),
            _mod_spec(layer, 1, tiles_per_batch),
            _mod_spec(layer, 0, tiles_per_batch),
            pl.BlockSpec((D_MODEL, DSA_PROJ_PAD), lambda i: (0, 0)),
            pl.BlockSpec((1, DSA_Q), lambda i: (0, 0)),
            pl.BlockSpec((1, DSA_KV), lambda i: (0, 0)),
            pl.BlockSpec((DSA_Q, 128), lambda i: (0, 0)),
            pl.BlockSpec((128, DSA_Q), lambda i: (0, 0)),
        ],
        out_specs=[heads(ATT_HEADS), heads(ATT_KV_HEADS),
                   pl.BlockSpec((ATT_KV_HEADS, tm, V_EXT), lambda i: (0, i, 0)), heads(IDX_HEADS),
                   pl.BlockSpec((tm, IDX_HEAD_DIM), lambda i: (i, 0)),
                   pl.BlockSpec((tm, IDX_HEADS), lambda i: (i, 0))],
        out_shape=[jax.ShapeDtypeStruct((ATT_HEADS, TOKENS, ATT_HEAD_DIM), BF16),
                   jax.ShapeDtypeStruct((ATT_KV_HEADS, TOKENS, ATT_HEAD_DIM), BF16),
                   jax.ShapeDtypeStruct((ATT_KV_HEADS, TOKENS, V_EXT), BF16),
                   jax.ShapeDtypeStruct((IDX_HEADS, TOKENS, IDX_HEAD_DIM), BF16),
                   jax.ShapeDtypeStruct((TOKENS, IDX_HEAD_DIM), BF16),
                   jax.ShapeDtypeStruct((TOKENS, IDX_HEADS), F32)],
        compiler_params=pltpu.CompilerParams(
            dimension_semantics=("arbitrary",), vmem_limit_bytes=VMEM_LIMIT),
        name="dsa_inproj",
    )(x, nw, mod5, mod5, w, q_norm, k_norm, seg, seg.T)


def _dsa_kernel(q_ref, k_ref, v_ref, qi_ref, ki_ref, wi_ref, xres_ref, wout_ref, gate_ref, o_ref,
                score_ref, ocat_ref, *, n_keys, first_block):
    hd = ATT_HEAD_DIM
    q_pos = (first_block + pl.program_id(1)) * Q_BLOCK + lax.broadcasted_iota(jnp.int32, (Q_BLOCK, 1), 0)

    wi = wi_ref[...]
    qi = qi_ref[...].reshape(IDX_HEADS * Q_BLOCK, IDX_HEAD_DIM)
    for kt in range(n_keys // DSA_KEY_TILE):
        ks = slice(kt * DSA_KEY_TILE, (kt + 1) * DSA_KEY_TILE)
        raw = _dot_nt(qi, ki_ref[ks, :])
        acc = jnp.zeros((Q_BLOCK, DSA_KEY_TILE), F32)
        for n in range(IDX_HEADS):
            acc = acc + wi[:, n:n + 1] * jnp.maximum(raw[n * Q_BLOCK:(n + 1) * Q_BLOCK, :], 0.0)
        key_pos = kt * DSA_KEY_TILE + lax.broadcasted_iota(jnp.int32, (Q_BLOCK, DSA_KEY_TILE), 1)
        score_ref[:, ks] = jnp.where(key_pos <= q_pos, acc, -jnp.inf)

    if n_keys > TOPK:
        _select_topk(score_ref, q_pos, n_keys)
        bias = score_ref[...][None, :, :]
    else:
        bias = jnp.where(score_ref[...] == -jnp.inf, -jnp.inf, 0.0)[None, :, :]

    g_per_pass = ATT_Q_PER_KV if n_keys <= DSA_WIDE_KEYS else ATT_Q_PER_KV // 2
    for n in range(ATT_KV_HEADS):
        for g0 in range(0, ATT_Q_PER_KV, g_per_pass):
            first = n * ATT_Q_PER_KV + g0
            qg = q_ref[first:first + g_per_pass].reshape(g_per_pass * Q_BLOCK, hd)
            s = _dot_nt(qg, k_ref[n]).reshape(g_per_pass, Q_BLOCK, n_keys) + bias
            p = jnp.exp2(s - jnp.max(s, axis=-1, keepdims=True))
            o = _dot(p.reshape(g_per_pass * Q_BLOCK, n_keys).astype(BF16), v_ref[n])
            o = o[:, 0:hd] * (1.0 / o[:, hd:hd + 1])
            for g in range(g_per_pass):
                col = (first + g) * hd
                ocat_ref[:, col:col + hd] = o[g * Q_BLOCK:(g + 1) * Q_BLOCK, :]
    out = _dot(ocat_ref[...].astype(BF16), wout_ref[...])
    o_ref[...] = xres_ref[...] + gate_ref[...] * out


def _dsa_mixer(q, k, v, qi, ki, wi, x, w_out, mod5, layer):
    nb = SEQ // Q_BLOCK
    k4 = k.reshape(ATT_KV_HEADS, BATCH, SEQ, ATT_HEAD_DIM)
    v4 = v.reshape(ATT_KV_HEADS, BATCH, SEQ, V_EXT)
    ki3 = ki.reshape(BATCH, SEQ, IDX_HEAD_DIM)
    for cls in range(DSA_CLASSES):
        n_keys = (cls + 1) * (SEQ // DSA_CLASSES)
        first_block = cls * DSA_BLOCKS_PER_CLASS
        row = lambda b, i, fb=first_block: b * nb + fb + i
        heads = lambda n: pl.BlockSpec((n, Q_BLOCK, ATT_HEAD_DIM), lambda b, i: (0, row(b, i), 0))
        keys = lambda width: pl.BlockSpec((ATT_KV_HEADS, None, n_keys, width), lambda b, i: (0, b, 0, 0))
        x = pl.pallas_call(
            functools.partial(_dsa_kernel, n_keys=n_keys, first_block=first_block),
            grid=(BATCH, DSA_BLOCKS_PER_CLASS),
            in_specs=[
                heads(ATT_HEADS), keys(ATT_HEAD_DIM), keys(V_EXT), heads(IDX_HEADS),
                pl.BlockSpec((None, n_keys, IDX_HEAD_DIM), lambda b, i: (b, 0, 0)),
                pl.BlockSpec((Q_BLOCK, IDX_HEADS), lambda b, i: (row(b, i), 0)),
                pl.BlockSpec((Q_BLOCK, D_MODEL), lambda b, i: (row(b, i), 0)),
                pl.BlockSpec((DSA_Q, D_MODEL), lambda b, i: (0, 0)),
                pl.BlockSpec((None, None, None, 1, D_MODEL), lambda b, i: (layer, b, 2, 0, 0)),
            ],
            out_specs=pl.BlockSpec((Q_BLOCK, D_MODEL), lambda b, i: (row(b, i), 0)),
            out_shape=jax.ShapeDtypeStruct((TOKENS, D_MODEL), F32),
            scratch_shapes=[
                pltpu.VMEM((Q_BLOCK, n_keys), F32),
                pltpu.VMEM((Q_BLOCK, D_MODEL), F32),
            ],
            input_output_aliases={6: 0},
            compiler_params=pltpu.CompilerParams(
                dimension_semantics=("arbitrary", "arbitrary"), vmem_limit_bytes=VMEM_LIMIT),
            name=f"dsa_mixer_c{cls}",
        )(q, k4, v4, qi, ki3, wi, x, w_out, mod5)
    return x


MOE_PAIRS = MOE_EPG * (MOE_EPG - 1) // 2
MOE_CLASSES = MOE_GROUPS * MOE_PAIRS
PAIR_LO = (0, 0, 0, 1, 1, 2)
PAIR_HI = (1, 2, 3, 2, 3, 3)
ROUTE_TM = 1024
META_W = 128
META_CLASS, META_RANK, META_WLO, META_WHI = 0, 1, 2, 3
TILE_ROWS, TILE_LANES = 8, 128
H_WORDS = D_MODEL // 2
H_SUBLANES = H_WORDS // TILE_LANES
SORT_BLOCK = 256
N_SORT_BLOCKS = TOKENS // SORT_BLOCK
MAX_ITEMS = N_SORT_BLOCKS + MOE_CLASSES
PERMUTE_TM = 512
COMBINE_TM = 512


def _route_kernel(x_ref, nw_ref, scale_ref, shift_ref, wrt_ref, brt_ref, tri_ref, pay_ref, cls_ref, rank_ref,
                  cnt_ref, carry_ref):
    tm = ROUTE_TM

    @pl.when(pl.program_id(0) == 0)
    def _():
        carry_ref[...] = jnp.zeros_like(carry_ref)

    h = _norm_mod(x_ref[...], nw_ref[...], scale_ref[...], shift_ref[...])
    logits = _dot_nt(wrt_ref[...], h.astype(BF16)) + brt_ref[...]
    sub = lax.broadcasted_iota(jnp.int32, logits.shape, 0)
    neg = -jnp.inf
    big = jnp.int32(ROUTE_PAD)
    is_group = (sub >= MOE_EXPERTS) & (sub < MOE_EXPERTS + MOE_GROUPS)
    gl = jnp.where(is_group, logits, neg)
    g_max = jnp.max(gl, axis=0, keepdims=True)
    g_idx = jnp.min(jnp.where(gl == g_max, sub - MOE_EXPERTS, big), axis=0, keepdims=True)
    g_val = 1.0 / jnp.sum(jnp.exp(gl - g_max), axis=0, keepdims=True)
    in_group = (sub < MOE_EXPERTS) & ((sub // MOE_EPG) == g_idx)
    el = jnp.where(in_group, logits, neg)
    m1 = jnp.max(el, axis=0, keepdims=True)
    i1 = jnp.min(jnp.where(el == m1, sub, big), axis=0, keepdims=True)
    el2 = jnp.where(sub == i1, neg, el)
    m2 = jnp.max(el2, axis=0, keepdims=True)
    i2 = jnp.min(jnp.where(el2 == m2, sub, big), axis=0, keepdims=True)
    r = jnp.exp(m2 - m1)
    w_top1 = g_val / (1.0 + r)
    w_top2 = g_val * r / (1.0 + r)

    lo = jnp.minimum(i1, i2) - g_idx * MOE_EPG
    hi = jnp.maximum(i1, i2) - g_idx * MOE_EPG
    pair = (lo * (2 * MOE_EPG - 1 - lo)) // 2 + (hi - lo - 1)
    cls = g_idx * MOE_PAIRS + pair
    w_lo = jnp.where(i1 < i2, w_top1, w_top2)
    w_hi = jnp.where(i1 < i2, w_top2, w_top1)

    onehot = sub == cls
    before = _dot(onehot.astype(BF16), tri_ref[...]) + carry_ref[...]
    rank = jnp.sum(jnp.where(onehot, before, 0.0), axis=0, keepdims=True)
    carry_ref[...] += jnp.sum(jnp.where(onehot, 1.0, 0.0), axis=1, keepdims=True)
    cnt_ref[...] = carry_ref[...]
    cls_ref[...] = cls
    rank_ref[...] = rank.astype(jnp.int32)

    words = pltpu.pack_elementwise([h[:, 0:H_WORDS], h[:, H_WORDS:D_MODEL]], packed_dtype=BF16)
    for s in range(H_SUBLANES):
        pay_ref[pl.ds(s, tm, stride=TILE_ROWS), :] = words[:, s * TILE_LANES:(s + 1) * TILE_LANES]
    sub8 = lax.broadcasted_iota(jnp.int32, (TILE_ROWS, tm), 0)
    rec = jnp.where(sub8 == META_CLASS, cls.astype(F32),
                    jnp.where(sub8 == META_RANK, rank,
                              jnp.where(sub8 == META_WLO, w_lo, jnp.where(sub8 == META_WHI, w_hi, 0.0))))
    rec = jnp.concatenate([rec, jnp.zeros((META_W - TILE_ROWS, tm), F32)], axis=0)
    for b in range(tm // META_W):
        meta = rec[:, b * META_W:(b + 1) * META_W].T
        pay_ref[pl.ds(b * META_W * TILE_ROWS + H_SUBLANES, META_W, stride=TILE_ROWS), :] = (
            lax.bitcast_convert_type(meta, jnp.int32))
    for s in range(H_SUBLANES + 1, TILE_ROWS):
        pay_ref[pl.ds(s, tm, stride=TILE_ROWS), :] = jnp.zeros((tm, TILE_LANES), jnp.int32)


def _moe_route(x, nw, mod5, layer, w_route, b_route):
    tm = ROUTE_TM
    tiles_per_batch = SEQ // tm
    tri = (jnp.arange(tm, dtype=jnp.int32)[:, None] < jnp.arange(tm, dtype=jnp.int32)[None, :]).astype(BF16)
    return pl.pallas_call(
        _route_kernel,
        grid=(TOKENS // tm,),
        in_specs=[
            pl.BlockSpec((tm, D_MODEL), lambda i: (i, 0)),
            pl.BlockSpec((1, D_MODEL), lambda i: (0, 0)),
            _mod_spec(layer, 4, tiles_per_batch),
            _mod_spec(layer, 3, tiles_per_batch),
            pl.BlockSpec((ROUTE_PAD, D_MODEL), lambda i: (0, 0)),
            pl.BlockSpec((ROUTE_PAD, 1), lambda i: (0, 0)),
            pl.BlockSpec((tm, tm), lambda i: (0, 0)),
        ],
        out_specs=[pl.BlockSpec((tm * TILE_ROWS, TILE_LANES), lambda i: (i, 0)),
                   pl.BlockSpec((1, tm), lambda i: (0, i)),
                   pl.BlockSpec((1, tm), lambda i: (0, i)),
                   pl.BlockSpec((ROUTE_PAD, 1), lambda i: (0, 0))],
        out_shape=[jax.ShapeDtypeStruct((TOKENS * TILE_ROWS, TILE_LANES), jnp.int32),
                   jax.ShapeDtypeStruct((1, TOKENS), jnp.int32),
                   jax.ShapeDtypeStruct((1, TOKENS), jnp.int32),
                   jax.ShapeDtypeStruct((ROUTE_PAD, 1), F32)],
        scratch_shapes=[pltpu.VMEM((ROUTE_PAD, 1), F32)],
        compiler_params=pltpu.CompilerParams(
            dimension_semantics=("arbitrary",), vmem_limit_bytes=VMEM_LIMIT),
        name="moe_route",
    )(x, nw, mod5, mod5, w_route.T, b_route.reshape(ROUTE_PAD, 1), tri)


def _permute_kernel(pos_ref, src_ref, dst_hbm, stage_ref, sem):
    i = pl.program_id(0)
    slot = i % 2
    rows = PERMUTE_TM * TILE_ROWS

    def slot_wait(s):
        whole = stage_ref.at[pl.ds(s * rows, rows), :]
        pltpu.make_async_copy(whole, whole, sem.at[s]).wait()

    @pl.when(i >= 2)
    def _():
        slot_wait(slot)

    base = pl.multiple_of(slot * rows, rows)
    stage_ref[pl.ds(base, rows), :] = src_ref[...]

    def issue(r, _):
        pltpu.make_async_copy(stage_ref.at[pl.ds(base + r * TILE_ROWS, TILE_ROWS), :],
                              dst_hbm.at[pos_ref[i * PERMUTE_TM + r]], sem.at[slot]).start()
        return 0

    lax.fori_loop(0, PERMUTE_TM, issue, 0, unroll=16)

    @pl.when(i == pl.num_programs(0) - 1)
    def _():
        slot_wait(1 - slot)
        slot_wait(slot)


def _moe_permute(pos, payload):
    rows = PERMUTE_TM * TILE_ROWS
    return pl.pallas_call(
        _permute_kernel,
        grid_spec=pltpu.PrefetchScalarGridSpec(
            num_scalar_prefetch=1, grid=(TOKENS // PERMUTE_TM,),
            in_specs=[pl.BlockSpec((rows, TILE_LANES), lambda i, pos: (i, 0))],
            out_specs=pl.BlockSpec(memory_space=pl.ANY),
            scratch_shapes=[pltpu.VMEM((2 * rows, TILE_LANES), jnp.int32), pltpu.SemaphoreType.DMA((2,))]),
        out_shape=jax.ShapeDtypeStruct((TOKENS, TILE_ROWS, TILE_LANES), jnp.int32),
        compiler_params=pltpu.CompilerParams(
            dimension_semantics=("arbitrary",), vmem_limit_bytes=VMEM_LIMIT),
        name="moe_permute",
    )(pos, payload)


def _experts_kernel(blk_ref, cls_ref, elo_ref, ehi_ref, first_ref, last_ref, valid_ref,
                    pay_ref, w1lo_ref, w3lo_ref, w2lo_ref, w1hi_ref, w3hi_ref, w2hi_ref, o_ref, acc_ref):
    k = pl.program_id(0)
    rows = SORT_BLOCK

    @pl.when(valid_ref[k] == 1)
    def _():
        def sublane(s):
            return pay_ref[pl.ds(s, rows, stride=TILE_ROWS), :]

        halves = [[pltpu.unpack_elementwise(sublane(s), index=i, packed_dtype=BF16, unpacked_dtype=F32)
                   for s in range(H_SUBLANES)] for i in range(2)]
        hb = jnp.concatenate(halves[0] + halves[1], axis=1).astype(BF16)
        meta = lax.bitcast_convert_type(sublane(H_SUBLANES), F32)
        mine = meta[:, META_CLASS:META_CLASS + 1] == cls_ref[k].astype(F32)
        w_lo = jnp.where(mine, meta[:, META_WLO:META_WLO + 1], 0.0)
        w_hi = jnp.where(mine, meta[:, META_WHI:META_WHI + 1], 0.0)
        bf = lambda w_ref: w_ref[...].astype(BF16)
        hid_lo = _silu(_dot(hb, bf(w1lo_ref))) * _dot(hb, bf(w3lo_ref)) * w_lo
        hid_hi = _silu(_dot(hb, bf(w1hi_ref))) * _dot(hb, bf(w3hi_ref)) * w_hi
        y = _dot(hid_lo.astype(BF16), bf(w2lo_ref)) + _dot(hid_hi.astype(BF16), bf(w2hi_ref))

        @pl.when(first_ref[k] == 1)
        def _():
            acc_ref[...] = y

        @pl.when(first_ref[k] == 0)
        def _():
            acc_ref[...] += y

        @pl.when(last_ref[k] == 1)
        def _():
            for s in range(TILE_ROWS):
                o_ref[pl.ds(s, rows, stride=TILE_ROWS), :] = acc_ref[:, s * TILE_LANES:(s + 1) * TILE_LANES]


def _moe_experts(items, payload_sorted, layer, w1, w3, w2):
    blk, cls, elo, ehi, first, last, valid = items
    tiles = pl.BlockSpec((SORT_BLOCK * TILE_ROWS, TILE_LANES), lambda k, blk, *_: (blk[k], 0))
    w_in = lambda which: pl.BlockSpec(
        (None, None, D_MODEL, MOE_HIDDEN), lambda k, blk, cls, elo, ehi, *_: (layer, (elo, ehi)[which][k], 0, 0))
    w_out = lambda which: pl.BlockSpec(
        (None, None, MOE_HIDDEN, D_MODEL), lambda k, blk, cls, elo, ehi, *_: (layer, (elo, ehi)[which][k], 0, 0))
    return pl.pallas_call(
        _experts_kernel,
        grid_spec=pltpu.PrefetchScalarGridSpec(
            num_scalar_prefetch=7, grid=(MAX_ITEMS,),
            in_specs=[tiles, w_in(0), w_in(0), w_out(0), w_in(1), w_in(1), w_out(1)],
            out_specs=tiles,
            scratch_shapes=[pltpu.VMEM((SORT_BLOCK, D_MODEL), F32)]),
        out_shape=jax.ShapeDtypeStruct((TOKENS * TILE_ROWS, TILE_LANES), F32),
        compiler_params=pltpu.CompilerParams(
            dimension_semantics=("arbitrary",), vmem_limit_bytes=VMEM_LIMIT),
        name="moe_experts",
    )(blk, cls, elo, ehi, first, last, valid, payload_sorted, w1, w3, w2, w1, w3, w2)


def _combine_kernel(pos_ref, x_ref, gate_ref, y_hbm, o_ref, buf_ref, sem):
    tm = COMBINE_TM
    i = pl.program_id(0)
    n = pl.num_programs(0)

    def gather(tile, slot):
        def issue(r, _):
            pltpu.make_async_copy(y_hbm.at[pos_ref[tile * tm + r]],
                                  buf_ref.at[pl.ds((slot * tm + r) * TILE_ROWS, TILE_ROWS), :], sem.at[slot]).start()
            return 0
        lax.fori_loop(0, tm, issue, 0, unroll=16)

    @pl.when(i == 0)
    def _():
        gather(0, 0)

    @pl.when(i + 1 < n)
    def _():
        gather(i + 1, (i + 1) % 2)

    slot = i % 2
    base = slot * tm * TILE_ROWS
    whole_slot = buf_ref.at[pl.ds(base, tm * TILE_ROWS), :]
    pltpu.make_async_copy(whole_slot, whole_slot, sem.at[slot]).wait()
    y = jnp.concatenate([buf_ref[pl.ds(base + s, tm, stride=TILE_ROWS), :] for s in range(TILE_ROWS)], axis=1)
    o_ref[...] = x_ref[...] + gate_ref[...] * y


def _moe_combine(pos, x, mod5, layer, y_sorted):
    tm = COMBINE_TM
    tiles_per_batch = SEQ // tm
    return pl.pallas_call(
        _combine_kernel,
        grid_spec=pltpu.PrefetchScalarGridSpec(
            num_scalar_prefetch=1, grid=(TOKENS // tm,),
            in_specs=[
                pl.BlockSpec((tm, D_MODEL), lambda i, pos: (i, 0)),
                pl.BlockSpec((None, None, None, 1, D_MODEL),
                             lambda i, pos: (layer, i // tiles_per_batch, 5, 0, 0)),
                pl.BlockSpec(memory_space=pl.ANY),
            ],
            out_specs=pl.BlockSpec((tm, D_MODEL), lambda i, pos: (i, 0)),
            scratch_shapes=[pltpu.VMEM((2 * tm * TILE_ROWS, TILE_LANES), F32), pltpu.SemaphoreType.DMA((2,))]),
        out_shape=jax.ShapeDtypeStruct((TOKENS, D_MODEL), F32),
        compiler_params=pltpu.CompilerParams(
            dimension_semantics=("arbitrary",), vmem_limit_bytes=VMEM_LIMIT),
        name="moe_combine",
    )(pos, x, mod5, y_sorted)


def _moe_plan(cls, rank, counts):
    count = counts[:MOE_CLASSES, 0].astype(jnp.int32)
    ends = jnp.cumsum(count)
    starts = ends - count
    class_ids = jnp.arange(MOE_CLASSES, dtype=jnp.int32)

    def lookup(table, idx):
        return jnp.sum(jnp.where(idx[..., None] == class_ids, table, 0), axis=-1)

    pos = (lookup(starts, cls) + rank).reshape(TOKENS)

    first_blk = starts // SORT_BLOCK
    n_items = jnp.where(count > 0, (ends - 1) // SORT_BLOCK - first_blk + 1, 0)
    item_end = jnp.cumsum(n_items)
    item_start = item_end - n_items
    k = jnp.arange(MAX_ITEMS, dtype=jnp.int32)
    valid = k < item_end[-1]
    kc = jnp.minimum(k, item_end[-1] - 1)
    icls = jnp.sum((item_end[None, :] <= kc[:, None]).astype(jnp.int32), axis=1)
    blk = lookup(first_blk, icls) + (kc - lookup(item_start, icls))
    first = jnp.concatenate([jnp.ones((1,), jnp.int32), (blk[1:] != blk[:-1]).astype(jnp.int32)])
    last = jnp.concatenate([(blk[1:] != blk[:-1]) | ~valid[1:], jnp.ones((1,), bool)]).astype(jnp.int32)
    group = icls // MOE_PAIRS
    elo = group * MOE_EPG + lookup(jnp.asarray(PAIR_LO * MOE_GROUPS, jnp.int32), icls)
    ehi = group * MOE_EPG + lookup(jnp.asarray(PAIR_HI * MOE_GROUPS, jnp.int32), icls)
    return pos, (blk, icls, elo, ehi, first * valid, last * valid, valid.astype(jnp.int32))


def _moe(x, nw, mod5, layer, w_route, b_route, w1, w3, w2):
    payload, cls, rank, counts = _moe_route(x, nw, mod5, layer, w_route, b_route)
    pos, items = _moe_plan(cls, rank, counts)
    tiles = (TOKENS, TILE_ROWS, TILE_LANES)
    sorted_payload = _moe_permute(pos, payload).reshape(TOKENS * TILE_ROWS, TILE_LANES)
    y_sorted = _moe_experts(items, sorted_payload, layer, w1, w3, w2)
    return _moe_combine(pos, x, mod5, layer, y_sorted.reshape(tiles))


def _pad_cols(w, width):
    return jnp.pad(w, ((0, 0), (0, width - w.shape[1])))


def kernel(x, c, ada_w, ada_b, norm_mix, norm_ffn, ssd_w_in, ssd_conv_w, ssd_conv_b, ssd_dt_bias,
           ssd_a_log, ssd_d, ssd_norm, ssd_w_out, dsa_w_in, dsa_q_norm, dsa_k_norm, dsa_w_out,
           moe_w_group, moe_b_group, moe_w_expert, moe_b_expert, moe_w1, moe_w3, moe_w2):
    depth = ada_w.shape[0]
    xt = x.reshape(TOKENS, D_MODEL)
    mod = _modulation(c, ada_w, ada_b)
    mod5 = mod.reshape(depth, BATCH, 6, 1, D_MODEL)

    head_of_col = jnp.arange(SSD_D_INNER, dtype=jnp.int32) // SSD_HEAD_DIM
    expand = (jnp.arange(SSD_DT_PAD, dtype=jnp.int32)[:, None] == head_of_col[None, :]).astype(BF16)

    for i in range(depth):
        j = i // 2
        nw_mix = norm_mix[i].reshape(1, D_MODEL)
        if i % 2 == 0:
            z, xbc, dt = _inproj(xt, nw_mix, mod5, i, 1, 0, ssd_w_in[j].astype(BF16))
            xt = _ssd_mixer(
                z, xbc, dt, xt, ssd_conv_w[j], ssd_conv_b[j].reshape(1, SSD_CONV_DIM),
                _pad_cols(ssd_dt_bias[j].reshape(1, SSD_HEADS), SSD_DT_PAD),
                _pad_cols(ssd_a_log[j].reshape(1, SSD_HEADS), SSD_DT_PAD),
                jnp.repeat(ssd_d[j], SSD_HEAD_DIM).reshape(1, SSD_D_INNER),
                ssd_norm[j].reshape(1, SSD_D_INNER), expand, ssd_w_out[j].astype(BF16), mod5, i)
        else:
            w = dsa_w_in[j]
            w_in = jnp.concatenate(
                [_pad_cols(w[:, :DSA_KI_START + IDX_HEAD_DIM], DSA_WI_START),
                 _pad_cols(w[:, DSA_KI_START + IDX_HEAD_DIM:], 128)], axis=1).astype(BF16)
            q, k, v, qi, ki, wi = _dsa_inproj(
                xt, nw_mix, mod5, i, w_in, dsa_q_norm[j].reshape(1, ATT_HEAD_DIM),
                dsa_k_norm[j].reshape(1, ATT_HEAD_DIM))
            xt = _dsa_mixer(q, k, v, qi, ki, wi, xt, dsa_w_out[j].astype(BF16), mod5, i)

        w_route = _pad_cols(jnp.concatenate([moe_w_expert[i], moe_w_group[i]], axis=1), ROUTE_PAD).astype(BF16)
        b_route = _pad_cols(jnp.concatenate([moe_b_expert[i], moe_b_group[i]]).reshape(1, -1), ROUTE_PAD)
        xt = _moe(xt, norm_ffn[i].reshape(1, D_MODEL), mod5, i, w_route, b_route,
                  moe_w1, moe_w3, moe_w2)
    return xt.reshape(BATCH, SEQ, D_MODEL)
```

```python
import functools

import jax
import jax.numpy as jnp
from jax import lax
from jax.experimental import pallas as pl
from jax.experimental.pallas import tpu as pltpu

F32 = jnp.float32
BF16 = jnp.bfloat16

D_MODEL = 1024
BATCH = 8
SEQ = 2048
TOKENS = BATCH * SEQ
EPS = 1e-6

SSD_D_INNER = 2048
SSD_HEAD_DIM = 64
SSD_HEADS = 32
SSD_GROUPS = 8
SSD_HEADS_PER_GROUP = 4
SSD_STATE = 128
SSD_CONV = 4
SSD_CHUNK = 128
SSD_GN = SSD_GROUPS * SSD_STATE
SSD_CONV_DIM = SSD_D_INNER + 2 * SSD_GN
SSD_GROUP_W = SSD_HEADS_PER_GROUP * SSD_HEAD_DIM
SSD_DT_PAD = 128
CONV_HALO = 8

ATT_HEADS = 16
ATT_KV_HEADS = 4
ATT_Q_PER_KV = 4
ATT_HEAD_DIM = 64
IDX_HEADS = 8
IDX_HEAD_DIM = 64
TOPK = 256
Q_BLOCK = 256
DSA_Q = ATT_HEADS * ATT_HEAD_DIM
DSA_KV = ATT_KV_HEADS * ATT_HEAD_DIM
DSA_QI = IDX_HEADS * IDX_HEAD_DIM
DSA_KI_START = DSA_Q + 2 * DSA_KV + DSA_QI
DSA_WI_START = DSA_KI_START + 128
DSA_PROJ_PAD = DSA_WI_START + 128

MOE_GROUPS = 4
MOE_EPG = 4
MOE_EXPERTS = 16
MOE_HIDDEN = 256
ROUTE_PAD = 128

VMEM_LIMIT = 56 * 1024 * 1024


def _sigmoid(v):
    return 1.0 / (1.0 + jnp.exp(-v))


def _silu(v):
    return v * _sigmoid(v)


def _split3(a):
    hi = a.astype(BF16)
    r = a - hi.astype(F32)
    mid = r.astype(BF16)
    lo = (r - mid.astype(F32)).astype(BF16)
    return hi, mid, lo


def _dot(a, b):
    return jnp.dot(a, b, preferred_element_type=F32)


def _dot_nt(a, b):
    return lax.dot_general(a, b, (((1,), (1,)), ((), ())), preferred_element_type=F32)


def _dot3_exact_rhs(a, m):
    hi, mid, lo = _split3(a)
    return _dot(hi, m) + _dot(mid, m) + _dot(lo, m)


def _dot3_exact_lhs(m, a):
    hi, mid, lo = _split3(a)
    return _dot(m, hi) + _dot(m, mid) + _dot(m, lo)


def _norm_mod(x, nw, scale, shift):
    ms = jnp.mean(x * x, axis=-1, keepdims=True)
    return x * lax.rsqrt(ms + EPS) * nw * (1.0 + scale) + shift


MOD_TN = 1536


def _mod_kernel(c_ref, w_ref, b_ref, o_ref):
    cond = _silu(c_ref[...]).astype(BF16)
    o_ref[...] = _dot(cond, w_ref[...].astype(BF16)) + b_ref[...]


def _modulation(c, ada_w, ada_b):
    depth = ada_w.shape[0]
    n = ada_w.shape[2]
    return pl.pallas_call(
        _mod_kernel,
        grid=(depth, n // MOD_TN),
        in_specs=[
            pl.BlockSpec((BATCH, D_MODEL), lambda i, j: (0, 0)),
            pl.BlockSpec((None, D_MODEL, MOD_TN), lambda i, j: (i, 0, j)),
            pl.BlockSpec((None, 1, MOD_TN), lambda i, j: (i, 0, j)),
        ],
        out_specs=pl.BlockSpec((None, BATCH, MOD_TN), lambda i, j: (i, 0, j)),
        out_shape=jax.ShapeDtypeStruct((depth, BATCH, n), F32),
        compiler_params=pltpu.CompilerParams(
            dimension_semantics=("arbitrary", "arbitrary"), vmem_limit_bytes=VMEM_LIMIT),
        name="adaln_mod",
    )(c, ada_w, ada_b.reshape(depth, 1, n))


def _mod_spec(layer, chunk, rows_per_batch_tile):
    return pl.BlockSpec((None, None, None, 1, D_MODEL),
                        lambda i, *_: (layer, i // rows_per_batch_tile, chunk, 0, 0))


INPROJ_TM = 256
DSA_INPROJ_TM = 512


def _inproj_kernel(x_ref, nw_ref, scale_ref, shift_ref, w_ref, z_ref, xbc_ref, dt_ref):
    h = _norm_mod(x_ref[...], nw_ref[...], scale_ref[...], shift_ref[...]).astype(BF16)
    n_main = SSD_D_INNER + SSD_CONV_DIM
    z_ref[...] = _silu(_dot(h, w_ref[:, 0:SSD_D_INNER]))
    xbc_ref[...] = _dot(h, w_ref[:, SSD_D_INNER:n_main])
    dt_ref[:, 0:SSD_HEADS] = _dot(h, w_ref[:, n_main:n_main + SSD_HEADS])
    dt_ref[:, SSD_HEADS:SSD_DT_PAD] = jnp.zeros((dt_ref.shape[0], SSD_DT_PAD - SSD_HEADS), F32)


def _inproj(x, nw, mod5, layer, scale_chunk, shift_chunk, w):
    tm = INPROJ_TM
    tiles_per_batch = SEQ // tm
    widths = (SSD_D_INNER, SSD_CONV_DIM, SSD_DT_PAD)
    return pl.pallas_call(
        _inproj_kernel,
        grid=(TOKENS // tm,),
        in_specs=[
            pl.BlockSpec((tm, D_MODEL), lambda i: (i, 0)),
            pl.BlockSpec((1, D_MODEL), lambda i: (0, 0)),
            _mod_spec(layer, scale_chunk, tiles_per_batch),
            _mod_spec(layer, shift_chunk, tiles_per_batch),
            pl.BlockSpec((D_MODEL, SSD_D_INNER + SSD_CONV_DIM + SSD_HEADS), lambda i: (0, 0)),
        ],
        out_specs=[pl.BlockSpec((tm, width), lambda i: (i, 0)) for width in widths],
        out_shape=[jax.ShapeDtypeStruct((TOKENS, width), F32) for width in widths],
        compiler_params=pltpu.CompilerParams(
            dimension_semantics=("arbitrary",), vmem_limit_bytes=VMEM_LIMIT),
        name="norm_inproj",
    )(x, nw, mod5, mod5, w)


CONV_COLS = 512


def _ssd_kernel(z_ref, xbc_ref, dt_ref, xres_ref, cw_ref, cb_ref, dtb_ref, alog_ref, de_ref, nw_ref,
                e_ref, wout_ref, gate_ref, o_ref, state_ref, ext_ref, act_ref, yn_ref):
    q = SSD_CHUNK
    c = pl.program_id(1)

    @pl.when(c == 0)
    def _():
        state_ref[...] = jnp.zeros_like(state_ref)
        ext_ref[0:CONV_HALO, :] = jnp.zeros((CONV_HALO, SSD_CONV_DIM), F32)

    @pl.when(c > 0)
    def _():
        ext_ref[0:CONV_HALO, :] = ext_ref[q:q + CONV_HALO, :]

    ext_ref[CONV_HALO:CONV_HALO + q, :] = xbc_ref[...]

    for s in range(SSD_CONV_DIM // CONV_COLS):
        cs = slice(s * CONV_COLS, (s + 1) * CONV_COLS)
        u = ext_ref[:, cs]
        acc = cw_ref[0:1, cs] * u
        for k in range(1, SSD_CONV):
            acc = pltpu.roll(acc, 1, axis=0) + cw_ref[k:k + 1, cs] * u
        act_ref[:, cs] = _silu(acc[CONV_HALO:CONV_HALO + q, :] + cb_ref[:, cs])

    dt_raw = dt_ref[...] + dtb_ref[...]
    dt = jnp.maximum(dt_raw, 0.0) + jnp.log1p(jnp.exp(-jnp.abs(dt_raw)))
    a = dt * (-jnp.exp(alog_ref[...]) * LOG2E)
    row = lax.broadcasted_iota(jnp.int32, (q, q), 0)
    col = lax.broadcasted_iota(jnp.int32, (q, q), 1)
    tril = row >= col
    acs = _dot3_exact_lhs(tril.astype(BF16), a)
    acs_t = acs.T
    expand = e_ref[...]
    acs_e = _dot3_exact_rhs(acs, expand)
    dt_e = _dot3_exact_rhs(dt, expand)
    tot_e = acs_e[q - 1:q, :]
    decay_from_start = jnp.exp2(acs_e)
    decay_to_end = jnp.exp2(tot_e - acs_e)
    chunk_decay = jnp.exp2(tot_e)

    lane_head = lax.broadcasted_iota(jnp.int32, (q, SSD_GROUP_W), 1) // SSD_HEAD_DIM
    for g in range(SSD_GROUPS):
        xs = act_ref[:, g * SSD_GROUP_W:(g + 1) * SSD_GROUP_W]
        gs = slice(g * SSD_GROUP_W, (g + 1) * SSD_GROUP_W)
        bm = act_ref[:, SSD_D_INNER + g * SSD_STATE:SSD_D_INNER + (g + 1) * SSD_STATE]
        cm = act_ref[:, SSD_D_INNER + SSD_GN + g * SSD_STATE:SSD_D_INNER + SSD_GN + (g + 1) * SSD_STATE]
        bm_t = bm.T.astype(BF16)
        cm_b = cm.astype(BF16)
        cb = _dot(cm_b, bm_t)
        xd = xs * dt_e[:, gs]
        ms = []
        xds = []
        for j in range(SSD_HEADS_PER_GROUP):
            h = g * SSD_HEADS_PER_GROUP + j
            seg = acs[:, h:h + 1] - acs_t[h:h + 1, :]
            dec = jnp.exp2(jnp.where(tril, seg, -jnp.inf))
            ms.append((cb * dec).astype(BF16))
            xds.append(jnp.where(lane_head == j, xd, 0.0).astype(BF16))
        y_diag = _dot(jnp.concatenate(ms, axis=1), jnp.concatenate(xds, axis=0))
        prev = state_ref[g]
        y_off = _dot(cm_b, prev.astype(BF16)) * decay_from_start[:, gs]
        state_ref[g] = prev * chunk_decay[:, gs] + _dot(bm_t, (xd * decay_to_end[:, gs]).astype(BF16))
        y = y_diag + y_off + xs * de_ref[:, gs]
        y = y * z_ref[:, gs]
        y = y * lax.rsqrt(jnp.mean(y * y, axis=-1, keepdims=True) + EPS) * nw_ref[:, gs]
        yn_ref[:, gs] = y.astype(BF16)

    out = _dot(yn_ref[...], wout_ref[...])
    o_ref[...] = xres_ref[...] + gate_ref[...] * out


def _ssd_mixer(z, xbc, dt, x, conv_w, conv_b, dt_bias, a_log, d_e, norm_w, expand, w_out, mod5, layer):
    q = SSD_CHUNK
    nc = SEQ // q
    tok = lambda w: pl.BlockSpec((q, w), lambda b, c: (b * nc + c, 0))
    full = lambda r, w: pl.BlockSpec((r, w), lambda b, c: (0, 0))
    return pl.pallas_call(
        _ssd_kernel,
        grid=(BATCH, nc),
        in_specs=[
            tok(SSD_D_INNER), tok(SSD_CONV_DIM), tok(SSD_DT_PAD), tok(D_MODEL),
            full(SSD_CONV, SSD_CONV_DIM), full(1, SSD_CONV_DIM), full(1, SSD_DT_PAD), full(1, SSD_DT_PAD),
            full(1, SSD_D_INNER), full(1, SSD_D_INNER), full(SSD_DT_PAD, SSD_D_INNER),
            full(SSD_D_INNER, D_MODEL),
            pl.BlockSpec((None, None, None, 1, D_MODEL), lambda b, c: (layer, b, 2, 0, 0)),
        ],
        out_specs=tok(D_MODEL),
        out_shape=jax.ShapeDtypeStruct((TOKENS, D_MODEL), F32),
        scratch_shapes=[
            pltpu.VMEM((SSD_GROUPS, SSD_STATE, SSD_GROUP_W), F32),
            pltpu.VMEM((q + CONV_HALO, SSD_CONV_DIM), F32),
            pltpu.VMEM((q, SSD_CONV_DIM), F32),
            pltpu.VMEM((q, SSD_D_INNER), BF16),
        ],
        compiler_params=pltpu.CompilerParams(
            dimension_semantics=("arbitrary", "arbitrary"), vmem_limit_bytes=VMEM_LIMIT),
        name="ssd_mixer",
    )(z, xbc, dt, x, conv_w, conv_b, dt_bias, a_log, d_e, norm_w, expand, w_out, mod5)


DSA_KEY_TILE = 256
DSA_CLASSES = 8
V_EXT = 2 * ATT_HEAD_DIM
DSA_BLOCKS_PER_CLASS = (SEQ // Q_BLOCK) // DSA_CLASSES
DSA_WIDE_KEYS = 768
N_BISECT = 12
F32_MIN = float(jnp.finfo(jnp.float32).min)
LOG2E = 1.4426950408889634


def _count(mask):
    return jnp.sum(jnp.where(mask, 1.0, 0.0), axis=-1, keepdims=True)


def _select_topk(score_ref, q_pos, n_keys):
    kf = float(TOPK)
    small = (q_pos + 1) <= TOPK
    sc = score_ref[...]
    hi0 = jnp.max(sc, axis=-1, keepdims=True)
    lo0 = jnp.min(jnp.where(sc == -jnp.inf, jnp.inf, sc), axis=-1, keepdims=True)

    def bisect(_, carry):
        lo, hi = carry
        mid = lo + 0.5 * (hi - lo)
        ok = _count(score_ref[...] >= mid) >= kf
        return jnp.where(ok, mid, lo), jnp.where(ok, hi, mid)

    _, hi = lax.fori_loop(0, N_BISECT, bisect, (lo0, hi0))

    v0 = jnp.max(jnp.where(sc <= hi, sc, -jnp.inf), axis=-1, keepdims=True)
    c0 = _count(sc >= v0)
    pend0 = jnp.where((c0 >= kf) | small, 0.0, 1.0)

    def walk_cond(carry):
        return (carry[3] > 0.0) & (carry[4] < n_keys)

    def walk(carry):
        v, c, pend, _, it = carry
        s = score_ref[...]
        v2 = jnp.max(jnp.where(s < v, s, -jnp.inf), axis=-1, keepdims=True)
        c2 = _count(s >= v2)
        v = jnp.where(pend > 0.0, v2, v)
        c = jnp.where(pend > 0.0, c2, c)
        pend = jnp.where(c2 >= kf, 0.0, pend)
        return v, c, pend, jnp.max(pend), it + 1

    v, c, _, _, _ = lax.while_loop(walk_cond, walk, (v0, c0, pend0, jnp.max(pend0), jnp.int32(0)))
    thr = jnp.where(small, F32_MIN, v)
    any_tie = jnp.max(jnp.where(small, 0.0, c - kf)) > 0.0

    @pl.when(jnp.logical_not(any_tie))
    def _():
        score_ref[...] = jnp.where(score_ref[...] >= thr, 0.0, -jnp.inf)

    @pl.when(any_tie)
    def _():
        s = score_ref[...]
        key_pos = lax.broadcasted_iota(jnp.int32, (Q_BLOCK, n_keys), 1)
        gt = s > thr
        eq = s == thr
        need = kf - _count(gt)

        def body(_, carry):
            lo, hi = carry
            mid = (lo + hi) >> 1
            ok = _count((score_ref[...] == thr) & (key_pos <= mid)) >= need
            return jnp.where(ok, lo, mid), jnp.where(ok, mid, hi)

        init = (jnp.full((Q_BLOCK, 1), -1, jnp.int32), jnp.full((Q_BLOCK, 1), n_keys - 1, jnp.int32))
        cut = lax.fori_loop(0, (n_keys - 1).bit_length() + 1, body, init)[1]
        score_ref[...] = jnp.where(gt | (eq & (key_pos <= cut)), 0.0, -jnp.inf)


def _dsa_inproj_kernel(x_ref, nw_ref, scale_ref, shift_ref, w_ref, qn_ref, kn_ref, seg_ref, segt_ref,
                       q_ref, k_ref, v_ref, qi_ref, ki_ref, wi_ref):
    hd = ATT_HEAD_DIM
    h = _norm_mod(x_ref[...], nw_ref[...], scale_ref[...], shift_ref[...]).astype(BF16)

    def head_norm(t, w):
        width = t.shape[1]
        ss = _dot((t * t).astype(BF16), seg_ref[0:width, :])
        r = lax.rsqrt(ss * (1.0 / hd) + EPS)
        r_hi = r.astype(BF16)
        r_lo = (r - r_hi.astype(F32)).astype(BF16)
        return t * (_dot(r_hi, segt_ref[:, 0:width]) + _dot(r_lo, segt_ref[:, 0:width])) * w

    q = head_norm(_dot(h, w_ref[:, 0:DSA_Q]), qn_ref[...] * (hd ** -0.5 * LOG2E))
    for n in range(ATT_HEADS):
        q_ref[n] = q[:, n * hd:(n + 1) * hd].astype(BF16)
    kv = _dot(h, w_ref[:, DSA_Q:DSA_Q + 2 * DSA_KV])
    k = head_norm(kv[:, 0:DSA_KV], kn_ref[...])
    for n in range(ATT_KV_HEADS):
        k_ref[n] = k[:, n * hd:(n + 1) * hd].astype(BF16)
        v_ref[n] = jnp.concatenate([kv[:, DSA_KV + n * hd:DSA_KV + (n + 1) * hd],
                                    jnp.ones((kv.shape[0], V_EXT - hd), F32)], axis=1).astype(BF16)
    qi = _dot(h, w_ref[:, DSA_Q + 2 * DSA_KV:DSA_KI_START])
    for n in range(IDX_HEADS):
        qi_ref[n] = qi[:, n * IDX_HEAD_DIM:(n + 1) * IDX_HEAD_DIM].astype(BF16)
    ki_ref[...] = _dot(h, w_ref[:, DSA_KI_START:DSA_KI_START + IDX_HEAD_DIM]).astype(BF16)
    wi_ref[...] = _dot(h, w_ref[:, DSA_WI_START:DSA_WI_START + IDX_HEADS]) * ((IDX_HEADS * IDX_HEAD_DIM) ** -0.5)


def _dsa_inproj(x, nw, mod5, layer, w, q_norm, k_norm):
    tm = DSA_INPROJ_TM
    tiles_per_batch = SEQ // tm
    heads = lambda n: pl.BlockSpec((n, tm, ATT_HEAD_DIM), lambda i: (0, i, 0))
    head_of = jnp.arange(DSA_Q, dtype=jnp.int32) // ATT_HEAD_DIM
    seg = (head_of[:, None] == jnp.arange(128, dtype=jnp.int32)[None, :]).astype(BF16)
    q_norm = jnp.tile(q_norm, (1, ATT_HEADS))
    k_norm = jnp.tile(k_norm, (1, ATT_KV_HEADS))
    return pl.pallas_call(
        _dsa_inproj_kernel,
        grid=(TOKENS // tm,),
        in_specs=[
            pl.BlockSpec((tm, D_MODEL), lambda i: (i, 0)),
            pl.BlockSpec((1, D_MODEL), lambda i: (0, 0)),
            _mod_spec(layer, 1, tiles_per_batch),
            _mod_spec(layer, 0, tiles_per_batch),
            pl.BlockSpec((D_MODEL, DSA_PROJ_PAD), lambda i: (0, 0)),
            pl.BlockSpec((1, DSA_Q), lambda i: (0, 0)),
            pl.BlockSpec((1, DSA_KV), lambda i: (0, 0)),
            pl.BlockSpec((DSA_Q, 128), lambda i: (0, 0)),
            pl.BlockSpec((128, DSA_Q), lambda i: (0, 0)),
        ],
        out_specs=[heads(ATT_HEADS), heads(ATT_KV_HEADS),
                   pl.BlockSpec((ATT_KV_HEADS, tm, V_EXT), lambda i: (0, i, 0)), heads(IDX_HEADS),
                   pl.BlockSpec((tm, IDX_HEAD_DIM), lambda i: (i, 0)),
                   pl.BlockSpec((tm, IDX_HEADS), lambda i: (i, 0))],
        out_shape=[jax.ShapeDtypeStruct((ATT_HEADS, TOKENS, ATT_HEAD_DIM), BF16),
                   jax.ShapeDtypeStruct((ATT_KV_HEADS, TOKENS, ATT_HEAD_DIM), BF16),
                   jax.ShapeDtypeStruct((ATT_KV_HEADS, TOKENS, V_EXT), BF16),
                   jax.ShapeDtypeStruct((IDX_HEADS, TOKENS, IDX_HEAD_DIM), BF16),
                   jax.ShapeDtypeStruct((TOKENS, IDX_HEAD_DIM), BF16),
                   jax.ShapeDtypeStruct((TOKENS, IDX_HEADS), F32)],
        compiler_params=pltpu.CompilerParams(
            dimension_semantics=("arbitrary",), vmem_limit_bytes=VMEM_LIMIT),
        name="dsa_inproj",
    )(x, nw, mod5, mod5, w, q_norm, k_norm, seg, seg.T)


def _dsa_kernel(q_ref, k_ref, v_ref, qi_ref, ki_ref, wi_ref, xres_ref, wout_ref, gate_ref, o_ref,
                score_ref, ocat_ref, *, n_keys, first_block):
    hd = ATT_HEAD_DIM
    q_pos = (first_block + pl.program_id(1)) * Q_BLOCK + lax.broadcasted_iota(jnp.int32, (Q_BLOCK, 1), 0)

    wi = wi_ref[...]
    qi = qi_ref[...].reshape(IDX_HEADS * Q_BLOCK, IDX_HEAD_DIM)
    for kt in range(n_keys // DSA_KEY_TILE):
        ks = slice(kt * DSA_KEY_TILE, (kt + 1) * DSA_KEY_TILE)
        raw = _dot_nt(qi, ki_ref[ks, :])
        acc = jnp.zeros((Q_BLOCK, DSA_KEY_TILE), F32)
        for n in range(IDX_HEADS):
            acc = acc + wi[:, n:n + 1] * jnp.maximum(raw[n * Q_BLOCK:(n + 1) * Q_BLOCK, :], 0.0)
        key_pos = kt * DSA_KEY_TILE + lax.broadcasted_iota(jnp.int32, (Q_BLOCK, DSA_KEY_TILE), 1)
        score_ref[:, ks] = jnp.where(key_pos <= q_pos, acc, -jnp.inf)

    if n_keys > TOPK:
        _select_topk(score_ref, q_pos, n_keys)
        bias = score_ref[...][None, :, :]
    else:
        bias = jnp.where(score_ref[...] == -jnp.inf, -jnp.inf, 0.0)[None, :, :]

    g_per_pass = ATT_Q_PER_KV if n_keys <= DSA_WIDE_KEYS else ATT_Q_PER_KV // 2
    for n in range(ATT_KV_HEADS):
        for g0 in range(0, ATT_Q_PER_KV, g_per_pass):
            first = n * ATT_Q_PER_KV + g0
            qg = q_ref[first:first + g_per_pass].reshape(g_per_pass * Q_BLOCK, hd)
            s = _dot_nt(qg, k_ref[n]).reshape(g_per_pass, Q_BLOCK, n_keys) + bias
            p = jnp.exp2(s - jnp.max(s, axis=-1, keepdims=True))
            o = _dot(p.reshape(g_per_pass * Q_BLOCK, n_keys).astype(BF16), v_ref[n])
            o = o[:, 0:hd] * (1.0 / o[:, hd:hd + 1])
            for g in range(g_per_pass):
                col = (first + g) * hd
                ocat_ref[:, col:col + hd] = o[g * Q_BLOCK:(g + 1) * Q_BLOCK, :]
    out = _dot(ocat_ref[...].astype(BF16), wout_ref[...])
    o_ref[...] = xres_ref[...] + gate_ref[...] * out


def _dsa_mixer(q, k, v, qi, ki, wi, x, w_out, mod5, layer):
    nb = SEQ // Q_BLOCK
    k4 = k.reshape(ATT_KV_HEADS, BATCH, SEQ, ATT_HEAD_DIM)
    v4 = v.reshape(ATT_KV_HEADS, BATCH, SEQ, V_EXT)
    ki3 = ki.reshape(BATCH, SEQ, IDX_HEAD_DIM)
    for cls in range(DSA_CLASSES):
        n_keys = (cls + 1) * (SEQ // DSA_CLASSES)
        first_block = cls * DSA_BLOCKS_PER_CLASS
        row = lambda b, i, fb=first_block: b * nb + fb + i
        heads = lambda n: pl.BlockSpec((n, Q_BLOCK, ATT_HEAD_DIM), lambda b, i: (0, row(b, i), 0))
        keys = lambda width: pl.BlockSpec((ATT_KV_HEADS, None, n_keys, width), lambda b, i: (0, b, 0, 0))
        x = pl.pallas_call(
            functools.partial(_dsa_kernel, n_keys=n_keys, first_block=first_block),
            grid=(BATCH, DSA_BLOCKS_PER_CLASS),
            in_specs=[
                heads(ATT_HEADS), keys(ATT_HEAD_DIM), keys(V_EXT), heads(IDX_HEADS),
                pl.BlockSpec((None, n_keys, IDX_HEAD_DIM), lambda b, i: (b, 0, 0)),
                pl.BlockSpec((Q_BLOCK, IDX_HEADS), lambda b, i: (row(b, i), 0)),
                pl.BlockSpec((Q_BLOCK, D_MODEL), lambda b, i: (row(b, i), 0)),
                pl.BlockSpec((DSA_Q, D_MODEL), lambda b, i: (0, 0)),
                pl.BlockSpec((None, None, None, 1, D_MODEL), lambda b, i: (layer, b, 2, 0, 0)),
            ],
            out_specs=pl.BlockSpec((Q_BLOCK, D_MODEL), lambda b, i: (row(b, i), 0)),
            out_shape=jax.ShapeDtypeStruct((TOKENS, D_MODEL), F32),
            scratch_shapes=[
                pltpu.VMEM((Q_BLOCK, n_keys), F32),
                pltpu.VMEM((Q_BLOCK, D_MODEL), F32),
            ],
            input_output_aliases={6: 0},
            compiler_params=pltpu.CompilerParams(
                dimension_semantics=("arbitrary", "arbitrary"), vmem_limit_bytes=VMEM_LIMIT),
            name=f"dsa_mixer_c{cls}",
        )(q, k4, v4, qi, ki3, wi, x, w_out, mod5)
    return x


MOE_PAIRS = MOE_EPG * (MOE_EPG - 1) // 2
MOE_CLASSES = MOE_GROUPS * MOE_PAIRS
PAIR_LO = (0, 0, 0, 1, 1, 2)
PAIR_HI = (1, 2, 3, 2, 3, 3)
ROUTE_TM = 1024
META_W = 128
META_CLASS, META_RANK, META_WLO, META_WHI = 0, 1, 2, 3
TILE_ROWS, TILE_LANES = 8, 128
H_WORDS = D_MODEL // 2
H_SUBLANES = H_WORDS // TILE_LANES
SORT_BLOCK = 256
N_SORT_BLOCKS = TOKENS // SORT_BLOCK
MAX_ITEMS = N_SORT_BLOCKS + MOE_CLASSES
PERMUTE_TM = 512
COMBINE_TM = 512


def _route_kernel(x_ref, nw_ref, scale_ref, shift_ref, wrt_ref, brt_ref, tri_ref, pay_ref, cls_ref, rank_ref,
                  cnt_ref, carry_ref):
    tm = ROUTE_TM

    @pl.when(pl.program_id(0) == 0)
    def _():
        carry_ref[...] = jnp.zeros_like(carry_ref)

    h = _norm_mod(x_ref[...], nw_ref[...], scale_ref[...], shift_ref[...])
    logits = _dot_nt(wrt_ref[...], h.astype(BF16)) + brt_ref[...]
    sub = lax.broadcasted_iota(jnp.int32, logits.shape, 0)
    neg = -jnp.inf
    big = jnp.int32(ROUTE_PAD)
    is_group = (sub >= MOE_EXPERTS) & (sub < MOE_EXPERTS + MOE_GROUPS)
    gl = jnp.where(is_group, logits, neg)
    g_max = jnp.max(gl, axis=0, keepdims=True)
    g_idx = jnp.min(jnp.where(gl == g_max, sub - MOE_EXPERTS, big), axis=0, keepdims=True)
    g_val = 1.0 / jnp.sum(jnp.exp(gl - g_max), axis=0, keepdims=True)
    in_group = (sub < MOE_EXPERTS) & ((sub // MOE_EPG) == g_idx)
    el = jnp.where(in_group, logits, neg)
    m1 = jnp.max(el, axis=0, keepdims=True)
    i1 = jnp.min(jnp.where(el == m1, sub, big), axis=0, keepdims=True)
    el2 = jnp.where(sub == i1, neg, el)
    m2 = jnp.max(el2, axis=0, keepdims=True)
    i2 = jnp.min(jnp.where(el2 == m2, sub, big), axis=0, keepdims=True)
    r = jnp.exp(m2 - m1)
    w_top1 = g_val / (1.0 + r)
    w_top2 = g_val * r / (1.0 + r)

    lo = jnp.minimum(i1, i2) - g_idx * MOE_EPG
    hi = jnp.maximum(i1, i2) - g_idx * MOE_EPG
    pair = (lo * (2 * MOE_EPG - 1 - lo)) // 2 + (hi - lo - 1)
    cls = g_idx * MOE_PAIRS + pair
    w_lo = jnp.where(i1 < i2, w_top1, w_top2)
    w_hi = jnp.where(i1 < i2, w_top2, w_top1)

    onehot = sub == cls
    before = _dot(onehot.astype(BF16), tri_ref[...]) + carry_ref[...]
    rank = jnp.sum(jnp.where(onehot, before, 0.0), axis=0, keepdims=True)
    carry_ref[...] += jnp.sum(jnp.where(onehot, 1.0, 0.0), axis=1, keepdims=True)
    cnt_ref[...] = carry_ref[...]
    cls_ref[...] = cls
    rank_ref[...] = rank.astype(jnp.int32)

    words = pltpu.pack_elementwise([h[:, 0:H_WORDS], h[:, H_WORDS:D_MODEL]], packed_dtype=BF16)
    for s in range(H_SUBLANES):
        pay_ref[pl.ds(s, tm, stride=TILE_ROWS), :] = words[:, s * TILE_LANES:(s + 1) * TILE_LANES]
    sub8 = lax.broadcasted_iota(jnp.int32, (TILE_ROWS, tm), 0)
    rec = jnp.where(sub8 == META_CLASS, cls.astype(F32),
                    jnp.where(sub8 == META_RANK, rank,
                              jnp.where(sub8 == META_WLO, w_lo, jnp.where(sub8 == META_WHI, w_hi, 0.0))))
    rec = jnp.concatenate([rec, jnp.zeros((META_W - TILE_ROWS, tm), F32)], axis=0)
    for b in range(tm // META_W):
        meta = rec[:, b * META_W:(b + 1) * META_W].T
        pay_ref[pl.ds(b * META_W * TILE_ROWS + H_SUBLANES, META_W, stride=TILE_ROWS), :] = (
            lax.bitcast_convert_type(meta, jnp.int32))
    for s in range(H_SUBLANES + 1, TILE_ROWS):
        pay_ref[pl.ds(s, tm, stride=TILE_ROWS), :] = jnp.zeros((tm, TILE_LANES), jnp.int32)


def _moe_route(x, nw, mod5, layer, w_route, b_route):
    tm = ROUTE_TM
    tiles_per_batch = SEQ // tm
    tri = (jnp.arange(tm, dtype=jnp.int32)[:, None] < jnp.arange(tm, dtype=jnp.int32)[None, :]).astype(BF16)
    return pl.pallas_call(
        _route_kernel,
        grid=(TOKENS // tm,),
        in_specs=[
            pl.BlockSpec((tm, D_MODEL), lambda i: (i, 0)),
            pl.BlockSpec((1, D_MODEL), lambda i: (0, 0)),
            _mod_spec(layer, 4, tiles_per_batch),
            _mod_spec(layer, 3, tiles_per_batch),
            pl.BlockSpec((ROUTE_PAD, D_MODEL), lambda i: (0, 0)),
            pl.BlockSpec((ROUTE_PAD, 1), lambda i: (0, 0)),
            pl.BlockSpec((tm, tm), lambda i: (0, 0)),
        ],
        out_specs=[pl.BlockSpec((tm * TILE_ROWS, TILE_LANES), lambda i: (i, 0)),
                   pl.BlockSpec((1, tm), lambda i: (0, i)),
                   pl.BlockSpec((1, tm), lambda i: (0, i)),
                   pl.BlockSpec((ROUTE_PAD, 1), lambda i: (0, 0))],
        out_shape=[jax.ShapeDtypeStruct((TOKENS * TILE_ROWS, TILE_LANES), jnp.int32),
                   jax.ShapeDtypeStruct((1, TOKENS), jnp.int32),
                   jax.ShapeDtypeStruct((1, TOKENS), jnp.int32),
                   jax.ShapeDtypeStruct((ROUTE_PAD, 1), F32)],
        scratch_shapes=[pltpu.VMEM((ROUTE_PAD, 1), F32)],
        compiler_params=pltpu.CompilerParams(
            dimension_semantics=("arbitrary",), vmem_limit_bytes=VMEM_LIMIT),
        name="moe_route",
    )(x, nw, mod5, mod5, w_route.T, b_route.reshape(ROUTE_PAD, 1), tri)


def _permute_kernel(pos_ref, src_ref, dst_hbm, stage_ref, sem):
    i = pl.program_id(0)
    slot = i % 2
    rows = PERMUTE_TM * TILE_ROWS

    def slot_wait(s):
        whole = stage_ref.at[pl.ds(s * rows, rows), :]
        pltpu.make_async_copy(whole, whole, sem.at[s]).wait()

    @pl.when(i >= 2)
    def _():
        slot_wait(slot)

    base = pl.multiple_of(slot * rows, rows)
    stage_ref[pl.ds(base, rows), :] = src_ref[...]

    def issue(r, _):
        pltpu.make_async_copy(stage_ref.at[pl.ds(base + r * TILE_ROWS, TILE_ROWS), :],
                              dst_hbm.at[pos_ref[i * PERMUTE_TM + r]], sem.at[slot]).start()
        return 0

    lax.fori_loop(0, PERMUTE_TM, issue, 0, unroll=16)

    @pl.when(i == pl.num_programs(0) - 1)
    def _():
        slot_wait(1 - slot)
        slot_wait(slot)


def _moe_permute(pos, payload):
    rows = PERMUTE_TM * TILE_ROWS
    return pl.pallas_call(
        _permute_kernel,
        grid_spec=pltpu.PrefetchScalarGridSpec(
            num_scalar_prefetch=1, grid=(TOKENS // PERMUTE_TM,),
            in_specs=[pl.BlockSpec((rows, TILE_LANES), lambda i, pos: (i, 0))],
            out_specs=pl.BlockSpec(memory_space=pl.ANY),
            scratch_shapes=[pltpu.VMEM((2 * rows, TILE_LANES), jnp.int32), pltpu.SemaphoreType.DMA((2,))]),
        out_shape=jax.ShapeDtypeStruct((TOKENS, TILE_ROWS, TILE_LANES), jnp.int32),
        compiler_params=pltpu.CompilerParams(
            dimension_semantics=("arbitrary",), vmem_limit_bytes=VMEM_LIMIT),
        name="moe_permute",
    )(pos, payload)


def _experts_kernel(blk_ref, cls_ref, elo_ref, ehi_ref, first_ref, last_ref, valid_ref,
                    pay_ref, w1lo_ref, w3lo_ref, w2lo_ref, w1hi_ref, w3hi_ref, w2hi_ref, o_ref, acc_ref):
    k = pl.program_id(0)
    rows = SORT_BLOCK

    @pl.when(valid_ref[k] == 1)
    def _():
        def sublane(s):
            return pay_ref[pl.ds(s, rows, stride=TILE_ROWS), :]

        halves = [[pltpu.unpack_elementwise(sublane(s), index=i, packed_dtype=BF16, unpacked_dtype=F32)
                   for s in range(H_SUBLANES)] for i in range(2)]
        hb = jnp.concatenate(halves[0] + halves[1], axis=1).astype(BF16)
        meta = lax.bitcast_convert_type(sublane(H_SUBLANES), F32)
        mine = meta[:, META_CLASS:META_CLASS + 1] == cls_ref[k].astype(F32)
        w_lo = jnp.where(mine, meta[:, META_WLO:META_WLO + 1], 0.0)
        w_hi = jnp.where(mine, meta[:, META_WHI:META_WHI + 1], 0.0)
        bf = lambda w_ref: w_ref[...].astype(BF16)
        hid_lo = _silu(_dot(hb, bf(w1lo_ref))) * _dot(hb, bf(w3lo_ref)) * w_lo
        hid_hi = _silu(_dot(hb, bf(w1hi_ref))) * _dot(hb, bf(w3hi_ref)) * w_hi
        y = _dot(hid_lo.astype(BF16), bf(w2lo_ref)) + _dot(hid_hi.astype(BF16), bf(w2hi_ref))

        @pl.when(first_ref[k] == 1)
        def _():
            acc_ref[...] = y

        @pl.when(first_ref[k] == 0)
        def _():
            acc_ref[...] += y

        @pl.when(last_ref[k] == 1)
        def _():
            for s in range(TILE_ROWS):
                o_ref[pl.ds(s, rows, stride=TILE_ROWS), :] = acc_ref[:, s * TILE_LANES:(s + 1) * TILE_LANES]


def _moe_experts(items, payload_sorted, layer, w1, w3, w2):
    blk, cls, elo, ehi, first, last, valid = items
    tiles = pl.BlockSpec((SORT_BLOCK * TILE_ROWS, TILE_LANES), lambda k, blk, *_: (blk[k], 0))
    w_in = lambda which: pl.BlockSpec(
        (None, None, D_MODEL, MOE_HIDDEN), lambda k, blk, cls, elo, ehi, *_: (layer, (elo, ehi)[which][k], 0, 0))
    w_out = lambda which: pl.BlockSpec(
        (None, None, MOE_HIDDEN, D_MODEL), lambda k, blk, cls, elo, ehi, *_: (layer, (elo, ehi)[which][k], 0, 0))
    return pl.pallas_call(
        _experts_kernel,
        grid_spec=pltpu.PrefetchScalarGridSpec(
            num_scalar_prefetch=7, grid=(MAX_ITEMS,),
            in_specs=[tiles, w_in(0), w_in(0), w_out(0), w_in(1), w_in(1), w_out(1)],
            out_specs=tiles,
            scratch_shapes=[pltpu.VMEM((SORT_BLOCK, D_MODEL), F32)]),
        out_shape=jax.ShapeDtypeStruct((TOKENS * TILE_ROWS, TILE_LANES), F32),
        compiler_params=pltpu.CompilerParams(
            dimension_semantics=("arbitrary",), vmem_limit_bytes=VMEM_LIMIT),
        name="moe_experts",
    )(blk, cls, elo, ehi, first, last, valid, payload_sorted, w1, w3, w2, w1, w3, w2)


def _combine_kernel(pos_ref, x_ref, gate_ref, y_hbm, o_ref, buf_ref, sem):
    tm = COMBINE_TM
    i = pl.program_id(0)
    n = pl.num_programs(0)

    def gather(tile, slot):
        def issue(r, _):
            pltpu.make_async_copy(y_hbm.at[pos_ref[tile * tm + r]],
                                  buf_ref.at[pl.ds((slot * tm + r) * TILE_ROWS, TILE_ROWS), :], sem.at[slot]).start()
            return 0
        lax.fori_loop(0, tm, issue, 0, unroll=16)

    @pl.when(i == 0)
    def _():
        gather(0, 0)

    @pl.when(i + 1 < n)
    def _():
        gather(i + 1, (i + 1) % 2)

    slot = i % 2
    base = slot * tm * TILE_ROWS
    whole_slot = buf_ref.at[pl.ds(base, tm * TILE_ROWS), :]
    pltpu.make_async_copy(whole_slot, whole_slot, sem.at[slot]).wait()
    y = jnp.concatenate([buf_ref[pl.ds(base + s, tm, stride=TILE_ROWS), :] for s in range(TILE_ROWS)], axis=1)
    o_ref[...] = x_ref[...] + gate_ref[...] * y


def _moe_combine(pos, x, mod5, layer, y_sorted):
    tm = COMBINE_TM
    tiles_per_batch = SEQ // tm
    return pl.pallas_call(
        _combine_kernel,
        grid_spec=pltpu.PrefetchScalarGridSpec(
            num_scalar_prefetch=1, grid=(TOKENS // tm,),
            in_specs=[
                pl.BlockSpec((tm, D_MODEL), lambda i, pos: (i, 0)),
                pl.BlockSpec((None, None, None, 1, D_MODEL),
                             lambda i, pos: (layer, i // tiles_per_batch, 5, 0, 0)),
                pl.BlockSpec(memory_space=pl.ANY),
            ],
            out_specs=pl.BlockSpec((tm, D_MODEL), lambda i, pos: (i, 0)),
            scratch_shapes=[pltpu.VMEM((2 * tm * TILE_ROWS, TILE_LANES), F32), pltpu.SemaphoreType.DMA((2,))]),
        out_shape=jax.ShapeDtypeStruct((TOKENS, D_MODEL), F32),
        compiler_params=pltpu.CompilerParams(
            dimension_semantics=("arbitrary",), vmem_limit_bytes=VMEM_LIMIT),
        name="moe_combine",
    )(pos, x, mod5, y_sorted)


def _moe_plan(cls, rank, counts):
    count = counts[:MOE_CLASSES, 0].astype(jnp.int32)
    ends = jnp.cumsum(count)
    starts = ends - count
    class_ids = jnp.arange(MOE_CLASSES, dtype=jnp.int32)

    def lookup(table, idx):
        return jnp.sum(jnp.where(idx[..., None] == class_ids, table, 0), axis=-1)

    pos = (lookup(starts, cls) + rank).reshape(TOKENS)

    first_blk = starts // SORT_BLOCK
    n_items = jnp.where(count > 0, (ends - 1) // SORT_BLOCK - first_blk + 1, 0)
    item_end = jnp.cumsum(n_items)
    item_start = item_end - n_items
    k = jnp.arange(MAX_ITEMS, dtype=jnp.int32)
    valid = k < item_end[-1]
    kc = jnp.minimum(k, item_end[-1] - 1)
    icls = jnp.sum((item_end[None, :] <= kc[:, None]).astype(jnp.int32), axis=1)
    blk = lookup(first_blk, icls) + (kc - lookup(item_start, icls))
    first = jnp.concatenate([jnp.ones((1,), jnp.int32), (blk[1:] != blk[:-1]).astype(jnp.int32)])
    last = jnp.concatenate([(blk[1:] != blk[:-1]) | ~valid[1:], jnp.ones((1,), bool)]).astype(jnp.int32)
    group = icls // MOE_PAIRS
    elo = group * MOE_EPG + lookup(jnp.asarray(PAIR_LO * MOE_GROUPS, jnp.int32), icls)
    ehi = group * MOE_EPG + lookup(jnp.asarray(PAIR_HI * MOE_GROUPS, jnp.int32), icls)
    return pos, (blk, icls, elo, ehi, first * valid, last * valid, valid.astype(jnp.int32))


def _moe(x, nw, mod5, layer, w_route, b_route, w1, w3, w2):
    payload, cls, rank, counts = _moe_route(x, nw, mod5, layer, w_route, b_route)
    pos, items = _moe_plan(cls, rank, counts)
    tiles = (TOKENS, TILE_ROWS, TILE_LANES)
    sorted_payload = _moe_permute(pos, payload).reshape(TOKENS * TILE_ROWS, TILE_LANES)
    y_sorted = _moe_experts(items, sorted_payload, layer, w1, w3, w2)
    return _moe_combine(pos, x, mod5, layer, y_sorted.reshape(tiles))


def _pad_cols(w, width):
    return jnp.pad(w, ((0, 0), (0, width - w.shape[1])))


def kernel(x, c, ada_w, ada_b, norm_mix, norm_ffn, ssd_w_in, ssd_conv_w, ssd_conv_b, ssd_dt_bias,
           ssd_a_log, ssd_d, ssd_norm, ssd_w_out, dsa_w_in, dsa_q_norm, dsa_k_norm, dsa_w_out,
           moe_w_group, moe_b_group, moe_w_expert, moe_b_expert, moe_w1, moe_w3, moe_w2):
    depth = ada_w.shape[0]
    xt = x.reshape(TOKENS, D_MODEL)
    mod = _modulation(c, ada_w, ada_b)
    mod5 = mod.reshape(depth, BATCH, 6, 1, D_MODEL)

    head_of_col = jnp.arange(SSD_D_INNER, dtype=jnp.int32) // SSD_HEAD_DIM
    expand = (jnp.arange(SSD_DT_PAD, dtype=jnp.int32)[:, None] == head_of_col[None, :]).astype(BF16)

    for i in range(depth):
        j = i // 2
        nw_mix = norm_mix[i].reshape(1, D_MODEL)
        if i % 2 == 0:
            z, xbc, dt = _inproj(xt, nw_mix, mod5, i, 1, 0, ssd_w_in[j].astype(BF16))
            xt = _ssd_mixer(
                z, xbc, dt, xt, ssd_conv_w[j], ssd_conv_b[j].reshape(1, SSD_CONV_DIM),
                _pad_cols(ssd_dt_bias[j].reshape(1, SSD_HEADS), SSD_DT_PAD),
                _pad_cols(ssd_a_log[j].reshape(1, SSD_HEADS), SSD_DT_PAD),
                jnp.repeat(ssd_d[j], SSD_HEAD_DIM).reshape(1, SSD_D_INNER),
                ssd_norm[j].reshape(1, SSD_D_INNER), expand, ssd_w_out[j].astype(BF16), mod5, i)
        else:
            w = dsa_w_in[j]
            w_in = jnp.concatenate(
                [_pad_cols(w[:, :DSA_KI_START + IDX_HEAD_DIM], DSA_WI_START),
                 _pad_cols(w[:, DSA_KI_START + IDX_HEAD_DIM:], 128)], axis=1).astype(BF16)
            q, k, v, qi, ki, wi = _dsa_inproj(
                xt, nw_mix, mod5, i, w_in, dsa_q_norm[j].reshape(1, ATT_HEAD_DIM),
                dsa_k_norm[j].reshape(1, ATT_HEAD_DIM))
            xt = _dsa_mixer(q, k, v, qi, ki, wi, xt, dsa_w_out[j].astype(BF16), mod5, i)

        w_route = _pad_cols(jnp.concatenate([moe_w_expert[i], moe_w_group[i]], axis=1), ROUTE_PAD).astype(BF16)
        b_route = _pad_cols(jnp.concatenate([moe_b_expert[i], moe_b_group[i]]).reshape(1, -1), ROUTE_PAD)
        xt = _moe(xt, norm_ffn[i].reshape(1, D_MODEL), mod5, i, w_route, b_route,
                  moe_w1, moe_w3, moe_w2)
    return xt.reshape(BATCH, SEQ, D_MODEL)
```
